```python
import math
import jax
import jax.numpy as jnp
from jax import lax
import numpy as np

D_MODEL = 1024
BATCH = 8
SEQ = 2048
DEPTH = 2
DEC_BATCH = 128
DEC_SEQ = 1
PAST_LEN = 16384
PAGE_SIZE = 128

N_BRANCH = 4
BRANCH_W = D_MODEL // 4
RET_HEADS = 4
RET_DK = BRANCH_W // RET_HEADS
RET_DV = BRANCH_W // RET_HEADS
ROPE_BASE = 10000.0
HG_HEADS = 4
HG_DK = BRANCH_W // HG_HEADS
HG_DV = BRANCH_W // HG_HEADS
RW_HEADS = 4
RW_N = BRANCH_W // RW_HEADS
RW_G_LORA = 128
RW_W_LORA = 64
RW_A_LORA = 64
RW_COLS = 3 * BRANCH_W + RW_G_LORA + RW_W_LORA + RW_A_LORA
RW_GN_EPS = 64e-5
LRU_BLOCKS = 4
LRU_BW = BRANCH_W // LRU_BLOCKS
CONV_W = 4
LRU_C = 8.0
RET_COLS = 4 * BRANCH_W
HG_COLS = 4 * BRANCH_W
LRU_COLS = 2 * BRANCH_W
GATE_COLS = N_BRANCH * D_MODEL
IN_COLS = RET_COLS + HG_COLS + RW_COLS + LRU_COLS + GATE_COLS
COL_SPLITS = (RET_COLS, RET_COLS + HG_COLS, RET_COLS + HG_COLS + RW_COLS,
              RET_COLS + HG_COLS + RW_COLS + LRU_COLS)
CHUNK = 64
PEER_HEADS = 8
PEER_NKEYS = 128
PEER_N = PEER_NKEYS * PEER_NKEYS
PEER_QDIM = 256
PEER_TOPK = 16
PEER_BLOCK = 256
PLE_DIM = 256
ALPHA = (2 * DEPTH) ** 0.25
BETA = (8 * DEPTH) ** -0.25
LN_EPS = 1e-5

kernel_name = 'hybrid_ret_hgrn2_rwkv7_rglru_peer_decode_step'


def layer_norm(x, w, b, eps=LN_EPS):
    xf = x.astype(jnp.float32)
    mu = jnp.mean(xf, axis=-1, keepdims=True)
    var = jnp.mean(jnp.square(xf - mu), axis=-1, keepdims=True)
    return ((xf - mu) * lax.rsqrt(var + eps)).astype(x.dtype) * w + b


def rms_norm(x, w, eps=LN_EPS):
    xf = x.astype(jnp.float32)
    y = xf * lax.rsqrt(jnp.mean(jnp.square(xf), axis=-1, keepdims=True) + eps)
    return y.astype(x.dtype) * w


def rotary(x, pos):
    half = x.shape[-1] // 2
    inv_freq = ROPE_BASE ** (-jnp.arange(half, dtype=jnp.float32) / half)
    ang = pos.astype(jnp.float32)[:, None] * inv_freq[None, :]
    cos = jnp.cos(ang)[None, :, None, :].astype(x.dtype)
    sin = jnp.sin(ang)[None, :, None, :].astype(x.dtype)
    x1, x2 = x[..., :half], x[..., half:]
    return jnp.concatenate([x1 * cos - x2 * sin, x1 * sin + x2 * cos], axis=-1)


def chunked_decay_recurrence(q, k, v, log_f, s0):
    B, T, H, K = q.shape
    V = v.shape[-1]
    c = math.gcd(T, CHUNK)
    n = T // c
    f32 = jnp.float32

    def chunks(a):
        return a.astype(f32).reshape(B, n, c, H, a.shape[-1]).transpose(1, 0, 3, 2, 4)

    causal = jnp.tril(jnp.ones((c, c), dtype=bool))[None, None, :, :, None]

    def step(S, inp):
        qc, kc, vc, gc = inp
        b = jnp.cumsum(gc, axis=2)
        rel = b[:, :, :, None, :] - b[:, :, None, :, :]
        decay = jnp.where(causal, jnp.exp(jnp.where(causal, rel, 0.0)), 0.0)
        scores = jnp.einsum('bhtk,bhsk,bhtsk->bhts', qc, kc, decay)
        o = (jnp.einsum('bhts,bhsv->bhtv', scores, vc)
             + jnp.einsum('bhtk,bhkv->bhtv', qc * jnp.exp(b), S))
        b_end = b[:, :, -1:, :]
        S = (jnp.exp(b_end[:, :, 0, :])[..., None] * S
             + jnp.einsum('bhsk,bhsv->bhkv', kc * jnp.exp(b_end - b), vc))
        return S, o

    s_T, o = lax.scan(step, s0.astype(f32), (chunks(q), chunks(k), chunks(v), chunks(log_f)))
    o = o.transpose(1, 0, 3, 2, 4).reshape(B, T, H, V)
    return o.astype(v.dtype), s_T.astype(s0.dtype)


def retention_branch(cols, pos, s0, gn_w, gn_b):
    B, T, _ = cols.shape
    q, k, v, g = jnp.split(cols, 4, axis=-1)
    q = rotary(q.reshape(B, T, RET_HEADS, RET_DK), pos)
    k = rotary(k.reshape(B, T, RET_HEADS, RET_DK), pos) * (RET_DK ** -0.5)
    v = v.reshape(B, T, RET_HEADS, RET_DV)
    log_gamma = jnp.log1p(-jnp.exp2(-5.0 - jnp.arange(RET_HEADS, dtype=jnp.float32)))
    log_f = jnp.broadcast_to(log_gamma[None, None, :, None], q.shape)
    o, s = chunked_decay_recurrence(q, k, v, log_f, s0)
    o = layer_norm(o, gn_w.reshape(RET_HEADS, RET_DV), gn_b.reshape(RET_HEADS, RET_DV))
    return o.reshape(B, T, BRANCH_W) * jax.nn.silu(g), s


def hgrn2_branch(cols, s0, lb, norm_w):
    B, T, _ = cols.shape
    q, f, i, g = jnp.split(cols, 4, axis=-1)
    forget = lb.astype(jnp.float32) + (1.0 - lb.astype(jnp.float32)) * jax.nn.sigmoid(f.astype(jnp.float32))
    shp = (B, T, HG_HEADS, HG_DK)
    q = jax.nn.silu(q).reshape(shp)
    k = (1.0 - forget).reshape(shp)
    v = i.reshape(B, T, HG_HEADS, HG_DV)
    o, s = chunked_decay_recurrence(q, k, v, jnp.log(forget).reshape(shp), s0)
    o = rms_norm(o, norm_w.reshape(HG_HEADS, HG_DV))
    return o.reshape(B, T, BRANCH_W) * jax.nn.silu(g), s


def rwkv7_branch(cols, shift0, s0, mu, w0, w2, a0, a2, g2, k_k, k_a, r_k, lnx_w, lnx_b):
    B, T, _ = cols.shape
    W = BRANCH_W
    f32 = jnp.float32
    prev = jnp.concatenate([shift0[:, None, :], cols[:, :-1, :]], axis=1)
    xs = cols + mu * (prev - cols)
    r, k, v, xg, xw, xa = jnp.split(
        xs, [W, 2 * W, 3 * W, 3 * W + RW_G_LORA, 3 * W + RW_G_LORA + RW_W_LORA], axis=-1)
    w = -jax.nn.softplus(-(w0 + jnp.tanh(xw) @ w2)) - 0.5
    decay = jnp.exp(-jnp.exp(w.astype(f32)))
    a = jax.nn.sigmoid(a0 + xa @ a2)
    g = jax.nn.sigmoid(xg) @ g2
    hs = (B, T, RW_HEADS, RW_N)
    kk = (k * k_k).reshape(hs).astype(f32)
    kk = kk / jnp.maximum(jnp.sqrt(jnp.sum(jnp.square(kk), axis=-1, keepdims=True)), 1e-12)
    k = k * (1.0 + (a - 1.0) * k_a)
    rh, kh, vh = r.reshape(hs), k.reshape(hs), v.reshape(hs)

    def tmajor(t):
        return t.astype(f32).transpose(1, 0, 2, 3)

    def step(S, inp):
        r_t, w_t, k_t, v_t, kk_t, a_t = inp
        sa = jnp.einsum('bhvk,bhk->bhv', S, -kk_t)
        S = (S * w_t[:, :, None, :] + sa[..., None] * (kk_t * a_t)[:, :, None, :]
             + v_t[..., None] * k_t[:, :, None, :])
        return S, jnp.einsum('bhvk,bhk->bhv', S, r_t)

    s_T, o = lax.scan(step, s0.astype(f32),
                      (tmajor(rh), tmajor(decay.reshape(hs)), tmajor(kh), tmajor(vh),
                       tmajor(kk), tmajor(a.reshape(hs))))
    o = o.transpose(1, 0, 2, 3).astype(cols.dtype)
    o = layer_norm(o, lnx_w.reshape(RW_HEADS, RW_N), lnx_b.reshape(RW_HEADS, RW_N), eps=RW_GN_EPS)
    bonus = jnp.sum(rh * kh * r_k.reshape(RW_HEADS, RW_N), axis=-1, keepdims=True) * vh
    y = (o + bonus).reshape(B, T, W) * g
    return y, cols[:, -1, :], s_T.astype(s0.dtype)


def rglru_branch(cols, conv0, h0, conv_w, conv_b, wa, ba, wx, bx, lam):
    B, T, _ = cols.shape
    f32 = jnp.float32
    xb, gate = jnp.split(cols, 2, axis=-1)
    xpad = jnp.concatenate([conv0, xb], axis=1)
    new_conv = xpad[:, -(CONV_W - 1):, :]
    xc = lax.conv_general_dilated(xpad, conv_w[:, None, :], window_strides=(1,), padding='VALID',
                                  dimension_numbers=('NWC', 'WIO', 'NWC'),
                                  feature_group_count=BRANCH_W) + conv_b
    xg = xc.reshape(B, T, LRU_BLOCKS, LRU_BW)
    r = jax.nn.sigmoid((jnp.einsum('btnc,ncd->btnd', xg, wa).reshape(B, T, BRANCH_W) + ba).astype(f32))
    i = jax.nn.sigmoid(jnp.einsum('btnc,ncd->btnd', xg, wx).reshape(B, T, BRANCH_W) + bx)
    log_a = -LRU_C * r * jax.nn.softplus(-lam.astype(f32))
    a = jnp.exp(log_a)
    u = jnp.sqrt(-jnp.expm1(2.0 * log_a)) * (i * xc).astype(f32)

    def combine(left, right):
        a_l, b_l = left
        a_r, b_r = right
        return a_l * a_r, a_r * b_l + b_r

    a_cum, b_cum = lax.associative_scan(combine, (a, u), axis=1)
    h = a_cum * h0.astype(f32)[:, None, :] + b_cum
    y = h.astype(cols.dtype) * jax.nn.gelu(gate)
    return y, new_conv, h[:, -1, :].astype(h0.dtype)


def peer(x, wq, keys, u_tab, v_tab):
    B, T, D = x.shape
    n = B * T
    blk = min(PEER_BLOCK, n)
    pad = (-n) % blk
    xt = jnp.pad(x.reshape(n, D), ((0, pad), (0, 0)))

    def block(xb):
        q = (xb @ wq).reshape(blk, PEER_HEADS, 2, PEER_QDIM // 2)
        s = jnp.einsum('nhpc,hpkc->nhpk', q, keys).astype(jnp.float32)
        s_top, i_top = lax.top_k(s, PEER_TOPK)
        cand = s_top[:, :, 0, :, None] + s_top[:, :, 1, None, :]
        cand_idx = i_top[:, :, 0, :, None] * PEER_NKEYS + i_top[:, :, 1, None, :]
        best, sel = lax.top_k(cand.reshape(blk, PEER_HEADS, PEER_TOPK * PEER_TOPK), PEER_TOPK)
        eidx = jnp.take_along_axis(cand_idx.reshape(blk, PEER_HEADS, PEER_TOPK * PEER_TOPK), sel, axis=-1)
        gw = jax.nn.softmax(best, axis=-1)
        ue = u_tab[eidx]
        ve = v_tab[eidx]
        act = jax.nn.gelu(jnp.einsum('nhkd,nd->nhk', ue, xb).astype(jnp.float32))
        return jnp.einsum('nhk,nhkd->nd', (gw * act).astype(x.dtype), ve)

    y = lax.map(block, xt.reshape(-1, blk, D))
    return y.reshape(-1, D)[:n].reshape(B, T, D)


def decoder_layer(x, p_emb, pos, state, lb, lp):
    B, T, D = x.shape
    s_ret, s_hg, s_rw, s_shift, s_lru, s_conv = state
    cols = x @ lp['w_in']
    c_ret, c_hg, c_rw, c_lru, c_gate = jnp.split(cols, COL_SPLITS, axis=-1)
    o_a, s_ret = retention_branch(c_ret, pos, s_ret, lp['ret_gn_w'], lp['ret_gn_b'])
    o_b, s_hg = hgrn2_branch(c_hg, s_hg, lb, lp['hg_norm_w'])
    o_c, s_shift, s_rw = rwkv7_branch(c_rw, s_shift, s_rw, lp['rw_mu'], lp['rw_w0'], lp['rw_w2'],
                                      lp['rw_a0'], lp['rw_a2'], lp['rw_g2'], lp['rw_kk'], lp['rw_ka'],
                                      lp['rw_rk'], lp['rw_lnx_w'], lp['rw_lnx_b'])
    o_d, s_conv, s_lru = rglru_branch(c_lru, s_conv, s_lru, lp['lru_conv_w'], lp['lru_conv_b'],
                                      lp['lru_wa'], lp['lru_ba'], lp['lru_wx'], lp['lru_bx'],
                                      lp['lru_lambda'])
    branches = jnp.stack([o_a, o_b, o_c, o_d], axis=2)
    up = jnp.einsum('btgw,gwd->btgd', branches, lp['w_branch'])
    gates = jax.nn.sigmoid(c_gate.reshape(B, T, N_BRANCH, D))
    mixed = jnp.sum(gates * up, axis=2) @ lp['w_out']
    x = layer_norm(ALPHA * x + mixed, lp['ln1_w'], lp['ln1_b'])
    x = layer_norm(ALPHA * x + peer(x, lp['peer_wq'], lp['peer_keys'], lp['peer_u'], lp['peer_v']),
                   lp['ln2_w'], lp['ln2_b'])
    x = x + jax.nn.sigmoid(x @ lp['ple_gate_w'] + lp['ple_gate_b']) * (p_emb @ lp['ple_w'])
    return x, (s_ret, s_hg, s_rw, s_shift, s_lru, s_conv)


def setup_inputs(seed: int = 0) -> dict:
    key = jax.random.key(seed)
    ks = iter(jax.random.split(key, 64))
    f32 = jnp.float32

    def nrm(shape, scale):
        return scale * jax.random.normal(next(ks), shape, f32)

    def uni(shape, lo, hi):
        return jax.random.uniform(next(ks), shape, f32, lo, hi)

    L, W, D = DEPTH, BRANCH_W, D_MODEL
    a_init = uni((L, W), 0.9, 0.999)
    sig_lam = a_init ** (1.0 / LRU_C)
    lru_lambda = jnp.log(sig_lam) - jnp.log1p(-sig_lam)
    return {
        'x_prompt': nrm((BATCH, SEQ, D), 1.0),
        'x_sample': nrm((DEC_BATCH, DEC_SEQ, D), 1.0),
        'state_ret': nrm((L, DEC_BATCH, RET_HEADS, RET_DK, RET_DV), 0.5),
        'state_hgrn': nrm((L, DEC_BATCH, HG_HEADS, HG_DK, HG_DV), 0.5),
        'state_rwkv': nrm((L, DEC_BATCH, RW_HEADS, RW_N, RW_N), 0.3),
        'state_shift': nrm((L, DEC_BATCH, RW_COLS), 1.0),
        'state_lru': nrm((L, DEC_BATCH, W), 0.5),
        'state_conv': nrm((L, DEC_BATCH, CONV_W - 1, W), 1.0),
        'p_prompt': nrm((L, BATCH, SEQ, PLE_DIM), 1.0),
        'p_sample': nrm((L, DEC_BATCH, DEC_SEQ, PLE_DIM), 1.0),
        'w_in': nrm((L, D, IN_COLS), D ** -0.5),
        'ret_gn_w': 1.0 + nrm((L, W), 0.02),
        'ret_gn_b': nrm((L, W), 0.02),
        'hg_lb': nrm((L, W), 1.0),
        'hg_norm_w': 1.0 + nrm((L, W), 0.02),
        'rw_mu': uni((L, RW_COLS), 0.0, 1.0),
        'rw_w0': uni((L, W), -4.0, 1.0),
        'rw_w2': nrm((L, RW_W_LORA, W), 0.1),
        'rw_a0': nrm((L, W), 0.5),
        'rw_a2': nrm((L, RW_A_LORA, W), 0.1),
        'rw_g2': nrm((L, RW_G_LORA, W), RW_G_LORA ** -0.5),
        'rw_kk': 0.85 + nrm((L, W), 0.05),
        'rw_ka': 1.0 + nrm((L, W), 0.05),
        'rw_rk': nrm((L, W), 0.1),
        'rw_lnx_w': 1.0 + nrm((L, W), 0.02),
        'rw_lnx_b': nrm((L, W), 0.02),
        'lru_conv_w': nrm((L, CONV_W, W), CONV_W ** -0.5),
        'lru_conv_b': nrm((L, W), 0.02),
        'lru_wa': nrm((L, LRU_BLOCKS, LRU_BW, LRU_BW), LRU_BW ** -0.5),
        'lru_ba': nrm((L, W), 0.02),
        'lru_wx': nrm((L, LRU_BLOCKS, LRU_BW, LRU_BW), LRU_BW ** -0.5),
        'lru_bx': nrm((L, W), 0.02),
        'lru_lambda': lru_lambda,
        'w_branch': nrm((L, N_BRANCH, W, D), W ** -0.5),
        'w_out': nrm((L, D, D), BETA * D ** -0.5),
        'ln1_w': 1.0 + nrm((L, D), 0.02),
        'ln1_b': nrm((L, D), 0.02),
        'peer_wq': nrm((L, D, PEER_HEADS * PEER_QDIM), D ** -0.5),
        'peer_keys': nrm((L, PEER_HEADS, 2, PEER_NKEYS, PEER_QDIM // 2), (PEER_QDIM // 2) ** -0.5),
        'peer_u': nrm((L, PEER_N, D), D ** -0.5),
        'peer_v': nrm((L, PEER_N, D), BETA),
        'ln2_w': 1.0 + nrm((L, D), 0.02),
        'ln2_b': nrm((L, D), 0.02),
        'ple_w': nrm((L, PLE_DIM, D), PLE_DIM ** -0.5),
        'ple_gate_w': nrm((L, D, D), D ** -0.5),
        'ple_gate_b': nrm((L, D), 0.02),
    }


def reference(x_prompt, x_sample, state_ret, state_hgrn, state_rwkv, state_shift, state_lru, state_conv,
              p_prompt, p_sample, w_in, ret_gn_w, ret_gn_b, hg_lb, hg_norm_w, rw_mu, rw_w0, rw_w2,
              rw_a0, rw_a2, rw_g2, rw_kk, rw_ka, rw_rk, rw_lnx_w, rw_lnx_b, lru_conv_w, lru_conv_b,
              lru_wa, lru_ba, lru_wx, lru_bx, lru_lambda, w_branch, w_out, ln1_w, ln1_b, peer_wq,
              peer_keys, peer_u, peer_v, ln2_w, ln2_b, ple_w, ple_gate_w, ple_gate_b):
    lb_cum = jnp.cumsum(jax.nn.softmax(hg_lb.astype(jnp.float32), axis=0), axis=0)
    lb_all = lb_cum - lb_cum[0:1]
    bp, tp = x_prompt.shape[0], x_prompt.shape[1]
    pos_p = jnp.arange(tp)
    pos_s = PAST_LEN + jnp.arange(x_sample.shape[1])
    dt = x_prompt.dtype
    zero_state = (jnp.zeros((bp, RET_HEADS, RET_DK, RET_DV), dt),
                  jnp.zeros((bp, HG_HEADS, HG_DK, HG_DV), dt),
                  jnp.zeros((bp, RW_HEADS, RW_N, RW_N), dt),
                  jnp.zeros((bp, RW_COLS), dt),
                  jnp.zeros((bp, BRANCH_W), dt),
                  jnp.zeros((bp, CONV_W - 1, BRANCH_W), dt))
    h_p, h_s = x_prompt, x_sample
    new_p, new_s = [], []
    for l in range(DEPTH):
        lp = {'w_in': w_in[l], 'ret_gn_w': ret_gn_w[l], 'ret_gn_b': ret_gn_b[l],
              'hg_norm_w': hg_norm_w[l], 'rw_mu': rw_mu[l], 'rw_w0': rw_w0[l], 'rw_w2': rw_w2[l],
              'rw_a0': rw_a0[l], 'rw_a2': rw_a2[l], 'rw_g2': rw_g2[l], 'rw_kk': rw_kk[l],
              'rw_ka': rw_ka[l], 'rw_rk': rw_rk[l], 'rw_lnx_w': rw_lnx_w[l], 'rw_lnx_b': rw_lnx_b[l],
              'lru_conv_w': lru_conv_w[l], 'lru_conv_b': lru_conv_b[l], 'lru_wa': lru_wa[l],
              'lru_ba': lru_ba[l], 'lru_wx': lru_wx[l], 'lru_bx': lru_bx[l],
              'lru_lambda': lru_lambda[l], 'w_branch': w_branch[l], 'w_out': w_out[l],
              'ln1_w': ln1_w[l], 'ln1_b': ln1_b[l], 'peer_wq': peer_wq[l], 'peer_keys': peer_keys[l],
              'peer_u': peer_u[l], 'peer_v': peer_v[l], 'ln2_w': ln2_w[l], 'ln2_b': ln2_b[l],
              'ple_w': ple_w[l], 'ple_gate_w': ple_gate_w[l], 'ple_gate_b': ple_gate_b[l]}
        h_p, st_p = decoder_layer(h_p, p_prompt[l], pos_p, zero_state, lb_all[l], lp)
        st_in = (state_ret[l], state_hgrn[l], state_rwkv[l], state_shift[l], state_lru[l], state_conv[l])
        h_s, st_s = decoder_layer(h_s, p_sample[l], pos_s, st_in, lb_all[l], lp)
        new_p.append(st_p)
        new_s.append(st_s)
    ret_p, hgrn_p, rwkv_p, shift_p, lru_p, conv_p = [jnp.stack(z) for z in zip(*new_p)]
    ret_s, hgrn_s, rwkv_s, shift_s, lru_s, conv_s = [jnp.stack(z) for z in zip(*new_s)]
    return (h_p, h_s, ret_p, hgrn_p, rwkv_p, shift_p, lru_p, conv_p,
            ret_s, hgrn_s, rwkv_s, shift_s, lru_s, conv_s)
```

```python
import functools
import math

import jax
import jax.numpy as jnp
from jax import lax
from jax.experimental import pallas as pl
from jax.experimental.pallas import tpu as pltpu

F32 = jnp.float32
BF16 = jnp.bfloat16
HIGHEST = lax.Precision.HIGHEST

D_MODEL = 1024
BRANCH_W = 256
N_HEADS = 4
HEAD_DIM = 64
RET_LOG_GAMMA = tuple(math.log1p(-(2.0 ** (-5.0 - h))) for h in range(N_HEADS))
ROPE_BASE = 10000.0
RW_GN_EPS = 64e-5
LN_EPS = 1e-5
LRU_C = 8.0
CONV_W = 4
DEPTH = 2
ALPHA = (2 * DEPTH) ** 0.25
BRANCH_COLS = 3584
MIB = 1024 * 1024


def _cparams(semantics, vmem_mib):
    return pltpu.CompilerParams(dimension_semantics=semantics, vmem_limit_bytes=vmem_mib * MIB)


def _mm(a, b):
    return jnp.dot(a.astype(BF16), b.astype(BF16), preferred_element_type=F32)


def _mmf(a, b):
    return jnp.dot(a, b, preferred_element_type=F32, precision=HIGHEST)


def _mmf_nt(a, b):
    return lax.dot_general(a, b, (((1,), (1,)), ((), ())), preferred_element_type=F32, precision=HIGHEST)


def _mmf_tn(a, b):
    return lax.dot_general(a, b, (((0,), (0,)), ((), ())), preferred_element_type=F32, precision=HIGHEST)


def _sigmoid(x):
    return 1.0 / (1.0 + jnp.exp(-x))


def _silu(x):
    return x * _sigmoid(x)


def _gelu(x):
    return 0.5 * x * (1.0 + jnp.tanh(0.7978845608028654 * (x + 0.044715 * (x * x * x))))


def _softplus(x):
    return jnp.maximum(x, 0.0) + jnp.log1p(jnp.exp(-jnp.abs(x)))


def _head(x, h):
    return x[:, h * HEAD_DIM:(h + 1) * HEAD_DIM]


def _block_diag(w):
    n, c, d = w.shape
    eye = jnp.eye(n, dtype=w.dtype)
    return (eye[:, None, :, None] * w[:, :, None, :]).reshape(n * c, n * d)


def _rope_tables(pos):
    half = HEAD_DIM // 2
    inv_freq = ROPE_BASE ** (-jnp.arange(half, dtype=F32) / half)
    ang = pos.astype(F32)[:, None] * inv_freq[None, :]
    cos = jnp.cos(ang)
    sin = jnp.sin(ang)
    cos_t = jnp.tile(jnp.concatenate([cos, cos], axis=-1), (1, N_HEADS))
    sin_t = jnp.tile(jnp.concatenate([-sin, sin], axis=-1), (1, N_HEADS))
    return cos_t, sin_t


def _layer_params(p, l, lb_all):
    r2 = lambda a: a.reshape(1, -1)
    w_in = p['w_in'][l]
    lp = {
        'w_in_br': w_in[:, :BRANCH_COLS].astype(BF16),
        'w_gate': w_in[:, BRANCH_COLS:].astype(BF16),
        'ret_gn_w': r2(p['ret_gn_w'][l]), 'ret_gn_b': r2(p['ret_gn_b'][l]),
        'hg_lb': r2(lb_all[l]), 'hg_norm_w': r2(p['hg_norm_w'][l]),
        'rw_mu': r2(p['rw_mu'][l]), 'rw_w0': r2(p['rw_w0'][l]), 'rw_w2': p['rw_w2'][l].astype(BF16),
        'rw_a0': r2(p['rw_a0'][l]), 'rw_a2': p['rw_a2'][l].astype(BF16), 'rw_g2': p['rw_g2'][l].astype(BF16),
        'rw_kk': r2(p['rw_kk'][l]), 'rw_ka': r2(p['rw_ka'][l]), 'rw_rk': r2(p['rw_rk'][l]),
        'rw_lnx_w': r2(p['rw_lnx_w'][l]), 'rw_lnx_b': r2(p['rw_lnx_b'][l]),
        'lru_conv_w': p['lru_conv_w'][l], 'lru_conv_b': r2(p['lru_conv_b'][l]),
        'lru_wa_bd': _block_diag(p['lru_wa'][l]).astype(BF16), 'lru_ba': r2(p['lru_ba'][l]),
        'lru_wx_bd': _block_diag(p['lru_wx'][l]).astype(BF16), 'lru_bx': r2(p['lru_bx'][l]),
        'lru_lambda': r2(p['lru_lambda'][l]),
        'w_branch': p['w_branch'][l].astype(BF16), 'w_out': p['w_out'][l].astype(BF16),
        'ln1_w': r2(p['ln1_w'][l]), 'ln1_b': r2(p['ln1_b'][l]),
        'peer_wq_t': p['peer_wq'][l].T.astype(BF16),
        'peer_keys': p['peer_keys'][l].reshape(16, 128, 128).astype(BF16),
        'peer_u': p['peer_u'][l].astype(BF16), 'peer_v_t': p['peer_v'][l].T.astype(BF16),
        'ln2_w': r2(p['ln2_w'][l]), 'ln2_b': r2(p['ln2_b'][l]),
        'ple_w': p['ple_w'][l].astype(BF16), 'ple_gate_w': p['ple_gate_w'][l].astype(BF16),
        'ple_gate_b': r2(p['ple_gate_b'][l]),
    }
    return lp


def _proj_kernel(x_ref, w_ref, o_ref):
    o_ref[...] = jnp.dot(x_ref[...].astype(BF16), w_ref[...], preferred_element_type=F32)


def _proj(x, w_bf16, tn):
    n, k = x.shape
    m = w_bf16.shape[1]
    return pl.pallas_call(
        _proj_kernel,
        grid=(n // tn,),
        in_specs=[pl.BlockSpec((tn, k), lambda i: (i, 0)),
                  pl.BlockSpec((k, m), lambda i: (0, 0))],
        out_specs=pl.BlockSpec((tn, m), lambda i: (i, 0)),
        out_shape=jax.ShapeDtypeStruct((n, m), F32),
        compiler_params=_cparams(("parallel",), 48),
        name="in_proj",
    )(x, w_bf16)


def _lru_gates(xc, wa_ref, ba_ref, wx_ref, bx_ref, lam_ref):
    r = _sigmoid(_mm(xc, wa_ref[...]) + ba_ref[...])
    i = _sigmoid(_mm(xc, wx_ref[...]) + bx_ref[...])
    log_a = -LRU_C * r * _softplus(-lam_ref[...])
    a = jnp.exp(log_a)
    u = jnp.sqrt(1.0 - jnp.exp(2.0 * log_a)) * (i * xc)
    return a, u


def _lru_kernel(c_ref, cw_ref, cb_ref, wa_ref, ba_ref, wx_ref, bx_ref, lam_ref,
                y_ref, h_ref, conv_ref, xbuf, hcar, a_s, u_s, hs, *, tc):
    c = pl.program_id(1)

    @pl.when(c == 0)
    def _():
        xbuf[0:8, :] = jnp.zeros((8, BRANCH_W), F32)
        hcar[...] = jnp.zeros((1, BRANCH_W), F32)

    xb = c_ref[:, 0:BRANCH_W]
    gate = c_ref[:, BRANCH_W:2 * BRANCH_W]
    xbuf[8:8 + tc, :] = xb
    xc = (xbuf[pl.ds(5, tc), :] * cw_ref[0:1, :] + xbuf[pl.ds(6, tc), :] * cw_ref[1:2, :]
          + xbuf[pl.ds(7, tc), :] * cw_ref[2:3, :] + xb * cw_ref[3:4, :]) + cb_ref[...]
    a, u = _lru_gates(xc, wa_ref, ba_ref, wx_ref, bx_ref, lam_ref)
    a_s[...] = a
    u_s[...] = u

    def body(t, h):
        h = a_s[pl.ds(t, 1), :] * h + u_s[pl.ds(t, 1), :]
        hs[pl.ds(t, 1), :] = h
        return h

    h = lax.fori_loop(0, tc, body, hcar[...], unroll=8)
    hcar[...] = h
    y_ref[...] = hs[...] * _gelu(gate)
    xbuf[0:8, :] = xbuf[tc:tc + 8, :]

    @pl.when(c == pl.num_programs(1) - 1)
    def _():
        h_ref[...] = h
        conv_ref[...] = xbuf[5:8, :]


def _lru_prompt(cols3, lp, tc=256):
    b, t, _ = cols3.shape
    row = lambda: pl.BlockSpec((1, BRANCH_W), lambda i, j: (0, 0))
    full = lambda r: pl.BlockSpec((r, BRANCH_W), lambda i, j: (0, 0))
    return pl.pallas_call(
        functools.partial(_lru_kernel, tc=tc),
        grid=(b, t // tc),
        in_specs=[pl.BlockSpec((None, tc, 512), lambda i, j: (i, j, 6)),
                  full(CONV_W), row(), full(BRANCH_W), row(), full(BRANCH_W), row(), row()],
        out_specs=[pl.BlockSpec((None, tc, BRANCH_W), lambda i, j: (i, j, 0)),
                   pl.BlockSpec((None, 1, BRANCH_W), lambda i, j: (i, 0, 0)),
                   pl.BlockSpec((None, CONV_W - 1, BRANCH_W), lambda i, j: (i, 0, 0))],
        out_shape=[jax.ShapeDtypeStruct((b, t, BRANCH_W), F32),
                   jax.ShapeDtypeStruct((b, 1, BRANCH_W), F32),
                   jax.ShapeDtypeStruct((b, CONV_W - 1, BRANCH_W), F32)],
        scratch_shapes=[pltpu.VMEM((tc + 8, BRANCH_W), F32), pltpu.VMEM((1, BRANCH_W), F32),
                        pltpu.VMEM((tc, BRANCH_W), F32), pltpu.VMEM((tc, BRANCH_W), F32),
                        pltpu.VMEM((tc, BRANCH_W), F32)],
        compiler_params=_cparams(("parallel", "arbitrary"), 32),
        name="lru_prompt",
    )(cols3, lp['lru_conv_w'], lp['lru_conv_b'], lp['lru_wa_bd'], lp['lru_ba'], lp['lru_wx_bd'],
      lp['lru_bx'], lp['lru_lambda'])


def _rotary(x, cos, sin_signed, first_half):
    partner = jnp.where(first_half, pltpu.roll(x, BRANCH_W - 32, 1), pltpu.roll(x, 32, 1))
    return x * cos + partner * sin_signed


def _ret_kernel(c_ref, cos_ref, sin_ref, gw_ref, gb_ref, o_ref, s_ref, S, *, tc):
    c = pl.program_id(1)

    @pl.when(c == 0)
    def _():
        S[...] = jnp.zeros(S.shape, F32)

    lane = lax.broadcasted_iota(jnp.int32, (tc, BRANCH_W), 1)
    first_half = (lane % HEAD_DIM) < (HEAD_DIM // 2)
    cos = cos_ref[...]
    sin = sin_ref[...]
    q = _rotary(c_ref[:, 0:256], cos, sin, first_half)
    k = _rotary(c_ref[:, 256:512], cos, sin, first_half) * (HEAD_DIM ** -0.5)
    v = c_ref[:, 512:768]
    g = c_ref[:, 768:1024]
    row = lax.broadcasted_iota(jnp.int32, (tc, tc), 0)
    col = lax.broadcasted_iota(jnp.int32, (tc, tc), 1)
    causal = row >= col
    dist = jnp.where(causal, row - col, 0).astype(F32)
    tpos = lax.broadcasted_iota(jnp.int32, (tc, HEAD_DIM), 0).astype(F32)
    for h in range(N_HEADS):
        lg = RET_LOG_GAMMA[h]
        qh, kh, vh = _head(q, h), _head(k, h), _head(v, h)
        decay = jnp.where(causal, jnp.exp(dist * lg), 0.0)
        scores = _mmf_nt(qh, kh) * decay
        oh = _mmf(scores, vh) + _mmf(qh * jnp.exp((tpos + 1.0) * lg), S[h])
        S[h] = math.exp(tc * lg) * S[h] + _mmf_tn(kh * jnp.exp((tc - 1.0 - tpos) * lg), vh)
        mu = jnp.mean(oh, axis=-1, keepdims=True)
        d = oh - mu
        var = jnp.mean(d * d, axis=-1, keepdims=True)
        on = d * lax.rsqrt(var + LN_EPS) * _head(gw_ref[...], h) + _head(gb_ref[...], h)
        o_ref[:, h * HEAD_DIM:(h + 1) * HEAD_DIM] = on * _silu(_head(g, h))

    @pl.when(c == pl.num_programs(1) - 1)
    def _():
        s_ref[...] = S[...]


def _ret_prompt(cols3, cos_t, sin_t, lp, tc=128):
    b, t, _ = cols3.shape
    row = lambda: pl.BlockSpec((1, BRANCH_W), lambda i, j: (0, 0))
    return pl.pallas_call(
        functools.partial(_ret_kernel, tc=tc),
        grid=(b, t // tc),
        in_specs=[pl.BlockSpec((None, tc, 1024), lambda i, j: (i, j, 0)),
                  pl.BlockSpec((tc, BRANCH_W), lambda i, j: (j, 0)),
                  pl.BlockSpec((tc, BRANCH_W), lambda i, j: (j, 0)), row(), row()],
        out_specs=[pl.BlockSpec((None, tc, BRANCH_W), lambda i, j: (i, j, 0)),
                   pl.BlockSpec((None, N_HEADS, HEAD_DIM, HEAD_DIM), lambda i, j: (i, 0, 0, 0))],
        out_shape=[jax.ShapeDtypeStruct((b, t, BRANCH_W), F32),
                   jax.ShapeDtypeStruct((b, N_HEADS, HEAD_DIM, HEAD_DIM), F32)],
        scratch_shapes=[pltpu.VMEM((N_HEADS, HEAD_DIM, HEAD_DIM), F32)],
        compiler_params=_cparams(("parallel", "arbitrary"), 32),
        name="ret_prompt",
    )(cols3, cos_t, sin_t, lp['ret_gn_w'], lp['ret_gn_b'])


def _hgrn_kernel(c_ref, lb_ref, nw_ref, o_ref, s_ref, St, b_s, kk_s, *, tc):
    c = pl.program_id(1)

    @pl.when(c == 0)
    def _():
        St[...] = jnp.zeros(St.shape, F32)

    lb = lb_ref[...]
    q = _silu(c_ref[:, 0:256])
    forget = lb + (1.0 - lb) * _sigmoid(c_ref[:, 256:512])
    v = c_ref[:, 512:768]
    g = c_ref[:, 768:1024]
    lg = jnp.log(forget)
    kk = 1.0 - forget
    r64 = lax.broadcasted_iota(jnp.int32, (tc, tc), 0)
    c64 = lax.broadcasted_iota(jnp.int32, (tc, tc), 1)
    b = _mmf((r64 >= c64).astype(F32), lg)
    b_s[...] = b
    kk_s[...] = kk
    rowt = lax.broadcasted_iota(jnp.int32, (tc, BRANCH_W), 0)
    lane = lax.broadcasted_iota(jnp.int32, (tc, BRANCH_W), 1)
    lane_s = lane % HEAD_DIM
    rr = lax.broadcasted_iota(jnp.int32, (BRANCH_W, BRANCH_W), 0) // HEAD_DIM
    cc = lax.broadcasted_iota(jnp.int32, (BRANCH_W, BRANCH_W), 1) // HEAD_DIM
    ones_bd = (rr == cc).astype(BF16)

    def body(s, sc):
        brow = b_s[pl.ds(s, 1), :]
        krow = kk_s[pl.ds(s, 1), :]
        m = rowt >= s
        p = jnp.where(m, q * krow * jnp.exp(jnp.where(m, b - brow, 0.0)), 0.0)
        hi = p.astype(BF16)
        lo = (p - hi.astype(F32)).astype(BF16)
        red = (jnp.dot(hi, ones_bd, preferred_element_type=F32)
               + jnp.dot(lo, ones_bd, preferred_element_type=F32))
        return jnp.where(lane_s == s, red, sc)

    sc = lax.fori_loop(0, tc, body, jnp.zeros((tc, BRANCH_W), F32))
    b_end = b_s[tc - 1:tc, :]
    qd = q * jnp.exp(b)
    kd = kk * jnp.exp(b_end - b)
    eye = (r64 == c64).astype(F32)
    for h in range(N_HEADS):
        vh = _head(v, h)
        oh = _mmf(_head(sc, h), vh) + _mmf_nt(_head(qd, h), St[h])
        St[h] = St[h] * jnp.exp(_head(b_end, h)) + _mmf_tn(vh, _head(kd, h))
        ms = jnp.mean(oh * oh, axis=-1, keepdims=True)
        on = oh * lax.rsqrt(ms + LN_EPS) * _head(nw_ref[...], h)
        o_ref[:, h * HEAD_DIM:(h + 1) * HEAD_DIM] = on * _silu(_head(g, h))

    @pl.when(c == pl.num_programs(1) - 1)
    def _():
        for h in range(N_HEADS):
            s_ref[h] = _mmf_tn(St[h], eye)


def _hgrn_prompt(cols3, lb, lp, tc=64):
    b, t, _ = cols3.shape
    row = lambda: pl.BlockSpec((1, BRANCH_W), lambda i, j: (0, 0))
    return pl.pallas_call(
        functools.partial(_hgrn_kernel, tc=tc),
        grid=(b, t // tc),
        in_specs=[pl.BlockSpec((None, tc, 1024), lambda i, j: (i, j, 1)), row(), row()],
        out_specs=[pl.BlockSpec((None, tc, BRANCH_W), lambda i, j: (i, j, 0)),
                   pl.BlockSpec((None, N_HEADS, HEAD_DIM, HEAD_DIM), lambda i, j: (i, 0, 0, 0))],
        out_shape=[jax.ShapeDtypeStruct((b, t, BRANCH_W), F32),
                   jax.ShapeDtypeStruct((b, N_HEADS, HEAD_DIM, HEAD_DIM), F32)],
        scratch_shapes=[pltpu.VMEM((N_HEADS, HEAD_DIM, HEAD_DIM), F32),
                        pltpu.VMEM((tc, BRANCH_W), F32), pltpu.VMEM((tc, BRANCH_W), F32)],
        compiler_params=_cparams(("parallel", "arbitrary"), 32),
        name="hgrn_prompt",
    )(cols3, lb, lp['hg_norm_w'])


def _rwkv_token_mix(cols, prev, mu_ref, w0_ref, w2_ref, a0_ref, a2_ref, g2_ref, kkw_ref, ka_ref):
    xs = cols + mu_ref[...] * (prev - cols)
    r = xs[:, 0:256]
    k = xs[:, 256:512]
    v = xs[:, 512:768]
    xg = xs[:, 768:896]
    xw = xs[:, 896:960]
    xa = xs[:, 960:1024]
    w = -_softplus(-(w0_ref[...] + _mm(jnp.tanh(xw), w2_ref[...]))) - 0.5
    lw = -jnp.exp(w)
    a = _sigmoid(a0_ref[...] + _mm(xa, a2_ref[...]))
    g = _mm(_sigmoid(xg), g2_ref[...])
    kk = k * kkw_ref[...]
    parts = []
    for h in range(N_HEADS):
        kh = _head(kk, h)
        nrm = jnp.sqrt(jnp.sum(kh * kh, axis=-1, keepdims=True))
        parts.append(kh / jnp.maximum(nrm, 1e-12))
    kkn = jnp.concatenate(parts, axis=-1)
    k2 = k * (1.0 + (a - 1.0) * ka_ref[...])
    return r, lw, k2, v, kkn, a, g


def _rwkv_out(o, r, k2, v, g, rk_ref, lw_ref, lb_ref, h):
    mu = jnp.mean(o, axis=-1, keepdims=True)
    d = o - mu
    var = jnp.mean(d * d, axis=-1, keepdims=True)
    on = d * lax.rsqrt(var + RW_GN_EPS) * _head(lw_ref[...], h) + _head(lb_ref[...], h)
    bonus = jnp.sum(_head(r, h) * _head(k2, h) * _head(rk_ref[...], h), axis=-1, keepdims=True) * _head(v, h)
    return (on + bonus) * _head(g, h)


def _rwkv_kernel(c_ref, mu_ref, w0_ref, w2_ref, a0_ref, a2_ref, g2_ref, kkw_ref, ka_ref, rk_ref,
                 lnw_ref, lnb_ref, y_ref, shift_ref, s_ref, S, last_row, *, tc):
    c = pl.program_id(1)

    @pl.when(c == 0)
    def _():
        S[...] = jnp.zeros(S.shape, F32)
        last_row[...] = jnp.zeros(last_row.shape, F32)

    cols = c_ref[...]
    rowi = lax.broadcasted_iota(jnp.int32, cols.shape, 0)
    prev = jnp.where(rowi == 0, last_row[...], pltpu.roll(cols, 1, 0))
    last_row[...] = cols[tc - 1:tc, :]
    r, lw, k2, v, kkn, a, g = _rwkv_token_mix(cols, prev, mu_ref, w0_ref, w2_ref, a0_ref, a2_ref, g2_ref,
                                              kkw_ref, ka_ref)
    am = -kkn
    bm = kkn * a
    rr = lax.broadcasted_iota(jnp.int32, (tc, tc), 0)
    cc = lax.broadcasted_iota(jnp.int32, (tc, tc), 1)
    G = _mmf((rr >= cc).astype(F32), lw)
    g_end = G[tc - 1:tc, :]
    at = am * jnp.exp(G - lw)
    rt = r * jnp.exp(G)
    einv = jnp.exp(-G)
    bt = bm * einv
    kt = k2 * einv
    eend = jnp.exp(g_end - G)
    bbar = bm * eend
    kbar = k2 * eend
    strict = rr > cc
    incl = rr >= cc
    n_double = max(1, (tc - 1).bit_length())
    for h in range(N_HEADS):
        S0 = S[h]
        ath, rth, bth, kth, vh = _head(at, h), _head(rt, h), _head(bt, h), _head(kt, h), _head(v, h)
        a_ab = jnp.where(strict, _mmf_nt(ath, bth), 0.0)
        a_ak = jnp.where(strict, _mmf_nt(ath, kth), 0.0)
        a_rb = jnp.where(incl, _mmf_nt(rth, bth), 0.0)
        a_rk = jnp.where(incl, _mmf_nt(rth, kth), 0.0)
        u = _mmf_nt(ath, S0) + _mmf(a_ak, vh)
        npow = a_ab
        for j in range(n_double):
            u = u + _mmf(npow, u)
            if j + 1 < n_double:
                npow = _mmf(npow, npow)
        o = _mmf_nt(rth, S0) + _mmf(a_rb, u) + _mmf(a_rk, vh)
        S[h] = S0 * jnp.exp(_head(g_end, h)) + _mmf_tn(u, _head(bbar, h)) + _mmf_tn(vh, _head(kbar, h))
        y_ref[:, h * HEAD_DIM:(h + 1) * HEAD_DIM] = _rwkv_out(o, r, k2, v, g, rk_ref, lnw_ref, lnb_ref, h)

    @pl.when(c == pl.num_programs(1) - 1)
    def _():
        s_ref[...] = S[...]
        shift_ref[...] = cols[tc - 1:tc, :]


def _rwkv_prompt(cols3, lp, tc=64):
    b, t, _ = cols3.shape
    row = lambda w: pl.BlockSpec((1, w), lambda i, j: (0, 0))
    mat = lambda r, w: pl.BlockSpec((r, w), lambda i, j: (0, 0))
    return pl.pallas_call(
        functools.partial(_rwkv_kernel, tc=tc),
        grid=(b, t // tc),
        in_specs=[pl.BlockSpec((None, tc, 1024), lambda i, j: (i, j, 2)),
                  row(1024), row(256), mat(64, 256), row(256), mat(64, 256), mat(128, 256),
                  row(256), row(256), row(256), row(256), row(256)],
        out_specs=[pl.BlockSpec((None, tc, BRANCH_W), lambda i, j: (i, j, 0)),
                   pl.BlockSpec((None, 1, 1024), lambda i, j: (i, 0, 0)),
                   pl.BlockSpec((None, N_HEADS, HEAD_DIM, HEAD_DIM), lambda i, j: (i, 0, 0, 0))],
        out_shape=[jax.ShapeDtypeStruct((b, t, BRANCH_W), F32),
                   jax.ShapeDtypeStruct((b, 1, 1024), F32),
                   jax.ShapeDtypeStruct((b, N_HEADS, HEAD_DIM, HEAD_DIM), F32)],
        scratch_shapes=[pltpu.VMEM((N_HEADS, HEAD_DIM, HEAD_DIM), F32), pltpu.VMEM((1, 1024), F32)],
        compiler_params=_cparams(("parallel", "arbitrary"), 32),
        name="rwkv_prompt",
    )(cols3, lp['rw_mu'], lp['rw_w0'], lp['rw_w2'], lp['rw_a0'], lp['rw_a2'], lp['rw_g2'],
      lp['rw_kk'], lp['rw_ka'], lp['rw_rk'], lp['rw_lnx_w'], lp['rw_lnx_b'])


def _layer_norm(z, w, b):
    mu = jnp.mean(z, axis=-1, keepdims=True)
    d = z - mu
    var = jnp.mean(d * d, axis=-1, keepdims=True)
    return d * lax.rsqrt(var + LN_EPS) * w + b


def _mix_kernel(x_ref, oa_ref, ob_ref, oc_ref, od_ref, wg_ref, wb_ref, wo_ref, lw_ref, lb_ref, out_ref):
    x = x_ref[...]
    xb = x.astype(BF16)
    mixed = None
    for gi, o_ref in enumerate((oa_ref, ob_ref, oc_ref, od_ref)):
        gate = _sigmoid(jnp.dot(xb, wg_ref[:, gi * D_MODEL:(gi + 1) * D_MODEL], preferred_element_type=F32))
        up = jnp.dot(o_ref[...].astype(BF16), wb_ref[gi], preferred_element_type=F32)
        mixed = gate * up if mixed is None else mixed + gate * up
    y = jnp.dot(mixed.astype(BF16), wo_ref[...], preferred_element_type=F32)
    out_ref[...] = _layer_norm(ALPHA * x + y, lw_ref[...], lb_ref[...])


def _mix(x, outs, lp, tn):
    n = x.shape[0]
    tok = lambda w: pl.BlockSpec((tn, w), lambda i: (i, 0))
    const = lambda *s: pl.BlockSpec(s, lambda i: (0,) * len(s))
    return pl.pallas_call(
        _mix_kernel,
        grid=(n // tn,),
        in_specs=[tok(D_MODEL), tok(BRANCH_W), tok(BRANCH_W), tok(BRANCH_W), tok(BRANCH_W),
                  const(D_MODEL, 4 * D_MODEL), const(4, BRANCH_W, D_MODEL), const(D_MODEL, D_MODEL),
                  const(1, D_MODEL), const(1, D_MODEL)],
        out_specs=tok(D_MODEL),
        out_shape=jax.ShapeDtypeStruct((n, D_MODEL), F32),
        compiler_params=_cparams(("parallel",), 48),
        name="mix_ln1",
    )(x, *outs, lp['w_gate'], lp['w_branch'], lp['w_out'], lp['ln1_w'], lp['ln1_b'])


def _ple_kernel(x_ref, p_ref, wg_ref, bg_ref, wp_ref, out_ref):
    x = x_ref[...]
    gate = _sigmoid(jnp.dot(x.astype(BF16), wg_ref[...], preferred_element_type=F32) + bg_ref[...])
    emb = jnp.dot(p_ref[...].astype(BF16), wp_ref[...], preferred_element_type=F32)
    out_ref[...] = x + gate * emb


def _ple(x, p_emb, lp, tn):
    n = x.shape[0]
    tok = lambda w: pl.BlockSpec((tn, w), lambda i: (i, 0))
    const = lambda *s: pl.BlockSpec(s, lambda i: (0,) * len(s))
    return pl.pallas_call(
        _ple_kernel,
        grid=(n // tn,),
        in_specs=[tok(D_MODEL), tok(256), const(D_MODEL, D_MODEL), const(1, D_MODEL), const(256, D_MODEL)],
        out_specs=tok(D_MODEL),
        out_shape=jax.ShapeDtypeStruct((n, D_MODEL), F32),
        compiler_params=_cparams(("parallel",), 32),
        name="ple_gate",
    )(x, p_emb, lp['ple_gate_w'], lp['ple_gate_b'], lp['ple_w'])


PEER_HEADS = 8
PEER_NKEYS = 128
PEER_TOPK = 16
PEER_EB = 1024


def _oddeven_merge_sort_pairs(n):
    pairs = []
    p = 1
    while p < n:
        k = p
        while k >= 1:
            for j in range(k % p, n - k, 2 * k):
                for i in range(min(k, n - j - k)):
                    if (i + j) // (p * 2) == (i + j + k) // (p * 2):
                        pairs.append((i + j, i + j + k))
            k //= 2
        p *= 2
    return pairs


def _bitonic_merge_pairs(n):
    pairs = []
    k = n // 2
    while k >= 1:
        pairs.extend((i, i + k) for i in range(n) if (i & k) == 0)
        k //= 2
    return pairs


_SORT16 = _oddeven_merge_sort_pairs(PEER_TOPK)
_MERGE16 = _bitonic_merge_pairs(PEER_TOPK)
_CAND_LEN = tuple(PEER_TOPK // (a + 1) for a in range(PEER_TOPK))


def _network(vals, pairs):
    vals = list(vals)
    for i, j in pairs:
        hi = jnp.maximum(vals[i], vals[j])
        lo = jnp.minimum(vals[i], vals[j])
        vals[i], vals[j] = hi, lo
    return vals


def _top16_merge(x, y):
    return _network([jnp.maximum(x[i], y[PEER_TOPK - 1 - i]) for i in range(PEER_TOPK)], _MERGE16)


def _peer_head_stats(h, s_nat, e_nat, sk, th, g_count):
    rows0 = pl.ds(pl.multiple_of(2 * h * PEER_NKEYS, PEER_NKEYS), PEER_NKEYS)
    rows1 = pl.ds(pl.multiple_of((2 * h + 1) * PEER_NKEYS, PEER_NKEYS), PEER_NKEYS)
    for p, rows in enumerate((rows0, rows1)):
        for g in range(4):
            gg = g % g_count
            sk[:, p * 4 + g, :] = s_nat[rows, gg * 128:(gg + 1) * 128]
    groups = []
    for m in range(PEER_NKEYS // PEER_TOPK):
        groups.append(_network([sk[PEER_TOPK * m + i] for i in range(PEER_TOPK)], _SORT16))
    while len(groups) > 1:
        groups = [_top16_merge(groups[i], groups[i + 1]) for i in range(0, len(groups), 2)]
    top = groups[0]
    low = lax.broadcasted_iota(jnp.int32, (8, 128), 0) < 4
    ta = [jnp.where(low, t, pltpu.roll(t, 4, 0)) for t in top]
    tb = [jnp.where(low, pltpu.roll(t, 4, 0), t) for t in top]
    cand = [[ta[a] + tb[b] for b in range(_CAND_LEN[a])] for a in range(PEER_TOPK)]
    m1 = _network(cand[1] + [cand[a][0] for a in range(PEER_TOPK - 1, 7, -1)], _MERGE16)
    m2 = _network(cand[2] + cand[3] + cand[4] + cand[5] + cand[6], _SORT16)
    t1 = _top16_merge(cand[0], m1)
    t2 = _top16_merge(t1, m2)
    t2[15] = jnp.maximum(t2[15], cand[7][0])
    t2[14] = jnp.maximum(t2[14], cand[7][1])
    theta = t2[0]
    for t in t2[1:]:
        theta = jnp.minimum(theta, t)
    cmax = cand[0][0]
    z = jnp.zeros((8, 128), F32)
    for row in cand:
        for cv in row:
            z = z + jnp.where(cv >= theta, jnp.exp(cv - cmax), 0.0)
    inv_z = 1.0 / z
    for g in range(g_count):
        lanes = slice(g * 128, (g + 1) * 128)
        th[h, 0:1, lanes] = theta[g:g + 1, :]
        th[h, 1:2, lanes] = ta[0][g:g + 1, :]
        th[h, 2:3, lanes] = tb[0][g:g + 1, :]
        th[h, 3:4, lanes] = inv_z[g:g + 1, :]
    e_nat[rows0, :] = jnp.exp(s_nat[rows0, :] - th[h, 1:2, :])
    e_nat[rows1, :] = jnp.exp(s_nat[rows1, :] - th[h, 2:3, :]) * th[h, 3:4, :]


def _peer_kernel(x_ref, wq_ref, keys_ref, u_ref, vt_ref, lw_ref, lb_ref, out_ref,
                 xtb, s_nat, e_nat, sk, th, wact, yt, *, tn):
    j = pl.program_id(1)
    g_count = tn // 128

    @pl.when(j == 0)
    def _():
        xtb[...] = x_ref[...].T.astype(BF16)
        for hp in range(2 * PEER_HEADS):
            qt = jnp.dot(wq_ref[hp * 128:(hp + 1) * 128, :], xtb[...], preferred_element_type=F32)
            s_nat[hp * PEER_NKEYS:(hp + 1) * PEER_NKEYS, :] = jnp.dot(keys_ref[hp], qt.astype(BF16),
                                                                      preferred_element_type=F32)

        def head_body(h, carry):
            _peer_head_stats(h, s_nat, e_nat, sk, th, g_count)
            return carry

        lax.fori_loop(0, PEER_HEADS, head_body, 0)
        yt[...] = jnp.zeros(yt.shape, F32)

    act = jnp.dot(u_ref[...], xtb[...], preferred_element_type=F32)
    i0 = pl.multiple_of(j * (PEER_EB // PEER_NKEYS), 8)
    for ii in range(PEER_EB // PEER_NKEYS):
        rows = slice(ii * PEER_NKEYS, (ii + 1) * PEER_NKEYS)
        for g in range(g_count):
            lanes = slice(g * 128, (g + 1) * 128)
            acc = jnp.zeros((PEER_NKEYS, 128), F32)
            for h in range(PEER_HEADS):
                half1 = slice((2 * h + 1) * PEER_NKEYS, (2 * h + 2) * PEER_NKEYS)
                s0 = s_nat[pl.ds(2 * h * PEER_NKEYS + i0, 8), lanes][ii:ii + 1, :]
                e0 = e_nat[pl.ds(2 * h * PEER_NKEYS + i0, 8), lanes][ii:ii + 1, :]
                sel = (s0 + s_nat[half1, lanes]) >= th[h, 0:1, lanes]
                acc = acc + jnp.where(sel, e0 * e_nat[half1, lanes], 0.0)
            wact[rows, lanes] = (acc * _gelu(act[rows, lanes])).astype(BF16)
    yt[...] += jnp.dot(vt_ref[...], wact[...], preferred_element_type=F32)

    @pl.when(j == pl.num_programs(1) - 1)
    def _():
        z = ALPHA * x_ref[...] + yt[...].T
        out_ref[...] = _layer_norm(z, lw_ref[...], lb_ref[...])


def _peer(x, lp, tn):
    n = x.shape[0]
    n_exp = lp['peer_u'].shape[0]
    const = lambda *s: pl.BlockSpec(s, lambda i, j: (0,) * len(s))
    return pl.pallas_call(
        functools.partial(_peer_kernel, tn=tn),
        grid=(n // tn, n_exp // PEER_EB),
        in_specs=[pl.BlockSpec((tn, D_MODEL), lambda i, j: (i, 0)),
                  const(2 * PEER_HEADS * 128, D_MODEL), const(2 * PEER_HEADS, PEER_NKEYS, 128),
                  pl.BlockSpec((PEER_EB, D_MODEL), lambda i, j: (j, 0)),
                  pl.BlockSpec((D_MODEL, PEER_EB), lambda i, j: (0, j)),
                  const(1, D_MODEL), const(1, D_MODEL)],
        out_specs=pl.BlockSpec((tn, D_MODEL), lambda i, j: (i, 0)),
        out_shape=jax.ShapeDtypeStruct((n, D_MODEL), F32),
        scratch_shapes=[pltpu.VMEM((D_MODEL, tn), BF16),
                        pltpu.VMEM((2 * PEER_HEADS * PEER_NKEYS, tn), F32),
                        pltpu.VMEM((2 * PEER_HEADS * PEER_NKEYS, tn), F32),
                        pltpu.VMEM((PEER_NKEYS, 8, 128), F32),
                        pltpu.VMEM((PEER_HEADS, 8, tn), F32),
                        pltpu.VMEM((PEER_EB, tn), BF16),
                        pltpu.VMEM((D_MODEL, tn), F32)],
        compiler_params=_cparams(("parallel", "arbitrary"), 56),
        name="peer_ln2",
    )(x, lp['peer_wq_t'], lp['peer_keys'], lp['peer_u'], lp['peer_v_t'], lp['ln2_w'], lp['ln2_b'])


DEC_N = 128
HEAD_STATE = HEAD_DIM * HEAD_DIM


def _col(v):
    return jnp.broadcast_to(v.reshape(-1, 1), (v.size, DEC_N))


def _load_state_t(s_ref, st):
    st[...] = s_ref[...].T.reshape(HEAD_DIM, HEAD_DIM, DEC_N)


def _store_state_t(so_ref, st):
    so_ref[...] = st[...].reshape(HEAD_STATE, DEC_N).T


def _ret_dec_kernel(c_ref, cos_ref, sin_ref, gw_ref, gb_ref, s_ref, o_ref, so_ref, ct, qk, st):
    h = pl.program_id(0)
    ct[...] = c_ref[...].T
    r0 = pl.multiple_of(h * HEAD_DIM, HEAD_DIM)

    def rot(x):
        partner = jnp.concatenate([x[HEAD_DIM // 2:], x[:HEAD_DIM // 2]], axis=0)
        return x * cos_ref[...] + partner * sin_ref[...]

    qk[0] = rot(ct[pl.ds(r0, HEAD_DIM), :])
    qk[1] = rot(ct[pl.ds(256 + r0, HEAD_DIM), :]) * (HEAD_DIM ** -0.5)
    v = ct[pl.ds(512 + r0, HEAD_DIM), :]
    g = ct[pl.ds(768 + r0, HEAD_DIM), :]
    gamma = jnp.exp(jnp.zeros((1, 1), F32) + jnp.where(
        h == 0, RET_LOG_GAMMA[0], jnp.where(h == 1, RET_LOG_GAMMA[1],
                                            jnp.where(h == 2, RET_LOG_GAMMA[2], RET_LOG_GAMMA[3]))))
    _load_state_t(s_ref, st)

    def body(k, o):
        s_new = gamma * st[k] + qk[1, pl.ds(k, 1), :] * v
        st[k] = s_new
        return o + qk[0, pl.ds(k, 1), :] * s_new

    o = lax.fori_loop(0, HEAD_DIM, body, jnp.zeros((HEAD_DIM, DEC_N), F32))
    mu = jnp.mean(o, axis=0, keepdims=True)
    d = o - mu
    var = jnp.mean(d * d, axis=0, keepdims=True)
    o_ref[...] = (d * lax.rsqrt(var + LN_EPS) * gw_ref[...] + gb_ref[...]) * _silu(g)
    _store_state_t(so_ref, st)


def _hgrn_dec_kernel(c_ref, lb_ref, nw_ref, s_ref, o_ref, so_ref, ct, qk, st):
    h = pl.program_id(0)
    ct[...] = c_ref[...].T
    r0 = pl.multiple_of(h * HEAD_DIM, HEAD_DIM)
    lb = lb_ref[...]
    qk[0] = _silu(ct[pl.ds(r0, HEAD_DIM), :])
    forget = lb + (1.0 - lb) * _sigmoid(ct[pl.ds(256 + r0, HEAD_DIM), :])
    qk[1] = forget
    qk[2] = 1.0 - forget
    v = ct[pl.ds(512 + r0, HEAD_DIM), :]
    g = ct[pl.ds(768 + r0, HEAD_DIM), :]
    _load_state_t(s_ref, st)

    def body(k, o):
        s_new = qk[1, pl.ds(k, 1), :] * st[k] + qk[2, pl.ds(k, 1), :] * v
        st[k] = s_new
        return o + qk[0, pl.ds(k, 1), :] * s_new

    o = lax.fori_loop(0, HEAD_DIM, body, jnp.zeros((HEAD_DIM, DEC_N), F32))
    ms = jnp.mean(o * o, axis=0, keepdims=True)
    o_ref[...] = o * lax.rsqrt(ms + LN_EPS) * nw_ref[...] * _silu(g)
    _store_state_t(so_ref, st)


def _rwkv_dec_kernel(c_ref, sh_ref, mu_ref, w0_ref, w2_ref, a0_ref, a2_ref, g2_ref, kkw_ref, ka_ref,
                     rk_ref, lnw_ref, lnb_ref, s_ref, o_ref, so_ref, vt, st, osc):
    h = pl.program_id(0)
    r, lw, k2, v, kkn, a, g = _rwkv_token_mix(c_ref[...], sh_ref[...], mu_ref, w0_ref, w2_ref, a0_ref, a2_ref,
                                              g2_ref, kkw_ref, ka_ref)
    for idx, arr in enumerate((r, jnp.exp(lw), k2, v, kkn, a, g)):
        vt[idx] = arr.T
    r0 = pl.multiple_of(h * HEAD_DIM, HEAD_DIM)
    rows = pl.ds(r0, HEAD_DIM)
    rh, wh, kh, kkh, ah = vt[0, rows, :], vt[1, rows, :], vt[2, rows, :], vt[4, rows, :], vt[5, rows, :]
    vh, gh = vt[3, rows, :], vt[6, rows, :]
    kka = kkh * ah
    _load_state_t(s_ref, st)

    def body(vi, carry):
        s_old = st[vi]
        sa = jnp.sum(s_old * (-kkh), axis=0, keepdims=True)
        s_new = s_old * wh + sa * kka + vt[3, pl.ds(r0 + vi, 1), :] * kh
        st[vi] = s_new
        osc[pl.ds(vi, 1), :] = jnp.sum(s_new * rh, axis=0, keepdims=True)
        return carry

    lax.fori_loop(0, HEAD_DIM, body, 0)
    o = osc[...]
    mu = jnp.mean(o, axis=0, keepdims=True)
    d = o - mu
    var = jnp.mean(d * d, axis=0, keepdims=True)
    on = d * lax.rsqrt(var + RW_GN_EPS) * lnw_ref[...] + lnb_ref[...]
    bonus = jnp.sum(rh * kh * rk_ref[...], axis=0, keepdims=True) * vh
    o_ref[...] = (on + bonus) * gh
    _store_state_t(so_ref, st)


def _dec_specs():
    head_tab = pl.BlockSpec((HEAD_DIM, DEC_N), lambda h: (h, 0))
    state = pl.BlockSpec((DEC_N, HEAD_STATE), lambda h: (0, h))
    out = pl.BlockSpec((None, HEAD_DIM, DEC_N), lambda h: (h, 0, 0))
    return head_tab, state, out


def _dec_out_shapes():
    return [jax.ShapeDtypeStruct((N_HEADS, HEAD_DIM, DEC_N), F32),
            jax.ShapeDtypeStruct((DEC_N, N_HEADS * HEAD_STATE), F32)]


def _dec_finish(o_t, s_new):
    return o_t.reshape(BRANCH_W, DEC_N).T, s_new.reshape(DEC_N, N_HEADS, HEAD_DIM, HEAD_DIM)


def _ret_decode(cols, state, cos_c, sin_c, lp):
    head_tab, st_spec, out_spec = _dec_specs()
    same = pl.BlockSpec((HEAD_DIM, DEC_N), lambda h: (0, 0))
    o_t, s_new = pl.pallas_call(
        _ret_dec_kernel,
        grid=(N_HEADS,),
        in_specs=[pl.BlockSpec((DEC_N, 1024), lambda h: (0, 0)), same, same, head_tab, head_tab, st_spec],
        out_specs=[out_spec, st_spec],
        out_shape=_dec_out_shapes(),
        scratch_shapes=[pltpu.VMEM((1024, DEC_N), F32), pltpu.VMEM((2, HEAD_DIM, DEC_N), F32),
                        pltpu.VMEM((HEAD_DIM, HEAD_DIM, DEC_N), F32)],
        compiler_params=_cparams(("arbitrary",), 40),
        name="ret_decode",
    )(cols, cos_c, sin_c, _col(lp['ret_gn_w']), _col(lp['ret_gn_b']), state.reshape(DEC_N, -1))
    return _dec_finish(o_t, s_new)


def _hgrn_decode(cols, state, lp):
    head_tab, st_spec, out_spec = _dec_specs()
    o_t, s_new = pl.pallas_call(
        _hgrn_dec_kernel,
        grid=(N_HEADS,),
        in_specs=[pl.BlockSpec((DEC_N, 1024), lambda h: (0, 1)), head_tab, head_tab, st_spec],
        out_specs=[out_spec, st_spec],
        out_shape=_dec_out_shapes(),
        scratch_shapes=[pltpu.VMEM((1024, DEC_N), F32), pltpu.VMEM((3, HEAD_DIM, DEC_N), F32),
                        pltpu.VMEM((HEAD_DIM, HEAD_DIM, DEC_N), F32)],
        compiler_params=_cparams(("arbitrary",), 40),
        name="hgrn_decode",
    )(cols, _col(lp['hg_lb']), _col(lp['hg_norm_w']), state.reshape(DEC_N, -1))
    return _dec_finish(o_t, s_new)


def _rwkv_decode(cols, shift, state, lp):
    head_tab, st_spec, out_spec = _dec_specs()
    row = lambda w: pl.BlockSpec((1, w), lambda h: (0, 0))
    mat = lambda r, w: pl.BlockSpec((r, w), lambda h: (0, 0))
    o_t, s_new = pl.pallas_call(
        _rwkv_dec_kernel,
        grid=(N_HEADS,),
        in_specs=[pl.BlockSpec((DEC_N, 1024), lambda h: (0, 2)), mat(DEC_N, 1024),
                  row(1024), row(256), mat(64, 256), row(256), mat(64, 256), mat(128, 256),
                  row(256), row(256), head_tab, head_tab, head_tab, st_spec],
        out_specs=[out_spec, st_spec],
        out_shape=_dec_out_shapes(),
        scratch_shapes=[pltpu.VMEM((7, BRANCH_W, DEC_N), F32), pltpu.VMEM((HEAD_DIM, HEAD_DIM, DEC_N), F32),
                        pltpu.VMEM((HEAD_DIM, DEC_N), F32)],
        compiler_params=_cparams(("arbitrary",), 40),
        name="rwkv_decode",
    )(cols, shift, lp['rw_mu'], lp['rw_w0'], lp['rw_w2'], lp['rw_a0'], lp['rw_a2'], lp['rw_g2'],
      lp['rw_kk'], lp['rw_ka'], _col(lp['rw_rk']), _col(lp['rw_lnx_w']), _col(lp['rw_lnx_b']),
      state.reshape(DEC_N, -1))
    return _dec_finish(o_t, s_new)


def _lru_dec_kernel(c_ref, conv_ref, h0_ref, cw_ref, cb_ref, wa_ref, ba_ref, wx_ref, bx_ref, lam_ref,
                    y_ref, h_ref, nconv_ref):
    xb = c_ref[:, 0:BRANCH_W]
    gate = c_ref[:, BRANCH_W:2 * BRANCH_W]
    c0, c1, c2 = conv_ref[0], conv_ref[1], conv_ref[2]
    xc = (c0 * cw_ref[0:1, :] + c1 * cw_ref[1:2, :] + c2 * cw_ref[2:3, :] + xb * cw_ref[3:4, :]) + cb_ref[...]
    a, u = _lru_gates(xc, wa_ref, ba_ref, wx_ref, bx_ref, lam_ref)
    hn = a * h0_ref[...] + u
    h_ref[...] = hn
    y_ref[...] = hn * _gelu(gate)
    nconv_ref[0] = c1
    nconv_ref[1] = c2
    nconv_ref[2] = xb


def _lru_decode(cols, conv, h0, lp):
    full = lambda *s: pl.BlockSpec(s, lambda i: (0,) * len(s))
    return pl.pallas_call(
        _lru_dec_kernel,
        grid=(1,),
        in_specs=[pl.BlockSpec((DEC_N, 512), lambda i: (0, 6)), full(3, DEC_N, BRANCH_W), full(DEC_N, BRANCH_W),
                  full(CONV_W, BRANCH_W), full(1, BRANCH_W), full(BRANCH_W, BRANCH_W), full(1, BRANCH_W),
                  full(BRANCH_W, BRANCH_W), full(1, BRANCH_W), full(1, BRANCH_W)],
        out_specs=[full(DEC_N, BRANCH_W), full(DEC_N, BRANCH_W), full(3, DEC_N, BRANCH_W)],
        out_shape=[jax.ShapeDtypeStruct((DEC_N, BRANCH_W), F32), jax.ShapeDtypeStruct((DEC_N, BRANCH_W), F32),
                   jax.ShapeDtypeStruct((3, DEC_N, BRANCH_W), F32)],
        compiler_params=_cparams(("arbitrary",), 32),
        name="lru_decode",
    )(cols, conv, h0, lp['lru_conv_w'], lp['lru_conv_b'], lp['lru_wa_bd'], lp['lru_ba'], lp['lru_wx_bd'],
      lp['lru_bx'], lp['lru_lambda'])


def _prompt_layer(x, p_emb, lp, cos_t, sin_t):
    b, t, d = x.shape
    n = b * t
    xf = x.reshape(n, d)
    cols3 = _proj(xf, lp['w_in_br'], 512).reshape(b, t, BRANCH_COLS)
    o_a, s_ret = _ret_prompt(cols3, cos_t, sin_t, lp)
    o_b, s_hg = _hgrn_prompt(cols3, lp['hg_lb'], lp)
    o_c, s_shift, s_rw = _rwkv_prompt(cols3, lp)
    o_d, s_lru, s_conv = _lru_prompt(cols3, lp)
    outs = [o.reshape(n, BRANCH_W) for o in (o_a, o_b, o_c, o_d)]
    x1 = _mix(xf, outs, lp, 256)
    x2 = _peer(x1, lp, 512)
    x3 = _ple(x2, p_emb.reshape(n, -1), lp, 512)
    return x3.reshape(b, t, d), (s_ret, s_hg, s_rw, s_shift[:, 0], s_lru[:, 0], s_conv)


def _sample_layer(x, p_emb, state, lp, cos_c, sin_c):
    s_ret, s_hg, s_rw, s_shift, s_lru, s_conv = state
    xf = x.reshape(DEC_N, D_MODEL)
    cols = _proj(xf, lp['w_in_br'], DEC_N)
    o_a, s_ret = _ret_decode(cols, s_ret, cos_c, sin_c, lp)
    o_b, s_hg = _hgrn_decode(cols, s_hg, lp)
    o_c, s_rw = _rwkv_decode(cols, s_shift, s_rw, lp)
    o_d, s_lru, s_conv_t = _lru_decode(cols, jnp.swapaxes(s_conv, 0, 1), s_lru, lp)
    x1 = _mix(xf, [o_a, o_b, o_c, o_d], lp, DEC_N)
    x2 = _peer(x1, lp, DEC_N)
    x3 = _ple(x2, p_emb.reshape(DEC_N, -1), lp, DEC_N)
    new_shift = cols[:, 2048:3072]
    return x3.reshape(x.shape), (s_ret, s_hg, s_rw, new_shift, s_lru, jnp.swapaxes(s_conv_t, 0, 1))


def kernel(x_prompt, x_sample, state_ret, state_hgrn, state_rwkv, state_shift, state_lru, state_conv, p_prompt, p_sample, w_in, ret_gn_w, ret_gn_b, hg_lb, hg_norm_w, rw_mu, rw_w0, rw_w2, rw_a0, rw_a2, rw_g2, rw_kk, rw_ka, rw_rk, rw_lnx_w, rw_lnx_b, lru_conv_w, lru_conv_b, lru_wa, lru_ba, lru_wx, lru_bx, lru_lambda, w_branch, w_out, ln1_w, ln1_b, peer_wq, peer_keys, peer_u, peer_v, ln2_w, ln2_b, ple_w, ple_gate_w, ple_gate_b):
    params = dict(w_in=w_in, ret_gn_w=ret_gn_w, ret_gn_b=ret_gn_b, hg_norm_w=hg_norm_w, rw_mu=rw_mu, rw_w0=rw_w0,
                  rw_w2=rw_w2, rw_a0=rw_a0, rw_a2=rw_a2, rw_g2=rw_g2, rw_kk=rw_kk, rw_ka=rw_ka, rw_rk=rw_rk,
                  rw_lnx_w=rw_lnx_w, rw_lnx_b=rw_lnx_b, lru_conv_w=lru_conv_w, lru_conv_b=lru_conv_b,
                  lru_wa=lru_wa, lru_ba=lru_ba, lru_wx=lru_wx, lru_bx=lru_bx, lru_lambda=lru_lambda,
                  w_branch=w_branch, w_out=w_out, ln1_w=ln1_w, ln1_b=ln1_b, peer_wq=peer_wq, peer_keys=peer_keys,
                  peer_u=peer_u, peer_v=peer_v, ln2_w=ln2_w, ln2_b=ln2_b, ple_w=ple_w, ple_gate_w=ple_gate_w,
                  ple_gate_b=ple_gate_b)
    lb_cum = jnp.cumsum(jax.nn.softmax(hg_lb.astype(F32), axis=0), axis=0)
    lb_all = lb_cum - lb_cum[0:1]
    t_prompt = x_prompt.shape[1]
    past_len = 16384
    cos_t, sin_t = _rope_tables(jnp.arange(t_prompt))
    cos_s, sin_s = _rope_tables(past_len + jnp.arange(1))
    cos_c = _col(cos_s[0, :HEAD_DIM])
    sin_c = _col(sin_s[0, :HEAD_DIM])
    h_p, h_s = x_prompt, x_sample
    new_p, new_s = [], []
    for l in range(DEPTH):
        lp = _layer_params(params, l, lb_all)
        h_p, st_p = _prompt_layer(h_p, p_prompt[l], lp, cos_t, sin_t)
        st_in = (state_ret[l], state_hgrn[l], state_rwkv[l], state_shift[l], state_lru[l], state_conv[l])
        h_s, st_s = _sample_layer(h_s, p_sample[l], st_in, lp, cos_c, sin_c)
        new_p.append(st_p)
        new_s.append(st_s)
    outs_p = [jnp.stack(z) for z in zip(*new_p)]
    outs_s = [jnp.stack(z) for z in zip(*new_s)]
    return (h_p, h_s, *outs_p, *outs_s)
```

```python
import functools
import math

import jax
import jax.numpy as jnp
from jax import lax
from jax.experimental import pallas as pl
from jax.experimental.pallas import tpu as pltpu

F32 = jnp.float32
BF16 = jnp.bfloat16
HIGHEST = lax.Precision.HIGHEST

D_MODEL = 1024
BRANCH_W = 256
N_HEADS = 4
HEAD_DIM = 64
RET_LOG_GAMMA = tuple(math.log1p(-(2.0 ** (-5.0 - h))) for h in range(N_HEADS))
ROPE_BASE = 10000.0
RW_GN_EPS = 64e-5
LN_EPS = 1e-5
LRU_C = 8.0
CONV_W = 4
DEPTH = 2
ALPHA = (2 * DEPTH) ** 0.25
BRANCH_COLS = 3584
MIB = 1024 * 1024


def _cparams(semantics, vmem_mib):
    return pltpu.CompilerParams(dimension_semantics=semantics, vmem_limit_bytes=vmem_mib * MIB)


def _mm(a, b):
    return jnp.dot(a.astype(BF16), b.astype(BF16), preferred_element_type=F32)


def _mm_nt(a, b):
    return lax.dot_general(a.astype(BF16), b.astype(BF16), (((1,), (1,)), ((), ())), preferred_element_type=F32)


def _mm_tn(a, b):
    return lax.dot_general(a.astype(BF16), b.astype(BF16), (((0,), (0,)), ((), ())), preferred_element_type=F32)


def _transpose_exact(a):
    n = a.shape[0]
    eye = (lax.broadcasted_iota(jnp.int32, (n, n), 0) == lax.broadcasted_iota(jnp.int32, (n, n), 1)).astype(F32)
    return lax.dot_general(a, eye, (((0,), (0,)), ((), ())), preferred_element_type=F32, precision=HIGHEST)


def _select_rows_exact(sel, x):
    x1 = x.astype(BF16)
    r1 = x - x1.astype(F32)
    x2 = r1.astype(BF16)
    x3 = (r1 - x2.astype(F32)).astype(BF16)
    dot = lambda p: jnp.dot(sel, p, preferred_element_type=F32)
    return (dot(x1) + dot(x2)) + dot(x3)


def _tril_ones(n):
    r = lax.broadcasted_iota(jnp.int32, (n, n), 0)
    c = lax.broadcasted_iota(jnp.int32, (n, n), 1)
    return (r >= c).astype(BF16)


def _sigmoid(x):
    return 1.0 / (1.0 + jnp.exp(-x))


def _silu(x):
    return x * _sigmoid(x)


def _gelu(x):
    return 0.5 * x * (1.0 + jnp.tanh(0.7978845608028654 * (x + 0.044715 * (x * x * x))))


def _softplus(x):
    return jnp.maximum(x, 0.0) + jnp.log1p(jnp.exp(-jnp.abs(x)))


def _head(x, h):
    return x[:, h * HEAD_DIM:(h + 1) * HEAD_DIM]


def _block_diag(w):
    n, c, d = w.shape
    eye = jnp.eye(n, dtype=w.dtype)
    return (eye[:, None, :, None] * w[:, :, None, :]).reshape(n * c, n * d)


def _rope_tables(pos):
    half = HEAD_DIM // 2
    inv_freq = ROPE_BASE ** (-jnp.arange(half, dtype=F32) / half)
    ang = pos.astype(F32)[:, None] * inv_freq[None, :]
    cos = jnp.cos(ang)
    sin = jnp.sin(ang)
    cos_t = jnp.tile(jnp.concatenate([cos, cos], axis=-1), (1, N_HEADS))
    sin_t = jnp.tile(jnp.concatenate([-sin, sin], axis=-1), (1, N_HEADS))
    return cos_t, sin_t


def _layer_params(p, l, lb_all):
    r2 = lambda a: a.reshape(1, -1)
    w_in = p['w_in'][l]
    lp = {
        'w_in_br': w_in[:, :BRANCH_COLS].astype(BF16),
        'w_gate': w_in[:, BRANCH_COLS:].astype(BF16),
        'ret_gn_w': r2(p['ret_gn_w'][l]), 'ret_gn_b': r2(p['ret_gn_b'][l]),
        'hg_lb': r2(lb_all[l]), 'hg_norm_w': r2(p['hg_norm_w'][l]),
        'rw_mu': r2(p['rw_mu'][l]), 'rw_w0': r2(p['rw_w0'][l]), 'rw_w2': p['rw_w2'][l].astype(BF16),
        'rw_a0': r2(p['rw_a0'][l]), 'rw_a2': p['rw_a2'][l].astype(BF16), 'rw_g2': p['rw_g2'][l].astype(BF16),
        'rw_kk': r2(p['rw_kk'][l]), 'rw_ka': r2(p['rw_ka'][l]), 'rw_rk': r2(p['rw_rk'][l]),
        'rw_lnx_w': r2(p['rw_lnx_w'][l]), 'rw_lnx_b': r2(p['rw_lnx_b'][l]),
        'lru_conv_w': p['lru_conv_w'][l], 'lru_conv_b': r2(p['lru_conv_b'][l]),
        'lru_wa_bd': _block_diag(p['lru_wa'][l]).astype(BF16), 'lru_ba': r2(p['lru_ba'][l]),
        'lru_wx_bd': _block_diag(p['lru_wx'][l]).astype(BF16), 'lru_bx': r2(p['lru_bx'][l]),
        'lru_lambda': r2(p['lru_lambda'][l]),
        'w_branch': p['w_branch'][l].astype(BF16), 'w_out': p['w_out'][l].astype(BF16),
        'ln1_w': r2(p['ln1_w'][l]), 'ln1_b': r2(p['ln1_b'][l]),
        'peer_wq_t': p['peer_wq'][l].T.astype(BF16),
        'peer_keys': p['peer_keys'][l].reshape(16, 128, 128).astype(BF16),
        'peer_u': p['peer_u'][l].astype(BF16), 'peer_v_t': p['peer_v'][l].T.astype(BF16),
        'ln2_w': r2(p['ln2_w'][l]), 'ln2_b': r2(p['ln2_b'][l]),
        'ple_w': p['ple_w'][l].astype(BF16), 'ple_gate_w': p['ple_gate_w'][l].astype(BF16),
        'ple_gate_b': r2(p['ple_gate_b'][l]),
    }
    return lp


def _proj_kernel(x_ref, w_ref, o_ref):
    o_ref[...] = jnp.dot(x_ref[...].astype(BF16), w_ref[...], preferred_element_type=F32)


def _proj(x, w_bf16, tn):
    n, k = x.shape
    m = w_bf16.shape[1]
    return pl.pallas_call(
        _proj_kernel,
        grid=(n // tn,),
        in_specs=[pl.BlockSpec((tn, k), lambda i: (i, 0)),
                  pl.BlockSpec((k, m), lambda i: (0, 0))],
        out_specs=pl.BlockSpec((tn, m), lambda i: (i, 0)),
        out_shape=jax.ShapeDtypeStruct((n, m), F32),
        compiler_params=_cparams(("parallel",), 48),
        name="in_proj",
    )(x, w_bf16)


def _lru_gates(xc, wa_ref, ba_ref, wx_ref, bx_ref, lam_ref):
    r = _sigmoid(_mm(xc, wa_ref[...]) + ba_ref[...])
    i = _sigmoid(_mm(xc, wx_ref[...]) + bx_ref[...])
    log_a = -LRU_C * r * _softplus(-lam_ref[...])
    a = jnp.exp(log_a)
    u = jnp.sqrt(1.0 - jnp.exp(2.0 * log_a)) * (i * xc)
    return a, u


def _lru_kernel(c_ref, cw_ref, cb_ref, wa_ref, ba_ref, wx_ref, bx_ref, lam_ref,
                y_ref, h_ref, conv_ref, xbuf, hcar, a_s, u_s, hs, *, tc):
    c = pl.program_id(1)

    @pl.when(c == 0)
    def _():
        xbuf[0:8, :] = jnp.zeros((8, BRANCH_W), F32)
        hcar[...] = jnp.zeros((1, BRANCH_W), F32)

    xb = c_ref[:, 0:BRANCH_W]
    gate = c_ref[:, BRANCH_W:2 * BRANCH_W]
    xbuf[8:8 + tc, :] = xb
    xc = (xbuf[pl.ds(5, tc), :] * cw_ref[0:1, :] + xbuf[pl.ds(6, tc), :] * cw_ref[1:2, :]
          + xbuf[pl.ds(7, tc), :] * cw_ref[2:3, :] + xb * cw_ref[3:4, :]) + cb_ref[...]
    a, u = _lru_gates(xc, wa_ref, ba_ref, wx_ref, bx_ref, lam_ref)
    a_s[...] = a
    u_s[...] = u

    def body(t, h):
        h = a_s[pl.ds(t, 1), :] * h + u_s[pl.ds(t, 1), :]
        hs[pl.ds(t, 1), :] = h
        return h

    h = lax.fori_loop(0, tc, body, hcar[...], unroll=8)
    hcar[...] = h
    y_ref[...] = hs[...] * _gelu(gate)
    xbuf[0:8, :] = xbuf[tc:tc + 8, :]

    @pl.when(c == pl.num_programs(1) - 1)
    def _():
        h_ref[...] = h
        conv_ref[...] = xbuf[5:8, :]


def _lru_prompt(cols3, lp, tc=256):
    b, t, _ = cols3.shape
    row = lambda: pl.BlockSpec((1, BRANCH_W), lambda i, j: (0, 0))
    full = lambda r: pl.BlockSpec((r, BRANCH_W), lambda i, j: (0, 0))
    return pl.pallas_call(
        functools.partial(_lru_kernel, tc=tc),
        grid=(b, t // tc),
        in_specs=[pl.BlockSpec((None, tc, 512), lambda i, j: (i, j, 6)),
                  full(CONV_W), row(), full(BRANCH_W), row(), full(BRANCH_W), row(), row()],
        out_specs=[pl.BlockSpec((None, tc, BRANCH_W), lambda i, j: (i, j, 0)),
                   pl.BlockSpec((None, 1, BRANCH_W), lambda i, j: (i, 0, 0)),
                   pl.BlockSpec((None, CONV_W - 1, BRANCH_W), lambda i, j: (i, 0, 0))],
        out_shape=[jax.ShapeDtypeStruct((b, t, BRANCH_W), F32),
                   jax.ShapeDtypeStruct((b, 1, BRANCH_W), F32),
                   jax.ShapeDtypeStruct((b, CONV_W - 1, BRANCH_W), F32)],
        scratch_shapes=[pltpu.VMEM((tc + 8, BRANCH_W), F32), pltpu.VMEM((1, BRANCH_W), F32),
                        pltpu.VMEM((tc, BRANCH_W), F32), pltpu.VMEM((tc, BRANCH_W), F32),
                        pltpu.VMEM((tc, BRANCH_W), F32)],
        compiler_params=_cparams(("parallel", "arbitrary"), 32),
        name="lru_prompt",
    )(cols3, lp['lru_conv_w'], lp['lru_conv_b'], lp['lru_wa_bd'], lp['lru_ba'], lp['lru_wx_bd'],
      lp['lru_bx'], lp['lru_lambda'])


def _rotary(x, cos, sin_signed, first_half):
    partner = jnp.where(first_half, pltpu.roll(x, BRANCH_W - 32, 1), pltpu.roll(x, 32, 1))
    return x * cos + partner * sin_signed


def _ret_kernel(c_ref, cos_ref, sin_ref, gw_ref, gb_ref, o_ref, s_ref, S, *, tc):
    c = pl.program_id(1)

    @pl.when(c == 0)
    def _():
        S[...] = jnp.zeros(S.shape, F32)

    lane = lax.broadcasted_iota(jnp.int32, (tc, BRANCH_W), 1)
    first_half = (lane % HEAD_DIM) < (HEAD_DIM // 2)
    cos = cos_ref[...]
    sin = sin_ref[...]
    q = _rotary(c_ref[:, 0:256], cos, sin, first_half)
    k = _rotary(c_ref[:, 256:512], cos, sin, first_half) * (HEAD_DIM ** -0.5)
    v = c_ref[:, 512:768]
    g = c_ref[:, 768:1024]
    row = lax.broadcasted_iota(jnp.int32, (tc, tc), 0)
    col = lax.broadcasted_iota(jnp.int32, (tc, tc), 1)
    causal = row >= col
    dist = jnp.where(causal, row - col, 0).astype(F32)
    tpos = lax.broadcasted_iota(jnp.int32, (tc, HEAD_DIM), 0).astype(F32)
    for h in range(N_HEADS):
        lg = RET_LOG_GAMMA[h]
        qh, kh, vh = _head(q, h), _head(k, h), _head(v, h)
        decay = jnp.where(causal, jnp.exp(dist * lg), 0.0)
        scores = _mm_nt(qh, kh) * decay
        oh = _mm(scores, vh) + _mm(qh * jnp.exp((tpos + 1.0) * lg), S[h])
        S[h] = math.exp(tc * lg) * S[h] + _mm_tn(kh * jnp.exp((tc - 1.0 - tpos) * lg), vh)
        mu = jnp.mean(oh, axis=-1, keepdims=True)
        d = oh - mu
        var = jnp.mean(d * d, axis=-1, keepdims=True)
        on = d * lax.rsqrt(var + LN_EPS) * _head(gw_ref[...], h) + _head(gb_ref[...], h)
        o_ref[:, h * HEAD_DIM:(h + 1) * HEAD_DIM] = on * _silu(_head(g, h))

    @pl.when(c == pl.num_programs(1) - 1)
    def _():
        s_ref[...] = S[...]


def _ret_prompt(cols3, cos_t, sin_t, lp, tc=128):
    b, t, _ = cols3.shape
    row = lambda: pl.BlockSpec((1, BRANCH_W), lambda i, j: (0, 0))
    return pl.pallas_call(
        functools.partial(_ret_kernel, tc=tc),
        grid=(b, t // tc),
        in_specs=[pl.BlockSpec((None, tc, 1024), lambda i, j: (i, j, 0)),
                  pl.BlockSpec((tc, BRANCH_W), lambda i, j: (j, 0)),
                  pl.BlockSpec((tc, BRANCH_W), lambda i, j: (j, 0)), row(), row()],
        out_specs=[pl.BlockSpec((None, tc, BRANCH_W), lambda i, j: (i, j, 0)),
                   pl.BlockSpec((None, N_HEADS, HEAD_DIM, HEAD_DIM), lambda i, j: (i, 0, 0, 0))],
        out_shape=[jax.ShapeDtypeStruct((b, t, BRANCH_W), F32),
                   jax.ShapeDtypeStruct((b, N_HEADS, HEAD_DIM, HEAD_DIM), F32)],
        scratch_shapes=[pltpu.VMEM((N_HEADS, HEAD_DIM, HEAD_DIM), F32)],
        compiler_params=_cparams(("parallel", "arbitrary"), 32),
        name="ret_prompt",
    )(cols3, cos_t, sin_t, lp['ret_gn_w'], lp['ret_gn_b'])


HGRN_CHUNK = 64
HGRN_LEVELS = tuple(HGRN_CHUNK >> l for l in range(1, 7))


def _hgrn_select_matrix():
    t = jnp.arange(HGRN_CHUNK)
    s = jnp.arange(HGRN_CHUNK)
    blocks = [s[None, :] <= t[:, None]]
    for m in HGRN_LEVELS:
        mid = (t // (2 * m)) * (2 * m) + m - 1
        blocks.append(s[None, :] <= mid[:, None])
    return jnp.concatenate(blocks, axis=0).astype(BF16)


def _hgrn_kernel(c_ref, lb_ref, nw_ref, sel_ref, o_ref, s_ref, St, *, tc, nb):
    c = pl.program_id(1)

    @pl.when(c == 0)
    def _():
        St[...] = jnp.zeros(St.shape, F32)

    lb = lb_ref[...]
    row = lax.broadcasted_iota(jnp.int32, (tc, BRANCH_W), 0)
    r64 = lax.broadcasted_iota(jnp.int32, (tc, tc), 0)
    c64 = lax.broadcasted_iota(jnp.int32, (tc, tc), 1)
    tok = []
    for n in range(nb):
        q = _silu(c_ref[n, :, 0:256])
        forget = lb + (1.0 - lb) * _sigmoid(c_ref[n, :, 256:512])
        kk = 1.0 - forget
        sums = _select_rows_exact(sel_ref[...], jnp.log(forget))
        tok.append(dict(q=q, kk=kk, sums=sums, b=sums[0:tc]))
    chains = [(n, h) for n in range(nb) for h in range(N_HEADS)]
    scores = {(n, h): jnp.where(r64 == c64, _mm_nt(_head(tok[n]['q'], h), _head(tok[n]['kk'], h)), 0.0)
              for n, h in chains}
    for lvl, m in enumerate(HGRN_LEVELS):
        sh = m.bit_length() - 1
        upper = ((row >> sh) & 1) == 1
        same = (r64 >> (sh + 1)) == (c64 >> (sh + 1))
        scaled = []
        for n in range(nb):
            t = tok[n]
            e = jnp.exp(-jnp.abs(t['b'] - t['sums'][(lvl + 1) * tc:(lvl + 2) * tc]))
            scaled.append((jnp.where(upper, t['q'] * e, 0.0), jnp.where(upper, 0.0, t['kk'] * e)))
        for n, h in chains:
            scores[n, h] = scores[n, h] + jnp.where(same, _mm_nt(_head(scaled[n][0], h), _head(scaled[n][1], h)), 0.0)
    for n in range(nb):
        t = tok[n]
        b = t['b']
        v = c_ref[n, :, 512:768]
        g = c_ref[n, :, 768:1024]
        b_end = b[tc - 1:tc, :]
        qd = t['q'] * jnp.exp(b)
        kd = t['kk'] * jnp.exp(b_end - b)
        for h in range(N_HEADS):
            vh = _head(v, h)
            oh = _mm(scores[n, h], vh) + _mm_nt(_head(qd, h), St[n, h])
            St[n, h] = St[n, h] * jnp.exp(_head(b_end, h)) + _mm_tn(vh, _head(kd, h))
            ms = jnp.mean(oh * oh, axis=-1, keepdims=True)
            on = oh * lax.rsqrt(ms + LN_EPS) * _head(nw_ref[...], h)
            o_ref[n, :, h * HEAD_DIM:(h + 1) * HEAD_DIM] = on * _silu(_head(g, h))

    @pl.when(c == pl.num_programs(1) - 1)
    def _():
        for n in range(nb):
            for h in range(N_HEADS):
                s_ref[n, h] = _transpose_exact(St[n, h])


def _hgrn_prompt(cols3, lb, lp, nb=4):
    b, t, _ = cols3.shape
    nb = min(nb, b)
    tc = HGRN_CHUNK
    row = lambda: pl.BlockSpec((1, BRANCH_W), lambda i, j: (0, 0))
    return pl.pallas_call(
        functools.partial(_hgrn_kernel, tc=tc, nb=nb),
        grid=(b // nb, t // tc),
        in_specs=[pl.BlockSpec((nb, tc, 1024), lambda i, j: (i, j, 1)), row(), row(),
                  pl.BlockSpec((7 * tc, tc), lambda i, j: (0, 0))],
        out_specs=[pl.BlockSpec((nb, tc, BRANCH_W), lambda i, j: (i, j, 0)),
                   pl.BlockSpec((nb, N_HEADS, HEAD_DIM, HEAD_DIM), lambda i, j: (i, 0, 0, 0))],
        out_shape=[jax.ShapeDtypeStruct((b, t, BRANCH_W), F32),
                   jax.ShapeDtypeStruct((b, N_HEADS, HEAD_DIM, HEAD_DIM), F32)],
        scratch_shapes=[pltpu.VMEM((nb, N_HEADS, HEAD_DIM, HEAD_DIM), F32)],
        compiler_params=_cparams(("parallel", "arbitrary"), 32),
        name="hgrn_prompt",
    )(cols3, lb, lp['hg_norm_w'], _hgrn_select_matrix())


def _rwkv_token_mix(cols, prev, mu_ref, w0_ref, w2_ref, a0_ref, a2_ref, g2_ref, kkw_ref, ka_ref):
    xs = cols + mu_ref[...] * (prev - cols)
    r = xs[:, 0:256]
    k = xs[:, 256:512]
    v = xs[:, 512:768]
    xg = xs[:, 768:896]
    xw = xs[:, 896:960]
    xa = xs[:, 960:1024]
    w = -_softplus(-(w0_ref[...] + _mm(jnp.tanh(xw), w2_ref[...]))) - 0.5
    lw = -jnp.exp(w)
    a = _sigmoid(a0_ref[...] + _mm(xa, a2_ref[...]))
    g = _mm(_sigmoid(xg), g2_ref[...])
    kk = k * kkw_ref[...]
    parts = []
    for h in range(N_HEADS):
        kh = _head(kk, h)
        nrm = jnp.sqrt(jnp.sum(kh * kh, axis=-1, keepdims=True))
        parts.append(kh / jnp.maximum(nrm, 1e-12))
    kkn = jnp.concatenate(parts, axis=-1)
    k2 = k * (1.0 + (a - 1.0) * ka_ref[...])
    return r, lw, k2, v, kkn, a, g


def _rwkv_out(o, r, k2, v, g, rk_ref, lw_ref, lb_ref, h):
    mu = jnp.mean(o, axis=-1, keepdims=True)
    d = o - mu
    var = jnp.mean(d * d, axis=-1, keepdims=True)
    on = d * lax.rsqrt(var + RW_GN_EPS) * _head(lw_ref[...], h) + _head(lb_ref[...], h)
    bonus = jnp.sum(_head(r, h) * _head(k2, h) * _head(rk_ref[...], h), axis=-1, keepdims=True) * _head(v, h)
    return (on + bonus) * _head(g, h)


def _rwkv_kernel(c_ref, mu_ref, w0_ref, w2_ref, a0_ref, a2_ref, g2_ref, kkw_ref, ka_ref, rk_ref,
                 lnw_ref, lnb_ref, y_ref, shift_ref, s_ref, S, last_row, *, tc, nb):
    c = pl.program_id(1)

    @pl.when(c == 0)
    def _():
        S[...] = jnp.zeros(S.shape, F32)
        last_row[...] = jnp.zeros(last_row.shape, F32)

    rr = lax.broadcasted_iota(jnp.int32, (tc, tc), 0)
    cc = lax.broadcasted_iota(jnp.int32, (tc, tc), 1)
    strict = rr > cc
    incl2 = (lax.broadcasted_iota(jnp.int32, (tc, 2 * tc), 0)
             >= (lax.broadcasted_iota(jnp.int32, (tc, 2 * tc), 1) & (tc - 1)))
    tril = _tril_ones(tc)
    n_double = max(1, (tc - 1).bit_length())
    tok = []
    for n in range(nb):
        cols = c_ref[n]
        rowi = lax.broadcasted_iota(jnp.int32, cols.shape, 0)
        prev = jnp.where(rowi == 0, last_row[n], pltpu.roll(cols, 1, 0))
        last_row[n] = cols[tc - 1:tc, :]
        r, lw, k2, v, kkn, a, g = _rwkv_token_mix(cols, prev, mu_ref, w0_ref, w2_ref, a0_ref, a2_ref, g2_ref,
                                                  kkw_ref, ka_ref)
        am = -kkn
        bm = kkn * a
        G = _select_rows_exact(tril, lw)
        g_end = G[tc - 1:tc, :]
        einv = jnp.exp(-G)
        eend = jnp.exp(g_end - G)
        tok.append(dict(r=r, k2=k2, v=v, g=g, g_end=g_end, at=am * jnp.exp(G - lw), rt=r * jnp.exp(G),
                        bt=bm * einv, kt=k2 * einv, bbar=bm * eend, kbar=k2 * eend))
    chains = [(n, h) for n in range(nb) for h in range(N_HEADS)]
    lhs = {(n, h): jnp.concatenate([_head(tok[n]['at'], h), _head(tok[n]['rt'], h)], axis=0) for n, h in chains}
    inter = {(n, h): _mm_nt(lhs[n, h], jnp.concatenate([_head(tok[n]['bt'], h), _head(tok[n]['kt'], h)], axis=0))
             for n, h in chains}
    from_state = {(n, h): _mm_nt(lhs[n, h], S[n, h]) for n, h in chains}
    npow = {ch: jnp.where(strict, inter[ch][0:tc, 0:tc], 0.0) for ch in chains}
    u = {(n, h): from_state[n, h][0:tc]
         + _mm(jnp.where(strict, inter[n, h][0:tc, tc:2 * tc], 0.0), _head(tok[n]['v'], h)) for n, h in chains}
    for j in range(n_double):
        u = {ch: u[ch] + _mm(npow[ch], u[ch]) for ch in chains}
        if j + 1 < n_double:
            npow = {ch: _mm(npow[ch], npow[ch]) for ch in chains}
    for n, h in chains:
        t = tok[n]
        uv = jnp.concatenate([u[n, h], _head(t['v'], h)], axis=0)
        a_r = jnp.where(incl2, inter[n, h][tc:2 * tc, :], 0.0)
        o = from_state[n, h][tc:2 * tc] + _mm(a_r, uv)
        S[n, h] = (S[n, h] * jnp.exp(_head(t['g_end'], h))
                   + _mm_tn(uv, jnp.concatenate([_head(t['bbar'], h), _head(t['kbar'], h)], axis=0)))
        y_ref[n, :, h * HEAD_DIM:(h + 1) * HEAD_DIM] = _rwkv_out(o, t['r'], t['k2'], t['v'], t['g'], rk_ref,
                                                                 lnw_ref, lnb_ref, h)

    @pl.when(c == pl.num_programs(1) - 1)
    def _():
        s_ref[...] = S[...]
        for n in range(nb):
            shift_ref[n] = c_ref[n, tc - 1:tc, :]


def _rwkv_prompt(cols3, lp, tc=64, nb=4):
    b, t, _ = cols3.shape
    nb = min(nb, b)
    row = lambda w: pl.BlockSpec((1, w), lambda i, j: (0, 0))
    mat = lambda r, w: pl.BlockSpec((r, w), lambda i, j: (0, 0))
    return pl.pallas_call(
        functools.partial(_rwkv_kernel, tc=tc, nb=nb),
        grid=(b // nb, t // tc),
        in_specs=[pl.BlockSpec((nb, tc, 1024), lambda i, j: (i, j, 2)),
                  row(1024), row(256), mat(64, 256), row(256), mat(64, 256), mat(128, 256),
                  row(256), row(256), row(256), row(256), row(256)],
        out_specs=[pl.BlockSpec((nb, tc, BRANCH_W), lambda i, j: (i, j, 0)),
                   pl.BlockSpec((nb, 1, 1024), lambda i, j: (i, 0, 0)),
                   pl.BlockSpec((nb, N_HEADS, HEAD_DIM, HEAD_DIM), lambda i, j: (i, 0, 0, 0))],
        out_shape=[jax.ShapeDtypeStruct((b, t, BRANCH_W), F32),
                   jax.ShapeDtypeStruct((b, 1, 1024), F32),
                   jax.ShapeDtypeStruct((b, N_HEADS, HEAD_DIM, HEAD_DIM), F32)],
        scratch_shapes=[pltpu.VMEM((nb, N_HEADS, HEAD_DIM, HEAD_DIM), F32), pltpu.VMEM((nb, 1, 1024), F32)],
        compiler_params=_cparams(("parallel", "arbitrary"), 32),
        name="rwkv_prompt",
    )(cols3, lp['rw_mu'], lp['rw_w0'], lp['rw_w2'], lp['rw_a0'], lp['rw_a2'], lp['rw_g2'],
      lp['rw_kk'], lp['rw_ka'], lp['rw_rk'], lp['rw_lnx_w'], lp['rw_lnx_b'])


def _layer_norm(z, w, b):
    mu = jnp.mean(z, axis=-1, keepdims=True)
    d = z - mu
    var = jnp.mean(d * d, axis=-1, keepdims=True)
    return d * lax.rsqrt(var + LN_EPS) * w + b


def _mix_kernel(x_ref, oa_ref, ob_ref, oc_ref, od_ref, wg_ref, wb_ref, wo_ref, lw_ref, lb_ref, out_ref):
    x = x_ref[...]
    xb = x.astype(BF16)
    mixed = None
    for gi, o_ref in enumerate((oa_ref, ob_ref, oc_ref, od_ref)):
        gate = _sigmoid(jnp.dot(xb, wg_ref[:, gi * D_MODEL:(gi + 1) * D_MODEL], preferred_element_type=F32))
        up = jnp.dot(o_ref[...].astype(BF16), wb_ref[gi], preferred_element_type=F32)
        mixed = gate * up if mixed is None else mixed + gate * up
    y = jnp.dot(mixed.astype(BF16), wo_ref[...], preferred_element_type=F32)
    out_ref[...] = _layer_norm(ALPHA * x + y, lw_ref[...], lb_ref[...])


def _mix(x, outs, lp, tn):
    n = x.shape[0]
    tok = lambda w: pl.BlockSpec((tn, w), lambda i: (i, 0))
    const = lambda *s: pl.BlockSpec(s, lambda i: (0,) * len(s))
    return pl.pallas_call(
        _mix_kernel,
        grid=(n // tn,),
        in_specs=[tok(D_MODEL), tok(BRANCH_W), tok(BRANCH_W), tok(BRANCH_W), tok(BRANCH_W),
                  const(D_MODEL, 4 * D_MODEL), const(4, BRANCH_W, D_MODEL), const(D_MODEL, D_MODEL),
                  const(1, D_MODEL), const(1, D_MODEL)],
        out_specs=tok(D_MODEL),
        out_shape=jax.ShapeDtypeStruct((n, D_MODEL), F32),
        compiler_params=_cparams(("parallel",), 48),
        name="mix_ln1",
    )(x, *outs, lp['w_gate'], lp['w_branch'], lp['w_out'], lp['ln1_w'], lp['ln1_b'])


def _ple_kernel(x_ref, p_ref, wg_ref, bg_ref, wp_ref, out_ref):
    x = x_ref[...]
    gate = _sigmoid(jnp.dot(x.astype(BF16), wg_ref[...], preferred_element_type=F32) + bg_ref[...])
    emb = jnp.dot(p_ref[...].astype(BF16), wp_ref[...], preferred_element_type=F32)
    out_ref[...] = x + gate * emb


def _ple(x, p_emb, lp, tn):
    n = x.shape[0]
    tok = lambda w: pl.BlockSpec((tn, w), lambda i: (i, 0))
    const = lambda *s: pl.BlockSpec(s, lambda i: (0,) * len(s))
    return pl.pallas_call(
        _ple_kernel,
        grid=(n // tn,),
        in_specs=[tok(D_MODEL), tok(256), const(D_MODEL, D_MODEL), const(1, D_MODEL), const(256, D_MODEL)],
        out_specs=tok(D_MODEL),
        out_shape=jax.ShapeDtypeStruct((n, D_MODEL), F32),
        compiler_params=_cparams(("parallel",), 32),
        name="ple_gate",
    )(x, p_emb, lp['ple_gate_w'], lp['ple_gate_b'], lp['ple_w'])


PEER_HEADS = 8
PEER_NKEYS = 128
PEER_TOPK = 16
PEER_EB = 1024


def _oddeven_merge_sort_pairs(n):
    pairs = []
    p = 1
    while p < n:
        k = p
        while k >= 1:
            for j in range(k % p, n - k, 2 * k):
                for i in range(min(k, n - j - k)):
                    if (i + j) // (p * 2) == (i + j + k) // (p * 2):
                        pairs.append((i + j, i + j + k))
            k //= 2
        p *= 2
    return pairs


def _bitonic_merge_pairs(n):
    pairs = []
    k = n // 2
    while k >= 1:
        pairs.extend((i, i + k) for i in range(n) if (i & k) == 0)
        k //= 2
    return pairs


_SORT16 = _oddeven_merge_sort_pairs(PEER_TOPK)
_MERGE16 = _bitonic_merge_pairs(PEER_TOPK)
_CAND_LEN = tuple(PEER_TOPK // (a + 1) for a in range(PEER_TOPK))


def _network(vals, pairs):
    vals = list(vals)
    for i, j in pairs:
        hi = jnp.maximum(vals[i], vals[j])
        lo = jnp.minimum(vals[i], vals[j])
        vals[i], vals[j] = hi, lo
    return vals


def _top16_merge(x, y):
    return _network([jnp.maximum(x[i], y[PEER_TOPK - 1 - i]) for i in range(PEER_TOPK)], _MERGE16)


def _peer_head_stats(h, s_nat, e_nat, sk, th, g_count):
    rows0 = pl.ds(pl.multiple_of(2 * h * PEER_NKEYS, PEER_NKEYS), PEER_NKEYS)
    rows1 = pl.ds(pl.multiple_of((2 * h + 1) * PEER_NKEYS, PEER_NKEYS), PEER_NKEYS)
    for p, rows in enumerate((rows0, rows1)):
        for g in range(4):
            gg = g % g_count
            sk[:, p * 4 + g, :] = s_nat[rows, gg * 128:(gg + 1) * 128]
    groups = []
    for m in range(PEER_NKEYS // PEER_TOPK):
        groups.append(_network([sk[PEER_TOPK * m + i] for i in range(PEER_TOPK)], _SORT16))
    while len(groups) > 1:
        groups = [_top16_merge(groups[i], groups[i + 1]) for i in range(0, len(groups), 2)]
    top = groups[0]
    low = lax.broadcasted_iota(jnp.int32, (8, 128), 0) < 4
    ta = [jnp.where(low, t, pltpu.roll(t, 4, 0)) for t in top]
    tb = [jnp.where(low, pltpu.roll(t, 4, 0), t) for t in top]
    cand = [[ta[a] + tb[b] for b in range(_CAND_LEN[a])] for a in range(PEER_TOPK)]
    m1 = _network(cand[1] + [cand[a][0] for a in range(PEER_TOPK - 1, 7, -1)], _MERGE16)
    m2 = _network(cand[2] + cand[3] + cand[4] + cand[5] + cand[6], _SORT16)
    t1 = _top16_merge(cand[0], m1)
    t2 = _top16_merge(t1, m2)
    t2[15] = jnp.maximum(t2[15], cand[7][0])
    t2[14] = jnp.maximum(t2[14], cand[7][1])
    theta = t2[0]
    for t in t2[1:]:
        theta = jnp.minimum(theta, t)
    cmax = cand[0][0]
    z = jnp.zeros((8, 128), F32)
    for row in cand:
        for cv in row:
            z = z + jnp.where(cv >= theta, jnp.exp(cv - cmax), 0.0)
    inv_z = 1.0 / z
    for g in range(g_count):
        lanes = slice(g * 128, (g + 1) * 128)
        th[h, 0:1, lanes] = theta[g:g + 1, :]
        th[h, 1:2, lanes] = ta[0][g:g + 1, :]
        th[h, 2:3, lanes] = tb[0][g:g + 1, :]
        th[h, 3:4, lanes] = inv_z[g:g + 1, :]
        for b in range(PEER_TOPK):
            th[h, 8 + b:9 + b, lanes] = tb[b][g:g + 1, :]
    s0 = s_nat[rows0, :]
    e_nat[rows0, :] = jnp.exp(s0 - th[h, 1:2, :])
    e_nat[rows1, :] = jnp.exp(s_nat[rows1, :] - th[h, 2:3, :]) * th[h, 3:4, :]
    tau = jnp.full(s0.shape, jnp.inf, F32)
    for b in range(PEER_TOPK):
        sb = th[h, 8 + b:9 + b, :]
        tau = jnp.where((s0 + sb) >= th[h, 0:1, :], sb, tau)
    s_nat[rows0, :] = tau


def _peer_kernel(x_ref, wq_ref, keys_ref, u_ref, vt_ref, lw_ref, lb_ref, out_ref,
                 xtb, s_nat, e_nat, sk, th, wact, wraw, actb, yt, *, tn):
    j = pl.program_id(1)
    g_count = tn // 128

    @pl.when(j == 0)
    def _():
        xtb[...] = x_ref[...].T.astype(BF16)
        for hp in range(2 * PEER_HEADS):
            qt = jnp.dot(wq_ref[hp * 128:(hp + 1) * 128, :], xtb[...], preferred_element_type=F32)
            s_nat[hp * PEER_NKEYS:(hp + 1) * PEER_NKEYS, :] = jnp.dot(keys_ref[hp], qt.astype(BF16),
                                                                      preferred_element_type=F32)

        def head_body(h, carry):
            _peer_head_stats(h, s_nat, e_nat, sk, th, g_count)
            return carry

        lax.fori_loop(0, PEER_HEADS, head_body, 0)
        yt[...] = jnp.zeros(yt.shape, F32)

    last = pl.num_programs(1) - 1
    cur = j % 2

    def step(first_stage, second_stage):
        if first_stage:
            actb[...] = jnp.dot(u_ref[...], xtb[...], preferred_element_type=F32)
        if second_stage:
            yt[...] += jnp.dot(vt_ref[...], wact[1 - cur], preferred_element_type=F32)
        if not first_stage:
            return
        i0 = pl.multiple_of(j * (PEER_EB // PEER_NKEYS), 8)
        jh_rows = PEER_NKEYS // 2
        for g in range(g_count):
            lanes = slice(g * 128, (g + 1) * 128)
            for jh in range(2):
                for ip in range(PEER_EB // PEER_NKEYS // 2):
                    acc = [jnp.zeros((jh_rows, 128), F32), jnp.zeros((jh_rows, 128), F32)]
                    for h in range(PEER_HEADS):
                        base1 = (2 * h + 1) * PEER_NKEYS + jh * jh_rows
                        s1 = s_nat[base1:base1 + jh_rows, lanes]
                        e1 = e_nat[base1:base1 + jh_rows, lanes]
                        tau0 = s_nat[pl.ds(2 * h * PEER_NKEYS + i0, 8), lanes]
                        e0 = e_nat[pl.ds(2 * h * PEER_NKEYS + i0, 8), lanes]
                        for k in range(2):
                            ii = 2 * ip + k
                            sel = s1 >= tau0[ii:ii + 1, :]
                            acc[k] = acc[k] + jnp.where(sel, e0[ii:ii + 1, :] * e1, 0.0)
                    for k in range(2):
                        r0 = (2 * ip + k) * PEER_NKEYS + jh * jh_rows
                        wraw[r0:r0 + jh_rows, lanes] = acc[k]
        wact[cur] = (wraw[...] * _gelu(actb[...])).astype(BF16)

    @pl.when(j == 0)
    def _():
        step(True, False)

    @pl.when((j > 0) & (j < last))
    def _():
        step(True, True)

    @pl.when(j == last)
    def _():
        step(False, True)
        z = ALPHA * x_ref[...] + yt[...].T
        out_ref[...] = _layer_norm(z, lw_ref[...], lb_ref[...])


def _peer(x, lp, tn):
    n = x.shape[0]
    n_blk = lp['peer_u'].shape[0] // PEER_EB
    const = lambda *s: pl.BlockSpec(s, lambda i, j: (0,) * len(s))
    return pl.pallas_call(
        functools.partial(_peer_kernel, tn=tn),
        grid=(n // tn, n_blk + 1),
        in_specs=[pl.BlockSpec((tn, D_MODEL), lambda i, j: (i, 0)),
                  const(2 * PEER_HEADS * 128, D_MODEL), const(2 * PEER_HEADS, PEER_NKEYS, 128),
                  pl.BlockSpec((PEER_EB, D_MODEL), lambda i, j: (jnp.minimum(j, n_blk - 1), 0)),
                  pl.BlockSpec((D_MODEL, PEER_EB), lambda i, j: (0, jnp.maximum(j - 1, 0))),
                  const(1, D_MODEL), const(1, D_MODEL)],
        out_specs=pl.BlockSpec((tn, D_MODEL), lambda i, j: (i, 0)),
        out_shape=jax.ShapeDtypeStruct((n, D_MODEL), F32),
        scratch_shapes=[pltpu.VMEM((D_MODEL, tn), BF16),
                        pltpu.VMEM((2 * PEER_HEADS * PEER_NKEYS, tn), F32),
                        pltpu.VMEM((2 * PEER_HEADS * PEER_NKEYS, tn), F32),
                        pltpu.VMEM((PEER_NKEYS, 8, 128), F32),
                        pltpu.VMEM((PEER_HEADS, 8 + PEER_TOPK, tn), F32),
                        pltpu.VMEM((2, PEER_EB, tn), BF16),
                        pltpu.VMEM((PEER_EB, tn), F32),
                        pltpu.VMEM((PEER_EB, tn), F32),
                        pltpu.VMEM((D_MODEL, tn), F32)],
        compiler_params=_cparams(("parallel", "arbitrary"), 56),
        name="peer_ln2",
    )(x, lp['peer_wq_t'], lp['peer_keys'], lp['peer_u'], lp['peer_v_t'], lp['ln2_w'], lp['ln2_b'])


DEC_N = 128
HEAD_STATE = HEAD_DIM * HEAD_DIM


def _col(v):
    return jnp.broadcast_to(v.reshape(-1, 1), (v.size, DEC_N))


def _load_state_t(s_ref, st):
    st[...] = s_ref[...].T.reshape(HEAD_DIM, HEAD_DIM, DEC_N)


def _store_state_t(so_ref, st):
    so_ref[...] = st[...].reshape(HEAD_STATE, DEC_N).T


def _ret_dec_kernel(c_ref, cos_ref, sin_ref, gw_ref, gb_ref, s_ref, o_ref, so_ref, ct, qk, st):
    h = pl.program_id(0)
    ct[...] = c_ref[...].T
    r0 = pl.multiple_of(h * HEAD_DIM, HEAD_DIM)

    def rot(x):
        partner = jnp.concatenate([x[HEAD_DIM // 2:], x[:HEAD_DIM // 2]], axis=0)
        return x * cos_ref[...] + partner * sin_ref[...]

    qk[0] = rot(ct[pl.ds(r0, HEAD_DIM), :])
    qk[1] = rot(ct[pl.ds(256 + r0, HEAD_DIM), :]) * (HEAD_DIM ** -0.5)
    v = ct[pl.ds(512 + r0, HEAD_DIM), :]
    g = ct[pl.ds(768 + r0, HEAD_DIM), :]
    gamma = jnp.exp(jnp.zeros((1, 1), F32) + jnp.where(
        h == 0, RET_LOG_GAMMA[0], jnp.where(h == 1, RET_LOG_GAMMA[1],
                                            jnp.where(h == 2, RET_LOG_GAMMA[2], RET_LOG_GAMMA[3]))))
    _load_state_t(s_ref, st)

    def body(k, o):
        s_new = gamma * st[k] + qk[1, pl.ds(k, 1), :] * v
        st[k] = s_new
        return o + qk[0, pl.ds(k, 1), :] * s_new

    o = lax.fori_loop(0, HEAD_DIM, body, jnp.zeros((HEAD_DIM, DEC_N), F32))
    mu = jnp.mean(o, axis=0, keepdims=True)
    d = o - mu
    var = jnp.mean(d * d, axis=0, keepdims=True)
    o_ref[...] = (d * lax.rsqrt(var + LN_EPS) * gw_ref[...] + gb_ref[...]) * _silu(g)
    _store_state_t(so_ref, st)


def _hgrn_dec_kernel(c_ref, lb_ref, nw_ref, s_ref, o_ref, so_ref, ct, qk, st):
    h = pl.program_id(0)
    ct[...] = c_ref[...].T
    r0 = pl.multiple_of(h * HEAD_DIM, HEAD_DIM)
    lb = lb_ref[...]
    qk[0] = _silu(ct[pl.ds(r0, HEAD_DIM), :])
    forget = lb + (1.0 - lb) * _sigmoid(ct[pl.ds(256 + r0, HEAD_DIM), :])
    qk[1] = forget
    qk[2] = 1.0 - forget
    v = ct[pl.ds(512 + r0, HEAD_DIM), :]
    g = ct[pl.ds(768 + r0, HEAD_DIM), :]
    _load_state_t(s_ref, st)

    def body(k, o):
        s_new = qk[1, pl.ds(k, 1), :] * st[k] + qk[2, pl.ds(k, 1), :] * v
        st[k] = s_new
        return o + qk[0, pl.ds(k, 1), :] * s_new

    o = lax.fori_loop(0, HEAD_DIM, body, jnp.zeros((HEAD_DIM, DEC_N), F32))
    ms = jnp.mean(o * o, axis=0, keepdims=True)
    o_ref[...] = o * lax.rsqrt(ms + LN_EPS) * nw_ref[...] * _silu(g)
    _store_state_t(so_ref, st)


def _rwkv_dec_kernel(c_ref, sh_ref, mu_ref, w0_ref, w2_ref, a0_ref, a2_ref, g2_ref, kkw_ref, ka_ref,
                     rk_ref, lnw_ref, lnb_ref, s_ref, o_ref, so_ref, vt, st, osc):
    h = pl.program_id(0)
    r, lw, k2, v, kkn, a, g = _rwkv_token_mix(c_ref[...], sh_ref[...], mu_ref, w0_ref, w2_ref, a0_ref, a2_ref,
                                              g2_ref, kkw_ref, ka_ref)
    for idx, arr in enumerate((r, jnp.exp(lw), k2, v, kkn, a, g)):
        vt[idx] = arr.T
    r0 = pl.multiple_of(h * HEAD_DIM, HEAD_DIM)
    rows = pl.ds(r0, HEAD_DIM)
    rh, wh, kh, kkh, ah = vt[0, rows, :], vt[1, rows, :], vt[2, rows, :], vt[4, rows, :], vt[5, rows, :]
    vh, gh = vt[3, rows, :], vt[6, rows, :]
    kka = kkh * ah
    _load_state_t(s_ref, st)

    def body(vi, carry):
        s_old = st[vi]
        sa = jnp.sum(s_old * (-kkh), axis=0, keepdims=True)
        s_new = s_old * wh + sa * kka + vt[3, pl.ds(r0 + vi, 1), :] * kh
        st[vi] = s_new
        osc[pl.ds(vi, 1), :] = jnp.sum(s_new * rh, axis=0, keepdims=True)
        return carry

    lax.fori_loop(0, HEAD_DIM, body, 0)
    o = osc[...]
    mu = jnp.mean(o, axis=0, keepdims=True)
    d = o - mu
    var = jnp.mean(d * d, axis=0, keepdims=True)
    on = d * lax.rsqrt(var + RW_GN_EPS) * lnw_ref[...] + lnb_ref[...]
    bonus = jnp.sum(rh * kh * rk_ref[...], axis=0, keepdims=True) * vh
    o_ref[...] = (on + bonus) * gh
    _store_state_t(so_ref, st)


def _dec_specs():
    head_tab = pl.BlockSpec((HEAD_DIM, DEC_N), lambda h: (h, 0))
    state = pl.BlockSpec((DEC_N, HEAD_STATE), lambda h: (0, h))
    out = pl.BlockSpec((None, HEAD_DIM, DEC_N), lambda h: (h, 0, 0))
    return head_tab, state, out


def _dec_out_shapes():
    return [jax.ShapeDtypeStruct((N_HEADS, HEAD_DIM, DEC_N), F32),
            jax.ShapeDtypeStruct((DEC_N, N_HEADS * HEAD_STATE), F32)]


def _dec_finish(o_t, s_new):
    return o_t.reshape(BRANCH_W, DEC_N).T, s_new.reshape(DEC_N, N_HEADS, HEAD_DIM, HEAD_DIM)


def _ret_decode(cols, state, cos_c, sin_c, lp):
    head_tab, st_spec, out_spec = _dec_specs()
    same = pl.BlockSpec((HEAD_DIM, DEC_N), lambda h: (0, 0))
    o_t, s_new = pl.pallas_call(
        _ret_dec_kernel,
        grid=(N_HEADS,),
        in_specs=[pl.BlockSpec((DEC_N, 1024), lambda h: (0, 0)), same, same, head_tab, head_tab, st_spec],
        out_specs=[out_spec, st_spec],
        out_shape=_dec_out_shapes(),
        scratch_shapes=[pltpu.VMEM((1024, DEC_N), F32), pltpu.VMEM((2, HEAD_DIM, DEC_N), F32),
                        pltpu.VMEM((HEAD_DIM, HEAD_DIM, DEC_N), F32)],
        compiler_params=_cparams(("arbitrary",), 40),
        name="ret_decode",
    )(cols, cos_c, sin_c, _col(lp['ret_gn_w']), _col(lp['ret_gn_b']), state.reshape(DEC_N, -1))
    return _dec_finish(o_t, s_new)


def _hgrn_decode(cols, state, lp):
    head_tab, st_spec, out_spec = _dec_specs()
    o_t, s_new = pl.pallas_call(
        _hgrn_dec_kernel,
        grid=(N_HEADS,),
        in_specs=[pl.BlockSpec((DEC_N, 1024), lambda h: (0, 1)), head_tab, head_tab, st_spec],
        out_specs=[out_spec, st_spec],
        out_shape=_dec_out_shapes(),
        scratch_shapes=[pltpu.VMEM((1024, DEC_N), F32), pltpu.VMEM((3, HEAD_DIM, DEC_N), F32),
                        pltpu.VMEM((HEAD_DIM, HEAD_DIM, DEC_N), F32)],
        compiler_params=_cparams(("arbitrary",), 40),
        name="hgrn_decode",
    )(cols, _col(lp['hg_lb']), _col(lp['hg_norm_w']), state.reshape(DEC_N, -1))
    return _dec_finish(o_t, s_new)


def _rwkv_decode(cols, shift, state, lp):
    head_tab, st_spec, out_spec = _dec_specs()
    row = lambda w: pl.BlockSpec((1, w), lambda h: (0, 0))
    mat = lambda r, w: pl.BlockSpec((r, w), lambda h: (0, 0))
    o_t, s_new = pl.pallas_call(
        _rwkv_dec_kernel,
        grid=(N_HEADS,),
        in_specs=[pl.BlockSpec((DEC_N, 1024), lambda h: (0, 2)), mat(DEC_N, 1024),
                  row(1024), row(256), mat(64, 256), row(256), mat(64, 256), mat(128, 256),
                  row(256), row(256), head_tab, head_tab, head_tab, st_spec],
        out_specs=[out_spec, st_spec],
        out_shape=_dec_out_shapes(),
        scratch_shapes=[pltpu.VMEM((7, BRANCH_W, DEC_N), F32), pltpu.VMEM((HEAD_DIM, HEAD_DIM, DEC_N), F32),
                        pltpu.VMEM((HEAD_DIM, DEC_N), F32)],
        compiler_params=_cparams(("arbitrary",), 40),
        name="rwkv_decode",
    )(cols, shift, lp['rw_mu'], lp['rw_w0'], lp['rw_w2'], lp['rw_a0'], lp['rw_a2'], lp['rw_g2'],
      lp['rw_kk'], lp['rw_ka'], _col(lp['rw_rk']), _col(lp['rw_lnx_w']), _col(lp['rw_lnx_b']),
      state.reshape(DEC_N, -1))
    return _dec_finish(o_t, s_new)


def _lru_dec_kernel(c_ref, conv_ref, h0_ref, cw_ref, cb_ref, wa_ref, ba_ref, wx_ref, bx_ref, lam_ref,
                    y_ref, h_ref, nconv_ref):
    xb = c_ref[:, 0:BRANCH_W]
    gate = c_ref[:, BRANCH_W:2 * BRANCH_W]
    c0, c1, c2 = conv_ref[0], conv_ref[1], conv_ref[2]
    xc = (c0 * cw_ref[0:1, :] + c1 * cw_ref[1:2, :] + c2 * cw_ref[2:3, :] + xb * cw_ref[3:4, :]) + cb_ref[...]
    a, u = _lru_gates(xc, wa_ref, ba_ref, wx_ref, bx_ref, lam_ref)
    hn = a * h0_ref[...] + u
    h_ref[...] = hn
    y_ref[...] = hn * _gelu(gate)
    nconv_ref[0] = c1
    nconv_ref[1] = c2
    nconv_ref[2] = xb


def _lru_decode(cols, conv, h0, lp):
    full = lambda *s: pl.BlockSpec(s, lambda i: (0,) * len(s))
    return pl.pallas_call(
        _lru_dec_kernel,
        grid=(1,),
        in_specs=[pl.BlockSpec((DEC_N, 512), lambda i: (0, 6)), full(3, DEC_N, BRANCH_W), full(DEC_N, BRANCH_W),
                  full(CONV_W, BRANCH_W), full(1, BRANCH_W), full(BRANCH_W, BRANCH_W), full(1, BRANCH_W),
                  full(BRANCH_W, BRANCH_W), full(1, BRANCH_W), full(1, BRANCH_W)],
        out_specs=[full(DEC_N, BRANCH_W), full(DEC_N, BRANCH_W), full(3, DEC_N, BRANCH_W)],
        out_shape=[jax.ShapeDtypeStruct((DEC_N, BRANCH_W), F32), jax.ShapeDtypeStruct((DEC_N, BRANCH_W), F32),
                   jax.ShapeDtypeStruct((3, DEC_N, BRANCH_W), F32)],
        compiler_params=_cparams(("arbitrary",), 32),
        name="lru_decode",
    )(cols, conv, h0, lp['lru_conv_w'], lp['lru_conv_b'], lp['lru_wa_bd'], lp['lru_ba'], lp['lru_wx_bd'],
      lp['lru_bx'], lp['lru_lambda'])


def _prompt_layer(x, p_emb, lp, cos_t, sin_t):
    b, t, d = x.shape
    n = b * t
    xf = x.reshape(n, d)
    cols3 = _proj(xf, lp['w_in_br'], 512).reshape(b, t, BRANCH_COLS)
    o_a, s_ret = _ret_prompt(cols3, cos_t, sin_t, lp)
    o_b, s_hg = _hgrn_prompt(cols3, lp['hg_lb'], lp)
    o_c, s_shift, s_rw = _rwkv_prompt(cols3, lp)
    o_d, s_lru, s_conv = _lru_prompt(cols3, lp)
    outs = [o.reshape(n, BRANCH_W) for o in (o_a, o_b, o_c, o_d)]
    x1 = _mix(xf, outs, lp, 256)
    x2 = _peer(x1, lp, 512)
    x3 = _ple(x2, p_emb.reshape(n, -1), lp, 512)
    return x3.reshape(b, t, d), (s_ret, s_hg, s_rw, s_shift[:, 0], s_lru[:, 0], s_conv)


def _sample_layer(x, p_emb, state, lp, cos_c, sin_c):
    s_ret, s_hg, s_rw, s_shift, s_lru, s_conv = state
    xf = x.reshape(DEC_N, D_MODEL)
    cols = _proj(xf, lp['w_in_br'], DEC_N)
    o_a, s_ret = _ret_decode(cols, s_ret, cos_c, sin_c, lp)
    o_b, s_hg = _hgrn_decode(cols, s_hg, lp)
    o_c, s_rw = _rwkv_decode(cols, s_shift, s_rw, lp)
    o_d, s_lru, s_conv_t = _lru_decode(cols, jnp.swapaxes(s_conv, 0, 1), s_lru, lp)
    x1 = _mix(xf, [o_a, o_b, o_c, o_d], lp, DEC_N)
    x2 = _peer(x1, lp, DEC_N)
    x3 = _ple(x2, p_emb.reshape(DEC_N, -1), lp, DEC_N)
    new_shift = cols[:, 2048:3072]
    return x3.reshape(x.shape), (s_ret, s_hg, s_rw, new_shift, s_lru, jnp.swapaxes(s_conv_t, 0, 1))


def kernel(x_prompt, x_sample, state_ret, state_hgrn, state_rwkv, state_shift, state_lru, state_conv, p_prompt, p_sample, w_in, ret_gn_w, ret_gn_b, hg_lb, hg_norm_w, rw_mu, rw_w0, rw_w2, rw_a0, rw_a2, rw_g2, rw_kk, rw_ka, rw_rk, rw_lnx_w, rw_lnx_b, lru_conv_w, lru_conv_b, lru_wa, lru_ba, lru_wx, lru_bx, lru_lambda, w_branch, w_out, ln1_w, ln1_b, peer_wq, peer_keys, peer_u, peer_v, ln2_w, ln2_b, ple_w, ple_gate_w, ple_gate_b):
    params = dict(w_in=w_in, ret_gn_w=ret_gn_w, ret_gn_b=ret_gn_b, hg_norm_w=hg_norm_w, rw_mu=rw_mu, rw_w0=rw_w0,
                  rw_w2=rw_w2, rw_a0=rw_a0, rw_a2=rw_a2, rw_g2=rw_g2, rw_kk=rw_kk, rw_ka=rw_ka, rw_rk=rw_rk,
                  rw_lnx_w=rw_lnx_w, rw_lnx_b=rw_lnx_b, lru_conv_w=lru_conv_w, lru_conv_b=lru_conv_b,
                  lru_wa=lru_wa, lru_ba=lru_ba, lru_wx=lru_wx, lru_bx=lru_bx, lru_lambda=lru_lambda,
                  w_branch=w_branch, w_out=w_out, ln1_w=ln1_w, ln1_b=ln1_b, peer_wq=peer_wq, peer_keys=peer_keys,
                  peer_u=peer_u, peer_v=peer_v, ln2_w=ln2_w, ln2_b=ln2_b, ple_w=ple_w, ple_gate_w=ple_gate_w,
                  ple_gate_b=ple_gate_b)
    lb_cum = jnp.cumsum(jax.nn.softmax(hg_lb.astype(F32), axis=0), axis=0)
    lb_all = lb_cum - lb_cum[0:1]
    t_prompt = x_prompt.shape[1]
    past_len = 16384
    cos_t, sin_t = _rope_tables(jnp.arange(t_prompt))
    cos_s, sin_s = _rope_tables(past_len + jnp.arange(1))
    cos_c = _col(cos_s[0, :HEAD_DIM])
    sin_c = _col(sin_s[0, :HEAD_DIM])
    h_p, h_s = x_prompt, x_sample
    new_p, new_s = [], []
    for l in range(DEPTH):
        lp = _layer_params(params, l, lb_all)
        h_p, st_p = _prompt_layer(h_p, p_prompt[l], lp, cos_t, sin_t)
        st_in = (state_ret[l], state_hgrn[l], state_rwkv[l], state_shift[l], state_lru[l], state_conv[l])
        h_s, st_s = _sample_layer(h_s, p_sample[l], st_in, lp, cos_c, sin_c)
        new_p.append(st_p)
        new_s.append(st_s)
    outs_p = [jnp.stack(z) for z in zip(*new_p)]
    outs_s = [jnp.stack(z) for z in zip(*new_s)]
    return (h_p, h_s, *outs_p, *outs_s)
```

```python
import functools
import math

import jax
import jax.numpy as jnp
from jax import lax
from jax.experimental import pallas as pl
from jax.experimental.pallas import tpu as pltpu

F32 = jnp.float32
BF16 = jnp.bfloat16
HIGHEST = lax.Precision.HIGHEST

D_MODEL = 1024
BRANCH_W = 256
N_HEADS = 4
HEAD_DIM = 64
RET_LOG_GAMMA = tuple(math.log1p(-(2.0 ** (-5.0 - h))) for h in range(N_HEADS))
ROPE_BASE = 10000.0
RW_GN_EPS = 64e-5
LN_EPS = 1e-5
LRU_C = 8.0
CONV_W = 4
DEPTH = 2
ALPHA = (2 * DEPTH) ** 0.25
BRANCH_COLS = 3584
MIB = 1024 * 1024


def _cparams(semantics, vmem_mib):
    return pltpu.CompilerParams(dimension_semantics=semantics, vmem_limit_bytes=vmem_mib * MIB)


def _mm(a, b):
    return jnp.dot(a.astype(BF16), b.astype(BF16), preferred_element_type=F32)


def _mm_nt(a, b):
    return lax.dot_general(a.astype(BF16), b.astype(BF16), (((1,), (1,)), ((), ())), preferred_element_type=F32)


def _mm_tn(a, b):
    return lax.dot_general(a.astype(BF16), b.astype(BF16), (((0,), (0,)), ((), ())), preferred_element_type=F32)


def _transpose_exact(a):
    n = a.shape[0]
    eye = (lax.broadcasted_iota(jnp.int32, (n, n), 0) == lax.broadcasted_iota(jnp.int32, (n, n), 1)).astype(F32)
    return lax.dot_general(a, eye, (((0,), (0,)), ((), ())), preferred_element_type=F32, precision=HIGHEST)


def _select_rows_exact(sel, x):
    x1 = x.astype(BF16)
    r1 = x - x1.astype(F32)
    x2 = r1.astype(BF16)
    x3 = (r1 - x2.astype(F32)).astype(BF16)
    dot = lambda p: jnp.dot(sel, p, preferred_element_type=F32)
    return (dot(x1) + dot(x2)) + dot(x3)


def _tril_ones(n):
    r = lax.broadcasted_iota(jnp.int32, (n, n), 0)
    c = lax.broadcasted_iota(jnp.int32, (n, n), 1)
    return (r >= c).astype(BF16)


def _sigmoid(x):
    return 1.0 / (1.0 + jnp.exp(-x))


def _silu(x):
    return x * _sigmoid(x)


def _gelu(x):
    return 0.5 * x * (1.0 + jnp.tanh(0.7978845608028654 * (x + 0.044715 * (x * x * x))))


def _softplus(x):
    return jnp.maximum(x, 0.0) + jnp.log1p(jnp.exp(-jnp.abs(x)))


def _head(x, h):
    return x[:, h * HEAD_DIM:(h + 1) * HEAD_DIM]


def _block_diag(w):
    n, c, d = w.shape
    eye = jnp.eye(n, dtype=w.dtype)
    return (eye[:, None, :, None] * w[:, :, None, :]).reshape(n * c, n * d)


def _rope_tables(pos):
    half = HEAD_DIM // 2
    inv_freq = ROPE_BASE ** (-jnp.arange(half, dtype=F32) / half)
    ang = pos.astype(F32)[:, None] * inv_freq[None, :]
    cos = jnp.cos(ang)
    sin = jnp.sin(ang)
    cos_t = jnp.tile(jnp.concatenate([cos, cos], axis=-1), (1, N_HEADS))
    sin_t = jnp.tile(jnp.concatenate([-sin, sin], axis=-1), (1, N_HEADS))
    return cos_t, sin_t


def _layer_params(p, l, lb_all):
    r2 = lambda a: a.reshape(1, -1)
    w_in = p['w_in'][l]
    lp = {
        'w_in_br': w_in[:, :BRANCH_COLS].astype(BF16),
        'w_gate': w_in[:, BRANCH_COLS:].astype(BF16),
        'ret_gn_w': r2(p['ret_gn_w'][l]), 'ret_gn_b': r2(p['ret_gn_b'][l]),
        'hg_lb': r2(lb_all[l]), 'hg_norm_w': r2(p['hg_norm_w'][l]),
        'rw_mu': r2(p['rw_mu'][l]), 'rw_w0': r2(p['rw_w0'][l]), 'rw_w2': p['rw_w2'][l].astype(BF16),
        'rw_a0': r2(p['rw_a0'][l]), 'rw_a2': p['rw_a2'][l].astype(BF16), 'rw_g2': p['rw_g2'][l].astype(BF16),
        'rw_kk': r2(p['rw_kk'][l]), 'rw_ka': r2(p['rw_ka'][l]), 'rw_rk': r2(p['rw_rk'][l]),
        'rw_lnx_w': r2(p['rw_lnx_w'][l]), 'rw_lnx_b': r2(p['rw_lnx_b'][l]),
        'lru_conv_w': p['lru_conv_w'][l], 'lru_conv_b': r2(p['lru_conv_b'][l]),
        'lru_wa_bd': _block_diag(p['lru_wa'][l]).astype(BF16), 'lru_ba': r2(p['lru_ba'][l]),
        'lru_wx_bd': _block_diag(p['lru_wx'][l]).astype(BF16), 'lru_bx': r2(p['lru_bx'][l]),
        'lru_lambda': r2(p['lru_lambda'][l]),
        'w_branch': p['w_branch'][l].astype(BF16), 'w_out': p['w_out'][l].astype(BF16),
        'ln1_w': r2(p['ln1_w'][l]), 'ln1_b': r2(p['ln1_b'][l]),
        'peer_wq_t': p['peer_wq'][l].T.astype(BF16),
        'peer_keys': p['peer_keys'][l].reshape(16, 128, 128).astype(BF16),
        'peer_u': p['peer_u'][l].astype(BF16),
        'peer_v_t': jnp.swapaxes(p['peer_v'][l].astype(BF16).reshape(-1, PEER_EB, D_MODEL), 1, 2),
        'ln2_w': r2(p['ln2_w'][l]), 'ln2_b': r2(p['ln2_b'][l]),
        'ple_w': p['ple_w'][l].astype(BF16), 'ple_gate_w': p['ple_gate_w'][l].astype(BF16),
        'ple_gate_b': r2(p['ple_gate_b'][l]),
    }
    return lp


def _proj_kernel(x_ref, w_ref, o_ref):
    o_ref[...] = jnp.dot(x_ref[...].astype(BF16), w_ref[...], preferred_element_type=F32)


def _proj(x, w_bf16, tn):
    n, k = x.shape
    m = w_bf16.shape[1]
    return pl.pallas_call(
        _proj_kernel,
        grid=(n // tn,),
        in_specs=[pl.BlockSpec((tn, k), lambda i: (i, 0)),
                  pl.BlockSpec((k, m), lambda i: (0, 0))],
        out_specs=pl.BlockSpec((tn, m), lambda i: (i, 0)),
        out_shape=jax.ShapeDtypeStruct((n, m), F32),
        compiler_params=_cparams(("parallel",), 48),
        name="in_proj",
    )(x, w_bf16)


def _lru_gates(xc, wa_ref, ba_ref, wx_ref, bx_ref, lam_ref):
    r = _sigmoid(_mm(xc, wa_ref[...]) + ba_ref[...])
    i = _sigmoid(_mm(xc, wx_ref[...]) + bx_ref[...])
    log_a = -LRU_C * r * _softplus(-lam_ref[...])
    a = jnp.exp(log_a)
    u = jnp.sqrt(1.0 - jnp.exp(2.0 * log_a)) * (i * xc)
    return a, u


def _lru_kernel(c_ref, cw_ref, cb_ref, wa_ref, ba_ref, wx_ref, bx_ref, lam_ref,
                y_ref, h_ref, conv_ref, xbuf, hcar, a_s, u_s, hs, *, tc):
    c = pl.program_id(1)

    @pl.when(c == 0)
    def _():
        xbuf[0:8, :] = jnp.zeros((8, BRANCH_W), F32)
        hcar[...] = jnp.zeros((1, BRANCH_W), F32)

    xb = c_ref[:, 0:BRANCH_W]
    gate = c_ref[:, BRANCH_W:2 * BRANCH_W]
    xbuf[8:8 + tc, :] = xb
    xc = (xbuf[pl.ds(5, tc), :] * cw_ref[0:1, :] + xbuf[pl.ds(6, tc), :] * cw_ref[1:2, :]
          + xbuf[pl.ds(7, tc), :] * cw_ref[2:3, :] + xb * cw_ref[3:4, :]) + cb_ref[...]
    a, u = _lru_gates(xc, wa_ref, ba_ref, wx_ref, bx_ref, lam_ref)
    a_s[...] = a
    u_s[...] = u

    def body(t, h):
        h = a_s[pl.ds(t, 1), :] * h + u_s[pl.ds(t, 1), :]
        hs[pl.ds(t, 1), :] = h
        return h

    h = lax.fori_loop(0, tc, body, hcar[...], unroll=8)
    hcar[...] = h
    y_ref[...] = hs[...] * _gelu(gate)
    xbuf[0:8, :] = xbuf[tc:tc + 8, :]

    @pl.when(c == pl.num_programs(1) - 1)
    def _():
        h_ref[...] = h
        conv_ref[...] = xbuf[5:8, :]


def _lru_prompt(cols3, lp, tc=256):
    b, t, _ = cols3.shape
    row = lambda: pl.BlockSpec((1, BRANCH_W), lambda i, j: (0, 0))
    full = lambda r: pl.BlockSpec((r, BRANCH_W), lambda i, j: (0, 0))
    return pl.pallas_call(
        functools.partial(_lru_kernel, tc=tc),
        grid=(b, t // tc),
        in_specs=[pl.BlockSpec((None, tc, 512), lambda i, j: (i, j, 6)),
                  full(CONV_W), row(), full(BRANCH_W), row(), full(BRANCH_W), row(), row()],
        out_specs=[pl.BlockSpec((None, tc, BRANCH_W), lambda i, j: (i, j, 0)),
                   pl.BlockSpec((None, 1, BRANCH_W), lambda i, j: (i, 0, 0)),
                   pl.BlockSpec((None, CONV_W - 1, BRANCH_W), lambda i, j: (i, 0, 0))],
        out_shape=[jax.ShapeDtypeStruct((b, t, BRANCH_W), F32),
                   jax.ShapeDtypeStruct((b, 1, BRANCH_W), F32),
                   jax.ShapeDtypeStruct((b, CONV_W - 1, BRANCH_W), F32)],
        scratch_shapes=[pltpu.VMEM((tc + 8, BRANCH_W), F32), pltpu.VMEM((1, BRANCH_W), F32),
                        pltpu.VMEM((tc, BRANCH_W), F32), pltpu.VMEM((tc, BRANCH_W), F32),
                        pltpu.VMEM((tc, BRANCH_W), F32)],
        compiler_params=_cparams(("parallel", "arbitrary"), 32),
        name="lru_prompt",
    )(cols3, lp['lru_conv_w'], lp['lru_conv_b'], lp['lru_wa_bd'], lp['lru_ba'], lp['lru_wx_bd'],
      lp['lru_bx'], lp['lru_lambda'])


def _rotary(x, cos, sin_signed, first_half):
    partner = jnp.where(first_half, pltpu.roll(x, BRANCH_W - 32, 1), pltpu.roll(x, 32, 1))
    return x * cos + partner * sin_signed


def _ret_kernel(c_ref, cos_ref, sin_ref, gw_ref, gb_ref, o_ref, s_ref, S, *, tc):
    c = pl.program_id(1)

    @pl.when(c == 0)
    def _():
        S[...] = jnp.zeros(S.shape, F32)

    lane = lax.broadcasted_iota(jnp.int32, (tc, BRANCH_W), 1)
    first_half = (lane % HEAD_DIM) < (HEAD_DIM // 2)
    cos = cos_ref[...]
    sin = sin_ref[...]
    q = _rotary(c_ref[:, 0:256], cos, sin, first_half)
    k = _rotary(c_ref[:, 256:512], cos, sin, first_half) * (HEAD_DIM ** -0.5)
    v = c_ref[:, 512:768]
    g = c_ref[:, 768:1024]
    row = lax.broadcasted_iota(jnp.int32, (tc, tc), 0)
    col = lax.broadcasted_iota(jnp.int32, (tc, tc), 1)
    causal = row >= col
    dist = jnp.where(causal, row - col, 0).astype(F32)
    tpos = lax.broadcasted_iota(jnp.int32, (tc, HEAD_DIM), 0).astype(F32)
    for h in range(N_HEADS):
        lg = RET_LOG_GAMMA[h]
        qh, kh, vh = _head(q, h), _head(k, h), _head(v, h)
        decay = jnp.where(causal, jnp.exp(dist * lg), 0.0)
        scores = _mm_nt(qh, kh) * decay
        oh = _mm(scores, vh) + _mm(qh * jnp.exp((tpos + 1.0) * lg), S[h])
        S[h] = math.exp(tc * lg) * S[h] + _mm_tn(kh * jnp.exp((tc - 1.0 - tpos) * lg), vh)
        mu = jnp.mean(oh, axis=-1, keepdims=True)
        d = oh - mu
        var = jnp.mean(d * d, axis=-1, keepdims=True)
        on = d * lax.rsqrt(var + LN_EPS) * _head(gw_ref[...], h) + _head(gb_ref[...], h)
        o_ref[:, h * HEAD_DIM:(h + 1) * HEAD_DIM] = on * _silu(_head(g, h))

    @pl.when(c == pl.num_programs(1) - 1)
    def _():
        s_ref[...] = S[...]


def _ret_prompt(cols3, cos_t, sin_t, lp, tc=128):
    b, t, _ = cols3.shape
    row = lambda: pl.BlockSpec((1, BRANCH_W), lambda i, j: (0, 0))
    return pl.pallas_call(
        functools.partial(_ret_kernel, tc=tc),
        grid=(b, t // tc),
        in_specs=[pl.BlockSpec((None, tc, 1024), lambda i, j: (i, j, 0)),
                  pl.BlockSpec((tc, BRANCH_W), lambda i, j: (j, 0)),
                  pl.BlockSpec((tc, BRANCH_W), lambda i, j: (j, 0)), row(), row()],
        out_specs=[pl.BlockSpec((None, tc, BRANCH_W), lambda i, j: (i, j, 0)),
                   pl.BlockSpec((None, N_HEADS, HEAD_DIM, HEAD_DIM), lambda i, j: (i, 0, 0, 0))],
        out_shape=[jax.ShapeDtypeStruct((b, t, BRANCH_W), F32),
                   jax.ShapeDtypeStruct((b, N_HEADS, HEAD_DIM, HEAD_DIM), F32)],
        scratch_shapes=[pltpu.VMEM((N_HEADS, HEAD_DIM, HEAD_DIM), F32)],
        compiler_params=_cparams(("parallel", "arbitrary"), 32),
        name="ret_prompt",
    )(cols3, cos_t, sin_t, lp['ret_gn_w'], lp['ret_gn_b'])


HGRN_CHUNK = 64
HGRN_LEVELS = tuple(HGRN_CHUNK >> l for l in range(1, 7))


def _hgrn_select_matrix():
    t = jnp.arange(HGRN_CHUNK)
    s = jnp.arange(HGRN_CHUNK)
    blocks = [s[None, :] <= t[:, None]]
    for m in HGRN_LEVELS:
        mid = (t // (2 * m)) * (2 * m) + m - 1
        blocks.append(s[None, :] <= mid[:, None])
    return jnp.concatenate(blocks, axis=0).astype(BF16)


def _hgrn_kernel(c_ref, lb_ref, nw_ref, sel_ref, o_ref, s_ref, St, *, tc, nb):
    c = pl.program_id(1)

    @pl.when(c == 0)
    def _():
        St[...] = jnp.zeros(St.shape, F32)

    lb = lb_ref[...]
    row = lax.broadcasted_iota(jnp.int32, (tc, BRANCH_W), 0)
    r64 = lax.broadcasted_iota(jnp.int32, (tc, tc), 0)
    c64 = lax.broadcasted_iota(jnp.int32, (tc, tc), 1)
    tok = []
    for n in range(nb):
        q = _silu(c_ref[n, :, 0:256])
        forget = lb + (1.0 - lb) * _sigmoid(c_ref[n, :, 256:512])
        kk = 1.0 - forget
        sums = _select_rows_exact(sel_ref[...], jnp.log(forget))
        tok.append(dict(q=q, kk=kk, sums=sums, b=sums[0:tc]))
    chains = [(n, h) for n in range(nb) for h in range(N_HEADS)]
    scores = {(n, h): jnp.where(r64 == c64, _mm_nt(_head(tok[n]['q'], h), _head(tok[n]['kk'], h)), 0.0)
              for n, h in chains}
    for lvl, m in enumerate(HGRN_LEVELS):
        sh = m.bit_length() - 1
        upper = ((row >> sh) & 1) == 1
        same = (r64 >> (sh + 1)) == (c64 >> (sh + 1))
        scaled = []
        for n in range(nb):
            t = tok[n]
            e = jnp.exp(-jnp.abs(t['b'] - t['sums'][(lvl + 1) * tc:(lvl + 2) * tc]))
            scaled.append((jnp.where(upper, t['q'] * e, 0.0), jnp.where(upper, 0.0, t['kk'] * e)))
        for n, h in chains:
            scores[n, h] = scores[n, h] + jnp.where(same, _mm_nt(_head(scaled[n][0], h), _head(scaled[n][1], h)), 0.0)
    for n in range(nb):
        t = tok[n]
        b = t['b']
        v = c_ref[n, :, 512:768]
        g = c_ref[n, :, 768:1024]
        b_end = b[tc - 1:tc, :]
        qd = t['q'] * jnp.exp(b)
        kd = t['kk'] * jnp.exp(b_end - b)
        for h in range(N_HEADS):
            vh = _head(v, h)
            oh = _mm(scores[n, h], vh) + _mm_nt(_head(qd, h), St[n, h])
            St[n, h] = St[n, h] * jnp.exp(_head(b_end, h)) + _mm_tn(vh, _head(kd, h))
            ms = jnp.mean(oh * oh, axis=-1, keepdims=True)
            on = oh * lax.rsqrt(ms + LN_EPS) * _head(nw_ref[...], h)
            o_ref[n, :, h * HEAD_DIM:(h + 1) * HEAD_DIM] = on * _silu(_head(g, h))

    @pl.when(c == pl.num_programs(1) - 1)
    def _():
        for n in range(nb):
            for h in range(N_HEADS):
                s_ref[n, h] = _transpose_exact(St[n, h])


def _hgrn_prompt(cols3, lb, lp, nb=4):
    b, t, _ = cols3.shape
    nb = min(nb, b)
    tc = HGRN_CHUNK
    row = lambda: pl.BlockSpec((1, BRANCH_W), lambda i, j: (0, 0))
    return pl.pallas_call(
        functools.partial(_hgrn_kernel, tc=tc, nb=nb),
        grid=(b // nb, t // tc),
        in_specs=[pl.BlockSpec((nb, tc, 1024), lambda i, j: (i, j, 1)), row(), row(),
                  pl.BlockSpec((7 * tc, tc), lambda i, j: (0, 0))],
        out_specs=[pl.BlockSpec((nb, tc, BRANCH_W), lambda i, j: (i, j, 0)),
                   pl.BlockSpec((nb, N_HEADS, HEAD_DIM, HEAD_DIM), lambda i, j: (i, 0, 0, 0))],
        out_shape=[jax.ShapeDtypeStruct((b, t, BRANCH_W), F32),
                   jax.ShapeDtypeStruct((b, N_HEADS, HEAD_DIM, HEAD_DIM), F32)],
        scratch_shapes=[pltpu.VMEM((nb, N_HEADS, HEAD_DIM, HEAD_DIM), F32)],
        compiler_params=_cparams(("parallel", "arbitrary"), 32),
        name="hgrn_prompt",
    )(cols3, lb, lp['hg_norm_w'], _hgrn_select_matrix())


def _rwkv_token_mix(cols, prev, mu_ref, w0_ref, w2_ref, a0_ref, a2_ref, g2_ref, kkw_ref, ka_ref):
    xs = cols + mu_ref[...] * (prev - cols)
    r = xs[:, 0:256]
    k = xs[:, 256:512]
    v = xs[:, 512:768]
    xg = xs[:, 768:896]
    xw = xs[:, 896:960]
    xa = xs[:, 960:1024]
    w = -_softplus(-(w0_ref[...] + _mm(jnp.tanh(xw), w2_ref[...]))) - 0.5
    lw = -jnp.exp(w)
    a = _sigmoid(a0_ref[...] + _mm(xa, a2_ref[...]))
    g = _mm(_sigmoid(xg), g2_ref[...])
    kk = k * kkw_ref[...]
    parts = []
    for h in range(N_HEADS):
        kh = _head(kk, h)
        nrm = jnp.sqrt(jnp.sum(kh * kh, axis=-1, keepdims=True))
        parts.append(kh / jnp.maximum(nrm, 1e-12))
    kkn = jnp.concatenate(parts, axis=-1)
    k2 = k * (1.0 + (a - 1.0) * ka_ref[...])
    return r, lw, k2, v, kkn, a, g


def _rwkv_out(o, r, k2, v, g, rk_ref, lw_ref, lb_ref, h):
    mu = jnp.mean(o, axis=-1, keepdims=True)
    d = o - mu
    var = jnp.mean(d * d, axis=-1, keepdims=True)
    on = d * lax.rsqrt(var + RW_GN_EPS) * _head(lw_ref[...], h) + _head(lb_ref[...], h)
    bonus = jnp.sum(_head(r, h) * _head(k2, h) * _head(rk_ref[...], h), axis=-1, keepdims=True) * _head(v, h)
    return (on + bonus) * _head(g, h)


def _rwkv_kernel(c_ref, mu_ref, w0_ref, w2_ref, a0_ref, a2_ref, g2_ref, kkw_ref, ka_ref, rk_ref,
                 lnw_ref, lnb_ref, y_ref, shift_ref, s_ref, S, last_row, *, tc, nb):
    c = pl.program_id(1)

    @pl.when(c == 0)
    def _():
        S[...] = jnp.zeros(S.shape, F32)
        last_row[...] = jnp.zeros(last_row.shape, F32)

    rr = lax.broadcasted_iota(jnp.int32, (tc, tc), 0)
    cc = lax.broadcasted_iota(jnp.int32, (tc, tc), 1)
    strict = rr > cc
    incl2 = (lax.broadcasted_iota(jnp.int32, (tc, 2 * tc), 0)
             >= (lax.broadcasted_iota(jnp.int32, (tc, 2 * tc), 1) & (tc - 1)))
    tril = _tril_ones(tc)
    n_double = max(1, (tc - 1).bit_length())
    tok = []
    for n in range(nb):
        cols = c_ref[n]
        rowi = lax.broadcasted_iota(jnp.int32, cols.shape, 0)
        prev = jnp.where(rowi == 0, last_row[n], pltpu.roll(cols, 1, 0))
        last_row[n] = cols[tc - 1:tc, :]
        r, lw, k2, v, kkn, a, g = _rwkv_token_mix(cols, prev, mu_ref, w0_ref, w2_ref, a0_ref, a2_ref, g2_ref,
                                                  kkw_ref, ka_ref)
        am = -kkn
        bm = kkn * a
        G = _select_rows_exact(tril, lw)
        g_end = G[tc - 1:tc, :]
        einv = jnp.exp(-G)
        eend = jnp.exp(g_end - G)
        tok.append(dict(r=r, k2=k2, v=v, g=g, g_end=g_end, at=am * jnp.exp(G - lw), rt=r * jnp.exp(G),
                        bt=bm * einv, kt=k2 * einv, bbar=bm * eend, kbar=k2 * eend))
    chains = [(n, h) for n in range(nb) for h in range(N_HEADS)]
    lhs = {(n, h): jnp.concatenate([_head(tok[n]['at'], h), _head(tok[n]['rt'], h)], axis=0) for n, h in chains}
    inter = {(n, h): _mm_nt(lhs[n, h], jnp.concatenate([_head(tok[n]['bt'], h), _head(tok[n]['kt'], h)], axis=0))
             for n, h in chains}
    from_state = {(n, h): _mm_nt(lhs[n, h], S[n, h]) for n, h in chains}
    npow = {ch: jnp.where(strict, inter[ch][0:tc, 0:tc], 0.0) for ch in chains}
    u = {(n, h): from_state[n, h][0:tc]
         + _mm(jnp.where(strict, inter[n, h][0:tc, tc:2 * tc], 0.0), _head(tok[n]['v'], h)) for n, h in chains}
    for j in range(n_double):
        u = {ch: u[ch] + _mm(npow[ch], u[ch]) for ch in chains}
        if j + 1 < n_double:
            npow = {ch: _mm(npow[ch], npow[ch]) for ch in chains}
    for n, h in chains:
        t = tok[n]
        uv = jnp.concatenate([u[n, h], _head(t['v'], h)], axis=0)
        a_r = jnp.where(incl2, inter[n, h][tc:2 * tc, :], 0.0)
        o = from_state[n, h][tc:2 * tc] + _mm(a_r, uv)
        S[n, h] = (S[n, h] * jnp.exp(_head(t['g_end'], h))
                   + _mm_tn(uv, jnp.concatenate([_head(t['bbar'], h), _head(t['kbar'], h)], axis=0)))
        y_ref[n, :, h * HEAD_DIM:(h + 1) * HEAD_DIM] = _rwkv_out(o, t['r'], t['k2'], t['v'], t['g'], rk_ref,
                                                                 lnw_ref, lnb_ref, h)

    @pl.when(c == pl.num_programs(1) - 1)
    def _():
        s_ref[...] = S[...]
        for n in range(nb):
            shift_ref[n] = c_ref[n, tc - 1:tc, :]


def _rwkv_prompt(cols3, lp, tc=64, nb=4):
    b, t, _ = cols3.shape
    nb = min(nb, b)
    row = lambda w: pl.BlockSpec((1, w), lambda i, j: (0, 0))
    mat = lambda r, w: pl.BlockSpec((r, w), lambda i, j: (0, 0))
    return pl.pallas_call(
        functools.partial(_rwkv_kernel, tc=tc, nb=nb),
        grid=(b // nb, t // tc),
        in_specs=[pl.BlockSpec((nb, tc, 1024), lambda i, j: (i, j, 2)),
                  row(1024), row(256), mat(64, 256), row(256), mat(64, 256), mat(128, 256),
                  row(256), row(256), row(256), row(256), row(256)],
        out_specs=[pl.BlockSpec((nb, tc, BRANCH_W), lambda i, j: (i, j, 0)),
                   pl.BlockSpec((nb, 1, 1024), lambda i, j: (i, 0, 0)),
                   pl.BlockSpec((nb, N_HEADS, HEAD_DIM, HEAD_DIM), lambda i, j: (i, 0, 0, 0))],
        out_shape=[jax.ShapeDtypeStruct((b, t, BRANCH_W), F32),
                   jax.ShapeDtypeStruct((b, 1, 1024), F32),
                   jax.ShapeDtypeStruct((b, N_HEADS, HEAD_DIM, HEAD_DIM), F32)],
        scratch_shapes=[pltpu.VMEM((nb, N_HEADS, HEAD_DIM, HEAD_DIM), F32), pltpu.VMEM((nb, 1, 1024), F32)],
        compiler_params=_cparams(("parallel", "arbitrary"), 32),
        name="rwkv_prompt",
    )(cols3, lp['rw_mu'], lp['rw_w0'], lp['rw_w2'], lp['rw_a0'], lp['rw_a2'], lp['rw_g2'],
      lp['rw_kk'], lp['rw_ka'], lp['rw_rk'], lp['rw_lnx_w'], lp['rw_lnx_b'])


def _layer_norm(z, w, b):
    mu = jnp.mean(z, axis=-1, keepdims=True)
    d = z - mu
    var = jnp.mean(d * d, axis=-1, keepdims=True)
    return d * lax.rsqrt(var + LN_EPS) * w + b


def _mix_kernel(x_ref, oa_ref, ob_ref, oc_ref, od_ref, wg_ref, wb_ref, wo_ref, lw_ref, lb_ref, out_ref):
    x = x_ref[...]
    xb = x.astype(BF16)
    mixed = None
    for gi, o_ref in enumerate((oa_ref, ob_ref, oc_ref, od_ref)):
        gate = _sigmoid(jnp.dot(xb, wg_ref[:, gi * D_MODEL:(gi + 1) * D_MODEL], preferred_element_type=F32))
        up = jnp.dot(o_ref[...].astype(BF16), wb_ref[gi], preferred_element_type=F32)
        mixed = gate * up if mixed is None else mixed + gate * up
    y = jnp.dot(mixed.astype(BF16), wo_ref[...], preferred_element_type=F32)
    out_ref[...] = _layer_norm(ALPHA * x + y, lw_ref[...], lb_ref[...])


def _mix(x, outs, lp, tn):
    n = x.shape[0]
    tok = lambda w: pl.BlockSpec((tn, w), lambda i: (i, 0))
    const = lambda *s: pl.BlockSpec(s, lambda i: (0,) * len(s))
    return pl.pallas_call(
        _mix_kernel,
        grid=(n // tn,),
        in_specs=[tok(D_MODEL), tok(BRANCH_W), tok(BRANCH_W), tok(BRANCH_W), tok(BRANCH_W),
                  const(D_MODEL, 4 * D_MODEL), const(4, BRANCH_W, D_MODEL), const(D_MODEL, D_MODEL),
                  const(1, D_MODEL), const(1, D_MODEL)],
        out_specs=tok(D_MODEL),
        out_shape=jax.ShapeDtypeStruct((n, D_MODEL), F32),
        compiler_params=_cparams(("parallel",), 48),
        name="mix_ln1",
    )(x, *outs, lp['w_gate'], lp['w_branch'], lp['w_out'], lp['ln1_w'], lp['ln1_b'])


def _ple_kernel(x_ref, p_ref, wg_ref, bg_ref, wp_ref, out_ref):
    x = x_ref[...]
    gate = _sigmoid(jnp.dot(x.astype(BF16), wg_ref[...], preferred_element_type=F32) + bg_ref[...])
    emb = jnp.dot(p_ref[...].astype(BF16), wp_ref[...], preferred_element_type=F32)
    out_ref[...] = x + gate * emb


def _ple(x, p_emb, lp, tn):
    n = x.shape[0]
    tok = lambda w: pl.BlockSpec((tn, w), lambda i: (i, 0))
    const = lambda *s: pl.BlockSpec(s, lambda i: (0,) * len(s))
    return pl.pallas_call(
        _ple_kernel,
        grid=(n // tn,),
        in_specs=[tok(D_MODEL), tok(256), const(D_MODEL, D_MODEL), const(1, D_MODEL), const(256, D_MODEL)],
        out_specs=tok(D_MODEL),
        out_shape=jax.ShapeDtypeStruct((n, D_MODEL), F32),
        compiler_params=_cparams(("parallel",), 32),
        name="ple_gate",
    )(x, p_emb, lp['ple_gate_w'], lp['ple_gate_b'], lp['ple_w'])


PEER_HEADS = 8
PEER_NKEYS = 128
PEER_TOPK = 16
PEER_EB = 1024


def _oddeven_merge_sort_pairs(n):
    pairs = []
    p = 1
    while p < n:
        k = p
        while k >= 1:
            for j in range(k % p, n - k, 2 * k):
                for i in range(min(k, n - j - k)):
                    if (i + j) // (p * 2) == (i + j + k) // (p * 2):
                        pairs.append((i + j, i + j + k))
            k //= 2
        p *= 2
    return pairs


def _bitonic_merge_pairs(n):
    pairs = []
    k = n // 2
    while k >= 1:
        pairs.extend((i, i + k) for i in range(n) if (i & k) == 0)
        k //= 2
    return pairs


_SORT16 = _oddeven_merge_sort_pairs(PEER_TOPK)
_MERGE16 = _bitonic_merge_pairs(PEER_TOPK)
_CAND_LEN = tuple(PEER_TOPK // (a + 1) for a in range(PEER_TOPK))


def _network(vals, pairs):
    vals = list(vals)
    for i, j in pairs:
        hi = jnp.maximum(vals[i], vals[j])
        lo = jnp.minimum(vals[i], vals[j])
        vals[i], vals[j] = hi, lo
    return vals


def _top16_merge(x, y):
    return _network([jnp.maximum(x[i], y[PEER_TOPK - 1 - i]) for i in range(PEER_TOPK)], _MERGE16)


def _peer_head_stats(h, s_nat, e_nat, sk, th, g_count):
    rows0 = pl.ds(pl.multiple_of(2 * h * PEER_NKEYS, PEER_NKEYS), PEER_NKEYS)
    rows1 = pl.ds(pl.multiple_of((2 * h + 1) * PEER_NKEYS, PEER_NKEYS), PEER_NKEYS)
    for p, rows in enumerate((rows0, rows1)):
        for g in range(4):
            gg = g % g_count
            sk[:, p * 4 + g, :] = s_nat[gg, rows, :]
    groups = []
    for m in range(PEER_NKEYS // PEER_TOPK):
        groups.append(_network([sk[PEER_TOPK * m + i] for i in range(PEER_TOPK)], _SORT16))
    while len(groups) > 1:
        groups = [_top16_merge(groups[i], groups[i + 1]) for i in range(0, len(groups), 2)]
    top = groups[0]
    low = lax.broadcasted_iota(jnp.int32, (8, 128), 0) < 4
    ta = [jnp.where(low, t, pltpu.roll(t, 4, 0)) for t in top]
    tb = [jnp.where(low, pltpu.roll(t, 4, 0), t) for t in top]
    cand = [[ta[a] + tb[b] for b in range(_CAND_LEN[a])] for a in range(PEER_TOPK)]
    m1 = _network(cand[1] + [cand[a][0] for a in range(PEER_TOPK - 1, 7, -1)], _MERGE16)
    m2 = _network(cand[2] + cand[3] + cand[4] + cand[5] + cand[6], _SORT16)
    t1 = _top16_merge(cand[0], m1)
    t2 = _top16_merge(t1, m2)
    t2[15] = jnp.maximum(t2[15], cand[7][0])
    t2[14] = jnp.maximum(t2[14], cand[7][1])
    theta = t2[0]
    for t in t2[1:]:
        theta = jnp.minimum(theta, t)
    cmax = cand[0][0]
    z = jnp.zeros((8, 128), F32)
    for row in cand:
        for cv in row:
            z = z + jnp.where(cv >= theta, jnp.exp(cv - cmax), 0.0)
    inv_z = 1.0 / z
    for g in range(g_count):
        lanes = slice(g * 128, (g + 1) * 128)
        th[h, 0:1, lanes] = theta[g:g + 1, :]
        th[h, 1:2, lanes] = ta[0][g:g + 1, :]
        th[h, 2:3, lanes] = tb[0][g:g + 1, :]
        th[h, 3:4, lanes] = inv_z[g:g + 1, :]
        for b in range(PEER_TOPK):
            th[h, 8 + b:9 + b, lanes] = tb[b][g:g + 1, :]
    for g in range(g_count):
        lanes = slice(g * 128, (g + 1) * 128)
        s0 = s_nat[g, rows0, :]
        e_nat[g, rows0, :] = jnp.exp(s0 - th[h, 1:2, lanes])
        e_nat[g, rows1, :] = jnp.exp(s_nat[g, rows1, :] - th[h, 2:3, lanes]) * th[h, 3:4, lanes]
        tau = jnp.full(s0.shape, jnp.inf, F32)
        for b in range(PEER_TOPK):
            sb = th[h, 8 + b:9 + b, lanes]
            tau = jnp.where((s0 + sb) >= th[h, 0:1, lanes], sb, tau)
        s_nat[g, rows0, :] = tau


def _peer_kernel(x_ref, wq_ref, keys_ref, u_ref, vt_ref, lw_ref, lb_ref, out_ref,
                 xtb, s_nat, e_nat, sk, th, wact, wraw, actb, yt, *, tn):
    j = pl.program_id(1)
    g_count = tn // 128

    @pl.when(j == 0)
    def _():
        xtb[...] = x_ref[...].T.astype(BF16)
        for hp in range(2 * PEER_HEADS):
            qt = jnp.dot(wq_ref[hp * 128:(hp + 1) * 128, :], xtb[...], preferred_element_type=F32)
            scores = jnp.dot(keys_ref[hp], qt.astype(BF16), preferred_element_type=F32)
            for g in range(g_count):
                s_nat[g, hp * PEER_NKEYS:(hp + 1) * PEER_NKEYS, :] = scores[:, g * 128:(g + 1) * 128]

        def head_body(h, carry):
            _peer_head_stats(h, s_nat, e_nat, sk, th, g_count)
            return carry

        lax.fori_loop(0, PEER_HEADS, head_body, 0)
        yt[...] = jnp.zeros(yt.shape, F32)

    last = pl.num_programs(1) - 1
    cur = j % 2

    def step(first_stage, second_stage):
        if first_stage:
            act = jnp.dot(u_ref[...], xtb[...], preferred_element_type=F32)
            for g in range(g_count):
                actb[g] = act[:, g * 128:(g + 1) * 128]
        if second_stage:
            yt[...] += jnp.dot(vt_ref[...], wact[1 - cur], preferred_element_type=F32)
        if not first_stage:
            return
        i0 = pl.multiple_of(j * (PEER_EB // PEER_NKEYS), 8)
        jh_rows = PEER_NKEYS // 2
        for g in range(g_count):
            lanes = slice(g * 128, (g + 1) * 128)
            for jh in range(2):
                for ip in range(PEER_EB // PEER_NKEYS // 2):
                    acc = [jnp.zeros((jh_rows, 128), F32), jnp.zeros((jh_rows, 128), F32)]
                    for h in range(PEER_HEADS):
                        base1 = (2 * h + 1) * PEER_NKEYS + jh * jh_rows
                        s1 = s_nat[g, base1:base1 + jh_rows, :]
                        e1 = e_nat[g, base1:base1 + jh_rows, :]
                        tau0 = s_nat[g, pl.ds(2 * h * PEER_NKEYS + i0, 8), :]
                        e0 = e_nat[g, pl.ds(2 * h * PEER_NKEYS + i0, 8), :]
                        for k in range(2):
                            ii = 2 * ip + k
                            sel = s1 >= tau0[ii:ii + 1, :]
                            acc[k] = acc[k] + jnp.where(sel, e0[ii:ii + 1, :] * e1, 0.0)
                    for k in range(2):
                        r0 = (2 * ip + k) * PEER_NKEYS + jh * jh_rows
                        wraw[g, r0:r0 + jh_rows, :] = acc[k]
        for g in range(g_count):
            wact[cur, :, g * 128:(g + 1) * 128] = (wraw[g] * _gelu(actb[g])).astype(BF16)

    @pl.when(j == 0)
    def _():
        step(True, False)

    @pl.when((j > 0) & (j < last))
    def _():
        step(True, True)

    @pl.when(j == last)
    def _():
        step(False, True)
        z = ALPHA * x_ref[...] + yt[...].T
        out_ref[...] = _layer_norm(z, lw_ref[...], lb_ref[...])


def _peer(x, lp, tn):
    n = x.shape[0]
    n_blk = lp['peer_u'].shape[0] // PEER_EB
    const = lambda *s: pl.BlockSpec(s, lambda i, j: (0,) * len(s))
    return pl.pallas_call(
        functools.partial(_peer_kernel, tn=tn),
        grid=(n // tn, n_blk + 1),
        in_specs=[pl.BlockSpec((tn, D_MODEL), lambda i, j: (i, 0)),
                  const(2 * PEER_HEADS * 128, D_MODEL), const(2 * PEER_HEADS, PEER_NKEYS, 128),
                  pl.BlockSpec((PEER_EB, D_MODEL), lambda i, j: (jnp.minimum(j, n_blk - 1), 0)),
                  pl.BlockSpec((None, D_MODEL, PEER_EB), lambda i, j: (jnp.maximum(j - 1, 0), 0, 0)),
                  const(1, D_MODEL), const(1, D_MODEL)],
        out_specs=pl.BlockSpec((tn, D_MODEL), lambda i, j: (i, 0)),
        out_shape=jax.ShapeDtypeStruct((n, D_MODEL), F32),
        scratch_shapes=[pltpu.VMEM((D_MODEL, tn), BF16),
                        pltpu.VMEM((tn // 128, 2 * PEER_HEADS * PEER_NKEYS, 128), F32),
                        pltpu.VMEM((tn // 128, 2 * PEER_HEADS * PEER_NKEYS, 128), F32),
                        pltpu.VMEM((PEER_NKEYS, 8, 128), F32),
                        pltpu.VMEM((PEER_HEADS, 8 + PEER_TOPK, tn), F32),
                        pltpu.VMEM((2, PEER_EB, tn), BF16),
                        pltpu.VMEM((tn // 128, PEER_EB, 128), F32),
                        pltpu.VMEM((tn // 128, PEER_EB, 128), F32),
                        pltpu.VMEM((D_MODEL, tn), F32)],
        compiler_params=_cparams(("parallel", "arbitrary"), 56),
        name="peer_ln2",
    )(x, lp['peer_wq_t'], lp['peer_keys'], lp['peer_u'], lp['peer_v_t'], lp['ln2_w'], lp['ln2_b'])


DEC_N = 128
HEAD_STATE = HEAD_DIM * HEAD_DIM


def _col(v):
    return jnp.broadcast_to(v.reshape(-1, 1), (v.size, DEC_N))


def _load_state_t(s_ref, st):
    st[...] = s_ref[...].T.reshape(HEAD_DIM, HEAD_DIM, DEC_N)


def _store_state_t(so_ref, st):
    so_ref[...] = st[...].reshape(HEAD_STATE, DEC_N).T


def _ret_dec_kernel(c_ref, cos_ref, sin_ref, gw_ref, gb_ref, s_ref, o_ref, so_ref, ct, qk, st):
    h = pl.program_id(0)
    ct[...] = c_ref[...].T
    r0 = pl.multiple_of(h * HEAD_DIM, HEAD_DIM)

    def rot(x):
        partner = jnp.concatenate([x[HEAD_DIM // 2:], x[:HEAD_DIM // 2]], axis=0)
        return x * cos_ref[...] + partner * sin_ref[...]

    qk[0] = rot(ct[pl.ds(r0, HEAD_DIM), :])
    qk[1] = rot(ct[pl.ds(256 + r0, HEAD_DIM), :]) * (HEAD_DIM ** -0.5)
    v = ct[pl.ds(512 + r0, HEAD_DIM), :]
    g = ct[pl.ds(768 + r0, HEAD_DIM), :]
    gamma = jnp.exp(jnp.zeros((1, 1), F32) + jnp.where(
        h == 0, RET_LOG_GAMMA[0], jnp.where(h == 1, RET_LOG_GAMMA[1],
                                            jnp.where(h == 2, RET_LOG_GAMMA[2], RET_LOG_GAMMA[3]))))
    _load_state_t(s_ref, st)

    def body(k, o):
        s_new = gamma * st[k] + qk[1, pl.ds(k, 1), :] * v
        st[k] = s_new
        return o + qk[0, pl.ds(k, 1), :] * s_new

    o = lax.fori_loop(0, HEAD_DIM, body, jnp.zeros((HEAD_DIM, DEC_N), F32))
    mu = jnp.mean(o, axis=0, keepdims=True)
    d = o - mu
    var = jnp.mean(d * d, axis=0, keepdims=True)
    o_ref[...] = (d * lax.rsqrt(var + LN_EPS) * gw_ref[...] + gb_ref[...]) * _silu(g)
    _store_state_t(so_ref, st)


def _hgrn_dec_kernel(c_ref, lb_ref, nw_ref, s_ref, o_ref, so_ref, ct, qk, st):
    h = pl.program_id(0)
    ct[...] = c_ref[...].T
    r0 = pl.multiple_of(h * HEAD_DIM, HEAD_DIM)
    lb = lb_ref[...]
    qk[0] = _silu(ct[pl.ds(r0, HEAD_DIM), :])
    forget = lb + (1.0 - lb) * _sigmoid(ct[pl.ds(256 + r0, HEAD_DIM), :])
    qk[1] = forget
    qk[2] = 1.0 - forget
    v = ct[pl.ds(512 + r0, HEAD_DIM), :]
    g = ct[pl.ds(768 + r0, HEAD_DIM), :]
    _load_state_t(s_ref, st)

    def body(k, o):
        s_new = qk[1, pl.ds(k, 1), :] * st[k] + qk[2, pl.ds(k, 1), :] * v
        st[k] = s_new
        return o + qk[0, pl.ds(k, 1), :] * s_new

    o = lax.fori_loop(0, HEAD_DIM, body, jnp.zeros((HEAD_DIM, DEC_N), F32))
    ms = jnp.mean(o * o, axis=0, keepdims=True)
    o_ref[...] = o * lax.rsqrt(ms + LN_EPS) * nw_ref[...] * _silu(g)
    _store_state_t(so_ref, st)


def _rwkv_dec_kernel(c_ref, sh_ref, mu_ref, w0_ref, w2_ref, a0_ref, a2_ref, g2_ref, kkw_ref, ka_ref,
                     rk_ref, lnw_ref, lnb_ref, s_ref, o_ref, so_ref, vt, st, osc):
    h = pl.program_id(0)
    r, lw, k2, v, kkn, a, g = _rwkv_token_mix(c_ref[...], sh_ref[...], mu_ref, w0_ref, w2_ref, a0_ref, a2_ref,
                                              g2_ref, kkw_ref, ka_ref)
    for idx, arr in enumerate((r, jnp.exp(lw), k2, v, kkn, a, g)):
        vt[idx] = arr.T
    r0 = pl.multiple_of(h * HEAD_DIM, HEAD_DIM)
    rows = pl.ds(r0, HEAD_DIM)
    rh, wh, kh, kkh, ah = vt[0, rows, :], vt[1, rows, :], vt[2, rows, :], vt[4, rows, :], vt[5, rows, :]
    vh, gh = vt[3, rows, :], vt[6, rows, :]
    kka = kkh * ah
    _load_state_t(s_ref, st)

    def body(vi, carry):
        s_old = st[vi]
        sa = jnp.sum(s_old * (-kkh), axis=0, keepdims=True)
        s_new = s_old * wh + sa * kka + vt[3, pl.ds(r0 + vi, 1), :] * kh
        st[vi] = s_new
        osc[pl.ds(vi, 1), :] = jnp.sum(s_new * rh, axis=0, keepdims=True)
        return carry

    lax.fori_loop(0, HEAD_DIM, body, 0)
    o = osc[...]
    mu = jnp.mean(o, axis=0, keepdims=True)
    d = o - mu
    var = jnp.mean(d * d, axis=0, keepdims=True)
    on = d * lax.rsqrt(var + RW_GN_EPS) * lnw_ref[...] + lnb_ref[...]
    bonus = jnp.sum(rh * kh * rk_ref[...], axis=0, keepdims=True) * vh
    o_ref[...] = (on + bonus) * gh
    _store_state_t(so_ref, st)


def _dec_specs():
    head_tab = pl.BlockSpec((HEAD_DIM, DEC_N), lambda h: (h, 0))
    state = pl.BlockSpec((DEC_N, HEAD_STATE), lambda h: (0, h))
    out = pl.BlockSpec((None, HEAD_DIM, DEC_N), lambda h: (h, 0, 0))
    return head_tab, state, out


def _dec_out_shapes():
    return [jax.ShapeDtypeStruct((N_HEADS, HEAD_DIM, DEC_N), F32),
            jax.ShapeDtypeStruct((DEC_N, N_HEADS * HEAD_STATE), F32)]


def _dec_finish(o_t, s_new):
    return o_t.reshape(BRANCH_W, DEC_N).T, s_new.reshape(DEC_N, N_HEADS, HEAD_DIM, HEAD_DIM)


def _ret_decode(cols, state, cos_c, sin_c, lp):
    head_tab, st_spec, out_spec = _dec_specs()
    same = pl.BlockSpec((HEAD_DIM, DEC_N), lambda h: (0, 0))
    o_t, s_new = pl.pallas_call(
        _ret_dec_kernel,
        grid=(N_HEADS,),
        in_specs=[pl.BlockSpec((DEC_N, 1024), lambda h: (0, 0)), same, same, head_tab, head_tab, st_spec],
        out_specs=[out_spec, st_spec],
        out_shape=_dec_out_shapes(),
        scratch_shapes=[pltpu.VMEM((1024, DEC_N), F32), pltpu.VMEM((2, HEAD_DIM, DEC_N), F32),
                        pltpu.VMEM((HEAD_DIM, HEAD_DIM, DEC_N), F32)],
        compiler_params=_cparams(("arbitrary",), 40),
        name="ret_decode",
    )(cols, cos_c, sin_c, _col(lp['ret_gn_w']), _col(lp['ret_gn_b']), state.reshape(DEC_N, -1))
    return _dec_finish(o_t, s_new)


def _hgrn_decode(cols, state, lp):
    head_tab, st_spec, out_spec = _dec_specs()
    o_t, s_new = pl.pallas_call(
        _hgrn_dec_kernel,
        grid=(N_HEADS,),
        in_specs=[pl.BlockSpec((DEC_N, 1024), lambda h: (0, 1)), head_tab, head_tab, st_spec],
        out_specs=[out_spec, st_spec],
        out_shape=_dec_out_shapes(),
        scratch_shapes=[pltpu.VMEM((1024, DEC_N), F32), pltpu.VMEM((3, HEAD_DIM, DEC_N), F32),
                        pltpu.VMEM((HEAD_DIM, HEAD_DIM, DEC_N), F32)],
        compiler_params=_cparams(("arbitrary",), 40),
        name="hgrn_decode",
    )(cols, _col(lp['hg_lb']), _col(lp['hg_norm_w']), state.reshape(DEC_N, -1))
    return _dec_finish(o_t, s_new)


def _rwkv_decode(cols, shift, state, lp):
    head_tab, st_spec, out_spec = _dec_specs()
    row = lambda w: pl.BlockSpec((1, w), lambda h: (0, 0))
    mat = lambda r, w: pl.BlockSpec((r, w), lambda h: (0, 0))
    o_t, s_new = pl.pallas_call(
        _rwkv_dec_kernel,
        grid=(N_HEADS,),
        in_specs=[pl.BlockSpec((DEC_N, 1024), lambda h: (0, 2)), mat(DEC_N, 1024),
                  row(1024), row(256), mat(64, 256), row(256), mat(64, 256), mat(128, 256),
                  row(256), row(256), head_tab, head_tab, head_tab, st_spec],
        out_specs=[out_spec, st_spec],
        out_shape=_dec_out_shapes(),
        scratch_shapes=[pltpu.VMEM((7, BRANCH_W, DEC_N), F32), pltpu.VMEM((HEAD_DIM, HEAD_DIM, DEC_N), F32),
                        pltpu.VMEM((HEAD_DIM, DEC_N), F32)],
        compiler_params=_cparams(("arbitrary",), 40),
        name="rwkv_decode",
    )(cols, shift, lp['rw_mu'], lp['rw_w0'], lp['rw_w2'], lp['rw_a0'], lp['rw_a2'], lp['rw_g2'],
      lp['rw_kk'], lp['rw_ka'], _col(lp['rw_rk']), _col(lp['rw_lnx_w']), _col(lp['rw_lnx_b']),
      state.reshape(DEC_N, -1))
    return _dec_finish(o_t, s_new)


def _lru_dec_kernel(c_ref, conv_ref, h0_ref, cw_ref, cb_ref, wa_ref, ba_ref, wx_ref, bx_ref, lam_ref,
                    y_ref, h_ref, nconv_ref):
    xb = c_ref[:, 0:BRANCH_W]
    gate = c_ref[:, BRANCH_W:2 * BRANCH_W]
    c0, c1, c2 = conv_ref[0], conv_ref[1], conv_ref[2]
    xc = (c0 * cw_ref[0:1, :] + c1 * cw_ref[1:2, :] + c2 * cw_ref[2:3, :] + xb * cw_ref[3:4, :]) + cb_ref[...]
    a, u = _lru_gates(xc, wa_ref, ba_ref, wx_ref, bx_ref, lam_ref)
    hn = a * h0_ref[...] + u
    h_ref[...] = hn
    y_ref[...] = hn * _gelu(gate)
    nconv_ref[0] = c1
    nconv_ref[1] = c2
    nconv_ref[2] = xb


def _lru_decode(cols, conv, h0, lp):
    full = lambda *s: pl.BlockSpec(s, lambda i: (0,) * len(s))
    return pl.pallas_call(
        _lru_dec_kernel,
        grid=(1,),
        in_specs=[pl.BlockSpec((DEC_N, 512), lambda i: (0, 6)), full(3, DEC_N, BRANCH_W), full(DEC_N, BRANCH_W),
                  full(CONV_W, BRANCH_W), full(1, BRANCH_W), full(BRANCH_W, BRANCH_W), full(1, BRANCH_W),
                  full(BRANCH_W, BRANCH_W), full(1, BRANCH_W), full(1, BRANCH_W)],
        out_specs=[full(DEC_N, BRANCH_W), full(DEC_N, BRANCH_W), full(3, DEC_N, BRANCH_W)],
        out_shape=[jax.ShapeDtypeStruct((DEC_N, BRANCH_W), F32), jax.ShapeDtypeStruct((DEC_N, BRANCH_W), F32),
                   jax.ShapeDtypeStruct((3, DEC_N, BRANCH_W), F32)],
        compiler_params=_cparams(("arbitrary",), 32),
        name="lru_decode",
    )(cols, conv, h0, lp['lru_conv_w'], lp['lru_conv_b'], lp['lru_wa_bd'], lp['lru_ba'], lp['lru_wx_bd'],
      lp['lru_bx'], lp['lru_lambda'])


def _prompt_layer(x, p_emb, lp, cos_t, sin_t):
    b, t, d = x.shape
    n = b * t
    xf = x.reshape(n, d)
    cols3 = _proj(xf, lp['w_in_br'], 512).reshape(b, t, BRANCH_COLS)
    o_a, s_ret = _ret_prompt(cols3, cos_t, sin_t, lp)
    o_b, s_hg = _hgrn_prompt(cols3, lp['hg_lb'], lp)
    o_c, s_shift, s_rw = _rwkv_prompt(cols3, lp)
    o_d, s_lru, s_conv = _lru_prompt(cols3, lp)
    outs = [o.reshape(n, BRANCH_W) for o in (o_a, o_b, o_c, o_d)]
    x1 = _mix(xf, outs, lp, 256)
    x2 = _peer(x1, lp, 512)
    x3 = _ple(x2, p_emb.reshape(n, -1), lp, 512)
    return x3.reshape(b, t, d), (s_ret, s_hg, s_rw, s_shift[:, 0], s_lru[:, 0], s_conv)


def _sample_layer(x, p_emb, state, lp, cos_c, sin_c):
    s_ret, s_hg, s_rw, s_shift, s_lru, s_conv = state
    xf = x.reshape(DEC_N, D_MODEL)
    cols = _proj(xf, lp['w_in_br'], DEC_N)
    o_a, s_ret = _ret_decode(cols, s_ret, cos_c, sin_c, lp)
    o_b, s_hg = _hgrn_decode(cols, s_hg, lp)
    o_c, s_rw = _rwkv_decode(cols, s_shift, s_rw, lp)
    o_d, s_lru, s_conv_t = _lru_decode(cols, jnp.swapaxes(s_conv, 0, 1), s_lru, lp)
    x1 = _mix(xf, [o_a, o_b, o_c, o_d], lp, DEC_N)
    x2 = _peer(x1, lp, DEC_N)
    x3 = _ple(x2, p_emb.reshape(DEC_N, -1), lp, DEC_N)
    new_shift = cols[:, 2048:3072]
    return x3.reshape(x.shape), (s_ret, s_hg, s_rw, new_shift, s_lru, jnp.swapaxes(s_conv_t, 0, 1))


def kernel(x_prompt, x_sample, state_ret, state_hgrn, state_rwkv, state_shift, state_lru, state_conv, p_prompt, p_sample, w_in, ret_gn_w, ret_gn_b, hg_lb, hg_norm_w, rw_mu, rw_w0, rw_w2, rw_a0, rw_a2, rw_g2, rw_kk, rw_ka, rw_rk, rw_lnx_w, rw_lnx_b, lru_conv_w, lru_conv_b, lru_wa, lru_ba, lru_wx, lru_bx, lru_lambda, w_branch, w_out, ln1_w, ln1_b, peer_wq, peer_keys, peer_u, peer_v, ln2_w, ln2_b, ple_w, ple_gate_w, ple_gate_b):
    params = dict(w_in=w_in, ret_gn_w=ret_gn_w, ret_gn_b=ret_gn_b, hg_norm_w=hg_norm_w, rw_mu=rw_mu, rw_w0=rw_w0,
                  rw_w2=rw_w2, rw_a0=rw_a0, rw_a2=rw_a2, rw_g2=rw_g2, rw_kk=rw_kk, rw_ka=rw_ka, rw_rk=rw_rk,
                  rw_lnx_w=rw_lnx_w, rw_lnx_b=rw_lnx_b, lru_conv_w=lru_conv_w, lru_conv_b=lru_conv_b,
                  lru_wa=lru_wa, lru_ba=lru_ba, lru_wx=lru_wx, lru_bx=lru_bx, lru_lambda=lru_lambda,
                  w_branch=w_branch, w_out=w_out, ln1_w=ln1_w, ln1_b=ln1_b, peer_wq=peer_wq, peer_keys=peer_keys,
                  peer_u=peer_u, peer_v=peer_v, ln2_w=ln2_w, ln2_b=ln2_b, ple_w=ple_w, ple_gate_w=ple_gate_w,
                  ple_gate_b=ple_gate_b)
    lb_cum = jnp.cumsum(jax.nn.softmax(hg_lb.astype(F32), axis=0), axis=0)
    lb_all = lb_cum - lb_cum[0:1]
    t_prompt = x_prompt.shape[1]
    past_len = 16384
    cos_t, sin_t = _rope_tables(jnp.arange(t_prompt))
    cos_s, sin_s = _rope_tables(past_len + jnp.arange(1))
    cos_c = _col(cos_s[0, :HEAD_DIM])
    sin_c = _col(sin_s[0, :HEAD_DIM])
    h_p, h_s = x_prompt, x_sample
    new_p, new_s = [], []
    for l in range(DEPTH):
        lp = _layer_params(params, l, lb_all)
        h_p, st_p = _prompt_layer(h_p, p_prompt[l], lp, cos_t, sin_t)
        st_in = (state_ret[l], state_hgrn[l], state_rwkv[l], state_shift[l], state_lru[l], state_conv[l])
        h_s, st_s = _sample_layer(h_s, p_sample[l], st_in, lp, cos_c, sin_c)
        new_p.append(st_p)
        new_s.append(st_s)
    outs_p = [jnp.stack(z) for z in zip(*new_p)]
    outs_s = [jnp.stack(z) for z in zip(*new_s)]
    return (h_p, h_s, *outs_p, *outs_s)
```

```python
import functools
import math

import jax
import jax.numpy as jnp
from jax import lax
from jax.experimental import pallas as pl
from jax.experimental.pallas import tpu as pltpu

F32 = jnp.float32
BF16 = jnp.bfloat16
HIGHEST = lax.Precision.HIGHEST

D_MODEL = 1024
BRANCH_W = 256
N_HEADS = 4
HEAD_DIM = 64
RET_LOG_GAMMA = tuple(math.log1p(-(2.0 ** (-5.0 - h))) for h in range(N_HEADS))
ROPE_BASE = 10000.0
RW_GN_EPS = 64e-5
LN_EPS = 1e-5
LRU_C = 8.0
CONV_W = 4
DEPTH = 2
ALPHA = (2 * DEPTH) ** 0.25
BRANCH_COLS = 3584
MIB = 1024 * 1024


def _cparams(semantics, vmem_mib):
    return pltpu.CompilerParams(dimension_semantics=semantics, vmem_limit_bytes=vmem_mib * MIB)


def _mm(a, b):
    return jnp.dot(a.astype(BF16), b.astype(BF16), preferred_element_type=F32)


def _mm_nt(a, b):
    return lax.dot_general(a.astype(BF16), b.astype(BF16), (((1,), (1,)), ((), ())), preferred_element_type=F32)


def _mm_tn(a, b):
    return lax.dot_general(a.astype(BF16), b.astype(BF16), (((0,), (0,)), ((), ())), preferred_element_type=F32)


def _transpose_exact(a):
    n = a.shape[0]
    eye = (lax.broadcasted_iota(jnp.int32, (n, n), 0) == lax.broadcasted_iota(jnp.int32, (n, n), 1)).astype(F32)
    return lax.dot_general(a, eye, (((0,), (0,)), ((), ())), preferred_element_type=F32, precision=HIGHEST)


def _select_rows_exact(sel, x):
    x1 = x.astype(BF16)
    r1 = x - x1.astype(F32)
    x2 = r1.astype(BF16)
    x3 = (r1 - x2.astype(F32)).astype(BF16)
    dot = lambda p: jnp.dot(sel, p, preferred_element_type=F32)
    return (dot(x1) + dot(x2)) + dot(x3)


def _tril_ones(n):
    r = lax.broadcasted_iota(jnp.int32, (n, n), 0)
    c = lax.broadcasted_iota(jnp.int32, (n, n), 1)
    return (r >= c).astype(BF16)


def _sigmoid(x):
    return 1.0 / (1.0 + jnp.exp(-x))


def _silu(x):
    return x * _sigmoid(x)


def _gelu(x):
    return 0.5 * x * (1.0 + jnp.tanh(0.7978845608028654 * (x + 0.044715 * (x * x * x))))


def _softplus(x):
    return jnp.maximum(x, 0.0) + jnp.log1p(jnp.exp(-jnp.abs(x)))


def _head(x, h):
    return x[:, h * HEAD_DIM:(h + 1) * HEAD_DIM]


def _block_diag(w):
    n, c, d = w.shape
    eye = jnp.eye(n, dtype=w.dtype)
    return (eye[:, None, :, None] * w[:, :, None, :]).reshape(n * c, n * d)


def _rope_tables(pos):
    half = HEAD_DIM // 2
    inv_freq = ROPE_BASE ** (-jnp.arange(half, dtype=F32) / half)
    ang = pos.astype(F32)[:, None] * inv_freq[None, :]
    cos = jnp.cos(ang)
    sin = jnp.sin(ang)
    cos_t = jnp.tile(jnp.concatenate([cos, cos], axis=-1), (1, N_HEADS))
    sin_t = jnp.tile(jnp.concatenate([-sin, sin], axis=-1), (1, N_HEADS))
    return cos_t, sin_t


def _layer_params(p, l, lb_all):
    r2 = lambda a: a.reshape(1, -1)
    w_in = p['w_in'][l]
    lp = {
        'w_in_br': w_in[:, :BRANCH_COLS].astype(BF16),
        'w_gate': w_in[:, BRANCH_COLS:].astype(BF16),
        'ret_gn_w': r2(p['ret_gn_w'][l]), 'ret_gn_b': r2(p['ret_gn_b'][l]),
        'hg_lb': r2(lb_all[l]), 'hg_norm_w': r2(p['hg_norm_w'][l]),
        'rw_mu': r2(p['rw_mu'][l]), 'rw_w0': r2(p['rw_w0'][l]), 'rw_w2': p['rw_w2'][l].astype(BF16),
        'rw_a0': r2(p['rw_a0'][l]), 'rw_a2': p['rw_a2'][l].astype(BF16), 'rw_g2': p['rw_g2'][l].astype(BF16),
        'rw_kk': r2(p['rw_kk'][l]), 'rw_ka': r2(p['rw_ka'][l]), 'rw_rk': r2(p['rw_rk'][l]),
        'rw_lnx_w': r2(p['rw_lnx_w'][l]), 'rw_lnx_b': r2(p['rw_lnx_b'][l]),
        'lru_conv_w': p['lru_conv_w'][l], 'lru_conv_b': r2(p['lru_conv_b'][l]),
        'lru_wa_bd': _block_diag(p['lru_wa'][l]).astype(BF16), 'lru_ba': r2(p['lru_ba'][l]),
        'lru_wx_bd': _block_diag(p['lru_wx'][l]).astype(BF16), 'lru_bx': r2(p['lru_bx'][l]),
        'lru_lambda': r2(p['lru_lambda'][l]),
        'w_branch': p['w_branch'][l].astype(BF16), 'w_out': p['w_out'][l].astype(BF16),
        'ln1_w': r2(p['ln1_w'][l]), 'ln1_b': r2(p['ln1_b'][l]),
        'peer_wq_t': p['peer_wq'][l].T.astype(BF16),
        'peer_keys': p['peer_keys'][l].reshape(16, 128, 128).astype(BF16),
        'peer_u': p['peer_u'][l].astype(BF16),
        'peer_v_t': jnp.swapaxes(p['peer_v'][l].astype(BF16).reshape(-1, PEER_EB, D_MODEL), 1, 2),
        'ln2_w': r2(p['ln2_w'][l]), 'ln2_b': r2(p['ln2_b'][l]),
        'ple_w': p['ple_w'][l].astype(BF16), 'ple_gate_w': p['ple_gate_w'][l].astype(BF16),
        'ple_gate_b': r2(p['ple_gate_b'][l]),
    }
    return lp


def _proj_kernel(x_ref, w_ref, o_ref):
    o_ref[...] = jnp.dot(x_ref[...].astype(BF16), w_ref[...], preferred_element_type=F32)


def _proj(x, w_bf16, tn):
    n, k = x.shape
    m = w_bf16.shape[1]
    return pl.pallas_call(
        _proj_kernel,
        grid=(n // tn,),
        in_specs=[pl.BlockSpec((tn, k), lambda i: (i, 0)),
                  pl.BlockSpec((k, m), lambda i: (0, 0))],
        out_specs=pl.BlockSpec((tn, m), lambda i: (i, 0)),
        out_shape=jax.ShapeDtypeStruct((n, m), F32),
        compiler_params=_cparams(("parallel",), 48),
        name="in_proj",
    )(x, w_bf16)


def _lru_gates(xc, wa_ref, ba_ref, wx_ref, bx_ref, lam_ref):
    r = _sigmoid(_mm(xc, wa_ref[...]) + ba_ref[...])
    i = _sigmoid(_mm(xc, wx_ref[...]) + bx_ref[...])
    log_a = -LRU_C * r * _softplus(-lam_ref[...])
    a = jnp.exp(log_a)
    u = jnp.sqrt(1.0 - jnp.exp(2.0 * log_a)) * (i * xc)
    return a, u


def _lru_kernel(c_ref, cw_ref, cb_ref, wa_ref, ba_ref, wx_ref, bx_ref, lam_ref,
                y_ref, h_ref, conv_ref, xbuf, hcar, a_s, u_s, hs, *, tc):
    c = pl.program_id(1)

    @pl.when(c == 0)
    def _():
        xbuf[0:8, :] = jnp.zeros((8, BRANCH_W), F32)
        hcar[...] = jnp.zeros((1, BRANCH_W), F32)

    xb = c_ref[:, 0:BRANCH_W]
    gate = c_ref[:, BRANCH_W:2 * BRANCH_W]
    xbuf[8:8 + tc, :] = xb
    xc = (xbuf[pl.ds(5, tc), :] * cw_ref[0:1, :] + xbuf[pl.ds(6, tc), :] * cw_ref[1:2, :]
          + xbuf[pl.ds(7, tc), :] * cw_ref[2:3, :] + xb * cw_ref[3:4, :]) + cb_ref[...]
    a, u = _lru_gates(xc, wa_ref, ba_ref, wx_ref, bx_ref, lam_ref)
    a_s[...] = a
    u_s[...] = u

    def body(t, h):
        h = a_s[pl.ds(t, 1), :] * h + u_s[pl.ds(t, 1), :]
        hs[pl.ds(t, 1), :] = h
        return h

    h = lax.fori_loop(0, tc, body, hcar[...], unroll=8)
    hcar[...] = h
    y_ref[...] = hs[...] * _gelu(gate)
    xbuf[0:8, :] = xbuf[tc:tc + 8, :]

    @pl.when(c == pl.num_programs(1) - 1)
    def _():
        h_ref[...] = h
        conv_ref[...] = xbuf[5:8, :]


def _lru_prompt(cols3, lp, tc=256):
    b, t, _ = cols3.shape
    row = lambda: pl.BlockSpec((1, BRANCH_W), lambda i, j: (0, 0))
    full = lambda r: pl.BlockSpec((r, BRANCH_W), lambda i, j: (0, 0))
    return pl.pallas_call(
        functools.partial(_lru_kernel, tc=tc),
        grid=(b, t // tc),
        in_specs=[pl.BlockSpec((None, tc, 512), lambda i, j: (i, j, 6)),
                  full(CONV_W), row(), full(BRANCH_W), row(), full(BRANCH_W), row(), row()],
        out_specs=[pl.BlockSpec((None, tc, BRANCH_W), lambda i, j: (i, j, 0)),
                   pl.BlockSpec((None, 1, BRANCH_W), lambda i, j: (i, 0, 0)),
                   pl.BlockSpec((None, CONV_W - 1, BRANCH_W), lambda i, j: (i, 0, 0))],
        out_shape=[jax.ShapeDtypeStruct((b, t, BRANCH_W), F32),
                   jax.ShapeDtypeStruct((b, 1, BRANCH_W), F32),
                   jax.ShapeDtypeStruct((b, CONV_W - 1, BRANCH_W), F32)],
        scratch_shapes=[pltpu.VMEM((tc + 8, BRANCH_W), F32), pltpu.VMEM((1, BRANCH_W), F32),
                        pltpu.VMEM((tc, BRANCH_W), F32), pltpu.VMEM((tc, BRANCH_W), F32),
                        pltpu.VMEM((tc, BRANCH_W), F32)],
        compiler_params=_cparams(("parallel", "arbitrary"), 32),
        name="lru_prompt",
    )(cols3, lp['lru_conv_w'], lp['lru_conv_b'], lp['lru_wa_bd'], lp['lru_ba'], lp['lru_wx_bd'],
      lp['lru_bx'], lp['lru_lambda'])


def _rotary(x, cos, sin_signed, first_half):
    partner = jnp.where(first_half, pltpu.roll(x, BRANCH_W - 32, 1), pltpu.roll(x, 32, 1))
    return x * cos + partner * sin_signed


def _ret_kernel(c_ref, cos_ref, sin_ref, gw_ref, gb_ref, o_ref, s_ref, S, *, tc):
    c = pl.program_id(1)

    @pl.when(c == 0)
    def _():
        S[...] = jnp.zeros(S.shape, F32)

    lane = lax.broadcasted_iota(jnp.int32, (tc, BRANCH_W), 1)
    first_half = (lane % HEAD_DIM) < (HEAD_DIM // 2)
    cos = cos_ref[...]
    sin = sin_ref[...]
    q = _rotary(c_ref[:, 0:256], cos, sin, first_half)
    k = _rotary(c_ref[:, 256:512], cos, sin, first_half) * (HEAD_DIM ** -0.5)
    v = c_ref[:, 512:768]
    g = c_ref[:, 768:1024]
    row = lax.broadcasted_iota(jnp.int32, (tc, tc), 0)
    col = lax.broadcasted_iota(jnp.int32, (tc, tc), 1)
    causal = row >= col
    dist = jnp.where(causal, row - col, 0).astype(F32)
    tpos = lax.broadcasted_iota(jnp.int32, (tc, HEAD_DIM), 0).astype(F32)
    for h in range(N_HEADS):
        lg = RET_LOG_GAMMA[h]
        qh, kh, vh = _head(q, h), _head(k, h), _head(v, h)
        decay = jnp.where(causal, jnp.exp(dist * lg), 0.0)
        scores = _mm_nt(qh, kh) * decay
        oh = _mm(scores, vh) + _mm(qh * jnp.exp((tpos + 1.0) * lg), S[h])
        S[h] = math.exp(tc * lg) * S[h] + _mm_tn(kh * jnp.exp((tc - 1.0 - tpos) * lg), vh)
        mu = jnp.mean(oh, axis=-1, keepdims=True)
        d = oh - mu
        var = jnp.mean(d * d, axis=-1, keepdims=True)
        on = d * lax.rsqrt(var + LN_EPS) * _head(gw_ref[...], h) + _head(gb_ref[...], h)
        o_ref[:, h * HEAD_DIM:(h + 1) * HEAD_DIM] = on * _silu(_head(g, h))

    @pl.when(c == pl.num_programs(1) - 1)
    def _():
        s_ref[...] = S[...]


def _ret_prompt(cols3, cos_t, sin_t, lp, tc=128):
    b, t, _ = cols3.shape
    row = lambda: pl.BlockSpec((1, BRANCH_W), lambda i, j: (0, 0))
    return pl.pallas_call(
        functools.partial(_ret_kernel, tc=tc),
        grid=(b, t // tc),
        in_specs=[pl.BlockSpec((None, tc, 1024), lambda i, j: (i, j, 0)),
                  pl.BlockSpec((tc, BRANCH_W), lambda i, j: (j, 0)),
                  pl.BlockSpec((tc, BRANCH_W), lambda i, j: (j, 0)), row(), row()],
        out_specs=[pl.BlockSpec((None, tc, BRANCH_W), lambda i, j: (i, j, 0)),
                   pl.BlockSpec((None, N_HEADS, HEAD_DIM, HEAD_DIM), lambda i, j: (i, 0, 0, 0))],
        out_shape=[jax.ShapeDtypeStruct((b, t, BRANCH_W), F32),
                   jax.ShapeDtypeStruct((b, N_HEADS, HEAD_DIM, HEAD_DIM), F32)],
        scratch_shapes=[pltpu.VMEM((N_HEADS, HEAD_DIM, HEAD_DIM), F32)],
        compiler_params=_cparams(("parallel", "arbitrary"), 32),
        name="ret_prompt",
    )(cols3, cos_t, sin_t, lp['ret_gn_w'], lp['ret_gn_b'])


HGRN_CHUNK = 64
HGRN_LEVELS = tuple(HGRN_CHUNK >> l for l in range(1, 7))


def _hgrn_select_matrix():
    t = jnp.arange(HGRN_CHUNK)
    s = jnp.arange(HGRN_CHUNK)
    blocks = [s[None, :] <= t[:, None]]
    for m in HGRN_LEVELS:
        mid = (t // (2 * m)) * (2 * m) + m - 1
        blocks.append(s[None, :] <= mid[:, None])
    return jnp.concatenate(blocks, axis=0).astype(BF16)


def _hgrn_kernel(c_ref, lb_ref, nw_ref, sel_ref, o_ref, s_ref, St, *, tc, nb):
    c = pl.program_id(1)

    @pl.when(c == 0)
    def _():
        St[...] = jnp.zeros(St.shape, F32)

    lb = lb_ref[...]
    row = lax.broadcasted_iota(jnp.int32, (tc, BRANCH_W), 0)
    r64 = lax.broadcasted_iota(jnp.int32, (tc, tc), 0)
    c64 = lax.broadcasted_iota(jnp.int32, (tc, tc), 1)
    tok = []
    for n in range(nb):
        q = _silu(c_ref[n, :, 0:256])
        forget = lb + (1.0 - lb) * _sigmoid(c_ref[n, :, 256:512])
        kk = 1.0 - forget
        sums = _select_rows_exact(sel_ref[...], jnp.log(forget))
        tok.append(dict(q=q, kk=kk, sums=sums, b=sums[0:tc]))
    chains = [(n, h) for n in range(nb) for h in range(N_HEADS)]
    scores = {(n, h): jnp.where(r64 == c64, _mm_nt(_head(tok[n]['q'], h), _head(tok[n]['kk'], h)), 0.0)
              for n, h in chains}
    for lvl, m in enumerate(HGRN_LEVELS):
        sh = m.bit_length() - 1
        upper = ((row >> sh) & 1) == 1
        same = (r64 >> (sh + 1)) == (c64 >> (sh + 1))
        scaled = []
        for n in range(nb):
            t = tok[n]
            e = jnp.exp(-jnp.abs(t['b'] - t['sums'][(lvl + 1) * tc:(lvl + 2) * tc]))
            scaled.append((jnp.where(upper, t['q'] * e, 0.0), jnp.where(upper, 0.0, t['kk'] * e)))
        for n, h in chains:
            scores[n, h] = scores[n, h] + jnp.where(same, _mm_nt(_head(scaled[n][0], h), _head(scaled[n][1], h)), 0.0)
    for n in range(nb):
        t = tok[n]
        b = t['b']
        v = c_ref[n, :, 512:768]
        g = c_ref[n, :, 768:1024]
        b_end = b[tc - 1:tc, :]
        qd = t['q'] * jnp.exp(b)
        kd = t['kk'] * jnp.exp(b_end - b)
        for h in range(N_HEADS):
            vh = _head(v, h)
            oh = _mm(scores[n, h], vh) + _mm_nt(_head(qd, h), St[n, h])
            St[n, h] = St[n, h] * jnp.exp(_head(b_end, h)) + _mm_tn(vh, _head(kd, h))
            ms = jnp.mean(oh * oh, axis=-1, keepdims=True)
            on = oh * lax.rsqrt(ms + LN_EPS) * _head(nw_ref[...], h)
            o_ref[n, :, h * HEAD_DIM:(h + 1) * HEAD_DIM] = on * _silu(_head(g, h))

    @pl.when(c == pl.num_programs(1) - 1)
    def _():
        for n in range(nb):
            for h in range(N_HEADS):
                s_ref[n, h] = _transpose_exact(St[n, h])


def _hgrn_prompt(cols3, lb, lp, nb=4):
    b, t, _ = cols3.shape
    nb = min(nb, b)
    tc = HGRN_CHUNK
    row = lambda: pl.BlockSpec((1, BRANCH_W), lambda i, j: (0, 0))
    return pl.pallas_call(
        functools.partial(_hgrn_kernel, tc=tc, nb=nb),
        grid=(b // nb, t // tc),
        in_specs=[pl.BlockSpec((nb, tc, 1024), lambda i, j: (i, j, 1)), row(), row(),
                  pl.BlockSpec((7 * tc, tc), lambda i, j: (0, 0))],
        out_specs=[pl.BlockSpec((nb, tc, BRANCH_W), lambda i, j: (i, j, 0)),
                   pl.BlockSpec((nb, N_HEADS, HEAD_DIM, HEAD_DIM), lambda i, j: (i, 0, 0, 0))],
        out_shape=[jax.ShapeDtypeStruct((b, t, BRANCH_W), F32),
                   jax.ShapeDtypeStruct((b, N_HEADS, HEAD_DIM, HEAD_DIM), F32)],
        scratch_shapes=[pltpu.VMEM((nb, N_HEADS, HEAD_DIM, HEAD_DIM), F32)],
        compiler_params=_cparams(("parallel", "arbitrary"), 32),
        name="hgrn_prompt",
    )(cols3, lb, lp['hg_norm_w'], _hgrn_select_matrix())


def _rwkv_token_mix(cols, prev, mu_ref, w0_ref, w2_ref, a0_ref, a2_ref, g2_ref, kkw_ref, ka_ref):
    xs = cols + mu_ref[...] * (prev - cols)
    r = xs[:, 0:256]
    k = xs[:, 256:512]
    v = xs[:, 512:768]
    xg = xs[:, 768:896]
    xw = xs[:, 896:960]
    xa = xs[:, 960:1024]
    w = -_softplus(-(w0_ref[...] + _mm(jnp.tanh(xw), w2_ref[...]))) - 0.5
    lw = -jnp.exp(w)
    a = _sigmoid(a0_ref[...] + _mm(xa, a2_ref[...]))
    g = _mm(_sigmoid(xg), g2_ref[...])
    kk = k * kkw_ref[...]
    parts = []
    for h in range(N_HEADS):
        kh = _head(kk, h)
        nrm = jnp.sqrt(jnp.sum(kh * kh, axis=-1, keepdims=True))
        parts.append(kh / jnp.maximum(nrm, 1e-12))
    kkn = jnp.concatenate(parts, axis=-1)
    k2 = k * (1.0 + (a - 1.0) * ka_ref[...])
    return r, lw, k2, v, kkn, a, g


def _rwkv_out(o, r, k2, v, g, rk_ref, lw_ref, lb_ref, h):
    mu = jnp.mean(o, axis=-1, keepdims=True)
    d = o - mu
    var = jnp.mean(d * d, axis=-1, keepdims=True)
    on = d * lax.rsqrt(var + RW_GN_EPS) * _head(lw_ref[...], h) + _head(lb_ref[...], h)
    bonus = jnp.sum(_head(r, h) * _head(k2, h) * _head(rk_ref[...], h), axis=-1, keepdims=True) * _head(v, h)
    return (on + bonus) * _head(g, h)


def _rwkv_kernel(c_ref, mu_ref, w0_ref, w2_ref, a0_ref, a2_ref, g2_ref, kkw_ref, ka_ref, rk_ref,
                 lnw_ref, lnb_ref, y_ref, shift_ref, s_ref, S, last_row, *, tc, nb):
    c = pl.program_id(1)

    @pl.when(c == 0)
    def _():
        S[...] = jnp.zeros(S.shape, F32)
        last_row[...] = jnp.zeros(last_row.shape, F32)

    rr = lax.broadcasted_iota(jnp.int32, (tc, tc), 0)
    cc = lax.broadcasted_iota(jnp.int32, (tc, tc), 1)
    strict = rr > cc
    incl2 = (lax.broadcasted_iota(jnp.int32, (tc, 2 * tc), 0)
             >= (lax.broadcasted_iota(jnp.int32, (tc, 2 * tc), 1) & (tc - 1)))
    tril = _tril_ones(tc)
    n_double = max(1, (tc - 1).bit_length())
    tok = []
    for n in range(nb):
        cols = c_ref[n]
        rowi = lax.broadcasted_iota(jnp.int32, cols.shape, 0)
        prev = jnp.where(rowi == 0, last_row[n], pltpu.roll(cols, 1, 0))
        last_row[n] = cols[tc - 1:tc, :]
        r, lw, k2, v, kkn, a, g = _rwkv_token_mix(cols, prev, mu_ref, w0_ref, w2_ref, a0_ref, a2_ref, g2_ref,
                                                  kkw_ref, ka_ref)
        am = -kkn
        bm = kkn * a
        G = _select_rows_exact(tril, lw)
        g_end = G[tc - 1:tc, :]
        einv = jnp.exp(-G)
        eend = jnp.exp(g_end - G)
        tok.append(dict(r=r, k2=k2, v=v, g=g, g_end=g_end, at=am * jnp.exp(G - lw), rt=r * jnp.exp(G),
                        bt=bm * einv, kt=k2 * einv, bbar=bm * eend, kbar=k2 * eend))
    chains = [(n, h) for n in range(nb) for h in range(N_HEADS)]
    lhs = {(n, h): jnp.concatenate([_head(tok[n]['at'], h), _head(tok[n]['rt'], h)], axis=0) for n, h in chains}
    inter = {(n, h): _mm_nt(lhs[n, h], jnp.concatenate([_head(tok[n]['bt'], h), _head(tok[n]['kt'], h)], axis=0))
             for n, h in chains}
    from_state = {(n, h): _mm_nt(lhs[n, h], S[n, h]) for n, h in chains}
    npow = {ch: jnp.where(strict, inter[ch][0:tc, 0:tc], 0.0) for ch in chains}
    u = {(n, h): from_state[n, h][0:tc]
         + _mm(jnp.where(strict, inter[n, h][0:tc, tc:2 * tc], 0.0), _head(tok[n]['v'], h)) for n, h in chains}
    for j in range(n_double):
        u = {ch: u[ch] + _mm(npow[ch], u[ch]) for ch in chains}
        if j + 1 < n_double:
            npow = {ch: _mm(npow[ch], npow[ch]) for ch in chains}
    for n, h in chains:
        t = tok[n]
        uv = jnp.concatenate([u[n, h], _head(t['v'], h)], axis=0)
        a_r = jnp.where(incl2, inter[n, h][tc:2 * tc, :], 0.0)
        o = from_state[n, h][tc:2 * tc] + _mm(a_r, uv)
        S[n, h] = (S[n, h] * jnp.exp(_head(t['g_end'], h))
                   + _mm_tn(uv, jnp.concatenate([_head(t['bbar'], h), _head(t['kbar'], h)], axis=0)))
        y_ref[n, :, h * HEAD_DIM:(h + 1) * HEAD_DIM] = _rwkv_out(o, t['r'], t['k2'], t['v'], t['g'], rk_ref,
                                                                 lnw_ref, lnb_ref, h)

    @pl.when(c == pl.num_programs(1) - 1)
    def _():
        s_ref[...] = S[...]
        for n in range(nb):
            shift_ref[n] = c_ref[n, tc - 1:tc, :]


def _rwkv_prompt(cols3, lp, tc=64, nb=4):
    b, t, _ = cols3.shape
    nb = min(nb, b)
    row = lambda w: pl.BlockSpec((1, w), lambda i, j: (0, 0))
    mat = lambda r, w: pl.BlockSpec((r, w), lambda i, j: (0, 0))
    return pl.pallas_call(
        functools.partial(_rwkv_kernel, tc=tc, nb=nb),
        grid=(b // nb, t // tc),
        in_specs=[pl.BlockSpec((nb, tc, 1024), lambda i, j: (i, j, 2)),
                  row(1024), row(256), mat(64, 256), row(256), mat(64, 256), mat(128, 256),
                  row(256), row(256), row(256), row(256), row(256)],
        out_specs=[pl.BlockSpec((nb, tc, BRANCH_W), lambda i, j: (i, j, 0)),
                   pl.BlockSpec((nb, 1, 1024), lambda i, j: (i, 0, 0)),
                   pl.BlockSpec((nb, N_HEADS, HEAD_DIM, HEAD_DIM), lambda i, j: (i, 0, 0, 0))],
        out_shape=[jax.ShapeDtypeStruct((b, t, BRANCH_W), F32),
                   jax.ShapeDtypeStruct((b, 1, 1024), F32),
                   jax.ShapeDtypeStruct((b, N_HEADS, HEAD_DIM, HEAD_DIM), F32)],
        scratch_shapes=[pltpu.VMEM((nb, N_HEADS, HEAD_DIM, HEAD_DIM), F32), pltpu.VMEM((nb, 1, 1024), F32)],
        compiler_params=_cparams(("parallel", "arbitrary"), 32),
        name="rwkv_prompt",
    )(cols3, lp['rw_mu'], lp['rw_w0'], lp['rw_w2'], lp['rw_a0'], lp['rw_a2'], lp['rw_g2'],
      lp['rw_kk'], lp['rw_ka'], lp['rw_rk'], lp['rw_lnx_w'], lp['rw_lnx_b'])


def _layer_norm(z, w, b):
    mu = jnp.mean(z, axis=-1, keepdims=True)
    d = z - mu
    var = jnp.mean(d * d, axis=-1, keepdims=True)
    return d * lax.rsqrt(var + LN_EPS) * w + b


def _mix_kernel(x_ref, oa_ref, ob_ref, oc_ref, od_ref, wg_ref, wb_ref, wo_ref, lw_ref, lb_ref, out_ref):
    x = x_ref[...]
    xb = x.astype(BF16)
    mixed = None
    for gi, o_ref in enumerate((oa_ref, ob_ref, oc_ref, od_ref)):
        gate = _sigmoid(jnp.dot(xb, wg_ref[:, gi * D_MODEL:(gi + 1) * D_MODEL], preferred_element_type=F32))
        up = jnp.dot(o_ref[...].astype(BF16), wb_ref[gi], preferred_element_type=F32)
        mixed = gate * up if mixed is None else mixed + gate * up
    y = jnp.dot(mixed.astype(BF16), wo_ref[...], preferred_element_type=F32)
    out_ref[...] = _layer_norm(ALPHA * x + y, lw_ref[...], lb_ref[...])


def _mix(x, outs, lp, tn):
    n = x.shape[0]
    tok = lambda w: pl.BlockSpec((tn, w), lambda i: (i, 0))
    const = lambda *s: pl.BlockSpec(s, lambda i: (0,) * len(s))
    return pl.pallas_call(
        _mix_kernel,
        grid=(n // tn,),
        in_specs=[tok(D_MODEL), tok(BRANCH_W), tok(BRANCH_W), tok(BRANCH_W), tok(BRANCH_W),
                  const(D_MODEL, 4 * D_MODEL), const(4, BRANCH_W, D_MODEL), const(D_MODEL, D_MODEL),
                  const(1, D_MODEL), const(1, D_MODEL)],
        out_specs=tok(D_MODEL),
        out_shape=jax.ShapeDtypeStruct((n, D_MODEL), F32),
        compiler_params=_cparams(("parallel",), 48),
        name="mix_ln1",
    )(x, *outs, lp['w_gate'], lp['w_branch'], lp['w_out'], lp['ln1_w'], lp['ln1_b'])


def _ple_kernel(x_ref, p_ref, wg_ref, bg_ref, wp_ref, out_ref):
    x = x_ref[...]
    gate = _sigmoid(jnp.dot(x.astype(BF16), wg_ref[...], preferred_element_type=F32) + bg_ref[...])
    emb = jnp.dot(p_ref[...].astype(BF16), wp_ref[...], preferred_element_type=F32)
    out_ref[...] = x + gate * emb


def _ple(x, p_emb, lp, tn):
    n = x.shape[0]
    tok = lambda w: pl.BlockSpec((tn, w), lambda i: (i, 0))
    const = lambda *s: pl.BlockSpec(s, lambda i: (0,) * len(s))
    return pl.pallas_call(
        _ple_kernel,
        grid=(n // tn,),
        in_specs=[tok(D_MODEL), tok(256), const(D_MODEL, D_MODEL), const(1, D_MODEL), const(256, D_MODEL)],
        out_specs=tok(D_MODEL),
        out_shape=jax.ShapeDtypeStruct((n, D_MODEL), F32),
        compiler_params=_cparams(("parallel",), 32),
        name="ple_gate",
    )(x, p_emb, lp['ple_gate_w'], lp['ple_gate_b'], lp['ple_w'])


PEER_HEADS = 8
PEER_NKEYS = 128
PEER_TOPK = 16
PEER_EB = 1024


def _oddeven_merge_sort_pairs(n):
    pairs = []
    p = 1
    while p < n:
        k = p
        while k >= 1:
            for j in range(k % p, n - k, 2 * k):
                for i in range(min(k, n - j - k)):
                    if (i + j) // (p * 2) == (i + j + k) // (p * 2):
                        pairs.append((i + j, i + j + k))
            k //= 2
        p *= 2
    return pairs


def _bitonic_merge_pairs(n):
    pairs = []
    k = n // 2
    while k >= 1:
        pairs.extend((i, i + k) for i in range(n) if (i & k) == 0)
        k //= 2
    return pairs


_SORT16 = _oddeven_merge_sort_pairs(PEER_TOPK)
_MERGE16 = _bitonic_merge_pairs(PEER_TOPK)
_CAND_LEN = tuple(PEER_TOPK // (a + 1) for a in range(PEER_TOPK))


def _network(vals, pairs):
    vals = list(vals)
    for i, j in pairs:
        hi = jnp.maximum(vals[i], vals[j])
        lo = jnp.minimum(vals[i], vals[j])
        vals[i], vals[j] = hi, lo
    return vals


def _top16_merge(x, y):
    return _network([jnp.maximum(x[i], y[PEER_TOPK - 1 - i]) for i in range(PEER_TOPK)], _MERGE16)


def _peer_head_stats(h, s_nat, e_nat, sk, th, r1b, e1b, g_count):
    rows0 = pl.ds(pl.multiple_of(2 * h * PEER_NKEYS, PEER_NKEYS), PEER_NKEYS)
    rows1 = pl.ds(pl.multiple_of((2 * h + 1) * PEER_NKEYS, PEER_NKEYS), PEER_NKEYS)
    for p, rows in enumerate((rows0, rows1)):
        for g in range(4):
            gg = g % g_count
            sk[:, p * 4 + g, :] = s_nat[gg, rows, :]
    groups = []
    for m in range(PEER_NKEYS // PEER_TOPK):
        groups.append(_network([sk[PEER_TOPK * m + i] for i in range(PEER_TOPK)], _SORT16))
    while len(groups) > 1:
        groups = [_top16_merge(groups[i], groups[i + 1]) for i in range(0, len(groups), 2)]
    top = groups[0]
    low = lax.broadcasted_iota(jnp.int32, (8, 128), 0) < 4
    ta = [jnp.where(low, t, pltpu.roll(t, 4, 0)) for t in top]
    tb = [jnp.where(low, pltpu.roll(t, 4, 0), t) for t in top]
    cand = [[ta[a] + tb[b] for b in range(_CAND_LEN[a])] for a in range(PEER_TOPK)]
    m1 = _network(cand[1] + [cand[a][0] for a in range(PEER_TOPK - 1, 7, -1)], _MERGE16)
    m2 = _network(cand[2] + cand[3] + cand[4] + cand[5] + cand[6], _SORT16)
    t1 = _top16_merge(cand[0], m1)
    t2 = _top16_merge(t1, m2)
    t2[15] = jnp.maximum(t2[15], cand[7][0])
    t2[14] = jnp.maximum(t2[14], cand[7][1])
    theta = t2[0]
    for t in t2[1:]:
        theta = jnp.minimum(theta, t)
    cmax = cand[0][0]
    z = jnp.zeros((8, 128), F32)
    for row in cand:
        for cv in row:
            z = z + jnp.where(cv >= theta, jnp.exp(cv - cmax), 0.0)
    inv_z = 1.0 / z
    for g in range(g_count):
        lanes = slice(g * 128, (g + 1) * 128)
        th[h, 0:1, lanes] = theta[g:g + 1, :]
        th[h, 1:2, lanes] = ta[0][g:g + 1, :]
        th[h, 2:3, lanes] = tb[0][g:g + 1, :]
        th[h, 3:4, lanes] = inv_z[g:g + 1, :]
        for b in range(PEER_TOPK):
            th[h, 8 + b:9 + b, lanes] = tb[b][g:g + 1, :]
    for g in range(g_count):
        lanes = slice(g * 128, (g + 1) * 128)
        s0 = s_nat[g, rows0, :]
        s1 = s_nat[g, rows1, :]
        e_nat[g, rows0, :] = jnp.exp(s0 - th[h, 1:2, lanes])
        e1 = jnp.exp(s1 - th[h, 2:3, lanes]) * th[h, 3:4, lanes]
        count = jnp.zeros(s0.shape, F32)
        rank = jnp.zeros(s0.shape, F32)
        for b in range(PEER_TOPK):
            sb = th[h, 8 + b:9 + b, lanes]
            count = count + jnp.where((s0 + sb) >= th[h, 0:1, lanes], 1.0, 0.0)
            rank = rank + jnp.where(sb > s1, 1.0, 0.0)
        s_nat[g, rows0, :] = count
        hrows = pl.ds(pl.multiple_of(h * PEER_NKEYS, PEER_NKEYS), PEER_NKEYS)
        r1b[g, hrows, :] = rank.astype(BF16)
        e1b[g, hrows, :] = e1.astype(BF16)


def _peer_kernel(x_ref, wq_ref, keys_ref, u_ref, vt_ref, lw_ref, lb_ref, out_ref,
                 xtb, s_nat, e_nat, sk, th, wact, wraw, actb, yt, r1b, e1b, *, tn):
    j = pl.program_id(1)
    g_count = tn // 128

    @pl.when(j == 0)
    def _():
        xtb[...] = x_ref[...].T.astype(BF16)
        for hp in range(2 * PEER_HEADS):
            qt = jnp.dot(wq_ref[hp * 128:(hp + 1) * 128, :], xtb[...], preferred_element_type=F32)
            scores = jnp.dot(keys_ref[hp], qt.astype(BF16), preferred_element_type=F32)
            for g in range(g_count):
                s_nat[g, hp * PEER_NKEYS:(hp + 1) * PEER_NKEYS, :] = scores[:, g * 128:(g + 1) * 128]

        def head_body(h, carry):
            _peer_head_stats(h, s_nat, e_nat, sk, th, r1b, e1b, g_count)
            return carry

        lax.fori_loop(0, PEER_HEADS, head_body, 0)
        yt[...] = jnp.zeros(yt.shape, F32)

    last = pl.num_programs(1) - 1
    cur = j % 2

    def step(first_stage, second_stage):
        if first_stage:
            act = jnp.dot(u_ref[...], xtb[...], preferred_element_type=F32)
            for g in range(g_count):
                actb[g] = act[:, g * 128:(g + 1) * 128].astype(BF16)
        if second_stage:
            yt[...] += jnp.dot(vt_ref[...], wact[1 - cur], preferred_element_type=F32)
        if not first_stage:
            return
        i0 = pl.multiple_of(j * (PEER_EB // PEER_NKEYS), 8)
        def row_tile(tile8, ii):
            one = jnp.broadcast_to(tile8[ii:ii + 1, :], (16, 128)).astype(BF16)
            return jnp.concatenate([one] * (PEER_NKEYS // 16), axis=0)

        for g in range(g_count):
            for ip in range(PEER_EB // PEER_NKEYS // 2):
                acc = [jnp.zeros((PEER_NKEYS, 128), BF16), jnp.zeros((PEER_NKEYS, 128), BF16)]
                for h in range(PEER_HEADS):
                    rank1 = r1b[g, h * PEER_NKEYS:(h + 1) * PEER_NKEYS, :]
                    e1 = e1b[g, h * PEER_NKEYS:(h + 1) * PEER_NKEYS, :]
                    count0 = s_nat[g, pl.ds(2 * h * PEER_NKEYS + i0, 8), :]
                    e0 = e_nat[g, pl.ds(2 * h * PEER_NKEYS + i0, 8), :]
                    for k in range(2):
                        ii = 2 * ip + k
                        sel = rank1 < row_tile(count0, ii)
                        acc[k] = acc[k] + jnp.where(sel, row_tile(e0, ii) * e1, jnp.zeros_like(e1))
                for k in range(2):
                    r0 = (2 * ip + k) * PEER_NKEYS
                    wraw[g, r0:r0 + PEER_NKEYS, :] = acc[k]
        for g in range(g_count):
            wact[cur, :, g * 128:(g + 1) * 128] = wraw[g] * _gelu(actb[g])

    @pl.when(j == 0)
    def _():
        step(True, False)

    @pl.when((j > 0) & (j < last))
    def _():
        step(True, True)

    @pl.when(j == last)
    def _():
        step(False, True)
        z = ALPHA * x_ref[...] + yt[...].T
        out_ref[...] = _layer_norm(z, lw_ref[...], lb_ref[...])


def _peer(x, lp, tn):
    n = x.shape[0]
    n_blk = lp['peer_u'].shape[0] // PEER_EB
    const = lambda *s: pl.BlockSpec(s, lambda i, j: (0,) * len(s))
    return pl.pallas_call(
        functools.partial(_peer_kernel, tn=tn),
        grid=(n // tn, n_blk + 1),
        in_specs=[pl.BlockSpec((tn, D_MODEL), lambda i, j: (i, 0)),
                  const(2 * PEER_HEADS * 128, D_MODEL), const(2 * PEER_HEADS, PEER_NKEYS, 128),
                  pl.BlockSpec((PEER_EB, D_MODEL), lambda i, j: (jnp.minimum(j, n_blk - 1), 0)),
                  pl.BlockSpec((None, D_MODEL, PEER_EB), lambda i, j: (jnp.maximum(j - 1, 0), 0, 0)),
                  const(1, D_MODEL), const(1, D_MODEL)],
        out_specs=pl.BlockSpec((tn, D_MODEL), lambda i, j: (i, 0)),
        out_shape=jax.ShapeDtypeStruct((n, D_MODEL), F32),
        scratch_shapes=[pltpu.VMEM((D_MODEL, tn), BF16),
                        pltpu.VMEM((tn // 128, 2 * PEER_HEADS * PEER_NKEYS, 128), F32),
                        pltpu.VMEM((tn // 128, 2 * PEER_HEADS * PEER_NKEYS, 128), F32),
                        pltpu.VMEM((PEER_NKEYS, 8, 128), F32),
                        pltpu.VMEM((PEER_HEADS, 8 + PEER_TOPK, tn), F32),
                        pltpu.VMEM((2, PEER_EB, tn), BF16),
                        pltpu.VMEM((tn // 128, PEER_EB, 128), BF16),
                        pltpu.VMEM((tn // 128, PEER_EB, 128), BF16),
                        pltpu.VMEM((D_MODEL, tn), F32),
                        pltpu.VMEM((tn // 128, PEER_HEADS * PEER_NKEYS, 128), BF16),
                        pltpu.VMEM((tn // 128, PEER_HEADS * PEER_NKEYS, 128), BF16)],
        compiler_params=_cparams(("parallel", "arbitrary"), 56),
        name="peer_ln2",
    )(x, lp['peer_wq_t'], lp['peer_keys'], lp['peer_u'], lp['peer_v_t'], lp['ln2_w'], lp['ln2_b'])


DEC_N = 128
HEAD_STATE = HEAD_DIM * HEAD_DIM


def _col(v):
    return jnp.broadcast_to(v.reshape(-1, 1), (v.size, DEC_N))


def _load_state_t(s_ref, st):
    st[...] = s_ref[...].T.reshape(HEAD_DIM, HEAD_DIM, DEC_N)


def _store_state_t(so_ref, st):
    so_ref[...] = st[...].reshape(HEAD_STATE, DEC_N).T


def _ret_dec_kernel(c_ref, cos_ref, sin_ref, gw_ref, gb_ref, s_ref, o_ref, so_ref, ct, qk, st):
    h = pl.program_id(0)
    ct[...] = c_ref[...].T
    r0 = pl.multiple_of(h * HEAD_DIM, HEAD_DIM)

    def rot(x):
        partner = jnp.concatenate([x[HEAD_DIM // 2:], x[:HEAD_DIM // 2]], axis=0)
        return x * cos_ref[...] + partner * sin_ref[...]

    qk[0] = rot(ct[pl.ds(r0, HEAD_DIM), :])
    qk[1] = rot(ct[pl.ds(256 + r0, HEAD_DIM), :]) * (HEAD_DIM ** -0.5)
    v = ct[pl.ds(512 + r0, HEAD_DIM), :]
    g = ct[pl.ds(768 + r0, HEAD_DIM), :]
    gamma = jnp.exp(jnp.zeros((1, 1), F32) + jnp.where(
        h == 0, RET_LOG_GAMMA[0], jnp.where(h == 1, RET_LOG_GAMMA[1],
                                            jnp.where(h == 2, RET_LOG_GAMMA[2], RET_LOG_GAMMA[3]))))
    _load_state_t(s_ref, st)

    def body(k, o):
        s_new = gamma * st[k] + qk[1, pl.ds(k, 1), :] * v
        st[k] = s_new
        return o + qk[0, pl.ds(k, 1), :] * s_new

    o = lax.fori_loop(0, HEAD_DIM, body, jnp.zeros((HEAD_DIM, DEC_N), F32))
    mu = jnp.mean(o, axis=0, keepdims=True)
    d = o - mu
    var = jnp.mean(d * d, axis=0, keepdims=True)
    o_ref[...] = (d * lax.rsqrt(var + LN_EPS) * gw_ref[...] + gb_ref[...]) * _silu(g)
    _store_state_t(so_ref, st)


def _hgrn_dec_kernel(c_ref, lb_ref, nw_ref, s_ref, o_ref, so_ref, ct, qk, st):
    h = pl.program_id(0)
    ct[...] = c_ref[...].T
    r0 = pl.multiple_of(h * HEAD_DIM, HEAD_DIM)
    lb = lb_ref[...]
    qk[0] = _silu(ct[pl.ds(r0, HEAD_DIM), :])
    forget = lb + (1.0 - lb) * _sigmoid(ct[pl.ds(256 + r0, HEAD_DIM), :])
    qk[1] = forget
    qk[2] = 1.0 - forget
    v = ct[pl.ds(512 + r0, HEAD_DIM), :]
    g = ct[pl.ds(768 + r0, HEAD_DIM), :]
    _load_state_t(s_ref, st)

    def body(k, o):
        s_new = qk[1, pl.ds(k, 1), :] * st[k] + qk[2, pl.ds(k, 1), :] * v
        st[k] = s_new
        return o + qk[0, pl.ds(k, 1), :] * s_new

    o = lax.fori_loop(0, HEAD_DIM, body, jnp.zeros((HEAD_DIM, DEC_N), F32))
    ms = jnp.mean(o * o, axis=0, keepdims=True)
    o_ref[...] = o * lax.rsqrt(ms + LN_EPS) * nw_ref[...] * _silu(g)
    _store_state_t(so_ref, st)


def _rwkv_dec_kernel(c_ref, sh_ref, mu_ref, w0_ref, w2_ref, a0_ref, a2_ref, g2_ref, kkw_ref, ka_ref,
                     rk_ref, lnw_ref, lnb_ref, s_ref, o_ref, so_ref, vt, st, osc):
    h = pl.program_id(0)
    r, lw, k2, v, kkn, a, g = _rwkv_token_mix(c_ref[...], sh_ref[...], mu_ref, w0_ref, w2_ref, a0_ref, a2_ref,
                                              g2_ref, kkw_ref, ka_ref)
    for idx, arr in enumerate((r, jnp.exp(lw), k2, v, kkn, a, g)):
        vt[idx] = arr.T
    r0 = pl.multiple_of(h * HEAD_DIM, HEAD_DIM)
    rows = pl.ds(r0, HEAD_DIM)
    rh, wh, kh, kkh, ah = vt[0, rows, :], vt[1, rows, :], vt[2, rows, :], vt[4, rows, :], vt[5, rows, :]
    vh, gh = vt[3, rows, :], vt[6, rows, :]
    kka = kkh * ah
    _load_state_t(s_ref, st)

    def body(vi, carry):
        s_old = st[vi]
        sa = jnp.sum(s_old * (-kkh), axis=0, keepdims=True)
        s_new = s_old * wh + sa * kka + vt[3, pl.ds(r0 + vi, 1), :] * kh
        st[vi] = s_new
        osc[pl.ds(vi, 1), :] = jnp.sum(s_new * rh, axis=0, keepdims=True)
        return carry

    lax.fori_loop(0, HEAD_DIM, body, 0)
    o = osc[...]
    mu = jnp.mean(o, axis=0, keepdims=True)
    d = o - mu
    var = jnp.mean(d * d, axis=0, keepdims=True)
    on = d * lax.rsqrt(var + RW_GN_EPS) * lnw_ref[...] + lnb_ref[...]
    bonus = jnp.sum(rh * kh * rk_ref[...], axis=0, keepdims=True) * vh
    o_ref[...] = (on + bonus) * gh
    _store_state_t(so_ref, st)


def _dec_specs():
    head_tab = pl.BlockSpec((HEAD_DIM, DEC_N), lambda h: (h, 0))
    state = pl.BlockSpec((DEC_N, HEAD_STATE), lambda h: (0, h))
    out = pl.BlockSpec((None, HEAD_DIM, DEC_N), lambda h: (h, 0, 0))
    return head_tab, state, out


def _dec_out_shapes():
    return [jax.ShapeDtypeStruct((N_HEADS, HEAD_DIM, DEC_N), F32),
            jax.ShapeDtypeStruct((DEC_N, N_HEADS * HEAD_STATE), F32)]


def _dec_finish(o_t, s_new):
    return o_t.reshape(BRANCH_W, DEC_N).T, s_new.reshape(DEC_N, N_HEADS, HEAD_DIM, HEAD_DIM)


def _ret_decode(cols, state, cos_c, sin_c, lp):
    head_tab, st_spec, out_spec = _dec_specs()
    same = pl.BlockSpec((HEAD_DIM, DEC_N), lambda h: (0, 0))
    o_t, s_new = pl.pallas_call(
        _ret_dec_kernel,
        grid=(N_HEADS,),
        in_specs=[pl.BlockSpec((DEC_N, 1024), lambda h: (0, 0)), same, same, head_tab, head_tab, st_spec],
        out_specs=[out_spec, st_spec],
        out_shape=_dec_out_shapes(),
        scratch_shapes=[pltpu.VMEM((1024, DEC_N), F32), pltpu.VMEM((2, HEAD_DIM, DEC_N), F32),
                        pltpu.VMEM((HEAD_DIM, HEAD_DIM, DEC_N), F32)],
        compiler_params=_cparams(("arbitrary",), 40),
        name="ret_decode",
    )(cols, cos_c, sin_c, _col(lp['ret_gn_w']), _col(lp['ret_gn_b']), state.reshape(DEC_N, -1))
    return _dec_finish(o_t, s_new)


def _hgrn_decode(cols, state, lp):
    head_tab, st_spec, out_spec = _dec_specs()
    o_t, s_new = pl.pallas_call(
        _hgrn_dec_kernel,
        grid=(N_HEADS,),
        in_specs=[pl.BlockSpec((DEC_N, 1024), lambda h: (0, 1)), head_tab, head_tab, st_spec],
        out_specs=[out_spec, st_spec],
        out_shape=_dec_out_shapes(),
        scratch_shapes=[pltpu.VMEM((1024, DEC_N), F32), pltpu.VMEM((3, HEAD_DIM, DEC_N), F32),
                        pltpu.VMEM((HEAD_DIM, HEAD_DIM, DEC_N), F32)],
        compiler_params=_cparams(("arbitrary",), 40),
        name="hgrn_decode",
    )(cols, _col(lp['hg_lb']), _col(lp['hg_norm_w']), state.reshape(DEC_N, -1))
    return _dec_finish(o_t, s_new)


def _rwkv_decode(cols, shift, state, lp):
    head_tab, st_spec, out_spec = _dec_specs()
    row = lambda w: pl.BlockSpec((1, w), lambda h: (0, 0))
    mat = lambda r, w: pl.BlockSpec((r, w), lambda h: (0, 0))
    o_t, s_new = pl.pallas_call(
        _rwkv_dec_kernel,
        grid=(N_HEADS,),
        in_specs=[pl.BlockSpec((DEC_N, 1024), lambda h: (0, 2)), mat(DEC_N, 1024),
                  row(1024), row(256), mat(64, 256), row(256), mat(64, 256), mat(128, 256),
                  row(256), row(256), head_tab, head_tab, head_tab, st_spec],
        out_specs=[out_spec, st_spec],
        out_shape=_dec_out_shapes(),
        scratch_shapes=[pltpu.VMEM((7, BRANCH_W, DEC_N), F32), pltpu.VMEM((HEAD_DIM, HEAD_DIM, DEC_N), F32),
                        pltpu.VMEM((HEAD_DIM, DEC_N), F32)],
        compiler_params=_cparams(("arbitrary",), 40),
        name="rwkv_decode",
    )(cols, shift, lp['rw_mu'], lp['rw_w0'], lp['rw_w2'], lp['rw_a0'], lp['rw_a2'], lp['rw_g2'],
      lp['rw_kk'], lp['rw_ka'], _col(lp['rw_rk']), _col(lp['rw_lnx_w']), _col(lp['rw_lnx_b']),
      state.reshape(DEC_N, -1))
    return _dec_finish(o_t, s_new)


def _lru_dec_kernel(c_ref, conv_ref, h0_ref, cw_ref, cb_ref, wa_ref, ba_ref, wx_ref, bx_ref, lam_ref,
                    y_ref, h_ref, nconv_ref):
    xb = c_ref[:, 0:BRANCH_W]
    gate = c_ref[:, BRANCH_W:2 * BRANCH_W]
    c0, c1, c2 = conv_ref[0], conv_ref[1], conv_ref[2]
    xc = (c0 * cw_ref[0:1, :] + c1 * cw_ref[1:2, :] + c2 * cw_ref[2:3, :] + xb * cw_ref[3:4, :]) + cb_ref[...]
    a, u = _lru_gates(xc, wa_ref, ba_ref, wx_ref, bx_ref, lam_ref)
    hn = a * h0_ref[...] + u
    h_ref[...] = hn
    y_ref[...] = hn * _gelu(gate)
    nconv_ref[0] = c1
    nconv_ref[1] = c2
    nconv_ref[2] = xb


def _lru_decode(cols, conv, h0, lp):
    full = lambda *s: pl.BlockSpec(s, lambda i: (0,) * len(s))
    return pl.pallas_call(
        _lru_dec_kernel,
        grid=(1,),
        in_specs=[pl.BlockSpec((DEC_N, 512), lambda i: (0, 6)), full(3, DEC_N, BRANCH_W), full(DEC_N, BRANCH_W),
                  full(CONV_W, BRANCH_W), full(1, BRANCH_W), full(BRANCH_W, BRANCH_W), full(1, BRANCH_W),
                  full(BRANCH_W, BRANCH_W), full(1, BRANCH_W), full(1, BRANCH_W)],
        out_specs=[full(DEC_N, BRANCH_W), full(DEC_N, BRANCH_W), full(3, DEC_N, BRANCH_W)],
        out_shape=[jax.ShapeDtypeStruct((DEC_N, BRANCH_W), F32), jax.ShapeDtypeStruct((DEC_N, BRANCH_W), F32),
                   jax.ShapeDtypeStruct((3, DEC_N, BRANCH_W), F32)],
        compiler_params=_cparams(("arbitrary",), 32),
        name="lru_decode",
    )(cols, conv, h0, lp['lru_conv_w'], lp['lru_conv_b'], lp['lru_wa_bd'], lp['lru_ba'], lp['lru_wx_bd'],
      lp['lru_bx'], lp['lru_lambda'])


def _prompt_layer(x, p_emb, lp, cos_t, sin_t):
    b, t, d = x.shape
    n = b * t
    xf = x.reshape(n, d)
    cols3 = _proj(xf, lp['w_in_br'], 512).reshape(b, t, BRANCH_COLS)
    o_a, s_ret = _ret_prompt(cols3, cos_t, sin_t, lp)
    o_b, s_hg = _hgrn_prompt(cols3, lp['hg_lb'], lp)
    o_c, s_shift, s_rw = _rwkv_prompt(cols3, lp)
    o_d, s_lru, s_conv = _lru_prompt(cols3, lp)
    outs = [o.reshape(n, BRANCH_W) for o in (o_a, o_b, o_c, o_d)]
    x1 = _mix(xf, outs, lp, 512)
    x2 = _peer(x1, lp, 512)
    x3 = _ple(x2, p_emb.reshape(n, -1), lp, 512)
    return x3.reshape(b, t, d), (s_ret, s_hg, s_rw, s_shift[:, 0], s_lru[:, 0], s_conv)


def _sample_layer(x, p_emb, state, lp, cos_c, sin_c):
    s_ret, s_hg, s_rw, s_shift, s_lru, s_conv = state
    xf = x.reshape(DEC_N, D_MODEL)
    cols = _proj(xf, lp['w_in_br'], DEC_N)
    o_a, s_ret = _ret_decode(cols, s_ret, cos_c, sin_c, lp)
    o_b, s_hg = _hgrn_decode(cols, s_hg, lp)
    o_c, s_rw = _rwkv_decode(cols, s_shift, s_rw, lp)
    o_d, s_lru, s_conv_t = _lru_decode(cols, jnp.swapaxes(s_conv, 0, 1), s_lru, lp)
    x1 = _mix(xf, [o_a, o_b, o_c, o_d], lp, DEC_N)
    x2 = _peer(x1, lp, DEC_N)
    x3 = _ple(x2, p_emb.reshape(DEC_N, -1), lp, DEC_N)
    new_shift = cols[:, 2048:3072]
    return x3.reshape(x.shape), (s_ret, s_hg, s_rw, new_shift, s_lru, jnp.swapaxes(s_conv_t, 0, 1))


def kernel(x_prompt, x_sample, state_ret, state_hgrn, state_rwkv, state_shift, state_lru, state_conv, p_prompt, p_sample, w_in, ret_gn_w, ret_gn_b, hg_lb, hg_norm_w, rw_mu, rw_w0, rw_w2, rw_a0, rw_a2, rw_g2, rw_kk, rw_ka, rw_rk, rw_lnx_w, rw_lnx_b, lru_conv_w, lru_conv_b, lru_wa, lru_ba, lru_wx, lru_bx, lru_lambda, w_branch, w_out, ln1_w, ln1_b, peer_wq, peer_keys, peer_u, peer_v, ln2_w, ln2_b, ple_w, ple_gate_w, ple_gate_b):
    params = dict(w_in=w_in, ret_gn_w=ret_gn_w, ret_gn_b=ret_gn_b, hg_norm_w=hg_norm_w, rw_mu=rw_mu, rw_w0=rw_w0,
                  rw_w2=rw_w2, rw_a0=rw_a0, rw_a2=rw_a2, rw_g2=rw_g2, rw_kk=rw_kk, rw_ka=rw_ka, rw_rk=rw_rk,
                  rw_lnx_w=rw_lnx_w, rw_lnx_b=rw_lnx_b, lru_conv_w=lru_conv_w, lru_conv_b=lru_conv_b,
                  lru_wa=lru_wa, lru_ba=lru_ba, lru_wx=lru_wx, lru_bx=lru_bx, lru_lambda=lru_lambda,
                  w_branch=w_branch, w_out=w_out, ln1_w=ln1_w, ln1_b=ln1_b, peer_wq=peer_wq, peer_keys=peer_keys,
                  peer_u=peer_u, peer_v=peer_v, ln2_w=ln2_w, ln2_b=ln2_b, ple_w=ple_w, ple_gate_w=ple_gate_w,
                  ple_gate_b=ple_gate_b)
    lb_cum = jnp.cumsum(jax.nn.softmax(hg_lb.astype(F32), axis=0), axis=0)
    lb_all = lb_cum - lb_cum[0:1]
    t_prompt = x_prompt.shape[1]
    past_len = 16384
    cos_t, sin_t = _rope_tables(jnp.arange(t_prompt))
    cos_s, sin_s = _rope_tables(past_len + jnp.arange(1))
    cos_c = _col(cos_s[0, :HEAD_DIM])
    sin_c = _col(sin_s[0, :HEAD_DIM])
    h_p, h_s = x_prompt, x_sample
    new_p, new_s = [], []
    for l in range(DEPTH):
        lp = _layer_params(params, l, lb_all)
        h_p, st_p = _prompt_layer(h_p, p_prompt[l], lp, cos_t, sin_t)
        st_in = (state_ret[l], state_hgrn[l], state_rwkv[l], state_shift[l], state_lru[l], state_conv[l])
        h_s, st_s = _sample_layer(h_s, p_sample[l], st_in, lp, cos_c, sin_c)
        new_p.append(st_p)
        new_s.append(st_s)
    outs_p = [jnp.stack(z) for z in zip(*new_p)]
    outs_s = [jnp.stack(z) for z in zip(*new_s)]
    return (h_p, h_s, *outs_p, *outs_s)
```

```python
import functools
import math

import jax
import jax.numpy as jnp
from jax import lax
from jax.experimental import pallas as pl
from jax.experimental.pallas import tpu as pltpu

F32 = jnp.float32
BF16 = jnp.bfloat16
HIGHEST = lax.Precision.HIGHEST

D_MODEL = 1024
BRANCH_W = 256
N_HEADS = 4
HEAD_DIM = 64
RET_LOG_GAMMA = tuple(math.log1p(-(2.0 ** (-5.0 - h))) for h in range(N_HEADS))
ROPE_BASE = 10000.0
RW_GN_EPS = 64e-5
LN_EPS = 1e-5
LRU_C = 8.0
CONV_W = 4
DEPTH = 2
ALPHA = (2 * DEPTH) ** 0.25
BRANCH_COLS = 3584
MIB = 1024 * 1024


def _cparams(semantics, vmem_mib):
    return pltpu.CompilerParams(dimension_semantics=semantics, vmem_limit_bytes=vmem_mib * MIB)


def _mm(a, b):
    return jnp.dot(a.astype(BF16), b.astype(BF16), preferred_element_type=F32)


def _mm_nt(a, b):
    return lax.dot_general(a.astype(BF16), b.astype(BF16), (((1,), (1,)), ((), ())), preferred_element_type=F32)


def _mm_tn(a, b):
    return lax.dot_general(a.astype(BF16), b.astype(BF16), (((0,), (0,)), ((), ())), preferred_element_type=F32)


def _transpose_exact(a):
    n = a.shape[0]
    eye = (lax.broadcasted_iota(jnp.int32, (n, n), 0) == lax.broadcasted_iota(jnp.int32, (n, n), 1)).astype(F32)
    return lax.dot_general(a, eye, (((0,), (0,)), ((), ())), preferred_element_type=F32, precision=HIGHEST)


def _select_rows_exact(sel, x):
    x1 = x.astype(BF16)
    r1 = x - x1.astype(F32)
    x2 = r1.astype(BF16)
    x3 = (r1 - x2.astype(F32)).astype(BF16)
    dot = lambda p: jnp.dot(sel, p, preferred_element_type=F32)
    return (dot(x1) + dot(x2)) + dot(x3)


def _tril_ones(n):
    r = lax.broadcasted_iota(jnp.int32, (n, n), 0)
    c = lax.broadcasted_iota(jnp.int32, (n, n), 1)
    return (r >= c).astype(BF16)


def _sigmoid(x):
    return 1.0 / (1.0 + jnp.exp(-x))


def _silu(x):
    return x * _sigmoid(x)


def _gelu(x):
    return 0.5 * x * (1.0 + jnp.tanh(0.7978845608028654 * (x + 0.044715 * (x * x * x))))


def _softplus(x):
    return jnp.maximum(x, 0.0) + jnp.log1p(jnp.exp(-jnp.abs(x)))


def _head(x, h):
    return x[:, h * HEAD_DIM:(h + 1) * HEAD_DIM]


def _block_diag(w):
    n, c, d = w.shape
    eye = jnp.eye(n, dtype=w.dtype)
    return (eye[:, None, :, None] * w[:, :, None, :]).reshape(n * c, n * d)


def _rope_tables(pos):
    half = HEAD_DIM // 2
    inv_freq = ROPE_BASE ** (-jnp.arange(half, dtype=F32) / half)
    ang = pos.astype(F32)[:, None] * inv_freq[None, :]
    cos = jnp.cos(ang)
    sin = jnp.sin(ang)
    cos_t = jnp.tile(jnp.concatenate([cos, cos], axis=-1), (1, N_HEADS))
    sin_t = jnp.tile(jnp.concatenate([-sin, sin], axis=-1), (1, N_HEADS))
    return cos_t, sin_t


def _layer_params(p, l, lb_all):
    r2 = lambda a: a.reshape(1, -1)
    w_in = p['w_in'][l]
    lp = {
        'w_in_br': w_in[:, :BRANCH_COLS].astype(BF16),
        'w_gate': w_in[:, BRANCH_COLS:].astype(BF16),
        'ret_gn_w': r2(p['ret_gn_w'][l]), 'ret_gn_b': r2(p['ret_gn_b'][l]),
        'hg_lb': r2(lb_all[l]), 'hg_norm_w': r2(p['hg_norm_w'][l]),
        'rw_mu': r2(p['rw_mu'][l]), 'rw_w0': r2(p['rw_w0'][l]), 'rw_w2': p['rw_w2'][l].astype(BF16),
        'rw_a0': r2(p['rw_a0'][l]), 'rw_a2': p['rw_a2'][l].astype(BF16), 'rw_g2': p['rw_g2'][l].astype(BF16),
        'rw_kk': r2(p['rw_kk'][l]), 'rw_ka': r2(p['rw_ka'][l]), 'rw_rk': r2(p['rw_rk'][l]),
        'rw_lnx_w': r2(p['rw_lnx_w'][l]), 'rw_lnx_b': r2(p['rw_lnx_b'][l]),
        'lru_conv_w': p['lru_conv_w'][l], 'lru_conv_b': r2(p['lru_conv_b'][l]),
        'lru_wa_bd': _block_diag(p['lru_wa'][l]).astype(BF16), 'lru_ba': r2(p['lru_ba'][l]),
        'lru_wx_bd': _block_diag(p['lru_wx'][l]).astype(BF16), 'lru_bx': r2(p['lru_bx'][l]),
        'lru_lambda': r2(p['lru_lambda'][l]),
        'w_branch': p['w_branch'][l].astype(BF16), 'w_out': p['w_out'][l].astype(BF16),
        'ln1_w': r2(p['ln1_w'][l]), 'ln1_b': r2(p['ln1_b'][l]),
        'peer_wq_t': p['peer_wq'][l].T.astype(BF16),
        'peer_keys': p['peer_keys'][l].reshape(16, 128, 128).astype(BF16),
        'peer_u': p['peer_u'][l].astype(BF16),
        'peer_v_t': jnp.swapaxes(p['peer_v'][l].astype(BF16).reshape(-1, PEER_EB, D_MODEL), 1, 2),
        'ln2_w': r2(p['ln2_w'][l]), 'ln2_b': r2(p['ln2_b'][l]),
        'ple_w': p['ple_w'][l].astype(BF16), 'ple_gate_w': p['ple_gate_w'][l].astype(BF16),
        'ple_gate_b': r2(p['ple_gate_b'][l]),
    }
    return lp


def _proj_kernel(x_ref, w_ref, o_ref):
    o_ref[...] = jnp.dot(x_ref[...].astype(BF16), w_ref[...], preferred_element_type=F32)


def _proj(x, w_bf16, tn):
    n, k = x.shape
    m = w_bf16.shape[1]
    return pl.pallas_call(
        _proj_kernel,
        grid=(n // tn,),
        in_specs=[pl.BlockSpec((tn, k), lambda i: (i, 0)),
                  pl.BlockSpec((k, m), lambda i: (0, 0))],
        out_specs=pl.BlockSpec((tn, m), lambda i: (i, 0)),
        out_shape=jax.ShapeDtypeStruct((n, m), F32),
        compiler_params=_cparams(("parallel",), 48),
        name="in_proj",
    )(x, w_bf16)


def _lru_gates(xc, wa_ref, ba_ref, wx_ref, bx_ref, lam_ref):
    r = _sigmoid(_mm(xc, wa_ref[...]) + ba_ref[...])
    i = _sigmoid(_mm(xc, wx_ref[...]) + bx_ref[...])
    log_a = -LRU_C * r * _softplus(-lam_ref[...])
    a = jnp.exp(log_a)
    u = jnp.sqrt(1.0 - jnp.exp(2.0 * log_a)) * (i * xc)
    return a, u


def _lru_kernel(c_ref, cw_ref, cb_ref, wa_ref, ba_ref, wx_ref, bx_ref, lam_ref,
                y_ref, h_ref, conv_ref, xbuf, hcar, a_s, u_s, hs, *, tc):
    c = pl.program_id(1)

    @pl.when(c == 0)
    def _():
        xbuf[0:8, :] = jnp.zeros((8, BRANCH_W), F32)
        hcar[...] = jnp.zeros((1, BRANCH_W), F32)

    xb = c_ref[:, 0:BRANCH_W]
    gate = c_ref[:, BRANCH_W:2 * BRANCH_W]
    xbuf[8:8 + tc, :] = xb
    xc = (xbuf[pl.ds(5, tc), :] * cw_ref[0:1, :] + xbuf[pl.ds(6, tc), :] * cw_ref[1:2, :]
          + xbuf[pl.ds(7, tc), :] * cw_ref[2:3, :] + xb * cw_ref[3:4, :]) + cb_ref[...]
    a, u = _lru_gates(xc, wa_ref, ba_ref, wx_ref, bx_ref, lam_ref)
    a_s[...] = a
    u_s[...] = u

    def body(t, h):
        h = a_s[pl.ds(t, 1), :] * h + u_s[pl.ds(t, 1), :]
        hs[pl.ds(t, 1), :] = h
        return h

    h = lax.fori_loop(0, tc, body, hcar[...], unroll=8)
    hcar[...] = h
    y_ref[...] = hs[...] * _gelu(gate)
    xbuf[0:8, :] = xbuf[tc:tc + 8, :]

    @pl.when(c == pl.num_programs(1) - 1)
    def _():
        h_ref[...] = h
        conv_ref[...] = xbuf[5:8, :]


def _lru_prompt(cols3, lp, tc=256):
    b, t, _ = cols3.shape
    row = lambda: pl.BlockSpec((1, BRANCH_W), lambda i, j: (0, 0))
    full = lambda r: pl.BlockSpec((r, BRANCH_W), lambda i, j: (0, 0))
    return pl.pallas_call(
        functools.partial(_lru_kernel, tc=tc),
        grid=(b, t // tc),
        in_specs=[pl.BlockSpec((None, tc, 512), lambda i, j: (i, j, 6)),
                  full(CONV_W), row(), full(BRANCH_W), row(), full(BRANCH_W), row(), row()],
        out_specs=[pl.BlockSpec((None, tc, BRANCH_W), lambda i, j: (i, j, 0)),
                   pl.BlockSpec((None, 1, BRANCH_W), lambda i, j: (i, 0, 0)),
                   pl.BlockSpec((None, CONV_W - 1, BRANCH_W), lambda i, j: (i, 0, 0))],
        out_shape=[jax.ShapeDtypeStruct((b, t, BRANCH_W), F32),
                   jax.ShapeDtypeStruct((b, 1, BRANCH_W), F32),
                   jax.ShapeDtypeStruct((b, CONV_W - 1, BRANCH_W), F32)],
        scratch_shapes=[pltpu.VMEM((tc + 8, BRANCH_W), F32), pltpu.VMEM((1, BRANCH_W), F32),
                        pltpu.VMEM((tc, BRANCH_W), F32), pltpu.VMEM((tc, BRANCH_W), F32),
                        pltpu.VMEM((tc, BRANCH_W), F32)],
        compiler_params=_cparams(("parallel", "arbitrary"), 32),
        name="lru_prompt",
    )(cols3, lp['lru_conv_w'], lp['lru_conv_b'], lp['lru_wa_bd'], lp['lru_ba'], lp['lru_wx_bd'],
      lp['lru_bx'], lp['lru_lambda'])


def _rotary(x, cos, sin_signed, first_half):
    partner = jnp.where(first_half, pltpu.roll(x, BRANCH_W - 32, 1), pltpu.roll(x, 32, 1))
    return x * cos + partner * sin_signed


def _ret_kernel(c_ref, cos_ref, sin_ref, gw_ref, gb_ref, o_ref, s_ref, S, *, tc, nb):
    c = pl.program_id(1)

    @pl.when(c == 0)
    def _():
        S[...] = jnp.zeros(S.shape, F32)

    lane = lax.broadcasted_iota(jnp.int32, (tc, BRANCH_W), 1)
    first_half = (lane % HEAD_DIM) < (HEAD_DIM // 2)
    cos = cos_ref[...]
    sin = sin_ref[...]
    row = lax.broadcasted_iota(jnp.int32, (tc, tc), 0)
    col = lax.broadcasted_iota(jnp.int32, (tc, tc), 1)
    causal = row >= col
    dist = jnp.where(causal, row - col, 0).astype(F32)
    tpos = lax.broadcasted_iota(jnp.int32, (tc, HEAD_DIM), 0).astype(F32)
    decay = [jnp.where(causal, jnp.exp(dist * RET_LOG_GAMMA[h]), 0.0) for h in range(N_HEADS)]
    q_in = [jnp.exp((tpos + 1.0) * RET_LOG_GAMMA[h]) for h in range(N_HEADS)]
    k_out = [jnp.exp((tc - 1.0 - tpos) * RET_LOG_GAMMA[h]) for h in range(N_HEADS)]
    q = [_rotary(c_ref[n, :, 0:256], cos, sin, first_half) for n in range(nb)]
    k = [_rotary(c_ref[n, :, 256:512], cos, sin, first_half) * (HEAD_DIM ** -0.5) for n in range(nb)]
    chains = [(n, h) for n in range(nb) for h in range(N_HEADS)]
    scores = {(n, h): _mm_nt(_head(q[n], h), _head(k[n], h)) * decay[h] for n, h in chains}
    out = {(n, h): _mm(scores[n, h], _head(c_ref[n, :, 512:768], h)) + _mm(_head(q[n], h) * q_in[h], S[n, h])
           for n, h in chains}
    for n, h in chains:
        S[n, h] = (math.exp(tc * RET_LOG_GAMMA[h]) * S[n, h]
                   + _mm_tn(_head(k[n], h) * k_out[h], _head(c_ref[n, :, 512:768], h)))
    for n, h in chains:
        oh = out[n, h]
        mu = jnp.mean(oh, axis=-1, keepdims=True)
        d = oh - mu
        var = jnp.mean(d * d, axis=-1, keepdims=True)
        on = d * lax.rsqrt(var + LN_EPS) * _head(gw_ref[...], h) + _head(gb_ref[...], h)
        o_ref[n, :, h * HEAD_DIM:(h + 1) * HEAD_DIM] = on * _silu(_head(c_ref[n, :, 768:1024], h))

    @pl.when(c == pl.num_programs(1) - 1)
    def _():
        s_ref[...] = S[...]


def _ret_prompt(cols3, cos_t, sin_t, lp, tc=128, nb=2):
    b, t, _ = cols3.shape
    nb = min(nb, b)
    row = lambda: pl.BlockSpec((1, BRANCH_W), lambda i, j: (0, 0))
    return pl.pallas_call(
        functools.partial(_ret_kernel, tc=tc, nb=nb),
        grid=(b // nb, t // tc),
        in_specs=[pl.BlockSpec((nb, tc, 1024), lambda i, j: (i, j, 0)),
                  pl.BlockSpec((tc, BRANCH_W), lambda i, j: (j, 0)),
                  pl.BlockSpec((tc, BRANCH_W), lambda i, j: (j, 0)), row(), row()],
        out_specs=[pl.BlockSpec((nb, tc, BRANCH_W), lambda i, j: (i, j, 0)),
                   pl.BlockSpec((nb, N_HEADS, HEAD_DIM, HEAD_DIM), lambda i, j: (i, 0, 0, 0))],
        out_shape=[jax.ShapeDtypeStruct((b, t, BRANCH_W), F32),
                   jax.ShapeDtypeStruct((b, N_HEADS, HEAD_DIM, HEAD_DIM), F32)],
        scratch_shapes=[pltpu.VMEM((nb, N_HEADS, HEAD_DIM, HEAD_DIM), F32)],
        compiler_params=_cparams(("parallel", "arbitrary"), 32),
        name="ret_prompt",
    )(cols3, cos_t, sin_t, lp['ret_gn_w'], lp['ret_gn_b'])


HGRN_CHUNK = 64
HGRN_LEVELS = tuple(HGRN_CHUNK >> l for l in range(1, 7))


def _hgrn_select_matrix():
    t = jnp.arange(HGRN_CHUNK)
    s = jnp.arange(HGRN_CHUNK)
    blocks = [s[None, :] <= t[:, None]]
    for m in HGRN_LEVELS:
        mid = (t // (2 * m)) * (2 * m) + m - 1
        blocks.append(s[None, :] <= mid[:, None])
    return jnp.concatenate(blocks, axis=0).astype(BF16)


def _hgrn_kernel(c_ref, lb_ref, nw_ref, sel_ref, o_ref, s_ref, St, *, tc, nb):
    c = pl.program_id(1)

    @pl.when(c == 0)
    def _():
        St[...] = jnp.zeros(St.shape, F32)

    lb = lb_ref[...]
    row = lax.broadcasted_iota(jnp.int32, (tc, BRANCH_W), 0)
    r64 = lax.broadcasted_iota(jnp.int32, (tc, tc), 0)
    c64 = lax.broadcasted_iota(jnp.int32, (tc, tc), 1)
    tok = []
    for n in range(nb):
        q = _silu(c_ref[n, :, 0:256])
        forget = lb + (1.0 - lb) * _sigmoid(c_ref[n, :, 256:512])
        kk = 1.0 - forget
        sums = _select_rows_exact(sel_ref[...], jnp.log(forget))
        tok.append(dict(q=q, kk=kk, sums=sums, b=sums[0:tc]))
    chains = [(n, h) for n in range(nb) for h in range(N_HEADS)]
    scores = {(n, h): jnp.where(r64 == c64, _mm_nt(_head(tok[n]['q'], h), _head(tok[n]['kk'], h)), 0.0)
              for n, h in chains}
    for lvl, m in enumerate(HGRN_LEVELS):
        sh = m.bit_length() - 1
        upper = ((row >> sh) & 1) == 1
        same = (r64 >> (sh + 1)) == (c64 >> (sh + 1))
        scaled = []
        for n in range(nb):
            t = tok[n]
            e = jnp.exp(-jnp.abs(t['b'] - t['sums'][(lvl + 1) * tc:(lvl + 2) * tc]))
            scaled.append((jnp.where(upper, t['q'] * e, 0.0), jnp.where(upper, 0.0, t['kk'] * e)))
        for n, h in chains:
            scores[n, h] = scores[n, h] + jnp.where(same, _mm_nt(_head(scaled[n][0], h), _head(scaled[n][1], h)), 0.0)
    for n in range(nb):
        t = tok[n]
        b = t['b']
        v = c_ref[n, :, 512:768]
        g = c_ref[n, :, 768:1024]
        b_end = b[tc - 1:tc, :]
        qd = t['q'] * jnp.exp(b)
        kd = t['kk'] * jnp.exp(b_end - b)
        for h in range(N_HEADS):
            vh = _head(v, h)
            oh = _mm(scores[n, h], vh) + _mm_nt(_head(qd, h), St[n, h])
            St[n, h] = St[n, h] * jnp.exp(_head(b_end, h)) + _mm_tn(vh, _head(kd, h))
            ms = jnp.mean(oh * oh, axis=-1, keepdims=True)
            on = oh * lax.rsqrt(ms + LN_EPS) * _head(nw_ref[...], h)
            o_ref[n, :, h * HEAD_DIM:(h + 1) * HEAD_DIM] = on * _silu(_head(g, h))

    @pl.when(c == pl.num_programs(1) - 1)
    def _():
        for n in range(nb):
            for h in range(N_HEADS):
                s_ref[n, h] = _transpose_exact(St[n, h])


def _hgrn_prompt(cols3, lb, lp, nb=8):
    b, t, _ = cols3.shape
    nb = min(nb, b)
    tc = HGRN_CHUNK
    row = lambda: pl.BlockSpec((1, BRANCH_W), lambda i, j: (0, 0))
    return pl.pallas_call(
        functools.partial(_hgrn_kernel, tc=tc, nb=nb),
        grid=(b // nb, t // tc),
        in_specs=[pl.BlockSpec((nb, tc, 1024), lambda i, j: (i, j, 1)), row(), row(),
                  pl.BlockSpec((7 * tc, tc), lambda i, j: (0, 0))],
        out_specs=[pl.BlockSpec((nb, tc, BRANCH_W), lambda i, j: (i, j, 0)),
                   pl.BlockSpec((nb, N_HEADS, HEAD_DIM, HEAD_DIM), lambda i, j: (i, 0, 0, 0))],
        out_shape=[jax.ShapeDtypeStruct((b, t, BRANCH_W), F32),
                   jax.ShapeDtypeStruct((b, N_HEADS, HEAD_DIM, HEAD_DIM), F32)],
        scratch_shapes=[pltpu.VMEM((nb, N_HEADS, HEAD_DIM, HEAD_DIM), F32)],
        compiler_params=_cparams(("parallel", "arbitrary"), 32),
        name="hgrn_prompt",
    )(cols3, lb, lp['hg_norm_w'], _hgrn_select_matrix())


def _rwkv_token_mix(cols, prev, mu_ref, w0_ref, w2_ref, a0_ref, a2_ref, g2_ref, kkw_ref, ka_ref):
    xs = cols + mu_ref[...] * (prev - cols)
    r = xs[:, 0:256]
    k = xs[:, 256:512]
    v = xs[:, 512:768]
    xg = xs[:, 768:896]
    xw = xs[:, 896:960]
    xa = xs[:, 960:1024]
    w = -_softplus(-(w0_ref[...] + _mm(jnp.tanh(xw), w2_ref[...]))) - 0.5
    lw = -jnp.exp(w)
    a = _sigmoid(a0_ref[...] + _mm(xa, a2_ref[...]))
    g = _mm(_sigmoid(xg), g2_ref[...])
    kk = k * kkw_ref[...]
    parts = []
    for h in range(N_HEADS):
        kh = _head(kk, h)
        nrm = jnp.sqrt(jnp.sum(kh * kh, axis=-1, keepdims=True))
        parts.append(kh / jnp.maximum(nrm, 1e-12))
    kkn = jnp.concatenate(parts, axis=-1)
    k2 = k * (1.0 + (a - 1.0) * ka_ref[...])
    return r, lw, k2, v, kkn, a, g


def _rwkv_out(o, r, k2, v, g, rk_ref, lw_ref, lb_ref, h):
    mu = jnp.mean(o, axis=-1, keepdims=True)
    d = o - mu
    var = jnp.mean(d * d, axis=-1, keepdims=True)
    on = d * lax.rsqrt(var + RW_GN_EPS) * _head(lw_ref[...], h) + _head(lb_ref[...], h)
    bonus = jnp.sum(_head(r, h) * _head(k2, h) * _head(rk_ref[...], h), axis=-1, keepdims=True) * _head(v, h)
    return (on + bonus) * _head(g, h)


def _rwkv_kernel(c_ref, mu_ref, w0_ref, w2_ref, a0_ref, a2_ref, g2_ref, kkw_ref, ka_ref, rk_ref,
                 lnw_ref, lnb_ref, y_ref, shift_ref, s_ref, S, last_row, *, tc, nb):
    c = pl.program_id(1)

    @pl.when(c == 0)
    def _():
        S[...] = jnp.zeros(S.shape, F32)
        last_row[...] = jnp.zeros(last_row.shape, F32)

    rr = lax.broadcasted_iota(jnp.int32, (tc, tc), 0)
    cc = lax.broadcasted_iota(jnp.int32, (tc, tc), 1)
    strict = rr > cc
    incl2 = (lax.broadcasted_iota(jnp.int32, (tc, 2 * tc), 0)
             >= (lax.broadcasted_iota(jnp.int32, (tc, 2 * tc), 1) & (tc - 1)))
    tril = _tril_ones(tc)
    n_double = max(1, (tc - 1).bit_length())
    tok = []
    for n in range(nb):
        cols = c_ref[n]
        rowi = lax.broadcasted_iota(jnp.int32, cols.shape, 0)
        prev = jnp.where(rowi == 0, last_row[n], pltpu.roll(cols, 1, 0))
        last_row[n] = cols[tc - 1:tc, :]
        r, lw, k2, v, kkn, a, g = _rwkv_token_mix(cols, prev, mu_ref, w0_ref, w2_ref, a0_ref, a2_ref, g2_ref,
                                                  kkw_ref, ka_ref)
        am = -kkn
        bm = kkn * a
        G = _select_rows_exact(tril, lw)
        g_end = G[tc - 1:tc, :]
        einv = jnp.exp(-G)
        eend = jnp.exp(g_end - G)
        tok.append(dict(r=r, k2=k2, v=v, g=g, g_end=g_end, at=am * jnp.exp(G - lw), rt=r * jnp.exp(G),
                        bt=bm * einv, kt=k2 * einv, bbar=bm * eend, kbar=k2 * eend))
    chains = [(n, h) for n in range(nb) for h in range(N_HEADS)]
    lhs = {(n, h): jnp.concatenate([_head(tok[n]['at'], h), _head(tok[n]['rt'], h)], axis=0) for n, h in chains}
    inter = {(n, h): _mm_nt(lhs[n, h], jnp.concatenate([_head(tok[n]['bt'], h), _head(tok[n]['kt'], h)], axis=0))
             for n, h in chains}
    from_state = {(n, h): _mm_nt(lhs[n, h], S[n, h]) for n, h in chains}
    npow = {ch: jnp.where(strict, inter[ch][0:tc, 0:tc], 0.0) for ch in chains}
    u = {(n, h): from_state[n, h][0:tc]
         + _mm(jnp.where(strict, inter[n, h][0:tc, tc:2 * tc], 0.0), _head(tok[n]['v'], h)) for n, h in chains}
    for j in range(n_double):
        u = {ch: u[ch] + _mm(npow[ch], u[ch]) for ch in chains}
        if j + 1 < n_double:
            npow = {ch: _mm(npow[ch], npow[ch]) for ch in chains}
    for n, h in chains:
        t = tok[n]
        uv = jnp.concatenate([u[n, h], _head(t['v'], h)], axis=0)
        a_r = jnp.where(incl2, inter[n, h][tc:2 * tc, :], 0.0)
        o = from_state[n, h][tc:2 * tc] + _mm(a_r, uv)
        S[n, h] = (S[n, h] * jnp.exp(_head(t['g_end'], h))
                   + _mm_tn(uv, jnp.concatenate([_head(t['bbar'], h), _head(t['kbar'], h)], axis=0)))
        y_ref[n, :, h * HEAD_DIM:(h + 1) * HEAD_DIM] = _rwkv_out(o, t['r'], t['k2'], t['v'], t['g'], rk_ref,
                                                                 lnw_ref, lnb_ref, h)

    @pl.when(c == pl.num_programs(1) - 1)
    def _():
        s_ref[...] = S[...]
        for n in range(nb):
            shift_ref[n] = c_ref[n, tc - 1:tc, :]


def _rwkv_prompt(cols3, lp, tc=64, nb=4):
    b, t, _ = cols3.shape
    nb = min(nb, b)
    row = lambda w: pl.BlockSpec((1, w), lambda i, j: (0, 0))
    mat = lambda r, w: pl.BlockSpec((r, w), lambda i, j: (0, 0))
    return pl.pallas_call(
        functools.partial(_rwkv_kernel, tc=tc, nb=nb),
        grid=(b // nb, t // tc),
        in_specs=[pl.BlockSpec((nb, tc, 1024), lambda i, j: (i, j, 2)),
                  row(1024), row(256), mat(64, 256), row(256), mat(64, 256), mat(128, 256),
                  row(256), row(256), row(256), row(256), row(256)],
        out_specs=[pl.BlockSpec((nb, tc, BRANCH_W), lambda i, j: (i, j, 0)),
                   pl.BlockSpec((nb, 1, 1024), lambda i, j: (i, 0, 0)),
                   pl.BlockSpec((nb, N_HEADS, HEAD_DIM, HEAD_DIM), lambda i, j: (i, 0, 0, 0))],
        out_shape=[jax.ShapeDtypeStruct((b, t, BRANCH_W), F32),
                   jax.ShapeDtypeStruct((b, 1, 1024), F32),
                   jax.ShapeDtypeStruct((b, N_HEADS, HEAD_DIM, HEAD_DIM), F32)],
        scratch_shapes=[pltpu.VMEM((nb, N_HEADS, HEAD_DIM, HEAD_DIM), F32), pltpu.VMEM((nb, 1, 1024), F32)],
        compiler_params=_cparams(("parallel", "arbitrary"), 32),
        name="rwkv_prompt",
    )(cols3, lp['rw_mu'], lp['rw_w0'], lp['rw_w2'], lp['rw_a0'], lp['rw_a2'], lp['rw_g2'],
      lp['rw_kk'], lp['rw_ka'], lp['rw_rk'], lp['rw_lnx_w'], lp['rw_lnx_b'])


def _layer_norm(z, w, b):
    mu = jnp.mean(z, axis=-1, keepdims=True)
    d = z - mu
    var = jnp.mean(d * d, axis=-1, keepdims=True)
    return d * lax.rsqrt(var + LN_EPS) * w + b


def _mix_kernel(x_ref, oa_ref, ob_ref, oc_ref, od_ref, wg_ref, wb_ref, wo_ref, lw_ref, lb_ref, out_ref):
    x = x_ref[...]
    xb = x.astype(BF16)
    mixed = None
    for gi, o_ref in enumerate((oa_ref, ob_ref, oc_ref, od_ref)):
        gate = _sigmoid(jnp.dot(xb, wg_ref[:, gi * D_MODEL:(gi + 1) * D_MODEL], preferred_element_type=F32))
        up = jnp.dot(o_ref[...].astype(BF16), wb_ref[gi], preferred_element_type=F32)
        mixed = gate * up if mixed is None else mixed + gate * up
    y = jnp.dot(mixed.astype(BF16), wo_ref[...], preferred_element_type=F32)
    out_ref[...] = _layer_norm(ALPHA * x + y, lw_ref[...], lb_ref[...])


def _mix(x, outs, lp, tn):
    n = x.shape[0]
    tok = lambda w: pl.BlockSpec((tn, w), lambda i: (i, 0))
    const = lambda *s: pl.BlockSpec(s, lambda i: (0,) * len(s))
    return pl.pallas_call(
        _mix_kernel,
        grid=(n // tn,),
        in_specs=[tok(D_MODEL), tok(BRANCH_W), tok(BRANCH_W), tok(BRANCH_W), tok(BRANCH_W),
                  const(D_MODEL, 4 * D_MODEL), const(4, BRANCH_W, D_MODEL), const(D_MODEL, D_MODEL),
                  const(1, D_MODEL), const(1, D_MODEL)],
        out_specs=tok(D_MODEL),
        out_shape=jax.ShapeDtypeStruct((n, D_MODEL), F32),
        compiler_params=_cparams(("parallel",), 48),
        name="mix_ln1",
    )(x, *outs, lp['w_gate'], lp['w_branch'], lp['w_out'], lp['ln1_w'], lp['ln1_b'])


def _ple_kernel(x_ref, p_ref, wg_ref, bg_ref, wp_ref, out_ref):
    x = x_ref[...]
    gate = _sigmoid(jnp.dot(x.astype(BF16), wg_ref[...], preferred_element_type=F32) + bg_ref[...])
    emb = jnp.dot(p_ref[...].astype(BF16), wp_ref[...], preferred_element_type=F32)
    out_ref[...] = x + gate * emb


def _ple(x, p_emb, lp, tn):
    n = x.shape[0]
    tok = lambda w: pl.BlockSpec((tn, w), lambda i: (i, 0))
    const = lambda *s: pl.BlockSpec(s, lambda i: (0,) * len(s))
    return pl.pallas_call(
        _ple_kernel,
        grid=(n // tn,),
        in_specs=[tok(D_MODEL), tok(256), const(D_MODEL, D_MODEL), const(1, D_MODEL), const(256, D_MODEL)],
        out_specs=tok(D_MODEL),
        out_shape=jax.ShapeDtypeStruct((n, D_MODEL), F32),
        compiler_params=_cparams(("parallel",), 32),
        name="ple_gate",
    )(x, p_emb, lp['ple_gate_w'], lp['ple_gate_b'], lp['ple_w'])


PEER_HEADS = 8
PEER_NKEYS = 128
PEER_TOPK = 16
PEER_EB = 1024
PEER_SUB = 2


def _oddeven_merge_sort_pairs(n):
    pairs = []
    p = 1
    while p < n:
        k = p
        while k >= 1:
            for j in range(k % p, n - k, 2 * k):
                for i in range(min(k, n - j - k)):
                    if (i + j) // (p * 2) == (i + j + k) // (p * 2):
                        pairs.append((i + j, i + j + k))
            k //= 2
        p *= 2
    return pairs


def _bitonic_merge_pairs(n):
    pairs = []
    k = n // 2
    while k >= 1:
        pairs.extend((i, i + k) for i in range(n) if (i & k) == 0)
        k //= 2
    return pairs


_SORT16 = _oddeven_merge_sort_pairs(PEER_TOPK)
_MERGE16 = _bitonic_merge_pairs(PEER_TOPK)
_CAND_LEN = tuple(PEER_TOPK // (a + 1) for a in range(PEER_TOPK))


def _network(vals, pairs):
    vals = list(vals)
    for i, j in pairs:
        hi = jnp.maximum(vals[i], vals[j])
        lo = jnp.minimum(vals[i], vals[j])
        vals[i], vals[j] = hi, lo
    return vals


def _top16_merge(x, y):
    return _network([jnp.maximum(x[i], y[PEER_TOPK - 1 - i]) for i in range(PEER_TOPK)], _MERGE16)


def _peer_head_stats(h, s_nat, e_nat, sk, th, g_count):
    rows0 = pl.ds(pl.multiple_of(2 * h * PEER_NKEYS, PEER_NKEYS), PEER_NKEYS)
    rows1 = pl.ds(pl.multiple_of((2 * h + 1) * PEER_NKEYS, PEER_NKEYS), PEER_NKEYS)
    for p, rows in enumerate((rows0, rows1)):
        for g in range(4):
            gg = g % g_count
            sk[:, p * 4 + g, :] = s_nat[gg, rows, :]
    groups = []
    for m in range(PEER_NKEYS // PEER_TOPK):
        groups.append(_network([sk[PEER_TOPK * m + i] for i in range(PEER_TOPK)], _SORT16))
    while len(groups) > 1:
        groups = [_top16_merge(groups[i], groups[i + 1]) for i in range(0, len(groups), 2)]
    top = groups[0]
    low = lax.broadcasted_iota(jnp.int32, (8, 128), 0) < 4
    ta = [jnp.where(low, t, pltpu.roll(t, 4, 0)) for t in top]
    tb = [jnp.where(low, pltpu.roll(t, 4, 0), t) for t in top]
    cand = [[ta[a] + tb[b] for b in range(_CAND_LEN[a])] for a in range(PEER_TOPK)]
    m1 = _network(cand[1] + [cand[a][0] for a in range(PEER_TOPK - 1, 7, -1)], _MERGE16)
    m2 = _network(cand[2] + cand[3] + cand[4] + cand[5] + cand[6], _SORT16)
    t1 = _top16_merge(cand[0], m1)
    t2 = _top16_merge(t1, m2)
    t2[15] = jnp.maximum(t2[15], cand[7][0])
    t2[14] = jnp.maximum(t2[14], cand[7][1])
    theta = t2[0]
    for t in t2[1:]:
        theta = jnp.minimum(theta, t)
    cmax = cand[0][0]
    z = jnp.zeros((8, 128), F32)
    for row in cand:
        for cv in row:
            z = z + jnp.where(cv >= theta, jnp.exp(cv - cmax), 0.0)
    inv_z = 1.0 / z
    for g in range(g_count):
        lanes = slice(g * 128, (g + 1) * 128)
        th[h, 0:1, lanes] = theta[g:g + 1, :]
        th[h, 1:2, lanes] = ta[0][g:g + 1, :]
        th[h, 2:3, lanes] = tb[0][g:g + 1, :]
        th[h, 3:4, lanes] = inv_z[g:g + 1, :]
        for b in range(PEER_TOPK):
            th[h, 8 + b:9 + b, lanes] = tb[b][g:g + 1, :]
    for g in range(g_count):
        lanes = slice(g * 128, (g + 1) * 128)
        s0 = s_nat[g, rows0, :]
        e_nat[g, rows0, :] = jnp.exp(s0 - th[h, 1:2, lanes])
        e_nat[g, rows1, :] = jnp.exp(s_nat[g, rows1, :] - th[h, 2:3, lanes]) * th[h, 3:4, lanes]
        tau = jnp.full(s0.shape, jnp.inf, F32)
        for b in range(PEER_TOPK):
            sb = th[h, 8 + b:9 + b, lanes]
            tau = jnp.where((s0 + sb) >= th[h, 0:1, lanes], sb, tau)
        s_nat[g, rows0, :] = tau


def _peer_kernel(x_ref, wq_ref, keys_ref, u_ref, vt_ref, lw_ref, lb_ref, out_ref,
                 xtb, s_nat, e_nat, sk, th, wact, wraw, actb, yt, *, tn):
    j = pl.program_id(1)
    g_count = tn // 128

    @pl.when(j == 0)
    def _():
        xtb[...] = x_ref[...].T.astype(BF16)
        for hp in range(2 * PEER_HEADS):
            qt = jnp.dot(wq_ref[hp * 128:(hp + 1) * 128, :], xtb[...], preferred_element_type=F32)
            scores = jnp.dot(keys_ref[hp], qt.astype(BF16), preferred_element_type=F32)
            for g in range(g_count):
                s_nat[g, hp * PEER_NKEYS:(hp + 1) * PEER_NKEYS, :] = scores[:, g * 128:(g + 1) * 128]

        def head_body(h, carry):
            _peer_head_stats(h, s_nat, e_nat, sk, th, g_count)
            return carry

        lax.fori_loop(0, PEER_HEADS, head_body, 0)
        yt[...] = jnp.zeros(yt.shape, F32)

    last = pl.num_programs(1) - 1
    cur = j % 2

    def step(first_stage, second_stage):
        for sb in range(PEER_SUB):
            if first_stage:
                act = jnp.dot(u_ref[sb * PEER_EB:(sb + 1) * PEER_EB, :], xtb[...],
                              preferred_element_type=F32)
                for g in range(g_count):
                    actb[sb, g] = act[:, g * 128:(g + 1) * 128]
            if second_stage:
                yt[...] += jnp.dot(vt_ref[sb], wact[1 - cur, sb], preferred_element_type=F32)
            if not first_stage:
                continue
            i0 = pl.multiple_of((j * PEER_SUB + sb) * (PEER_EB // PEER_NKEYS), 8)
            jh_rows = PEER_NKEYS // 2
            for g in range(g_count):
                for jh in range(2):
                    for ip in range(PEER_EB // PEER_NKEYS // 2):
                        acc = [jnp.zeros((jh_rows, 128), F32), jnp.zeros((jh_rows, 128), F32)]
                        for h in range(PEER_HEADS):
                            base1 = (2 * h + 1) * PEER_NKEYS + jh * jh_rows
                            s1 = s_nat[g, base1:base1 + jh_rows, :]
                            e1 = e_nat[g, base1:base1 + jh_rows, :]
                            tau0 = s_nat[g, pl.ds(2 * h * PEER_NKEYS + i0, 8), :]
                            e0 = e_nat[g, pl.ds(2 * h * PEER_NKEYS + i0, 8), :]
                            for k in range(2):
                                ii = 2 * ip + k
                                sel = s1 >= tau0[ii:ii + 1, :]
                                acc[k] = acc[k] + jnp.where(sel, e0[ii:ii + 1, :] * e1, 0.0)
                        for k in range(2):
                            r0 = (2 * ip + k) * PEER_NKEYS + jh * jh_rows
                            wraw[g, r0:r0 + jh_rows, :] = acc[k]
            for g in range(g_count):
                wact[cur, sb, :, g * 128:(g + 1) * 128] = (wraw[g] * _gelu(actb[sb, g])).astype(BF16)

    @pl.when(j == 0)
    def _():
        step(True, False)

    @pl.when((j > 0) & (j < last))
    def _():
        step(True, True)

    @pl.when(j == last)
    def _():
        step(False, True)
        z = ALPHA * x_ref[...] + yt[...].T
        out_ref[...] = _layer_norm(z, lw_ref[...], lb_ref[...])


def _peer(x, lp, tn):
    n = x.shape[0]
    n_blk = lp['peer_u'].shape[0] // (PEER_EB * PEER_SUB)
    once = lambda *s: pl.BlockSpec(s, lambda i, j: (0,) * len(s), pipeline_mode=pl.Buffered(1))
    const = lambda *s: pl.BlockSpec(s, lambda i, j: (0,) * len(s))
    return pl.pallas_call(
        functools.partial(_peer_kernel, tn=tn),
        grid=(n // tn, n_blk + 1),
        in_specs=[pl.BlockSpec((tn, D_MODEL), lambda i, j: (i, 0)),
                  once(2 * PEER_HEADS * 128, D_MODEL), once(2 * PEER_HEADS, PEER_NKEYS, 128),
                  pl.BlockSpec((PEER_SUB * PEER_EB, D_MODEL), lambda i, j: (jnp.minimum(j, n_blk - 1), 0)),
                  pl.BlockSpec((PEER_SUB, D_MODEL, PEER_EB), lambda i, j: (jnp.maximum(j - 1, 0), 0, 0)),
                  const(1, D_MODEL), const(1, D_MODEL)],
        out_specs=pl.BlockSpec((tn, D_MODEL), lambda i, j: (i, 0)),
        out_shape=jax.ShapeDtypeStruct((n, D_MODEL), F32),
        scratch_shapes=[pltpu.VMEM((D_MODEL, tn), BF16),
                        pltpu.VMEM((tn // 128, 2 * PEER_HEADS * PEER_NKEYS, 128), F32),
                        pltpu.VMEM((tn // 128, 2 * PEER_HEADS * PEER_NKEYS, 128), F32),
                        pltpu.VMEM((PEER_NKEYS, 8, 128), F32),
                        pltpu.VMEM((PEER_HEADS, 8 + PEER_TOPK, tn), F32),
                        pltpu.VMEM((2, PEER_SUB, PEER_EB, tn), BF16),
                        pltpu.VMEM((tn // 128, PEER_EB, 128), F32),
                        pltpu.VMEM((PEER_SUB, tn // 128, PEER_EB, 128), F32),
                        pltpu.VMEM((D_MODEL, tn), F32)],
        compiler_params=_cparams(("parallel", "arbitrary"), 56),
        name="peer_ln2",
    )(x, lp['peer_wq_t'], lp['peer_keys'], lp['peer_u'], lp['peer_v_t'], lp['ln2_w'], lp['ln2_b'])


DEC_N = 128
HEAD_STATE = HEAD_DIM * HEAD_DIM


def _col(v):
    return jnp.broadcast_to(v.reshape(-1, 1), (v.size, DEC_N))


def _load_state_t(s_ref, st):
    st[...] = s_ref[...].T.reshape(HEAD_DIM, HEAD_DIM, DEC_N)


def _store_state_t(so_ref, st):
    so_ref[...] = st[...].reshape(HEAD_STATE, DEC_N).T


def _ret_dec_kernel(c_ref, cos_ref, sin_ref, gw_ref, gb_ref, s_ref, o_ref, so_ref, ct, qk, st):
    h = pl.program_id(0)
    ct[...] = c_ref[...].T
    r0 = pl.multiple_of(h * HEAD_DIM, HEAD_DIM)

    def rot(x):
        partner = jnp.concatenate([x[HEAD_DIM // 2:], x[:HEAD_DIM // 2]], axis=0)
        return x * cos_ref[...] + partner * sin_ref[...]

    qk[0] = rot(ct[pl.ds(r0, HEAD_DIM), :])
    qk[1] = rot(ct[pl.ds(256 + r0, HEAD_DIM), :]) * (HEAD_DIM ** -0.5)
    v = ct[pl.ds(512 + r0, HEAD_DIM), :]
    g = ct[pl.ds(768 + r0, HEAD_DIM), :]
    gamma = jnp.exp(jnp.zeros((1, 1), F32) + jnp.where(
        h == 0, RET_LOG_GAMMA[0], jnp.where(h == 1, RET_LOG_GAMMA[1],
                                            jnp.where(h == 2, RET_LOG_GAMMA[2], RET_LOG_GAMMA[3]))))
    _load_state_t(s_ref, st)

    def body(k, o):
        s_new = gamma * st[k] + qk[1, pl.ds(k, 1), :] * v
        st[k] = s_new
        return o + qk[0, pl.ds(k, 1), :] * s_new

    o = lax.fori_loop(0, HEAD_DIM, body, jnp.zeros((HEAD_DIM, DEC_N), F32))
    mu = jnp.mean(o, axis=0, keepdims=True)
    d = o - mu
    var = jnp.mean(d * d, axis=0, keepdims=True)
    o_ref[...] = (d * lax.rsqrt(var + LN_EPS) * gw_ref[...] + gb_ref[...]) * _silu(g)
    _store_state_t(so_ref, st)


def _hgrn_dec_kernel(c_ref, lb_ref, nw_ref, s_ref, o_ref, so_ref, ct, qk, st):
    h = pl.program_id(0)
    ct[...] = c_ref[...].T
    r0 = pl.multiple_of(h * HEAD_DIM, HEAD_DIM)
    lb = lb_ref[...]
    qk[0] = _silu(ct[pl.ds(r0, HEAD_DIM), :])
    forget = lb + (1.0 - lb) * _sigmoid(ct[pl.ds(256 + r0, HEAD_DIM), :])
    qk[1] = forget
    qk[2] = 1.0 - forget
    v = ct[pl.ds(512 + r0, HEAD_DIM), :]
    g = ct[pl.ds(768 + r0, HEAD_DIM), :]
    _load_state_t(s_ref, st)

    def body(k, o):
        s_new = qk[1, pl.ds(k, 1), :] * st[k] + qk[2, pl.ds(k, 1), :] * v
        st[k] = s_new
        return o + qk[0, pl.ds(k, 1), :] * s_new

    o = lax.fori_loop(0, HEAD_DIM, body, jnp.zeros((HEAD_DIM, DEC_N), F32))
    ms = jnp.mean(o * o, axis=0, keepdims=True)
    o_ref[...] = o * lax.rsqrt(ms + LN_EPS) * nw_ref[...] * _silu(g)
    _store_state_t(so_ref, st)


def _rwkv_dec_kernel(c_ref, sh_ref, mu_ref, w0_ref, w2_ref, a0_ref, a2_ref, g2_ref, kkw_ref, ka_ref,
                     rk_ref, lnw_ref, lnb_ref, s_ref, o_ref, so_ref, vt, st, osc):
    h = pl.program_id(0)
    r, lw, k2, v, kkn, a, g = _rwkv_token_mix(c_ref[...], sh_ref[...], mu_ref, w0_ref, w2_ref, a0_ref, a2_ref,
                                              g2_ref, kkw_ref, ka_ref)
    for idx, arr in enumerate((r, jnp.exp(lw), k2, v, kkn, a, g)):
        vt[idx] = arr.T
    r0 = pl.multiple_of(h * HEAD_DIM, HEAD_DIM)
    rows = pl.ds(r0, HEAD_DIM)
    rh, wh, kh, kkh, ah = vt[0, rows, :], vt[1, rows, :], vt[2, rows, :], vt[4, rows, :], vt[5, rows, :]
    vh, gh = vt[3, rows, :], vt[6, rows, :]
    kka = kkh * ah
    _load_state_t(s_ref, st)

    def body(vi, carry):
        s_old = st[vi]
        sa = jnp.sum(s_old * (-kkh), axis=0, keepdims=True)
        s_new = s_old * wh + sa * kka + vt[3, pl.ds(r0 + vi, 1), :] * kh
        st[vi] = s_new
        osc[pl.ds(vi, 1), :] = jnp.sum(s_new * rh, axis=0, keepdims=True)
        return carry

    lax.fori_loop(0, HEAD_DIM, body, 0)
    o = osc[...]
    mu = jnp.mean(o, axis=0, keepdims=True)
    d = o - mu
    var = jnp.mean(d * d, axis=0, keepdims=True)
    on = d * lax.rsqrt(var + RW_GN_EPS) * lnw_ref[...] + lnb_ref[...]
    bonus = jnp.sum(rh * kh * rk_ref[...], axis=0, keepdims=True) * vh
    o_ref[...] = (on + bonus) * gh
    _store_state_t(so_ref, st)


def _dec_specs():
    head_tab = pl.BlockSpec((HEAD_DIM, DEC_N), lambda h: (h, 0))
    state = pl.BlockSpec((DEC_N, HEAD_STATE), lambda h: (0, h))
    out = pl.BlockSpec((None, HEAD_DIM, DEC_N), lambda h: (h, 0, 0))
    return head_tab, state, out


def _dec_out_shapes():
    return [jax.ShapeDtypeStruct((N_HEADS, HEAD_DIM, DEC_N), F32),
            jax.ShapeDtypeStruct((DEC_N, N_HEADS * HEAD_STATE), F32)]


def _dec_finish(o_t, s_new):
    return o_t.reshape(BRANCH_W, DEC_N).T, s_new.reshape(DEC_N, N_HEADS, HEAD_DIM, HEAD_DIM)


def _ret_decode(cols, state, cos_c, sin_c, lp):
    head_tab, st_spec, out_spec = _dec_specs()
    same = pl.BlockSpec((HEAD_DIM, DEC_N), lambda h: (0, 0))
    o_t, s_new = pl.pallas_call(
        _ret_dec_kernel,
        grid=(N_HEADS,),
        in_specs=[pl.BlockSpec((DEC_N, 1024), lambda h: (0, 0)), same, same, head_tab, head_tab, st_spec],
        out_specs=[out_spec, st_spec],
        out_shape=_dec_out_shapes(),
        scratch_shapes=[pltpu.VMEM((1024, DEC_N), F32), pltpu.VMEM((2, HEAD_DIM, DEC_N), F32),
                        pltpu.VMEM((HEAD_DIM, HEAD_DIM, DEC_N), F32)],
        compiler_params=_cparams(("arbitrary",), 40),
        name="ret_decode",
    )(cols, cos_c, sin_c, _col(lp['ret_gn_w']), _col(lp['ret_gn_b']), state.reshape(DEC_N, -1))
    return _dec_finish(o_t, s_new)


def _hgrn_decode(cols, state, lp):
    head_tab, st_spec, out_spec = _dec_specs()
    o_t, s_new = pl.pallas_call(
        _hgrn_dec_kernel,
        grid=(N_HEADS,),
        in_specs=[pl.BlockSpec((DEC_N, 1024), lambda h: (0, 1)), head_tab, head_tab, st_spec],
        out_specs=[out_spec, st_spec],
        out_shape=_dec_out_shapes(),
        scratch_shapes=[pltpu.VMEM((1024, DEC_N), F32), pltpu.VMEM((3, HEAD_DIM, DEC_N), F32),
                        pltpu.VMEM((HEAD_DIM, HEAD_DIM, DEC_N), F32)],
        compiler_params=_cparams(("arbitrary",), 40),
        name="hgrn_decode",
    )(cols, _col(lp['hg_lb']), _col(lp['hg_norm_w']), state.reshape(DEC_N, -1))
    return _dec_finish(o_t, s_new)


def _rwkv_decode(cols, shift, state, lp):
    head_tab, st_spec, out_spec = _dec_specs()
    row = lambda w: pl.BlockSpec((1, w), lambda h: (0, 0))
    mat = lambda r, w: pl.BlockSpec((r, w), lambda h: (0, 0))
    o_t, s_new = pl.pallas_call(
        _rwkv_dec_kernel,
        grid=(N_HEADS,),
        in_specs=[pl.BlockSpec((DEC_N, 1024), lambda h: (0, 2)), mat(DEC_N, 1024),
                  row(1024), row(256), mat(64, 256), row(256), mat(64, 256), mat(128, 256),
                  row(256), row(256), head_tab, head_tab, head_tab, st_spec],
        out_specs=[out_spec, st_spec],
        out_shape=_dec_out_shapes(),
        scratch_shapes=[pltpu.VMEM((7, BRANCH_W, DEC_N), F32), pltpu.VMEM((HEAD_DIM, HEAD_DIM, DEC_N), F32),
                        pltpu.VMEM((HEAD_DIM, DEC_N), F32)],
        compiler_params=_cparams(("arbitrary",), 40),
        name="rwkv_decode",
    )(cols, shift, lp['rw_mu'], lp['rw_w0'], lp['rw_w2'], lp['rw_a0'], lp['rw_a2'], lp['rw_g2'],
      lp['rw_kk'], lp['rw_ka'], _col(lp['rw_rk']), _col(lp['rw_lnx_w']), _col(lp['rw_lnx_b']),
      state.reshape(DEC_N, -1))
    return _dec_finish(o_t, s_new)


def _lru_dec_kernel(c_ref, conv_ref, h0_ref, cw_ref, cb_ref, wa_ref, ba_ref, wx_ref, bx_ref, lam_ref,
                    y_ref, h_ref, nconv_ref):
    xb = c_ref[:, 0:BRANCH_W]
    gate = c_ref[:, BRANCH_W:2 * BRANCH_W]
    c0, c1, c2 = conv_ref[0], conv_ref[1], conv_ref[2]
    xc = (c0 * cw_ref[0:1, :] + c1 * cw_ref[1:2, :] + c2 * cw_ref[2:3, :] + xb * cw_ref[3:4, :]) + cb_ref[...]
    a, u = _lru_gates(xc, wa_ref, ba_ref, wx_ref, bx_ref, lam_ref)
    hn = a * h0_ref[...] + u
    h_ref[...] = hn
    y_ref[...] = hn * _gelu(gate)
    nconv_ref[0] = c1
    nconv_ref[1] = c2
    nconv_ref[2] = xb


def _lru_decode(cols, conv, h0, lp):
    full = lambda *s: pl.BlockSpec(s, lambda i: (0,) * len(s))
    return pl.pallas_call(
        _lru_dec_kernel,
        grid=(1,),
        in_specs=[pl.BlockSpec((DEC_N, 512), lambda i: (0, 6)), full(3, DEC_N, BRANCH_W), full(DEC_N, BRANCH_W),
                  full(CONV_W, BRANCH_W), full(1, BRANCH_W), full(BRANCH_W, BRANCH_W), full(1, BRANCH_W),
                  full(BRANCH_W, BRANCH_W), full(1, BRANCH_W), full(1, BRANCH_W)],
        out_specs=[full(DEC_N, BRANCH_W), full(DEC_N, BRANCH_W), full(3, DEC_N, BRANCH_W)],
        out_shape=[jax.ShapeDtypeStruct((DEC_N, BRANCH_W), F32), jax.ShapeDtypeStruct((DEC_N, BRANCH_W), F32),
                   jax.ShapeDtypeStruct((3, DEC_N, BRANCH_W), F32)],
        compiler_params=_cparams(("arbitrary",), 32),
        name="lru_decode",
    )(cols, conv, h0, lp['lru_conv_w'], lp['lru_conv_b'], lp['lru_wa_bd'], lp['lru_ba'], lp['lru_wx_bd'],
      lp['lru_bx'], lp['lru_lambda'])


def _prompt_layer(x, p_emb, lp, cos_t, sin_t):
    b, t, d = x.shape
    n = b * t
    xf = x.reshape(n, d)
    cols3 = _proj(xf, lp['w_in_br'], 512).reshape(b, t, BRANCH_COLS)
    o_a, s_ret = _ret_prompt(cols3, cos_t, sin_t, lp)
    o_b, s_hg = _hgrn_prompt(cols3, lp['hg_lb'], lp)
    o_c, s_shift, s_rw = _rwkv_prompt(cols3, lp)
    o_d, s_lru, s_conv = _lru_prompt(cols3, lp)
    outs = [o.reshape(n, BRANCH_W) for o in (o_a, o_b, o_c, o_d)]
    x1 = _mix(xf, outs, lp, 512)
    x2 = _peer(x1, lp, 512)
    x3 = _ple(x2, p_emb.reshape(n, -1), lp, 512)
    return x3.reshape(b, t, d), (s_ret, s_hg, s_rw, s_shift[:, 0], s_lru[:, 0], s_conv)


def _sample_layer(x, p_emb, state, lp, cos_c, sin_c):
    s_ret, s_hg, s_rw, s_shift, s_lru, s_conv = state
    xf = x.reshape(DEC_N, D_MODEL)
    cols = _proj(xf, lp['w_in_br'], DEC_N)
    o_a, s_ret = _ret_decode(cols, s_ret, cos_c, sin_c, lp)
    o_b, s_hg = _hgrn_decode(cols, s_hg, lp)
    o_c, s_rw = _rwkv_decode(cols, s_shift, s_rw, lp)
    o_d, s_lru, s_conv_t = _lru_decode(cols, jnp.swapaxes(s_conv, 0, 1), s_lru, lp)
    x1 = _mix(xf, [o_a, o_b, o_c, o_d], lp, DEC_N)
    x2 = _peer(x1, lp, DEC_N)
    x3 = _ple(x2, p_emb.reshape(DEC_N, -1), lp, DEC_N)
    new_shift = cols[:, 2048:3072]
    return x3.reshape(x.shape), (s_ret, s_hg, s_rw, new_shift, s_lru, jnp.swapaxes(s_conv_t, 0, 1))


def kernel(x_prompt, x_sample, state_ret, state_hgrn, state_rwkv, state_shift, state_lru, state_conv, p_prompt, p_sample, w_in, ret_gn_w, ret_gn_b, hg_lb, hg_norm_w, rw_mu, rw_w0, rw_w2, rw_a0, rw_a2, rw_g2, rw_kk, rw_ka, rw_rk, rw_lnx_w, rw_lnx_b, lru_conv_w, lru_conv_b, lru_wa, lru_ba, lru_wx, lru_bx, lru_lambda, w_branch, w_out, ln1_w, ln1_b, peer_wq, peer_keys, peer_u, peer_v, ln2_w, ln2_b, ple_w, ple_gate_w, ple_gate_b):
    params = dict(w_in=w_in, ret_gn_w=ret_gn_w, ret_gn_b=ret_gn_b, hg_norm_w=hg_norm_w, rw_mu=rw_mu, rw_w0=rw_w0,
                  rw_w2=rw_w2, rw_a0=rw_a0, rw_a2=rw_a2, rw_g2=rw_g2, rw_kk=rw_kk, rw_ka=rw_ka, rw_rk=rw_rk,
                  rw_lnx_w=rw_lnx_w, rw_lnx_b=rw_lnx_b, lru_conv_w=lru_conv_w, lru_conv_b=lru_conv_b,
                  lru_wa=lru_wa, lru_ba=lru_ba, lru_wx=lru_wx, lru_bx=lru_bx, lru_lambda=lru_lambda,
                  w_branch=w_branch, w_out=w_out, ln1_w=ln1_w, ln1_b=ln1_b, peer_wq=peer_wq, peer_keys=peer_keys,
                  peer_u=peer_u, peer_v=peer_v, ln2_w=ln2_w, ln2_b=ln2_b, ple_w=ple_w, ple_gate_w=ple_gate_w,
                  ple_gate_b=ple_gate_b)
    lb_cum = jnp.cumsum(jax.nn.softmax(hg_lb.astype(F32), axis=0), axis=0)
    lb_all = lb_cum - lb_cum[0:1]
    t_prompt = x_prompt.shape[1]
    past_len = 16384
    cos_t, sin_t = _rope_tables(jnp.arange(t_prompt))
    cos_s, sin_s = _rope_tables(past_len + jnp.arange(1))
    cos_c = _col(cos_s[0, :HEAD_DIM])
    sin_c = _col(sin_s[0, :HEAD_DIM])
    h_p, h_s = x_prompt, x_sample
    new_p, new_s = [], []
    for l in range(DEPTH):
        lp = _layer_params(params, l, lb_all)
        h_p, st_p = _prompt_layer(h_p, p_prompt[l], lp, cos_t, sin_t)
        st_in = (state_ret[l], state_hgrn[l], state_rwkv[l], state_shift[l], state_lru[l], state_conv[l])
        h_s, st_s = _sample_layer(h_s, p_sample[l], st_in, lp, cos_c, sin_c)
        new_p.append(st_p)
        new_s.append(st_s)
    outs_p = [jnp.stack(z) for z in zip(*new_p)]
    outs_s = [jnp.stack(z) for z in zip(*new_s)]
    return (h_p, h_s, *outs_p, *outs_s)
```

```python
import functools
import math

import jax
import jax.numpy as jnp
from jax import lax
from jax.experimental import pallas as pl
from jax.experimental.pallas import tpu as pltpu

F32 = jnp.float32
BF16 = jnp.bfloat16
HIGHEST = lax.Precision.HIGHEST

D_MODEL = 1024
BRANCH_W = 256
N_HEADS = 4
HEAD_DIM = 64
RET_LOG_GAMMA = tuple(math.log1p(-(2.0 ** (-5.0 - h))) for h in range(N_HEADS))
ROPE_BASE = 10000.0
RW_GN_EPS = 64e-5
LN_EPS = 1e-5
LRU_C = 8.0
CONV_W = 4
DEPTH = 2
ALPHA = (2 * DEPTH) ** 0.25
BRANCH_COLS = 3584
MIB = 1024 * 1024


def _cparams(semantics, vmem_mib):
    return pltpu.CompilerParams(dimension_semantics=semantics, vmem_limit_bytes=vmem_mib * MIB)


def _mm(a, b):
    return jnp.dot(a.astype(BF16), b.astype(BF16), preferred_element_type=F32)


def _mm_nt(a, b):
    return lax.dot_general(a.astype(BF16), b.astype(BF16), (((1,), (1,)), ((), ())), preferred_element_type=F32)


def _mm_tn(a, b):
    return lax.dot_general(a.astype(BF16), b.astype(BF16), (((0,), (0,)), ((), ())), preferred_element_type=F32)


def _transpose_exact(a):
    n = a.shape[0]
    eye = (lax.broadcasted_iota(jnp.int32, (n, n), 0) == lax.broadcasted_iota(jnp.int32, (n, n), 1)).astype(F32)
    return lax.dot_general(a, eye, (((0,), (0,)), ((), ())), preferred_element_type=F32, precision=HIGHEST)


def _select_rows_exact(sel, x):
    x1 = x.astype(BF16)
    r1 = x - x1.astype(F32)
    x2 = r1.astype(BF16)
    x3 = (r1 - x2.astype(F32)).astype(BF16)
    dot = lambda p: jnp.dot(sel, p, preferred_element_type=F32)
    return (dot(x1) + dot(x2)) + dot(x3)


def _tril_ones(n):
    r = lax.broadcasted_iota(jnp.int32, (n, n), 0)
    c = lax.broadcasted_iota(jnp.int32, (n, n), 1)
    return (r >= c).astype(BF16)


def _sigmoid(x):
    return 1.0 / (1.0 + jnp.exp(-x))


def _silu(x):
    return x * _sigmoid(x)


def _gelu(x):
    c = 0.7978845608028654
    half = 0.5 * x
    return half + half * jnp.tanh(x * (c + (c * 0.044715) * (x * x)))


def _softplus(x):
    return jnp.maximum(x, 0.0) + jnp.log1p(jnp.exp(-jnp.abs(x)))


def _head(x, h):
    return x[:, h * HEAD_DIM:(h + 1) * HEAD_DIM]


def _project_rows(x_ref, w_ref, nb, tc):
    x = x_ref[...].reshape(nb * tc, x_ref.shape[-1]).astype(BF16)
    cols = jnp.dot(x, w_ref[...], preferred_element_type=F32)
    return [cols[n * tc:(n + 1) * tc] for n in range(nb)]


def _block_diag(w):
    n, c, d = w.shape
    eye = jnp.eye(n, dtype=w.dtype)
    return (eye[:, None, :, None] * w[:, :, None, :]).reshape(n * c, n * d)


def _rope_tables(pos):
    half = HEAD_DIM // 2
    inv_freq = ROPE_BASE ** (-jnp.arange(half, dtype=F32) / half)
    ang = pos.astype(F32)[:, None] * inv_freq[None, :]
    cos = jnp.cos(ang)
    sin = jnp.sin(ang)
    cos_t = jnp.tile(jnp.concatenate([cos, cos], axis=-1), (1, N_HEADS))
    sin_t = jnp.tile(jnp.concatenate([-sin, sin], axis=-1), (1, N_HEADS))
    return cos_t, sin_t


def _layer_params(p, l, lb_all):
    r2 = lambda a: a.reshape(1, -1)
    w_in = p['w_in'][l]
    lp = {
        'w_in_br': w_in[:, :BRANCH_COLS].astype(BF16),
        'w_gate': w_in[:, BRANCH_COLS:].astype(BF16),
        'ret_gn_w': r2(p['ret_gn_w'][l]), 'ret_gn_b': r2(p['ret_gn_b'][l]),
        'hg_lb': r2(lb_all[l]), 'hg_norm_w': r2(p['hg_norm_w'][l]),
        'rw_mu': r2(p['rw_mu'][l]), 'rw_w0': r2(p['rw_w0'][l]), 'rw_w2': p['rw_w2'][l].astype(BF16),
        'rw_a0': r2(p['rw_a0'][l]), 'rw_a2': p['rw_a2'][l].astype(BF16), 'rw_g2': p['rw_g2'][l].astype(BF16),
        'rw_kk': r2(p['rw_kk'][l]), 'rw_ka': r2(p['rw_ka'][l]), 'rw_rk': r2(p['rw_rk'][l]),
        'rw_lnx_w': r2(p['rw_lnx_w'][l]), 'rw_lnx_b': r2(p['rw_lnx_b'][l]),
        'lru_conv_w': p['lru_conv_w'][l], 'lru_conv_b': r2(p['lru_conv_b'][l]),
        'lru_wa_bd': _block_diag(p['lru_wa'][l]).astype(BF16), 'lru_ba': r2(p['lru_ba'][l]),
        'lru_wx_bd': _block_diag(p['lru_wx'][l]).astype(BF16), 'lru_bx': r2(p['lru_bx'][l]),
        'lru_lambda': r2(p['lru_lambda'][l]),
        'w_branch': p['w_branch'][l].astype(BF16), 'w_out': p['w_out'][l].astype(BF16),
        'ln1_w': r2(p['ln1_w'][l]), 'ln1_b': r2(p['ln1_b'][l]),
        'peer_wq_t': p['peer_wq'][l].T.astype(BF16),
        'peer_keys': p['peer_keys'][l].reshape(16, 128, 128).astype(BF16),
        'peer_u': p['peer_u'][l].astype(BF16),
        'peer_v_t': jnp.swapaxes(p['peer_v'][l].astype(BF16).reshape(-1, PEER_EB, D_MODEL), 1, 2),
        'ln2_w': r2(p['ln2_w'][l]), 'ln2_b': r2(p['ln2_b'][l]),
        'ple_w': p['ple_w'][l].astype(BF16), 'ple_gate_w': p['ple_gate_w'][l].astype(BF16),
        'ple_gate_b': r2(p['ple_gate_b'][l]),
    }
    return lp


def _proj_kernel(x_ref, w_ref, o_ref):
    o_ref[...] = jnp.dot(x_ref[...].astype(BF16), w_ref[...], preferred_element_type=F32)


def _proj(x, w_bf16, tn):
    n, k = x.shape
    m = w_bf16.shape[1]
    return pl.pallas_call(
        _proj_kernel,
        grid=(n // tn,),
        in_specs=[pl.BlockSpec((tn, k), lambda i: (i, 0)),
                  pl.BlockSpec((k, m), lambda i: (0, 0))],
        out_specs=pl.BlockSpec((tn, m), lambda i: (i, 0)),
        out_shape=jax.ShapeDtypeStruct((n, m), F32),
        compiler_params=_cparams(("parallel",), 48),
        name="in_proj",
    )(x, w_bf16)


def _lru_gates(xc, wa_ref, ba_ref, wx_ref, bx_ref, lam_ref):
    r = _sigmoid(_mm(xc, wa_ref[...]) + ba_ref[...])
    i = _sigmoid(_mm(xc, wx_ref[...]) + bx_ref[...])
    log_a = -LRU_C * r * _softplus(-lam_ref[...])
    a = jnp.exp(log_a)
    u = jnp.sqrt(1.0 - jnp.exp(2.0 * log_a)) * (i * xc)
    return a, u


def _lru_kernel(x_ref, w_ref, cw_ref, cb_ref, wa_ref, ba_ref, wx_ref, bx_ref, lam_ref,
                y_ref, h_ref, conv_ref, xbuf, hcar, a_s, u_s, hs, *, tc):
    c = pl.program_id(1)

    @pl.when(c == 0)
    def _():
        xbuf[0:8, :] = jnp.zeros((8, BRANCH_W), F32)
        hcar[...] = jnp.zeros((1, BRANCH_W), F32)

    cols = jnp.dot(x_ref[...].astype(BF16), w_ref[...], preferred_element_type=F32)
    xb = cols[:, 0:BRANCH_W]
    gate = cols[:, BRANCH_W:2 * BRANCH_W]
    xbuf[8:8 + tc, :] = xb
    xc = (xbuf[pl.ds(5, tc), :] * cw_ref[0:1, :] + xbuf[pl.ds(6, tc), :] * cw_ref[1:2, :]
          + xbuf[pl.ds(7, tc), :] * cw_ref[2:3, :] + xb * cw_ref[3:4, :]) + cb_ref[...]
    a, u = _lru_gates(xc, wa_ref, ba_ref, wx_ref, bx_ref, lam_ref)
    a_s[...] = a
    u_s[...] = u

    def body(t, h):
        h = a_s[pl.ds(t, 1), :] * h + u_s[pl.ds(t, 1), :]
        hs[pl.ds(t, 1), :] = h
        return h

    h = lax.fori_loop(0, tc, body, hcar[...], unroll=8)
    hcar[...] = h
    y_ref[...] = hs[...] * _gelu(gate)
    xbuf[0:8, :] = xbuf[tc:tc + 8, :]

    @pl.when(c == pl.num_programs(1) - 1)
    def _():
        h_ref[...] = h
        conv_ref[...] = xbuf[5:8, :]


def _lru_prompt(x3, lp, tc=256):
    b, t, d = x3.shape
    row = lambda: pl.BlockSpec((1, BRANCH_W), lambda i, j: (0, 0))
    full = lambda r: pl.BlockSpec((r, BRANCH_W), lambda i, j: (0, 0))
    return pl.pallas_call(
        functools.partial(_lru_kernel, tc=tc),
        grid=(b, t // tc),
        in_specs=[pl.BlockSpec((None, tc, d), lambda i, j: (i, j, 0)),
                  pl.BlockSpec((d, 2 * BRANCH_W), lambda i, j: (0, 6)),
                  full(CONV_W), row(), full(BRANCH_W), row(), full(BRANCH_W), row(), row()],
        out_specs=[pl.BlockSpec((None, tc, BRANCH_W), lambda i, j: (i, j, 0)),
                   pl.BlockSpec((None, 1, BRANCH_W), lambda i, j: (i, 0, 0)),
                   pl.BlockSpec((None, CONV_W - 1, BRANCH_W), lambda i, j: (i, 0, 0))],
        out_shape=[jax.ShapeDtypeStruct((b, t, BRANCH_W), F32),
                   jax.ShapeDtypeStruct((b, 1, BRANCH_W), F32),
                   jax.ShapeDtypeStruct((b, CONV_W - 1, BRANCH_W), F32)],
        scratch_shapes=[pltpu.VMEM((tc + 8, BRANCH_W), F32), pltpu.VMEM((1, BRANCH_W), F32),
                        pltpu.VMEM((tc, BRANCH_W), F32), pltpu.VMEM((tc, BRANCH_W), F32),
                        pltpu.VMEM((tc, BRANCH_W), F32)],
        compiler_params=_cparams(("parallel", "arbitrary"), 32),
        name="lru_prompt",
    )(x3, lp['w_in_br'], lp['lru_conv_w'], lp['lru_conv_b'], lp['lru_wa_bd'], lp['lru_ba'], lp['lru_wx_bd'],
      lp['lru_bx'], lp['lru_lambda'])


def _rotary(x, cos, sin_signed, first_half):
    partner = jnp.where(first_half, pltpu.roll(x, BRANCH_W - 32, 1), pltpu.roll(x, 32, 1))
    return x * cos + partner * sin_signed


def _ret_kernel(x_ref, w_ref, cos_ref, sin_ref, gw_ref, gb_ref, o_ref, s_ref, S, *, tc, nb):
    c = pl.program_id(1)

    @pl.when(c == 0)
    def _():
        S[...] = jnp.zeros(S.shape, F32)

    lane = lax.broadcasted_iota(jnp.int32, (tc, BRANCH_W), 1)
    first_half = (lane % HEAD_DIM) < (HEAD_DIM // 2)
    cos = cos_ref[...]
    sin = sin_ref[...]
    row = lax.broadcasted_iota(jnp.int32, (tc, tc), 0)
    col = lax.broadcasted_iota(jnp.int32, (tc, tc), 1)
    causal = row >= col
    dist = jnp.where(causal, row - col, 0).astype(F32)
    tpos = lax.broadcasted_iota(jnp.int32, (tc, HEAD_DIM), 0).astype(F32)
    decay = [jnp.where(causal, jnp.exp(dist * RET_LOG_GAMMA[h]), 0.0) for h in range(N_HEADS)]
    q_in = [jnp.exp((tpos + 1.0) * RET_LOG_GAMMA[h]) for h in range(N_HEADS)]
    k_out = [jnp.exp((tc - 1.0 - tpos) * RET_LOG_GAMMA[h]) for h in range(N_HEADS)]
    cols = _project_rows(x_ref, w_ref, nb, tc)
    q = [_rotary(cols[n][:, 0:256], cos, sin, first_half) for n in range(nb)]
    k = [_rotary(cols[n][:, 256:512], cos, sin, first_half) * (HEAD_DIM ** -0.5) for n in range(nb)]
    chains = [(n, h) for n in range(nb) for h in range(N_HEADS)]
    scores = {(n, h): _mm_nt(_head(q[n], h), _head(k[n], h)) * decay[h] for n, h in chains}
    out = {(n, h): _mm(scores[n, h], _head(cols[n][:, 512:768], h)) + _mm(_head(q[n], h) * q_in[h], S[n, h])
           for n, h in chains}
    for n, h in chains:
        S[n, h] = (math.exp(tc * RET_LOG_GAMMA[h]) * S[n, h]
                   + _mm_tn(_head(k[n], h) * k_out[h], _head(cols[n][:, 512:768], h)))
    for n, h in chains:
        oh = out[n, h]
        mu = jnp.mean(oh, axis=-1, keepdims=True)
        d = oh - mu
        var = jnp.mean(d * d, axis=-1, keepdims=True)
        on = d * lax.rsqrt(var + LN_EPS) * _head(gw_ref[...], h) + _head(gb_ref[...], h)
        o_ref[n, :, h * HEAD_DIM:(h + 1) * HEAD_DIM] = on * _silu(_head(cols[n][:, 768:1024], h))

    @pl.when(c == pl.num_programs(1) - 1)
    def _():
        s_ref[...] = S[...]


def _ret_prompt(x3, cos_t, sin_t, lp, tc=128, nb=2):
    b, t, d = x3.shape
    nb = min(nb, b)
    row = lambda: pl.BlockSpec((1, BRANCH_W), lambda i, j: (0, 0))
    return pl.pallas_call(
        functools.partial(_ret_kernel, tc=tc, nb=nb),
        grid=(b // nb, t // tc),
        in_specs=[pl.BlockSpec((nb, tc, d), lambda i, j: (i, j, 0)),
                  pl.BlockSpec((d, 1024), lambda i, j: (0, 0)),
                  pl.BlockSpec((tc, BRANCH_W), lambda i, j: (j, 0)),
                  pl.BlockSpec((tc, BRANCH_W), lambda i, j: (j, 0)), row(), row()],
        out_specs=[pl.BlockSpec((nb, tc, BRANCH_W), lambda i, j: (i, j, 0)),
                   pl.BlockSpec((nb, N_HEADS, HEAD_DIM, HEAD_DIM), lambda i, j: (i, 0, 0, 0))],
        out_shape=[jax.ShapeDtypeStruct((b, t, BRANCH_W), F32),
                   jax.ShapeDtypeStruct((b, N_HEADS, HEAD_DIM, HEAD_DIM), F32)],
        scratch_shapes=[pltpu.VMEM((nb, N_HEADS, HEAD_DIM, HEAD_DIM), F32)],
        compiler_params=_cparams(("parallel", "arbitrary"), 32),
        name="ret_prompt",
    )(x3, lp['w_in_br'], cos_t, sin_t, lp['ret_gn_w'], lp['ret_gn_b'])


HGRN_CHUNK = 64
HGRN_LEVELS = tuple(HGRN_CHUNK >> l for l in range(1, 7))


def _hgrn_select_matrix():
    t = jnp.arange(HGRN_CHUNK)
    s = jnp.arange(HGRN_CHUNK)
    blocks = [s[None, :] <= t[:, None]]
    for m in HGRN_LEVELS:
        mid = (t // (2 * m)) * (2 * m) + m - 1
        blocks.append(s[None, :] <= mid[:, None])
    return jnp.concatenate(blocks, axis=0).astype(BF16)


def _hgrn_kernel(x_ref, w_ref, lb_ref, nw_ref, sel_ref, o_ref, s_ref, St, *, tc, nb):
    c = pl.program_id(1)

    @pl.when(c == 0)
    def _():
        St[...] = jnp.zeros(St.shape, F32)

    lb = lb_ref[...]
    row = lax.broadcasted_iota(jnp.int32, (tc, BRANCH_W), 0)
    r64 = lax.broadcasted_iota(jnp.int32, (tc, tc), 0)
    c64 = lax.broadcasted_iota(jnp.int32, (tc, tc), 1)
    tok = []
    all_cols = _project_rows(x_ref, w_ref, nb, tc)
    for n in range(nb):
        cols = all_cols[n]
        q = _silu(cols[:, 0:256])
        forget = lb + (1.0 - lb) * _sigmoid(cols[:, 256:512])
        kk = 1.0 - forget
        sums = _select_rows_exact(sel_ref[...], jnp.log(forget))
        tok.append(dict(q=q, kk=kk, sums=sums, b=sums[0:tc],
                        v=cols[:, 512:768], g=cols[:, 768:1024]))
    chains = [(n, h) for n in range(nb) for h in range(N_HEADS)]
    scores = {(n, h): jnp.where(r64 == c64, _mm_nt(_head(tok[n]['q'], h), _head(tok[n]['kk'], h)), 0.0)
              for n, h in chains}
    for lvl, m in enumerate(HGRN_LEVELS):
        sh = m.bit_length() - 1
        upper = ((row >> sh) & 1) == 1
        same = (r64 >> (sh + 1)) == (c64 >> (sh + 1))
        scaled = []
        for n in range(nb):
            t = tok[n]
            e = jnp.exp(-jnp.abs(t['b'] - t['sums'][(lvl + 1) * tc:(lvl + 2) * tc]))
            scaled.append((jnp.where(upper, t['q'] * e, 0.0), jnp.where(upper, 0.0, t['kk'] * e)))
        for n, h in chains:
            scores[n, h] = scores[n, h] + jnp.where(same, _mm_nt(_head(scaled[n][0], h), _head(scaled[n][1], h)), 0.0)
    for n in range(nb):
        t = tok[n]
        b = t['b']
        v = t['v']
        g = t['g']
        b_end = b[tc - 1:tc, :]
        qd = t['q'] * jnp.exp(b)
        kd = t['kk'] * jnp.exp(b_end - b)
        for h in range(N_HEADS):
            vh = _head(v, h)
            oh = _mm(scores[n, h], vh) + _mm_nt(_head(qd, h), St[n, h])
            St[n, h] = St[n, h] * jnp.exp(_head(b_end, h)) + _mm_tn(vh, _head(kd, h))
            ms = jnp.mean(oh * oh, axis=-1, keepdims=True)
            on = oh * lax.rsqrt(ms + LN_EPS) * _head(nw_ref[...], h)
            o_ref[n, :, h * HEAD_DIM:(h + 1) * HEAD_DIM] = on * _silu(_head(g, h))

    @pl.when(c == pl.num_programs(1) - 1)
    def _():
        for n in range(nb):
            for h in range(N_HEADS):
                s_ref[n, h] = _transpose_exact(St[n, h])


def _hgrn_prompt(x3, lb, lp, nb=8):
    b, t, d = x3.shape
    nb = min(nb, b)
    tc = HGRN_CHUNK
    row = lambda: pl.BlockSpec((1, BRANCH_W), lambda i, j: (0, 0))
    return pl.pallas_call(
        functools.partial(_hgrn_kernel, tc=tc, nb=nb),
        grid=(b // nb, t // tc),
        in_specs=[pl.BlockSpec((nb, tc, d), lambda i, j: (i, j, 0)),
                  pl.BlockSpec((d, 1024), lambda i, j: (0, 1)), row(), row(),
                  pl.BlockSpec((7 * tc, tc), lambda i, j: (0, 0))],
        out_specs=[pl.BlockSpec((nb, tc, BRANCH_W), lambda i, j: (i, j, 0)),
                   pl.BlockSpec((nb, N_HEADS, HEAD_DIM, HEAD_DIM), lambda i, j: (i, 0, 0, 0))],
        out_shape=[jax.ShapeDtypeStruct((b, t, BRANCH_W), F32),
                   jax.ShapeDtypeStruct((b, N_HEADS, HEAD_DIM, HEAD_DIM), F32)],
        scratch_shapes=[pltpu.VMEM((nb, N_HEADS, HEAD_DIM, HEAD_DIM), F32)],
        compiler_params=_cparams(("parallel", "arbitrary"), 32),
        name="hgrn_prompt",
    )(x3, lp['w_in_br'], lb, lp['hg_norm_w'], _hgrn_select_matrix())


def _rwkv_token_mix(cols, prev, mu_ref, w0_ref, w2_ref, a0_ref, a2_ref, g2_ref, kkw_ref, ka_ref):
    xs = cols + mu_ref[...] * (prev - cols)
    r = xs[:, 0:256]
    k = xs[:, 256:512]
    v = xs[:, 512:768]
    xg = xs[:, 768:896]
    xw = xs[:, 896:960]
    xa = xs[:, 960:1024]
    w = -_softplus(-(w0_ref[...] + _mm(jnp.tanh(xw), w2_ref[...]))) - 0.5
    lw = -jnp.exp(w)
    a = _sigmoid(a0_ref[...] + _mm(xa, a2_ref[...]))
    g = _mm(_sigmoid(xg), g2_ref[...])
    kk = k * kkw_ref[...]
    parts = []
    for h in range(N_HEADS):
        kh = _head(kk, h)
        nrm = jnp.sqrt(jnp.sum(kh * kh, axis=-1, keepdims=True))
        parts.append(kh / jnp.maximum(nrm, 1e-12))
    kkn = jnp.concatenate(parts, axis=-1)
    k2 = k * (1.0 + (a - 1.0) * ka_ref[...])
    return r, lw, k2, v, kkn, a, g


def _rwkv_out(o, r, k2, v, g, rk_ref, lw_ref, lb_ref, h):
    mu = jnp.mean(o, axis=-1, keepdims=True)
    d = o - mu
    var = jnp.mean(d * d, axis=-1, keepdims=True)
    on = d * lax.rsqrt(var + RW_GN_EPS) * _head(lw_ref[...], h) + _head(lb_ref[...], h)
    bonus = jnp.sum(_head(r, h) * _head(k2, h) * _head(rk_ref[...], h), axis=-1, keepdims=True) * _head(v, h)
    return (on + bonus) * _head(g, h)


def _rwkv_kernel(x_ref, w_ref, mu_ref, w0_ref, w2_ref, a0_ref, a2_ref, g2_ref, kkw_ref, ka_ref, rk_ref,
                 lnw_ref, lnb_ref, y_ref, shift_ref, s_ref, S, last_row, *, tc, nb):
    c = pl.program_id(1)

    @pl.when(c == 0)
    def _():
        S[...] = jnp.zeros(S.shape, F32)
        last_row[...] = jnp.zeros(last_row.shape, F32)

    rr = lax.broadcasted_iota(jnp.int32, (tc, tc), 0)
    cc = lax.broadcasted_iota(jnp.int32, (tc, tc), 1)
    strict = rr > cc
    incl2 = (lax.broadcasted_iota(jnp.int32, (tc, 2 * tc), 0)
             >= (lax.broadcasted_iota(jnp.int32, (tc, 2 * tc), 1) & (tc - 1)))
    tril = _tril_ones(tc)
    n_double = max(1, (tc - 1).bit_length())
    tok = []
    new_shift = []
    all_cols = _project_rows(x_ref, w_ref, nb, tc)
    for n in range(nb):
        cols = all_cols[n]
        new_shift.append(cols[tc - 1:tc, :])
        rowi = lax.broadcasted_iota(jnp.int32, cols.shape, 0)
        prev = jnp.where(rowi == 0, last_row[n], pltpu.roll(cols, 1, 0))
        last_row[n] = cols[tc - 1:tc, :]
        r, lw, k2, v, kkn, a, g = _rwkv_token_mix(cols, prev, mu_ref, w0_ref, w2_ref, a0_ref, a2_ref, g2_ref,
                                                  kkw_ref, ka_ref)
        am = -kkn
        bm = kkn * a
        G = _select_rows_exact(tril, lw)
        g_end = G[tc - 1:tc, :]
        einv = jnp.exp(-G)
        eend = jnp.exp(g_end - G)
        tok.append(dict(r=r, k2=k2, v=v, g=g, g_end=g_end, at=am * jnp.exp(G - lw), rt=r * jnp.exp(G),
                        bt=bm * einv, kt=k2 * einv, bbar=bm * eend, kbar=k2 * eend))
    chains = [(n, h) for n in range(nb) for h in range(N_HEADS)]
    lhs = {(n, h): jnp.concatenate([_head(tok[n]['at'], h), _head(tok[n]['rt'], h)], axis=0) for n, h in chains}
    inter = {(n, h): _mm_nt(lhs[n, h], jnp.concatenate([_head(tok[n]['bt'], h), _head(tok[n]['kt'], h)], axis=0))
             for n, h in chains}
    from_state = {(n, h): _mm_nt(lhs[n, h], S[n, h]) for n, h in chains}
    npow = {ch: jnp.where(strict, inter[ch][0:tc, 0:tc], 0.0) for ch in chains}
    u = {(n, h): from_state[n, h][0:tc]
         + _mm(jnp.where(strict, inter[n, h][0:tc, tc:2 * tc], 0.0), _head(tok[n]['v'], h)) for n, h in chains}
    for j in range(n_double):
        u = {ch: u[ch] + _mm(npow[ch], u[ch]) for ch in chains}
        if j + 1 < n_double:
            npow = {ch: _mm(npow[ch], npow[ch]) for ch in chains}
    for n, h in chains:
        t = tok[n]
        uv = jnp.concatenate([u[n, h], _head(t['v'], h)], axis=0)
        a_r = jnp.where(incl2, inter[n, h][tc:2 * tc, :], 0.0)
        o = from_state[n, h][tc:2 * tc] + _mm(a_r, uv)
        S[n, h] = (S[n, h] * jnp.exp(_head(t['g_end'], h))
                   + _mm_tn(uv, jnp.concatenate([_head(t['bbar'], h), _head(t['kbar'], h)], axis=0)))
        y_ref[n, :, h * HEAD_DIM:(h + 1) * HEAD_DIM] = _rwkv_out(o, t['r'], t['k2'], t['v'], t['g'], rk_ref,
                                                                 lnw_ref, lnb_ref, h)

    @pl.when(c == pl.num_programs(1) - 1)
    def _():
        s_ref[...] = S[...]
        for n in range(nb):
            shift_ref[n] = new_shift[n]


def _rwkv_prompt(x3, lp, tc=64, nb=4):
    b, t, d = x3.shape
    nb = min(nb, b)
    row = lambda w: pl.BlockSpec((1, w), lambda i, j: (0, 0))
    mat = lambda r, w: pl.BlockSpec((r, w), lambda i, j: (0, 0))
    return pl.pallas_call(
        functools.partial(_rwkv_kernel, tc=tc, nb=nb),
        grid=(b // nb, t // tc),
        in_specs=[pl.BlockSpec((nb, tc, d), lambda i, j: (i, j, 0)),
                  pl.BlockSpec((d, 1024), lambda i, j: (0, 2)),
                  row(1024), row(256), mat(64, 256), row(256), mat(64, 256), mat(128, 256),
                  row(256), row(256), row(256), row(256), row(256)],
        out_specs=[pl.BlockSpec((nb, tc, BRANCH_W), lambda i, j: (i, j, 0)),
                   pl.BlockSpec((nb, 1, 1024), lambda i, j: (i, 0, 0)),
                   pl.BlockSpec((nb, N_HEADS, HEAD_DIM, HEAD_DIM), lambda i, j: (i, 0, 0, 0))],
        out_shape=[jax.ShapeDtypeStruct((b, t, BRANCH_W), F32),
                   jax.ShapeDtypeStruct((b, 1, 1024), F32),
                   jax.ShapeDtypeStruct((b, N_HEADS, HEAD_DIM, HEAD_DIM), F32)],
        scratch_shapes=[pltpu.VMEM((nb, N_HEADS, HEAD_DIM, HEAD_DIM), F32), pltpu.VMEM((nb, 1, 1024), F32)],
        compiler_params=_cparams(("parallel", "arbitrary"), 32),
        name="rwkv_prompt",
    )(x3, lp['w_in_br'], lp['rw_mu'], lp['rw_w0'], lp['rw_w2'], lp['rw_a0'], lp['rw_a2'], lp['rw_g2'],
      lp['rw_kk'], lp['rw_ka'], lp['rw_rk'], lp['rw_lnx_w'], lp['rw_lnx_b'])


def _layer_norm(z, w, b):
    mu = jnp.mean(z, axis=-1, keepdims=True)
    d = z - mu
    var = jnp.mean(d * d, axis=-1, keepdims=True)
    return d * lax.rsqrt(var + LN_EPS) * w + b


def _mix_kernel(x_ref, oa_ref, ob_ref, oc_ref, od_ref, wg_ref, wb_ref, wo_ref, lw_ref, lb_ref, out_ref):
    x = x_ref[...]
    xb = x.astype(BF16)
    mixed = None
    for gi, o_ref in enumerate((oa_ref, ob_ref, oc_ref, od_ref)):
        gate = _sigmoid(jnp.dot(xb, wg_ref[:, gi * D_MODEL:(gi + 1) * D_MODEL], preferred_element_type=F32))
        up = jnp.dot(o_ref[...].astype(BF16), wb_ref[gi], preferred_element_type=F32)
        mixed = gate * up if mixed is None else mixed + gate * up
    y = jnp.dot(mixed.astype(BF16), wo_ref[...], preferred_element_type=F32)
    out_ref[...] = _layer_norm(ALPHA * x + y, lw_ref[...], lb_ref[...])


def _mix(x, outs, lp, tn):
    n = x.shape[0]
    tok = lambda w: pl.BlockSpec((tn, w), lambda i: (i, 0))
    const = lambda *s: pl.BlockSpec(s, lambda i: (0,) * len(s))
    return pl.pallas_call(
        _mix_kernel,
        grid=(n // tn,),
        in_specs=[tok(D_MODEL), tok(BRANCH_W), tok(BRANCH_W), tok(BRANCH_W), tok(BRANCH_W),
                  const(D_MODEL, 4 * D_MODEL), const(4, BRANCH_W, D_MODEL), const(D_MODEL, D_MODEL),
                  const(1, D_MODEL), const(1, D_MODEL)],
        out_specs=tok(D_MODEL),
        out_shape=jax.ShapeDtypeStruct((n, D_MODEL), F32),
        compiler_params=_cparams(("parallel",), 48),
        name="mix_ln1",
    )(x, *outs, lp['w_gate'], lp['w_branch'], lp['w_out'], lp['ln1_w'], lp['ln1_b'])


def _ple_kernel(x_ref, p_ref, wg_ref, bg_ref, wp_ref, out_ref):
    x = x_ref[...]
    gate = _sigmoid(jnp.dot(x.astype(BF16), wg_ref[...], preferred_element_type=F32) + bg_ref[...])
    emb = jnp.dot(p_ref[...].astype(BF16), wp_ref[...], preferred_element_type=F32)
    out_ref[...] = x + gate * emb


def _ple(x, p_emb, lp, tn):
    n = x.shape[0]
    tok = lambda w: pl.BlockSpec((tn, w), lambda i: (i, 0))
    const = lambda *s: pl.BlockSpec(s, lambda i: (0,) * len(s))
    return pl.pallas_call(
        _ple_kernel,
        grid=(n // tn,),
        in_specs=[tok(D_MODEL), tok(256), const(D_MODEL, D_MODEL), const(1, D_MODEL), const(256, D_MODEL)],
        out_specs=tok(D_MODEL),
        out_shape=jax.ShapeDtypeStruct((n, D_MODEL), F32),
        compiler_params=_cparams(("parallel",), 32),
        name="ple_gate",
    )(x, p_emb, lp['ple_gate_w'], lp['ple_gate_b'], lp['ple_w'])


PEER_HEADS = 8
PEER_NKEYS = 128
PEER_TOPK = 16
PEER_EB = 1024
PEER_SUB = 2
PEER_IGROUP = 2
PEER_JROWS = 64


def _oddeven_merge_sort_pairs(n):
    pairs = []
    p = 1
    while p < n:
        k = p
        while k >= 1:
            for j in range(k % p, n - k, 2 * k):
                for i in range(min(k, n - j - k)):
                    if (i + j) // (p * 2) == (i + j + k) // (p * 2):
                        pairs.append((i + j, i + j + k))
            k //= 2
        p *= 2
    return pairs


def _bitonic_merge_pairs(n):
    pairs = []
    k = n // 2
    while k >= 1:
        pairs.extend((i, i + k) for i in range(n) if (i & k) == 0)
        k //= 2
    return pairs


_SORT16 = _oddeven_merge_sort_pairs(PEER_TOPK)
_MERGE16 = _bitonic_merge_pairs(PEER_TOPK)
_CAND_LEN = tuple(PEER_TOPK // (a + 1) for a in range(PEER_TOPK))


def _network(vals, pairs):
    vals = list(vals)
    for i, j in pairs:
        hi = jnp.maximum(vals[i], vals[j])
        lo = jnp.minimum(vals[i], vals[j])
        vals[i], vals[j] = hi, lo
    return vals


def _top16_merge(x, y):
    return _network([jnp.maximum(x[i], y[PEER_TOPK - 1 - i]) for i in range(PEER_TOPK)], _MERGE16)


def _peer_head_stats(h, s_nat, e_nat, sk, th, g_count):
    rows0 = pl.ds(pl.multiple_of(2 * h * PEER_NKEYS, PEER_NKEYS), PEER_NKEYS)
    rows1 = pl.ds(pl.multiple_of((2 * h + 1) * PEER_NKEYS, PEER_NKEYS), PEER_NKEYS)
    for p, rows in enumerate((rows0, rows1)):
        for g in range(4):
            gg = g % g_count
            sk[:, p * 4 + g, :] = s_nat[gg, rows, :]
    groups = []
    for m in range(PEER_NKEYS // PEER_TOPK):
        groups.append(_network([sk[PEER_TOPK * m + i] for i in range(PEER_TOPK)], _SORT16))
    while len(groups) > 1:
        groups = [_top16_merge(groups[i], groups[i + 1]) for i in range(0, len(groups), 2)]
    top = groups[0]
    low = lax.broadcasted_iota(jnp.int32, (8, 128), 0) < 4
    ta = [jnp.where(low, t, pltpu.roll(t, 4, 0)) for t in top]
    tb = [jnp.where(low, pltpu.roll(t, 4, 0), t) for t in top]
    cand = [[ta[a] + tb[b] for b in range(_CAND_LEN[a])] for a in range(PEER_TOPK)]
    m1 = _network(cand[1] + [cand[a][0] for a in range(PEER_TOPK - 1, 7, -1)], _MERGE16)
    m2 = _network(cand[2] + cand[3] + cand[4] + cand[5] + cand[6], _SORT16)
    t1 = _top16_merge(cand[0], m1)
    t2 = _top16_merge(t1, m2)
    t2[15] = jnp.maximum(t2[15], cand[7][0])
    t2[14] = jnp.maximum(t2[14], cand[7][1])
    theta = t2[0]
    for t in t2[1:]:
        theta = jnp.minimum(theta, t)
    cmax = cand[0][0]
    z = jnp.zeros((8, 128), F32)
    for row in cand:
        for cv in row:
            z = z + jnp.where(cv >= theta, jnp.exp(cv - cmax), 0.0)
    inv_z = 1.0 / z
    for g in range(g_count):
        lanes = slice(g * 128, (g + 1) * 128)
        th[h, 0:1, lanes] = theta[g:g + 1, :]
        th[h, 1:2, lanes] = ta[0][g:g + 1, :]
        th[h, 2:3, lanes] = tb[0][g:g + 1, :]
        th[h, 3:4, lanes] = inv_z[g:g + 1, :]
        for b in range(PEER_TOPK):
            th[h, 8 + b:9 + b, lanes] = tb[b][g:g + 1, :]
    for g in range(g_count):
        lanes = slice(g * 128, (g + 1) * 128)
        s0 = s_nat[g, rows0, :]
        e_nat[g, rows0, :] = jnp.exp(s0 - th[h, 1:2, lanes])
        e_nat[g, rows1, :] = jnp.exp(s_nat[g, rows1, :] - th[h, 2:3, lanes]) * th[h, 3:4, lanes]
        tau = jnp.full(s0.shape, jnp.inf, F32)
        for b in range(PEER_TOPK):
            sb = th[h, 8 + b:9 + b, lanes]
            tau = jnp.where((s0 + sb) >= th[h, 0:1, lanes], sb, tau)
        s_nat[g, rows0, :] = tau


def _peer_kernel(x_ref, wq_ref, keys_ref, u_ref, vt_ref, lw_ref, lb_ref, out_ref,
                 xtb, s_nat, e_nat, sk, th, wact, wraw, actb, yt, *, tn):
    j = pl.program_id(1)
    g_count = tn // 128

    @pl.when(j == 0)
    def _():
        xtb[...] = x_ref[...].T.astype(BF16)
        for hp in range(2 * PEER_HEADS):
            qt = jnp.dot(wq_ref[hp * 128:(hp + 1) * 128, :], xtb[...], preferred_element_type=F32)
            scores = jnp.dot(keys_ref[hp], qt.astype(BF16), preferred_element_type=F32)
            for g in range(g_count):
                s_nat[g, hp * PEER_NKEYS:(hp + 1) * PEER_NKEYS, :] = scores[:, g * 128:(g + 1) * 128]

        def head_body(h, carry):
            _peer_head_stats(h, s_nat, e_nat, sk, th, g_count)
            return carry

        lax.fori_loop(0, PEER_HEADS, head_body, 0)
        yt[...] = jnp.zeros(yt.shape, F32)

    last = pl.num_programs(1) - 1
    cur = j % 2

    def step(first_stage, second_stage):
        for sb in range(PEER_SUB):
            if first_stage:
                act = jnp.dot(u_ref[sb * PEER_EB:(sb + 1) * PEER_EB, :], xtb[...],
                              preferred_element_type=F32)
                for g in range(g_count):
                    actb[sb, g] = act[:, g * 128:(g + 1) * 128]
            if second_stage:
                yt[...] += jnp.dot(vt_ref[sb], wact[1 - cur, sb], preferred_element_type=F32)
            if not first_stage:
                continue
            i0 = pl.multiple_of((j * PEER_SUB + sb) * (PEER_EB // PEER_NKEYS), 8)
            jr = PEER_JROWS
            for g in range(g_count):
                for jq in range(PEER_NKEYS // jr):
                    for ig in range(PEER_EB // PEER_NKEYS // PEER_IGROUP):
                        acc = [jnp.zeros((jr, 128), F32) for _ in range(PEER_IGROUP)]
                        for h in range(PEER_HEADS):
                            base1 = (2 * h + 1) * PEER_NKEYS + jq * jr
                            s1 = s_nat[g, base1:base1 + jr, :]
                            e1 = e_nat[g, base1:base1 + jr, :]
                            tau0 = s_nat[g, pl.ds(2 * h * PEER_NKEYS + i0, 8), :]
                            e0 = e_nat[g, pl.ds(2 * h * PEER_NKEYS + i0, 8), :]
                            for k in range(PEER_IGROUP):
                                ii = PEER_IGROUP * ig + k
                                sel = s1 >= tau0[ii:ii + 1, :]
                                acc[k] = acc[k] + jnp.where(sel, e0[ii:ii + 1, :] * e1, 0.0)
                        for k in range(PEER_IGROUP):
                            r0 = (PEER_IGROUP * ig + k) * PEER_NKEYS + jq * jr
                            wraw[g, r0:r0 + jr, :] = acc[k]
            for g in range(g_count):
                wact[cur, sb, :, g * 128:(g + 1) * 128] = (wraw[g] * _gelu(actb[sb, g])).astype(BF16)

    @pl.when(j == 0)
    def _():
        step(True, False)

    @pl.when((j > 0) & (j < last))
    def _():
        step(True, True)

    @pl.when(j == last)
    def _():
        step(False, True)
        z = ALPHA * x_ref[...] + yt[...].T
        out_ref[...] = _layer_norm(z, lw_ref[...], lb_ref[...])


def _peer(x, lp, tn):
    n = x.shape[0]
    n_blk = lp['peer_u'].shape[0] // (PEER_EB * PEER_SUB)
    once = lambda *s: pl.BlockSpec(s, lambda i, j: (0,) * len(s), pipeline_mode=pl.Buffered(1))
    const = lambda *s: pl.BlockSpec(s, lambda i, j: (0,) * len(s))
    return pl.pallas_call(
        functools.partial(_peer_kernel, tn=tn),
        grid=(n // tn, n_blk + 1),
        in_specs=[pl.BlockSpec((tn, D_MODEL), lambda i, j: (i, 0)),
                  once(2 * PEER_HEADS * 128, D_MODEL), once(2 * PEER_HEADS, PEER_NKEYS, 128),
                  pl.BlockSpec((PEER_SUB * PEER_EB, D_MODEL), lambda i, j: (jnp.minimum(j, n_blk - 1), 0)),
                  pl.BlockSpec((PEER_SUB, D_MODEL, PEER_EB), lambda i, j: (jnp.maximum(j - 1, 0), 0, 0)),
                  const(1, D_MODEL), const(1, D_MODEL)],
        out_specs=pl.BlockSpec((tn, D_MODEL), lambda i, j: (i, 0)),
        out_shape=jax.ShapeDtypeStruct((n, D_MODEL), F32),
        scratch_shapes=[pltpu.VMEM((D_MODEL, tn), BF16),
                        pltpu.VMEM((tn // 128, 2 * PEER_HEADS * PEER_NKEYS, 128), F32),
                        pltpu.VMEM((tn // 128, 2 * PEER_HEADS * PEER_NKEYS, 128), F32),
                        pltpu.VMEM((PEER_NKEYS, 8, 128), F32),
                        pltpu.VMEM((PEER_HEADS, 8 + PEER_TOPK, tn), F32),
                        pltpu.VMEM((2, PEER_SUB, PEER_EB, tn), BF16),
                        pltpu.VMEM((tn // 128, PEER_EB, 128), F32),
                        pltpu.VMEM((PEER_SUB, tn // 128, PEER_EB, 128), F32),
                        pltpu.VMEM((D_MODEL, tn), F32)],
        compiler_params=_cparams(("parallel", "arbitrary"), 56),
        name="peer_ln2",
    )(x, lp['peer_wq_t'], lp['peer_keys'], lp['peer_u'], lp['peer_v_t'], lp['ln2_w'], lp['ln2_b'])


DEC_N = 128
HEAD_STATE = HEAD_DIM * HEAD_DIM


def _col(v):
    return jnp.broadcast_to(v.reshape(-1, 1), (v.size, DEC_N))


def _load_state_t(s_ref, st):
    st[...] = s_ref[...].T.reshape(HEAD_DIM, HEAD_DIM, DEC_N)


def _store_state_t(so_ref, st):
    so_ref[...] = st[...].reshape(HEAD_STATE, DEC_N).T


def _ret_dec_kernel(c_ref, cos_ref, sin_ref, gw_ref, gb_ref, s_ref, o_ref, so_ref, ct, qk, st):
    h = pl.program_id(0)
    ct[...] = c_ref[...].T
    r0 = pl.multiple_of(h * HEAD_DIM, HEAD_DIM)

    def rot(x):
        partner = jnp.concatenate([x[HEAD_DIM // 2:], x[:HEAD_DIM // 2]], axis=0)
        return x * cos_ref[...] + partner * sin_ref[...]

    qk[0] = rot(ct[pl.ds(r0, HEAD_DIM), :])
    qk[1] = rot(ct[pl.ds(256 + r0, HEAD_DIM), :]) * (HEAD_DIM ** -0.5)
    v = ct[pl.ds(512 + r0, HEAD_DIM), :]
    g = ct[pl.ds(768 + r0, HEAD_DIM), :]
    gamma = jnp.exp(jnp.zeros((1, 1), F32) + jnp.where(
        h == 0, RET_LOG_GAMMA[0], jnp.where(h == 1, RET_LOG_GAMMA[1],
                                            jnp.where(h == 2, RET_LOG_GAMMA[2], RET_LOG_GAMMA[3]))))
    _load_state_t(s_ref, st)

    def body(k, o):
        s_new = gamma * st[k] + qk[1, pl.ds(k, 1), :] * v
        st[k] = s_new
        return o + qk[0, pl.ds(k, 1), :] * s_new

    o = lax.fori_loop(0, HEAD_DIM, body, jnp.zeros((HEAD_DIM, DEC_N), F32))
    mu = jnp.mean(o, axis=0, keepdims=True)
    d = o - mu
    var = jnp.mean(d * d, axis=0, keepdims=True)
    o_ref[...] = (d * lax.rsqrt(var + LN_EPS) * gw_ref[...] + gb_ref[...]) * _silu(g)
    _store_state_t(so_ref, st)


def _hgrn_dec_kernel(c_ref, lb_ref, nw_ref, s_ref, o_ref, so_ref, ct, qk, st):
    h = pl.program_id(0)
    ct[...] = c_ref[...].T
    r0 = pl.multiple_of(h * HEAD_DIM, HEAD_DIM)
    lb = lb_ref[...]
    qk[0] = _silu(ct[pl.ds(r0, HEAD_DIM), :])
    forget = lb + (1.0 - lb) * _sigmoid(ct[pl.ds(256 + r0, HEAD_DIM), :])
    qk[1] = forget
    qk[2] = 1.0 - forget
    v = ct[pl.ds(512 + r0, HEAD_DIM), :]
    g = ct[pl.ds(768 + r0, HEAD_DIM), :]
    _load_state_t(s_ref, st)

    def body(k, o):
        s_new = qk[1, pl.ds(k, 1), :] * st[k] + qk[2, pl.ds(k, 1), :] * v
        st[k] = s_new
        return o + qk[0, pl.ds(k, 1), :] * s_new

    o = lax.fori_loop(0, HEAD_DIM, body, jnp.zeros((HEAD_DIM, DEC_N), F32))
    ms = jnp.mean(o * o, axis=0, keepdims=True)
    o_ref[...] = o * lax.rsqrt(ms + LN_EPS) * nw_ref[...] * _silu(g)
    _store_state_t(so_ref, st)


def _rwkv_dec_kernel(c_ref, sh_ref, mu_ref, w0_ref, w2_ref, a0_ref, a2_ref, g2_ref, kkw_ref, ka_ref,
                     rk_ref, lnw_ref, lnb_ref, s_ref, o_ref, so_ref, vt, st, osc):
    h = pl.program_id(0)
    r, lw, k2, v, kkn, a, g = _rwkv_token_mix(c_ref[...], sh_ref[...], mu_ref, w0_ref, w2_ref, a0_ref, a2_ref,
                                              g2_ref, kkw_ref, ka_ref)
    for idx, arr in enumerate((r, jnp.exp(lw), k2, v, kkn, a, g)):
        vt[idx] = arr.T
    r0 = pl.multiple_of(h * HEAD_DIM, HEAD_DIM)
    rows = pl.ds(r0, HEAD_DIM)
    rh, wh, kh, kkh, ah = vt[0, rows, :], vt[1, rows, :], vt[2, rows, :], vt[4, rows, :], vt[5, rows, :]
    vh, gh = vt[3, rows, :], vt[6, rows, :]
    kka = kkh * ah
    _load_state_t(s_ref, st)

    def body(vi, carry):
        s_old = st[vi]
        sa = jnp.sum(s_old * (-kkh), axis=0, keepdims=True)
        s_new = s_old * wh + sa * kka + vt[3, pl.ds(r0 + vi, 1), :] * kh
        st[vi] = s_new
        osc[pl.ds(vi, 1), :] = jnp.sum(s_new * rh, axis=0, keepdims=True)
        return carry

    lax.fori_loop(0, HEAD_DIM, body, 0)
    o = osc[...]
    mu = jnp.mean(o, axis=0, keepdims=True)
    d = o - mu
    var = jnp.mean(d * d, axis=0, keepdims=True)
    on = d * lax.rsqrt(var + RW_GN_EPS) * lnw_ref[...] + lnb_ref[...]
    bonus = jnp.sum(rh * kh * rk_ref[...], axis=0, keepdims=True) * vh
    o_ref[...] = (on + bonus) * gh
    _store_state_t(so_ref, st)


def _dec_specs():
    head_tab = pl.BlockSpec((HEAD_DIM, DEC_N), lambda h: (h, 0))
    state = pl.BlockSpec((DEC_N, HEAD_STATE), lambda h: (0, h))
    out = pl.BlockSpec((None, HEAD_DIM, DEC_N), lambda h: (h, 0, 0))
    return head_tab, state, out


def _dec_out_shapes():
    return [jax.ShapeDtypeStruct((N_HEADS, HEAD_DIM, DEC_N), F32),
            jax.ShapeDtypeStruct((DEC_N, N_HEADS * HEAD_STATE), F32)]


def _dec_finish(o_t, s_new):
    return o_t.reshape(BRANCH_W, DEC_N).T, s_new.reshape(DEC_N, N_HEADS, HEAD_DIM, HEAD_DIM)


def _ret_decode(cols, state, cos_c, sin_c, lp):
    head_tab, st_spec, out_spec = _dec_specs()
    same = pl.BlockSpec((HEAD_DIM, DEC_N), lambda h: (0, 0))
    o_t, s_new = pl.pallas_call(
        _ret_dec_kernel,
        grid=(N_HEADS,),
        in_specs=[pl.BlockSpec((DEC_N, 1024), lambda h: (0, 0)), same, same, head_tab, head_tab, st_spec],
        out_specs=[out_spec, st_spec],
        out_shape=_dec_out_shapes(),
        scratch_shapes=[pltpu.VMEM((1024, DEC_N), F32), pltpu.VMEM((2, HEAD_DIM, DEC_N), F32),
                        pltpu.VMEM((HEAD_DIM, HEAD_DIM, DEC_N), F32)],
        compiler_params=_cparams(("arbitrary",), 40),
        name="ret_decode",
    )(cols, cos_c, sin_c, _col(lp['ret_gn_w']), _col(lp['ret_gn_b']), state.reshape(DEC_N, -1))
    return _dec_finish(o_t, s_new)


def _hgrn_decode(cols, state, lp):
    head_tab, st_spec, out_spec = _dec_specs()
    o_t, s_new = pl.pallas_call(
        _hgrn_dec_kernel,
        grid=(N_HEADS,),
        in_specs=[pl.BlockSpec((DEC_N, 1024), lambda h: (0, 1)), head_tab, head_tab, st_spec],
        out_specs=[out_spec, st_spec],
        out_shape=_dec_out_shapes(),
        scratch_shapes=[pltpu.VMEM((1024, DEC_N), F32), pltpu.VMEM((3, HEAD_DIM, DEC_N), F32),
                        pltpu.VMEM((HEAD_DIM, HEAD_DIM, DEC_N), F32)],
        compiler_params=_cparams(("arbitrary",), 40),
        name="hgrn_decode",
    )(cols, _col(lp['hg_lb']), _col(lp['hg_norm_w']), state.reshape(DEC_N, -1))
    return _dec_finish(o_t, s_new)


def _rwkv_decode(cols, shift, state, lp):
    head_tab, st_spec, out_spec = _dec_specs()
    row = lambda w: pl.BlockSpec((1, w), lambda h: (0, 0))
    mat = lambda r, w: pl.BlockSpec((r, w), lambda h: (0, 0))
    o_t, s_new = pl.pallas_call(
        _rwkv_dec_kernel,
        grid=(N_HEADS,),
        in_specs=[pl.BlockSpec((DEC_N, 1024), lambda h: (0, 2)), mat(DEC_N, 1024),
                  row(1024), row(256), mat(64, 256), row(256), mat(64, 256), mat(128, 256),
                  row(256), row(256), head_tab, head_tab, head_tab, st_spec],
        out_specs=[out_spec, st_spec],
        out_shape=_dec_out_shapes(),
        scratch_shapes=[pltpu.VMEM((7, BRANCH_W, DEC_N), F32), pltpu.VMEM((HEAD_DIM, HEAD_DIM, DEC_N), F32),
                        pltpu.VMEM((HEAD_DIM, DEC_N), F32)],
        compiler_params=_cparams(("arbitrary",), 40),
        name="rwkv_decode",
    )(cols, shift, lp['rw_mu'], lp['rw_w0'], lp['rw_w2'], lp['rw_a0'], lp['rw_a2'], lp['rw_g2'],
      lp['rw_kk'], lp['rw_ka'], _col(lp['rw_rk']), _col(lp['rw_lnx_w']), _col(lp['rw_lnx_b']),
      state.reshape(DEC_N, -1))
    return _dec_finish(o_t, s_new)


def _lru_dec_kernel(c_ref, conv_ref, h0_ref, cw_ref, cb_ref, wa_ref, ba_ref, wx_ref, bx_ref, lam_ref,
                    y_ref, h_ref, nconv_ref):
    xb = c_ref[:, 0:BRANCH_W]
    gate = c_ref[:, BRANCH_W:2 * BRANCH_W]
    c0, c1, c2 = conv_ref[0], conv_ref[1], conv_ref[2]
    xc = (c0 * cw_ref[0:1, :] + c1 * cw_ref[1:2, :] + c2 * cw_ref[2:3, :] + xb * cw_ref[3:4, :]) + cb_ref[...]
    a, u = _lru_gates(xc, wa_ref, ba_ref, wx_ref, bx_ref, lam_ref)
    hn = a * h0_ref[...] + u
    h_ref[...] = hn
    y_ref[...] = hn * _gelu(gate)
    nconv_ref[0] = c1
    nconv_ref[1] = c2
    nconv_ref[2] = xb


def _lru_decode(cols, conv, h0, lp):
    full = lambda *s: pl.BlockSpec(s, lambda i: (0,) * len(s))
    return pl.pallas_call(
        _lru_dec_kernel,
        grid=(1,),
        in_specs=[pl.BlockSpec((DEC_N, 512), lambda i: (0, 6)), full(3, DEC_N, BRANCH_W), full(DEC_N, BRANCH_W),
                  full(CONV_W, BRANCH_W), full(1, BRANCH_W), full(BRANCH_W, BRANCH_W), full(1, BRANCH_W),
                  full(BRANCH_W, BRANCH_W), full(1, BRANCH_W), full(1, BRANCH_W)],
        out_specs=[full(DEC_N, BRANCH_W), full(DEC_N, BRANCH_W), full(3, DEC_N, BRANCH_W)],
        out_shape=[jax.ShapeDtypeStruct((DEC_N, BRANCH_W), F32), jax.ShapeDtypeStruct((DEC_N, BRANCH_W), F32),
                   jax.ShapeDtypeStruct((3, DEC_N, BRANCH_W), F32)],
        compiler_params=_cparams(("arbitrary",), 32),
        name="lru_decode",
    )(cols, conv, h0, lp['lru_conv_w'], lp['lru_conv_b'], lp['lru_wa_bd'], lp['lru_ba'], lp['lru_wx_bd'],
      lp['lru_bx'], lp['lru_lambda'])


def _prompt_layer(x, p_emb, lp, cos_t, sin_t):
    b, t, d = x.shape
    n = b * t
    xf = x.reshape(n, d)
    o_a, s_ret = _ret_prompt(x, cos_t, sin_t, lp)
    o_b, s_hg = _hgrn_prompt(x, lp['hg_lb'], lp)
    o_c, s_shift, s_rw = _rwkv_prompt(x, lp)
    o_d, s_lru, s_conv = _lru_prompt(x, lp)
    outs = [o.reshape(n, BRANCH_W) for o in (o_a, o_b, o_c, o_d)]
    x1 = _mix(xf, outs, lp, 512)
    x2 = _peer(x1, lp, 512)
    x3 = _ple(x2, p_emb.reshape(n, -1), lp, 512)
    return x3.reshape(b, t, d), (s_ret, s_hg, s_rw, s_shift[:, 0], s_lru[:, 0], s_conv)


def _sample_layer(x, p_emb, state, lp, cos_c, sin_c):
    s_ret, s_hg, s_rw, s_shift, s_lru, s_conv = state
    xf = x.reshape(DEC_N, D_MODEL)
    cols = _proj(xf, lp['w_in_br'], DEC_N)
    o_a, s_ret = _ret_decode(cols, s_ret, cos_c, sin_c, lp)
    o_b, s_hg = _hgrn_decode(cols, s_hg, lp)
    o_c, s_rw = _rwkv_decode(cols, s_shift, s_rw, lp)
    o_d, s_lru, s_conv_t = _lru_decode(cols, jnp.swapaxes(s_conv, 0, 1), s_lru, lp)
    x1 = _mix(xf, [o_a, o_b, o_c, o_d], lp, DEC_N)
    x2 = _peer(x1, lp, DEC_N)
    x3 = _ple(x2, p_emb.reshape(DEC_N, -1), lp, DEC_N)
    new_shift = cols[:, 2048:3072]
    return x3.reshape(x.shape), (s_ret, s_hg, s_rw, new_shift, s_lru, jnp.swapaxes(s_conv_t, 0, 1))


def kernel(x_prompt, x_sample, state_ret, state_hgrn, state_rwkv, state_shift, state_lru, state_conv, p_prompt, p_sample, w_in, ret_gn_w, ret_gn_b, hg_lb, hg_norm_w, rw_mu, rw_w0, rw_w2, rw_a0, rw_a2, rw_g2, rw_kk, rw_ka, rw_rk, rw_lnx_w, rw_lnx_b, lru_conv_w, lru_conv_b, lru_wa, lru_ba, lru_wx, lru_bx, lru_lambda, w_branch, w_out, ln1_w, ln1_b, peer_wq, peer_keys, peer_u, peer_v, ln2_w, ln2_b, ple_w, ple_gate_w, ple_gate_b):
    params = dict(w_in=w_in, ret_gn_w=ret_gn_w, ret_gn_b=ret_gn_b, hg_norm_w=hg_norm_w, rw_mu=rw_mu, rw_w0=rw_w0,
                  rw_w2=rw_w2, rw_a0=rw_a0, rw_a2=rw_a2, rw_g2=rw_g2, rw_kk=rw_kk, rw_ka=rw_ka, rw_rk=rw_rk,
                  rw_lnx_w=rw_lnx_w, rw_lnx_b=rw_lnx_b, lru_conv_w=lru_conv_w, lru_conv_b=lru_conv_b,
                  lru_wa=lru_wa, lru_ba=lru_ba, lru_wx=lru_wx, lru_bx=lru_bx, lru_lambda=lru_lambda,
                  w_branch=w_branch, w_out=w_out, ln1_w=ln1_w, ln1_b=ln1_b, peer_wq=peer_wq, peer_keys=peer_keys,
                  peer_u=peer_u, peer_v=peer_v, ln2_w=ln2_w, ln2_b=ln2_b, ple_w=ple_w, ple_gate_w=ple_gate_w,
                  ple_gate_b=ple_gate_b)
    lb_cum = jnp.cumsum(jax.nn.softmax(hg_lb.astype(F32), axis=0), axis=0)
    lb_all = lb_cum - lb_cum[0:1]
    t_prompt = x_prompt.shape[1]
    past_len = 16384
    cos_t, sin_t = _rope_tables(jnp.arange(t_prompt))
    cos_s, sin_s = _rope_tables(past_len + jnp.arange(1))
    cos_c = _col(cos_s[0, :HEAD_DIM])
    sin_c = _col(sin_s[0, :HEAD_DIM])
    h_p, h_s = x_prompt, x_sample
    new_p, new_s = [], []
    for l in range(DEPTH):
        lp = _layer_params(params, l, lb_all)
        h_p, st_p = _prompt_layer(h_p, p_prompt[l], lp, cos_t, sin_t)
        st_in = (state_ret[l], state_hgrn[l], state_rwkv[l], state_shift[l], state_lru[l], state_conv[l])
        h_s, st_s = _sample_layer(h_s, p_sample[l], st_in, lp, cos_c, sin_c)
        new_p.append(st_p)
        new_s.append(st_s)
    outs_p = [jnp.stack(z) for z in zip(*new_p)]
    outs_s = [jnp.stack(z) for z in zip(*new_s)]
    return (h_p, h_s, *outs_p, *outs_s)
```

```python
import functools
import math

import jax
import jax.numpy as jnp
from jax import lax
from jax.experimental import pallas as pl
from jax.experimental.pallas import tpu as pltpu

F32 = jnp.float32
BF16 = jnp.bfloat16
HIGHEST = lax.Precision.HIGHEST

D_MODEL = 1024
BRANCH_W = 256
N_HEADS = 4
HEAD_DIM = 64
RET_LOG_GAMMA = tuple(math.log1p(-(2.0 ** (-5.0 - h))) for h in range(N_HEADS))
ROPE_BASE = 10000.0
RW_GN_EPS = 64e-5
LN_EPS = 1e-5
LRU_C = 8.0
CONV_W = 4
DEPTH = 2
ALPHA = (2 * DEPTH) ** 0.25
BRANCH_COLS = 3584
MIB = 1024 * 1024


def _cparams(semantics, vmem_mib):
    return pltpu.CompilerParams(dimension_semantics=semantics, vmem_limit_bytes=vmem_mib * MIB)


def _mm(a, b):
    return jnp.dot(a.astype(BF16), b.astype(BF16), preferred_element_type=F32)


def _mm_nt(a, b):
    return lax.dot_general(a.astype(BF16), b.astype(BF16), (((1,), (1,)), ((), ())), preferred_element_type=F32)


def _mm_tn(a, b):
    return lax.dot_general(a.astype(BF16), b.astype(BF16), (((0,), (0,)), ((), ())), preferred_element_type=F32)


def _transpose_exact(a):
    n = a.shape[0]
    eye = (lax.broadcasted_iota(jnp.int32, (n, n), 0) == lax.broadcasted_iota(jnp.int32, (n, n), 1)).astype(F32)
    return lax.dot_general(a, eye, (((0,), (0,)), ((), ())), preferred_element_type=F32, precision=HIGHEST)


def _select_rows_exact(sel, x):
    x1 = x.astype(BF16)
    r1 = x - x1.astype(F32)
    x2 = r1.astype(BF16)
    x3 = (r1 - x2.astype(F32)).astype(BF16)
    dot = lambda p: jnp.dot(sel, p, preferred_element_type=F32)
    return (dot(x1) + dot(x2)) + dot(x3)


def _tril_ones(n):
    r = lax.broadcasted_iota(jnp.int32, (n, n), 0)
    c = lax.broadcasted_iota(jnp.int32, (n, n), 1)
    return (r >= c).astype(BF16)


def _sigmoid(x):
    return 1.0 / (1.0 + jnp.exp(-x))


def _silu(x):
    return x * _sigmoid(x)


def _gelu(x):
    c = 0.7978845608028654
    half = 0.5 * x
    return half + half * jnp.tanh(x * (c + (c * 0.044715) * (x * x)))


def _softplus(x):
    return jnp.maximum(x, 0.0) + jnp.log1p(jnp.exp(-jnp.abs(x)))


def _head(x, h):
    return x[:, h * HEAD_DIM:(h + 1) * HEAD_DIM]


def _project_rows(x_ref, w_ref, nb, tc):
    x = x_ref[...].reshape(nb * tc, x_ref.shape[-1]).astype(BF16)
    cols = jnp.dot(x, w_ref[...], preferred_element_type=F32)
    return [cols[n * tc:(n + 1) * tc] for n in range(nb)]


def _block_diag(w):
    n, c, d = w.shape
    eye = jnp.eye(n, dtype=w.dtype)
    return (eye[:, None, :, None] * w[:, :, None, :]).reshape(n * c, n * d)


def _rope_tables(pos):
    half = HEAD_DIM // 2
    inv_freq = ROPE_BASE ** (-jnp.arange(half, dtype=F32) / half)
    ang = pos.astype(F32)[:, None] * inv_freq[None, :]
    cos = jnp.cos(ang)
    sin = jnp.sin(ang)
    cos_t = jnp.tile(jnp.concatenate([cos, cos], axis=-1), (1, N_HEADS))
    sin_t = jnp.tile(jnp.concatenate([-sin, sin], axis=-1), (1, N_HEADS))
    return cos_t, sin_t


def _layer_params(p, l, lb_all):
    r2 = lambda a: a.reshape(1, -1)
    w_in = p['w_in'][l]
    lp = {
        'w_in_br': w_in[:, :BRANCH_COLS].astype(BF16),
        'w_gate': w_in[:, BRANCH_COLS:].astype(BF16),
        'ret_gn_w': r2(p['ret_gn_w'][l]), 'ret_gn_b': r2(p['ret_gn_b'][l]),
        'hg_lb': r2(lb_all[l]), 'hg_norm_w': r2(p['hg_norm_w'][l]),
        'rw_mu': r2(p['rw_mu'][l]), 'rw_w0': r2(p['rw_w0'][l]), 'rw_w2': p['rw_w2'][l].astype(BF16),
        'rw_a0': r2(p['rw_a0'][l]), 'rw_a2': p['rw_a2'][l].astype(BF16), 'rw_g2': p['rw_g2'][l].astype(BF16),
        'rw_kk': r2(p['rw_kk'][l]), 'rw_ka': r2(p['rw_ka'][l]), 'rw_rk': r2(p['rw_rk'][l]),
        'rw_lnx_w': r2(p['rw_lnx_w'][l]), 'rw_lnx_b': r2(p['rw_lnx_b'][l]),
        'lru_conv_w': p['lru_conv_w'][l], 'lru_conv_b': r2(p['lru_conv_b'][l]),
        'lru_wa_bd': _block_diag(p['lru_wa'][l]).astype(BF16), 'lru_ba': r2(p['lru_ba'][l]),
        'lru_wx_bd': _block_diag(p['lru_wx'][l]).astype(BF16), 'lru_bx': r2(p['lru_bx'][l]),
        'lru_lambda': r2(p['lru_lambda'][l]),
        'w_branch': p['w_branch'][l].astype(BF16), 'w_out': p['w_out'][l].astype(BF16),
        'ln1_w': r2(p['ln1_w'][l]), 'ln1_b': r2(p['ln1_b'][l]),
        'peer_wq_t': p['peer_wq'][l].T.astype(BF16),
        'peer_keys': p['peer_keys'][l].reshape(16, 128, 128).astype(BF16),
        'peer_u': p['peer_u'][l].astype(BF16),
        'peer_raw': (p['peer_wq'][l], p['peer_keys'][l], p['peer_u'][l], p['peer_v'][l]),
        'peer_v_t': jnp.swapaxes(p['peer_v'][l].astype(BF16).reshape(-1, PEER_EB, D_MODEL), 1, 2),
        'ln2_w': r2(p['ln2_w'][l]), 'ln2_b': r2(p['ln2_b'][l]),
        'ple_w': p['ple_w'][l].astype(BF16), 'ple_gate_w': p['ple_gate_w'][l].astype(BF16),
        'ple_gate_b': r2(p['ple_gate_b'][l]),
    }
    return lp


def _proj_kernel(x_ref, w_ref, o_ref):
    o_ref[...] = jnp.dot(x_ref[...].astype(BF16), w_ref[...], preferred_element_type=F32)


def _proj(x, w_bf16, tn):
    n, k = x.shape
    m = w_bf16.shape[1]
    return pl.pallas_call(
        _proj_kernel,
        grid=(n // tn,),
        in_specs=[pl.BlockSpec((tn, k), lambda i: (i, 0)),
                  pl.BlockSpec((k, m), lambda i: (0, 0))],
        out_specs=pl.BlockSpec((tn, m), lambda i: (i, 0)),
        out_shape=jax.ShapeDtypeStruct((n, m), F32),
        compiler_params=_cparams(("parallel",), 48),
        name="in_proj",
    )(x, w_bf16)


def _lru_gates(xc, wa_ref, ba_ref, wx_ref, bx_ref, lam_ref):
    r = _sigmoid(_mm(xc, wa_ref[...]) + ba_ref[...])
    i = _sigmoid(_mm(xc, wx_ref[...]) + bx_ref[...])
    log_a = -LRU_C * r * _softplus(-lam_ref[...])
    a = jnp.exp(log_a)
    u = jnp.sqrt(1.0 - jnp.exp(2.0 * log_a)) * (i * xc)
    return a, u


def _lru_kernel(x_ref, w_ref, cw_ref, cb_ref, wa_ref, ba_ref, wx_ref, bx_ref, lam_ref,
                y_ref, h_ref, conv_ref, xbuf, hcar, a_s, u_s, hs, *, tc):
    c = pl.program_id(1)

    @pl.when(c == 0)
    def _():
        xbuf[0:8, :] = jnp.zeros((8, BRANCH_W), F32)
        hcar[...] = jnp.zeros((1, BRANCH_W), F32)

    cols = jnp.dot(x_ref[...].astype(BF16), w_ref[...], preferred_element_type=F32)
    xb = cols[:, 0:BRANCH_W]
    gate = cols[:, BRANCH_W:2 * BRANCH_W]
    xbuf[8:8 + tc, :] = xb
    xc = (xbuf[pl.ds(5, tc), :] * cw_ref[0:1, :] + xbuf[pl.ds(6, tc), :] * cw_ref[1:2, :]
          + xbuf[pl.ds(7, tc), :] * cw_ref[2:3, :] + xb * cw_ref[3:4, :]) + cb_ref[...]
    a, u = _lru_gates(xc, wa_ref, ba_ref, wx_ref, bx_ref, lam_ref)
    a_s[...] = a
    u_s[...] = u

    def body(t, h):
        h = a_s[pl.ds(t, 1), :] * h + u_s[pl.ds(t, 1), :]
        hs[pl.ds(t, 1), :] = h
        return h

    h = lax.fori_loop(0, tc, body, hcar[...], unroll=8)
    hcar[...] = h
    y_ref[...] = hs[...] * _gelu(gate)
    xbuf[0:8, :] = xbuf[tc:tc + 8, :]

    @pl.when(c == pl.num_programs(1) - 1)
    def _():
        h_ref[...] = h
        conv_ref[...] = xbuf[5:8, :]


def _lru_prompt(x3, lp, tc=256):
    b, t, d = x3.shape
    row = lambda: pl.BlockSpec((1, BRANCH_W), lambda i, j: (0, 0))
    full = lambda r: pl.BlockSpec((r, BRANCH_W), lambda i, j: (0, 0))
    return pl.pallas_call(
        functools.partial(_lru_kernel, tc=tc),
        grid=(b, t // tc),
        in_specs=[pl.BlockSpec((None, tc, d), lambda i, j: (i, j, 0)),
                  pl.BlockSpec((d, 2 * BRANCH_W), lambda i, j: (0, 6)),
                  full(CONV_W), row(), full(BRANCH_W), row(), full(BRANCH_W), row(), row()],
        out_specs=[pl.BlockSpec((None, tc, BRANCH_W), lambda i, j: (i, j, 0)),
                   pl.BlockSpec((None, 1, BRANCH_W), lambda i, j: (i, 0, 0)),
                   pl.BlockSpec((None, CONV_W - 1, BRANCH_W), lambda i, j: (i, 0, 0))],
        out_shape=[jax.ShapeDtypeStruct((b, t, BRANCH_W), F32),
                   jax.ShapeDtypeStruct((b, 1, BRANCH_W), F32),
                   jax.ShapeDtypeStruct((b, CONV_W - 1, BRANCH_W), F32)],
        scratch_shapes=[pltpu.VMEM((tc + 8, BRANCH_W), F32), pltpu.VMEM((1, BRANCH_W), F32),
                        pltpu.VMEM((tc, BRANCH_W), F32), pltpu.VMEM((tc, BRANCH_W), F32),
                        pltpu.VMEM((tc, BRANCH_W), F32)],
        compiler_params=_cparams(("parallel", "arbitrary"), 32),
        name="lru_prompt",
    )(x3, lp['w_in_br'], lp['lru_conv_w'], lp['lru_conv_b'], lp['lru_wa_bd'], lp['lru_ba'], lp['lru_wx_bd'],
      lp['lru_bx'], lp['lru_lambda'])


def _rotary(x, cos, sin_signed, first_half):
    partner = jnp.where(first_half, pltpu.roll(x, BRANCH_W - 32, 1), pltpu.roll(x, 32, 1))
    return x * cos + partner * sin_signed


def _ret_kernel(x_ref, w_ref, cos_ref, sin_ref, gw_ref, gb_ref, o_ref, s_ref, S, *, tc, nb):
    c = pl.program_id(1)

    @pl.when(c == 0)
    def _():
        S[...] = jnp.zeros(S.shape, F32)

    lane = lax.broadcasted_iota(jnp.int32, (tc, BRANCH_W), 1)
    first_half = (lane % HEAD_DIM) < (HEAD_DIM // 2)
    cos = cos_ref[...]
    sin = sin_ref[...]
    row = lax.broadcasted_iota(jnp.int32, (tc, tc), 0)
    col = lax.broadcasted_iota(jnp.int32, (tc, tc), 1)
    causal = row >= col
    dist = jnp.where(causal, row - col, 0).astype(F32)
    tpos = lax.broadcasted_iota(jnp.int32, (tc, HEAD_DIM), 0).astype(F32)
    decay = [jnp.where(causal, jnp.exp(dist * RET_LOG_GAMMA[h]), 0.0) for h in range(N_HEADS)]
    q_in = [jnp.exp((tpos + 1.0) * RET_LOG_GAMMA[h]) for h in range(N_HEADS)]
    k_out = [jnp.exp((tc - 1.0 - tpos) * RET_LOG_GAMMA[h]) for h in range(N_HEADS)]
    cols = _project_rows(x_ref, w_ref, nb, tc)
    q = [_rotary(cols[n][:, 0:256], cos, sin, first_half) for n in range(nb)]
    k = [_rotary(cols[n][:, 256:512], cos, sin, first_half) * (HEAD_DIM ** -0.5) for n in range(nb)]
    chains = [(n, h) for n in range(nb) for h in range(N_HEADS)]
    scores = {(n, h): _mm_nt(_head(q[n], h), _head(k[n], h)) * decay[h] for n, h in chains}
    out = {(n, h): _mm(scores[n, h], _head(cols[n][:, 512:768], h)) + _mm(_head(q[n], h) * q_in[h], S[n, h])
           for n, h in chains}
    for n, h in chains:
        S[n, h] = (math.exp(tc * RET_LOG_GAMMA[h]) * S[n, h]
                   + _mm_tn(_head(k[n], h) * k_out[h], _head(cols[n][:, 512:768], h)))
    for n, h in chains:
        oh = out[n, h]
        mu = jnp.mean(oh, axis=-1, keepdims=True)
        d = oh - mu
        var = jnp.mean(d * d, axis=-1, keepdims=True)
        on = d * lax.rsqrt(var + LN_EPS) * _head(gw_ref[...], h) + _head(gb_ref[...], h)
        o_ref[n, :, h * HEAD_DIM:(h + 1) * HEAD_DIM] = on * _silu(_head(cols[n][:, 768:1024], h))

    @pl.when(c == pl.num_programs(1) - 1)
    def _():
        s_ref[...] = S[...]


def _ret_prompt(x3, cos_t, sin_t, lp, tc=128, nb=2):
    b, t, d = x3.shape
    nb = min(nb, b)
    row = lambda: pl.BlockSpec((1, BRANCH_W), lambda i, j: (0, 0))
    return pl.pallas_call(
        functools.partial(_ret_kernel, tc=tc, nb=nb),
        grid=(b // nb, t // tc),
        in_specs=[pl.BlockSpec((nb, tc, d), lambda i, j: (i, j, 0)),
                  pl.BlockSpec((d, 1024), lambda i, j: (0, 0)),
                  pl.BlockSpec((tc, BRANCH_W), lambda i, j: (j, 0)),
                  pl.BlockSpec((tc, BRANCH_W), lambda i, j: (j, 0)), row(), row()],
        out_specs=[pl.BlockSpec((nb, tc, BRANCH_W), lambda i, j: (i, j, 0)),
                   pl.BlockSpec((nb, N_HEADS, HEAD_DIM, HEAD_DIM), lambda i, j: (i, 0, 0, 0))],
        out_shape=[jax.ShapeDtypeStruct((b, t, BRANCH_W), F32),
                   jax.ShapeDtypeStruct((b, N_HEADS, HEAD_DIM, HEAD_DIM), F32)],
        scratch_shapes=[pltpu.VMEM((nb, N_HEADS, HEAD_DIM, HEAD_DIM), F32)],
        compiler_params=_cparams(("parallel", "arbitrary"), 32),
        name="ret_prompt",
    )(x3, lp['w_in_br'], cos_t, sin_t, lp['ret_gn_w'], lp['ret_gn_b'])


HGRN_CHUNK = 64
HGRN_LEVELS = tuple(HGRN_CHUNK >> l for l in range(1, 7))


def _hgrn_select_matrix():
    t = jnp.arange(HGRN_CHUNK)
    s = jnp.arange(HGRN_CHUNK)
    blocks = [s[None, :] <= t[:, None]]
    for m in HGRN_LEVELS:
        mid = (t // (2 * m)) * (2 * m) + m - 1
        blocks.append(s[None, :] <= mid[:, None])
    return jnp.concatenate(blocks, axis=0).astype(BF16)


def _hgrn_kernel(x_ref, w_ref, lb_ref, nw_ref, sel_ref, o_ref, s_ref, St, *, tc, nb):
    c = pl.program_id(1)

    @pl.when(c == 0)
    def _():
        St[...] = jnp.zeros(St.shape, F32)

    lb = lb_ref[...]
    row = lax.broadcasted_iota(jnp.int32, (tc, BRANCH_W), 0)
    r64 = lax.broadcasted_iota(jnp.int32, (tc, tc), 0)
    c64 = lax.broadcasted_iota(jnp.int32, (tc, tc), 1)
    tok = []
    all_cols = _project_rows(x_ref, w_ref, nb, tc)
    for n in range(nb):
        cols = all_cols[n]
        q = _silu(cols[:, 0:256])
        forget = lb + (1.0 - lb) * _sigmoid(cols[:, 256:512])
        kk = 1.0 - forget
        sums = _select_rows_exact(sel_ref[...], jnp.log(forget))
        tok.append(dict(q=q, kk=kk, sums=sums, b=sums[0:tc],
                        v=cols[:, 512:768], g=cols[:, 768:1024]))
    chains = [(n, h) for n in range(nb) for h in range(N_HEADS)]
    scores = {(n, h): jnp.where(r64 == c64, _mm_nt(_head(tok[n]['q'], h), _head(tok[n]['kk'], h)), 0.0)
              for n, h in chains}
    for lvl, m in enumerate(HGRN_LEVELS):
        sh = m.bit_length() - 1
        upper = ((row >> sh) & 1) == 1
        same = (r64 >> (sh + 1)) == (c64 >> (sh + 1))
        scaled = []
        for n in range(nb):
            t = tok[n]
            e = jnp.exp(-jnp.abs(t['b'] - t['sums'][(lvl + 1) * tc:(lvl + 2) * tc]))
            scaled.append((jnp.where(upper, t['q'] * e, 0.0), jnp.where(upper, 0.0, t['kk'] * e)))
        for n, h in chains:
            scores[n, h] = scores[n, h] + jnp.where(same, _mm_nt(_head(scaled[n][0], h), _head(scaled[n][1], h)), 0.0)
    for n in range(nb):
        t = tok[n]
        b = t['b']
        v = t['v']
        g = t['g']
        b_end = b[tc - 1:tc, :]
        qd = t['q'] * jnp.exp(b)
        kd = t['kk'] * jnp.exp(b_end - b)
        for h in range(N_HEADS):
            vh = _head(v, h)
            oh = _mm(scores[n, h], vh) + _mm_nt(_head(qd, h), St[n, h])
            St[n, h] = St[n, h] * jnp.exp(_head(b_end, h)) + _mm_tn(vh, _head(kd, h))
            ms = jnp.mean(oh * oh, axis=-1, keepdims=True)
            on = oh * lax.rsqrt(ms + LN_EPS) * _head(nw_ref[...], h)
            o_ref[n, :, h * HEAD_DIM:(h + 1) * HEAD_DIM] = on * _silu(_head(g, h))

    @pl.when(c == pl.num_programs(1) - 1)
    def _():
        for n in range(nb):
            for h in range(N_HEADS):
                s_ref[n, h] = _transpose_exact(St[n, h])


def _hgrn_prompt(x3, lb, lp, nb=8):
    b, t, d = x3.shape
    nb = min(nb, b)
    tc = HGRN_CHUNK
    row = lambda: pl.BlockSpec((1, BRANCH_W), lambda i, j: (0, 0))
    return pl.pallas_call(
        functools.partial(_hgrn_kernel, tc=tc, nb=nb),
        grid=(b // nb, t // tc),
        in_specs=[pl.BlockSpec((nb, tc, d), lambda i, j: (i, j, 0)),
                  pl.BlockSpec((d, 1024), lambda i, j: (0, 1)), row(), row(),
                  pl.BlockSpec((7 * tc, tc), lambda i, j: (0, 0))],
        out_specs=[pl.BlockSpec((nb, tc, BRANCH_W), lambda i, j: (i, j, 0)),
                   pl.BlockSpec((nb, N_HEADS, HEAD_DIM, HEAD_DIM), lambda i, j: (i, 0, 0, 0))],
        out_shape=[jax.ShapeDtypeStruct((b, t, BRANCH_W), F32),
                   jax.ShapeDtypeStruct((b, N_HEADS, HEAD_DIM, HEAD_DIM), F32)],
        scratch_shapes=[pltpu.VMEM((nb, N_HEADS, HEAD_DIM, HEAD_DIM), F32)],
        compiler_params=_cparams(("parallel", "arbitrary"), 32),
        name="hgrn_prompt",
    )(x3, lp['w_in_br'], lb, lp['hg_norm_w'], _hgrn_select_matrix())


def _rwkv_token_mix(cols, prev, mu_ref, w0_ref, w2_ref, a0_ref, a2_ref, g2_ref, kkw_ref, ka_ref):
    xs = cols + mu_ref[...] * (prev - cols)
    r = xs[:, 0:256]
    k = xs[:, 256:512]
    v = xs[:, 512:768]
    xg = xs[:, 768:896]
    xw = xs[:, 896:960]
    xa = xs[:, 960:1024]
    w = -_softplus(-(w0_ref[...] + _mm(jnp.tanh(xw), w2_ref[...]))) - 0.5
    lw = -jnp.exp(w)
    a = _sigmoid(a0_ref[...] + _mm(xa, a2_ref[...]))
    g = _mm(_sigmoid(xg), g2_ref[...])
    kk = k * kkw_ref[...]
    parts = []
    for h in range(N_HEADS):
        kh = _head(kk, h)
        nrm = jnp.sqrt(jnp.sum(kh * kh, axis=-1, keepdims=True))
        parts.append(kh / jnp.maximum(nrm, 1e-12))
    kkn = jnp.concatenate(parts, axis=-1)
    k2 = k * (1.0 + (a - 1.0) * ka_ref[...])
    return r, lw, k2, v, kkn, a, g


def _rwkv_out(o, r, k2, v, g, rk_ref, lw_ref, lb_ref, h):
    mu = jnp.mean(o, axis=-1, keepdims=True)
    d = o - mu
    var = jnp.mean(d * d, axis=-1, keepdims=True)
    on = d * lax.rsqrt(var + RW_GN_EPS) * _head(lw_ref[...], h) + _head(lb_ref[...], h)
    bonus = jnp.sum(_head(r, h) * _head(k2, h) * _head(rk_ref[...], h), axis=-1, keepdims=True) * _head(v, h)
    return (on + bonus) * _head(g, h)


def _rwkv_kernel(x_ref, w_ref, mu_ref, w0_ref, w2_ref, a0_ref, a2_ref, g2_ref, kkw_ref, ka_ref, rk_ref,
                 lnw_ref, lnb_ref, y_ref, shift_ref, s_ref, S, last_row, *, tc, nb):
    c = pl.program_id(1)

    @pl.when(c == 0)
    def _():
        S[...] = jnp.zeros(S.shape, F32)
        last_row[...] = jnp.zeros(last_row.shape, F32)

    rr = lax.broadcasted_iota(jnp.int32, (tc, tc), 0)
    cc = lax.broadcasted_iota(jnp.int32, (tc, tc), 1)
    strict = rr > cc
    incl2 = (lax.broadcasted_iota(jnp.int32, (tc, 2 * tc), 0)
             >= (lax.broadcasted_iota(jnp.int32, (tc, 2 * tc), 1) & (tc - 1)))
    tril = _tril_ones(tc)
    n_double = max(1, (tc - 1).bit_length())
    tok = []
    new_shift = []
    all_cols = _project_rows(x_ref, w_ref, nb, tc)
    for n in range(nb):
        cols = all_cols[n]
        new_shift.append(cols[tc - 1:tc, :])
        rowi = lax.broadcasted_iota(jnp.int32, cols.shape, 0)
        prev = jnp.where(rowi == 0, last_row[n], pltpu.roll(cols, 1, 0))
        last_row[n] = cols[tc - 1:tc, :]
        r, lw, k2, v, kkn, a, g = _rwkv_token_mix(cols, prev, mu_ref, w0_ref, w2_ref, a0_ref, a2_ref, g2_ref,
                                                  kkw_ref, ka_ref)
        am = -kkn
        bm = kkn * a
        G = _select_rows_exact(tril, lw)
        g_end = G[tc - 1:tc, :]
        einv = jnp.exp(-G)
        eend = jnp.exp(g_end - G)
        tok.append(dict(r=r, k2=k2, v=v, g=g, g_end=g_end, at=am * jnp.exp(G - lw), rt=r * jnp.exp(G),
                        bt=bm * einv, kt=k2 * einv, bbar=bm * eend, kbar=k2 * eend))
    chains = [(n, h) for n in range(nb) for h in range(N_HEADS)]
    lhs = {(n, h): jnp.concatenate([_head(tok[n]['at'], h), _head(tok[n]['rt'], h)], axis=0) for n, h in chains}
    inter = {(n, h): _mm_nt(lhs[n, h], jnp.concatenate([_head(tok[n]['bt'], h), _head(tok[n]['kt'], h)], axis=0))
             for n, h in chains}
    from_state = {(n, h): _mm_nt(lhs[n, h], S[n, h]) for n, h in chains}
    npow = {ch: jnp.where(strict, inter[ch][0:tc, 0:tc], 0.0) for ch in chains}
    u = {(n, h): from_state[n, h][0:tc]
         + _mm(jnp.where(strict, inter[n, h][0:tc, tc:2 * tc], 0.0), _head(tok[n]['v'], h)) for n, h in chains}
    for j in range(n_double):
        u = {ch: u[ch] + _mm(npow[ch], u[ch]) for ch in chains}
        if j + 1 < n_double:
            npow = {ch: _mm(npow[ch], npow[ch]) for ch in chains}
    for n, h in chains:
        t = tok[n]
        uv = jnp.concatenate([u[n, h], _head(t['v'], h)], axis=0)
        a_r = jnp.where(incl2, inter[n, h][tc:2 * tc, :], 0.0)
        o = from_state[n, h][tc:2 * tc] + _mm(a_r, uv)
        S[n, h] = (S[n, h] * jnp.exp(_head(t['g_end'], h))
                   + _mm_tn(uv, jnp.concatenate([_head(t['bbar'], h), _head(t['kbar'], h)], axis=0)))
        y_ref[n, :, h * HEAD_DIM:(h + 1) * HEAD_DIM] = _rwkv_out(o, t['r'], t['k2'], t['v'], t['g'], rk_ref,
                                                                 lnw_ref, lnb_ref, h)

    @pl.when(c == pl.num_programs(1) - 1)
    def _():
        s_ref[...] = S[...]
        for n in range(nb):
            shift_ref[n] = new_shift[n]


def _rwkv_prompt(x3, lp, tc=64, nb=4):
    b, t, d = x3.shape
    nb = min(nb, b)
    row = lambda w: pl.BlockSpec((1, w), lambda i, j: (0, 0))
    mat = lambda r, w: pl.BlockSpec((r, w), lambda i, j: (0, 0))
    return pl.pallas_call(
        functools.partial(_rwkv_kernel, tc=tc, nb=nb),
        grid=(b // nb, t // tc),
        in_specs=[pl.BlockSpec((nb, tc, d), lambda i, j: (i, j, 0)),
                  pl.BlockSpec((d, 1024), lambda i, j: (0, 2)),
                  row(1024), row(256), mat(64, 256), row(256), mat(64, 256), mat(128, 256),
                  row(256), row(256), row(256), row(256), row(256)],
        out_specs=[pl.BlockSpec((nb, tc, BRANCH_W), lambda i, j: (i, j, 0)),
                   pl.BlockSpec((nb, 1, 1024), lambda i, j: (i, 0, 0)),
                   pl.BlockSpec((nb, N_HEADS, HEAD_DIM, HEAD_DIM), lambda i, j: (i, 0, 0, 0))],
        out_shape=[jax.ShapeDtypeStruct((b, t, BRANCH_W), F32),
                   jax.ShapeDtypeStruct((b, 1, 1024), F32),
                   jax.ShapeDtypeStruct((b, N_HEADS, HEAD_DIM, HEAD_DIM), F32)],
        scratch_shapes=[pltpu.VMEM((nb, N_HEADS, HEAD_DIM, HEAD_DIM), F32), pltpu.VMEM((nb, 1, 1024), F32)],
        compiler_params=_cparams(("parallel", "arbitrary"), 32),
        name="rwkv_prompt",
    )(x3, lp['w_in_br'], lp['rw_mu'], lp['rw_w0'], lp['rw_w2'], lp['rw_a0'], lp['rw_a2'], lp['rw_g2'],
      lp['rw_kk'], lp['rw_ka'], lp['rw_rk'], lp['rw_lnx_w'], lp['rw_lnx_b'])


def _layer_norm(z, w, b):
    mu = jnp.mean(z, axis=-1, keepdims=True)
    d = z - mu
    var = jnp.mean(d * d, axis=-1, keepdims=True)
    return d * lax.rsqrt(var + LN_EPS) * w + b


def _mix_kernel(x_ref, oa_ref, ob_ref, oc_ref, od_ref, wg_ref, wb_ref, wo_ref, lw_ref, lb_ref, out_ref):
    x = x_ref[...]
    xb = x.astype(BF16)
    mixed = None
    for gi, o_ref in enumerate((oa_ref, ob_ref, oc_ref, od_ref)):
        gate = _sigmoid(jnp.dot(xb, wg_ref[:, gi * D_MODEL:(gi + 1) * D_MODEL], preferred_element_type=F32))
        up = jnp.dot(o_ref[...].astype(BF16), wb_ref[gi], preferred_element_type=F32)
        mixed = gate * up if mixed is None else mixed + gate * up
    y = jnp.dot(mixed.astype(BF16), wo_ref[...], preferred_element_type=F32)
    out_ref[...] = _layer_norm(ALPHA * x + y, lw_ref[...], lb_ref[...])


def _mix(x, outs, lp, tn):
    n = x.shape[0]
    tok = lambda w: pl.BlockSpec((tn, w), lambda i: (i, 0))
    const = lambda *s: pl.BlockSpec(s, lambda i: (0,) * len(s))
    return pl.pallas_call(
        _mix_kernel,
        grid=(n // tn,),
        in_specs=[tok(D_MODEL), tok(BRANCH_W), tok(BRANCH_W), tok(BRANCH_W), tok(BRANCH_W),
                  const(D_MODEL, 4 * D_MODEL), const(4, BRANCH_W, D_MODEL), const(D_MODEL, D_MODEL),
                  const(1, D_MODEL), const(1, D_MODEL)],
        out_specs=tok(D_MODEL),
        out_shape=jax.ShapeDtypeStruct((n, D_MODEL), F32),
        compiler_params=_cparams(("parallel",), 48),
        name="mix_ln1",
    )(x, *outs, lp['w_gate'], lp['w_branch'], lp['w_out'], lp['ln1_w'], lp['ln1_b'])


def _ple_kernel(x_ref, p_ref, wg_ref, bg_ref, wp_ref, out_ref):
    x = x_ref[...]
    gate = _sigmoid(jnp.dot(x.astype(BF16), wg_ref[...], preferred_element_type=F32) + bg_ref[...])
    emb = jnp.dot(p_ref[...].astype(BF16), wp_ref[...], preferred_element_type=F32)
    out_ref[...] = x + gate * emb


def _ple(x, p_emb, lp, tn):
    n = x.shape[0]
    tok = lambda w: pl.BlockSpec((tn, w), lambda i: (i, 0))
    const = lambda *s: pl.BlockSpec(s, lambda i: (0,) * len(s))
    return pl.pallas_call(
        _ple_kernel,
        grid=(n // tn,),
        in_specs=[tok(D_MODEL), tok(256), const(D_MODEL, D_MODEL), const(1, D_MODEL), const(256, D_MODEL)],
        out_specs=tok(D_MODEL),
        out_shape=jax.ShapeDtypeStruct((n, D_MODEL), F32),
        compiler_params=_cparams(("parallel",), 32),
        name="ple_gate",
    )(x, p_emb, lp['ple_gate_w'], lp['ple_gate_b'], lp['ple_w'])


PEER_HEADS = 8
PEER_NKEYS = 128
PEER_TOPK = 16
PEER_EB = 1024
PEER_SUB = 2
PEER_IGROUP = 2
PEER_JROWS = 64


def _oddeven_merge_sort_pairs(n):
    pairs = []
    p = 1
    while p < n:
        k = p
        while k >= 1:
            for j in range(k % p, n - k, 2 * k):
                for i in range(min(k, n - j - k)):
                    if (i + j) // (p * 2) == (i + j + k) // (p * 2):
                        pairs.append((i + j, i + j + k))
            k //= 2
        p *= 2
    return pairs


def _bitonic_merge_pairs(n):
    pairs = []
    k = n // 2
    while k >= 1:
        pairs.extend((i, i + k) for i in range(n) if (i & k) == 0)
        k //= 2
    return pairs


_SORT16 = _oddeven_merge_sort_pairs(PEER_TOPK)
_MERGE16 = _bitonic_merge_pairs(PEER_TOPK)
_CAND_LEN = tuple(PEER_TOPK // (a + 1) for a in range(PEER_TOPK))


def _network(vals, pairs):
    vals = list(vals)
    for i, j in pairs:
        hi = jnp.maximum(vals[i], vals[j])
        lo = jnp.minimum(vals[i], vals[j])
        vals[i], vals[j] = hi, lo
    return vals


def _top16_merge(x, y):
    return _network([jnp.maximum(x[i], y[PEER_TOPK - 1 - i]) for i in range(PEER_TOPK)], _MERGE16)


def _peer_head_stats(h, s_nat, e_nat, sk, th, tie, g_count):
    rows0 = pl.ds(pl.multiple_of(2 * h * PEER_NKEYS, PEER_NKEYS), PEER_NKEYS)
    rows1 = pl.ds(pl.multiple_of((2 * h + 1) * PEER_NKEYS, PEER_NKEYS), PEER_NKEYS)
    for p, rows in enumerate((rows0, rows1)):
        for g in range(4):
            gg = g % g_count
            sk[:, p * 4 + g, :] = s_nat[gg, rows, :]
    groups = []
    for m in range(PEER_NKEYS // PEER_TOPK):
        groups.append(_network([sk[PEER_TOPK * m + i] for i in range(PEER_TOPK)], _SORT16))
    while len(groups) > 1:
        groups = [_top16_merge(groups[i], groups[i + 1]) for i in range(0, len(groups), 2)]
    top = groups[0]
    low = lax.broadcasted_iota(jnp.int32, (8, 128), 0) < 4
    ta = [jnp.where(low, t, pltpu.roll(t, 4, 0)) for t in top]
    tb = [jnp.where(low, pltpu.roll(t, 4, 0), t) for t in top]
    cand = [[ta[a] + tb[b] for b in range(_CAND_LEN[a])] for a in range(PEER_TOPK)]
    m1 = _network(cand[1] + [cand[a][0] for a in range(PEER_TOPK - 1, 7, -1)], _MERGE16)
    m2 = _network(cand[2] + cand[3] + cand[4] + cand[5] + cand[6], _SORT16)
    t1 = _top16_merge(cand[0], m1)
    t2 = _top16_merge(t1, m2)
    t2[15] = jnp.maximum(t2[15], cand[7][0])
    t2[14] = jnp.maximum(t2[14], cand[7][1])
    theta = t2[0]
    for t in t2[1:]:
        theta = jnp.minimum(theta, t)
    cmax = cand[0][0]
    z = jnp.zeros((8, 128), F32)
    n_kept = jnp.zeros((8, 128), F32)
    for row in cand:
        for cv in row:
            keep = cv >= theta
            z = z + jnp.where(keep, jnp.exp(cv - cmax), 0.0)
            n_kept = n_kept + jnp.where(keep, 1.0, 0.0)
    inv_z = 1.0 / z
    cand_tie = jnp.where(n_kept != float(PEER_TOPK), 1.0, 0.0)
    for g in range(g_count):
        lanes = slice(g * 128, (g + 1) * 128)
        th[h, 0:1, lanes] = theta[g:g + 1, :]
        th[h, 1:2, lanes] = ta[0][g:g + 1, :]
        th[h, 2:3, lanes] = tb[0][g:g + 1, :]
        th[h, 3:4, lanes] = inv_z[g:g + 1, :]
        th[h, 4:5, lanes] = ta[PEER_TOPK - 1][g:g + 1, :]
        th[h, 5:6, lanes] = cand_tie[g:g + 1, :]
        for b in range(PEER_TOPK):
            th[h, 8 + b:9 + b, lanes] = tb[b][g:g + 1, :]
    for g in range(g_count):
        lanes = slice(g * 128, (g + 1) * 128)
        s0 = s_nat[g, rows0, :]
        s1 = s_nat[g, rows1, :]
        e_nat[g, rows0, :] = jnp.exp(s0 - th[h, 1:2, lanes])
        e_nat[g, rows1, :] = jnp.exp(s1 - th[h, 2:3, lanes]) * th[h, 3:4, lanes]
        n0 = jnp.sum(jnp.where(s0 >= th[h, 4:5, lanes], 1.0, 0.0), axis=0, keepdims=True)
        n1 = jnp.sum(jnp.where(s1 >= th[h, 7 + PEER_TOPK:8 + PEER_TOPK, lanes], 1.0, 0.0), axis=0, keepdims=True)
        key_tie = jnp.maximum(jnp.where(n0 != float(PEER_TOPK), 1.0, 0.0), jnp.where(n1 != float(PEER_TOPK), 1.0, 0.0))
        tie[0:1, lanes] = jnp.maximum(tie[0:1, lanes], jnp.maximum(key_tie, th[h, 5:6, lanes]))
        tau = jnp.full(s0.shape, jnp.inf, F32)
        for b in range(PEER_TOPK):
            sb = th[h, 8 + b:9 + b, lanes]
            tau = jnp.where((s0 + sb) >= th[h, 0:1, lanes], sb, tau)
        s_nat[g, rows0, :] = tau


def _peer_kernel(x_ref, wq_ref, keys_ref, u_ref, vt_ref, lw_ref, lb_ref, out_ref, tie_ref,
                 xtb, s_nat, e_nat, sk, th, wact, wraw, actb, yt, tie, *, tn):
    j = pl.program_id(1)
    g_count = tn // 128

    @pl.when(j == 0)
    def _():
        xtb[...] = x_ref[...].T.astype(BF16)
        for hp in range(2 * PEER_HEADS):
            qt = jnp.dot(wq_ref[hp * 128:(hp + 1) * 128, :], xtb[...], preferred_element_type=F32)
            scores = jnp.dot(keys_ref[hp], qt.astype(BF16), preferred_element_type=F32)
            for g in range(g_count):
                s_nat[g, hp * PEER_NKEYS:(hp + 1) * PEER_NKEYS, :] = scores[:, g * 128:(g + 1) * 128]

        tie[...] = jnp.zeros(tie.shape, F32)

        def head_body(h, carry):
            _peer_head_stats(h, s_nat, e_nat, sk, th, tie, g_count)
            return carry

        lax.fori_loop(0, PEER_HEADS, head_body, 0)
        tie_ref[...] = tie[0:1, :]
        yt[...] = jnp.zeros(yt.shape, F32)

    last = pl.num_programs(1) - 1
    cur = j % 2

    def step(first_stage, second_stage):
        for sb in range(PEER_SUB):
            if first_stage:
                act = jnp.dot(u_ref[sb * PEER_EB:(sb + 1) * PEER_EB, :], xtb[...],
                              preferred_element_type=F32)
                for g in range(g_count):
                    actb[sb, g] = act[:, g * 128:(g + 1) * 128]
            if second_stage:
                yt[...] += jnp.dot(vt_ref[sb], wact[1 - cur, sb], preferred_element_type=F32)
            if not first_stage:
                continue
            i0 = pl.multiple_of((j * PEER_SUB + sb) * (PEER_EB // PEER_NKEYS), 8)
            jr = PEER_JROWS
            for g in range(g_count):
                for jq in range(PEER_NKEYS // jr):
                    for ig in range(PEER_EB // PEER_NKEYS // PEER_IGROUP):
                        acc = [jnp.zeros((jr, 128), F32) for _ in range(PEER_IGROUP)]
                        for h in range(PEER_HEADS):
                            base1 = (2 * h + 1) * PEER_NKEYS + jq * jr
                            s1 = s_nat[g, base1:base1 + jr, :]
                            e1 = e_nat[g, base1:base1 + jr, :]
                            tau0 = s_nat[g, pl.ds(2 * h * PEER_NKEYS + i0, 8), :]
                            e0 = e_nat[g, pl.ds(2 * h * PEER_NKEYS + i0, 8), :]
                            for k in range(PEER_IGROUP):
                                ii = PEER_IGROUP * ig + k
                                sel = s1 >= tau0[ii:ii + 1, :]
                                acc[k] = acc[k] + jnp.where(sel, e0[ii:ii + 1, :] * e1, 0.0)
                        for k in range(PEER_IGROUP):
                            r0 = (PEER_IGROUP * ig + k) * PEER_NKEYS + jq * jr
                            wraw[g, r0:r0 + jr, :] = acc[k]
            for g in range(g_count):
                wact[cur, sb, :, g * 128:(g + 1) * 128] = (wraw[g] * _gelu(actb[sb, g])).astype(BF16)

    @pl.when(j == 0)
    def _():
        step(True, False)

    @pl.when((j > 0) & (j < last))
    def _():
        step(True, True)

    @pl.when(j == last)
    def _():
        step(False, True)
        z = ALPHA * x_ref[...] + yt[...].T
        out_ref[...] = _layer_norm(z, lw_ref[...], lb_ref[...])


def _peer(x, lp, tn):
    n = x.shape[0]
    n_blk = lp['peer_u'].shape[0] // (PEER_EB * PEER_SUB)
    once = lambda *s: pl.BlockSpec(s, lambda i, j: (0,) * len(s), pipeline_mode=pl.Buffered(1))
    const = lambda *s: pl.BlockSpec(s, lambda i, j: (0,) * len(s))
    out, tie_flags = pl.pallas_call(
        functools.partial(_peer_kernel, tn=tn),
        grid=(n // tn, n_blk + 1),
        in_specs=[pl.BlockSpec((tn, D_MODEL), lambda i, j: (i, 0)),
                  once(2 * PEER_HEADS * 128, D_MODEL), once(2 * PEER_HEADS, PEER_NKEYS, 128),
                  pl.BlockSpec((PEER_SUB * PEER_EB, D_MODEL), lambda i, j: (jnp.minimum(j, n_blk - 1), 0)),
                  pl.BlockSpec((PEER_SUB, D_MODEL, PEER_EB), lambda i, j: (jnp.maximum(j - 1, 0), 0, 0)),
                  const(1, D_MODEL), const(1, D_MODEL)],
        out_specs=[pl.BlockSpec((tn, D_MODEL), lambda i, j: (i, 0)),
                   pl.BlockSpec((None, 1, tn), lambda i, j: (i, 0, 0))],
        out_shape=[jax.ShapeDtypeStruct((n, D_MODEL), F32), jax.ShapeDtypeStruct((n // tn, 1, tn), F32)],
        scratch_shapes=[pltpu.VMEM((D_MODEL, tn), BF16),
                        pltpu.VMEM((tn // 128, 2 * PEER_HEADS * PEER_NKEYS, 128), F32),
                        pltpu.VMEM((tn // 128, 2 * PEER_HEADS * PEER_NKEYS, 128), F32),
                        pltpu.VMEM((PEER_NKEYS, 8, 128), F32),
                        pltpu.VMEM((PEER_HEADS, 8 + PEER_TOPK, tn), F32),
                        pltpu.VMEM((2, PEER_SUB, PEER_EB, tn), BF16),
                        pltpu.VMEM((tn // 128, PEER_EB, 128), F32),
                        pltpu.VMEM((PEER_SUB, tn // 128, PEER_EB, 128), F32),
                        pltpu.VMEM((D_MODEL, tn), F32),
                        pltpu.VMEM((8, tn), F32)],
        compiler_params=_cparams(("parallel", "arbitrary"), 56),
        name="peer_ln2",
    )(x, lp['peer_wq_t'], lp['peer_keys'], lp['peer_u'], lp['peer_v_t'], lp['ln2_w'], lp['ln2_b'])
    return lax.cond(jnp.any(tie_flags > 0.0), lambda: _peer_tie_fallback(x, lp), lambda: out)


def _peer_tie_fallback(x, lp):
    wq, keys, u_tab, v_tab = lp['peer_raw']
    n = x.shape[0]
    blk = min(256, n)

    def block(xb):
        q = (xb @ wq).reshape(blk, PEER_HEADS, 2, PEER_NKEYS)
        s = jnp.einsum('nhpc,hpkc->nhpk', q, keys).astype(F32)
        s_top, i_top = lax.top_k(s, PEER_TOPK)
        cand = (s_top[:, :, 0, :, None] + s_top[:, :, 1, None, :]).reshape(blk, PEER_HEADS, -1)
        cand_idx = (i_top[:, :, 0, :, None] * PEER_NKEYS + i_top[:, :, 1, None, :]).reshape(blk, PEER_HEADS, -1)
        best, pick = lax.top_k(cand, PEER_TOPK)
        eidx = jnp.take_along_axis(cand_idx, pick, axis=-1)
        gw = jax.nn.softmax(best, axis=-1)
        act = jax.nn.gelu(jnp.einsum('nhkd,nd->nhk', u_tab[eidx], xb).astype(F32))
        return jnp.einsum('nhk,nhkd->nd', gw * act, v_tab[eidx])

    y = lax.map(block, x.reshape(-1, blk, D_MODEL)).reshape(n, D_MODEL)
    return _layer_norm(ALPHA * x + y, lp['ln2_w'], lp['ln2_b'])


DEC_N = 128
HEAD_STATE = HEAD_DIM * HEAD_DIM


def _col(v):
    return jnp.broadcast_to(v.reshape(-1, 1), (v.size, DEC_N))


def _load_state_t(s_ref, st):
    st[...] = s_ref[...].T.reshape(HEAD_DIM, HEAD_DIM, DEC_N)


def _store_state_t(so_ref, st):
    so_ref[...] = st[...].reshape(HEAD_STATE, DEC_N).T


def _ret_dec_kernel(c_ref, cos_ref, sin_ref, gw_ref, gb_ref, s_ref, o_ref, so_ref, ct, qk, st):
    h = pl.program_id(0)
    ct[...] = c_ref[...].T
    r0 = pl.multiple_of(h * HEAD_DIM, HEAD_DIM)

    def rot(x):
        partner = jnp.concatenate([x[HEAD_DIM // 2:], x[:HEAD_DIM // 2]], axis=0)
        return x * cos_ref[...] + partner * sin_ref[...]

    qk[0] = rot(ct[pl.ds(r0, HEAD_DIM), :])
    qk[1] = rot(ct[pl.ds(256 + r0, HEAD_DIM), :]) * (HEAD_DIM ** -0.5)
    v = ct[pl.ds(512 + r0, HEAD_DIM), :]
    g = ct[pl.ds(768 + r0, HEAD_DIM), :]
    gamma = jnp.exp(jnp.zeros((1, 1), F32) + jnp.where(
        h == 0, RET_LOG_GAMMA[0], jnp.where(h == 1, RET_LOG_GAMMA[1],
                                            jnp.where(h == 2, RET_LOG_GAMMA[2], RET_LOG_GAMMA[3]))))
    _load_state_t(s_ref, st)

    def body(k, o):
        s_new = gamma * st[k] + qk[1, pl.ds(k, 1), :] * v
        st[k] = s_new
        return o + qk[0, pl.ds(k, 1), :] * s_new

    o = lax.fori_loop(0, HEAD_DIM, body, jnp.zeros((HEAD_DIM, DEC_N), F32))
    mu = jnp.mean(o, axis=0, keepdims=True)
    d = o - mu
    var = jnp.mean(d * d, axis=0, keepdims=True)
    o_ref[...] = (d * lax.rsqrt(var + LN_EPS) * gw_ref[...] + gb_ref[...]) * _silu(g)
    _store_state_t(so_ref, st)


def _hgrn_dec_kernel(c_ref, lb_ref, nw_ref, s_ref, o_ref, so_ref, ct, qk, st):
    h = pl.program_id(0)
    ct[...] = c_ref[...].T
    r0 = pl.multiple_of(h * HEAD_DIM, HEAD_DIM)
    lb = lb_ref[...]
    qk[0] = _silu(ct[pl.ds(r0, HEAD_DIM), :])
    forget = lb + (1.0 - lb) * _sigmoid(ct[pl.ds(256 + r0, HEAD_DIM), :])
    qk[1] = forget
    qk[2] = 1.0 - forget
    v = ct[pl.ds(512 + r0, HEAD_DIM), :]
    g = ct[pl.ds(768 + r0, HEAD_DIM), :]
    _load_state_t(s_ref, st)

    def body(k, o):
        s_new = qk[1, pl.ds(k, 1), :] * st[k] + qk[2, pl.ds(k, 1), :] * v
        st[k] = s_new
        return o + qk[0, pl.ds(k, 1), :] * s_new

    o = lax.fori_loop(0, HEAD_DIM, body, jnp.zeros((HEAD_DIM, DEC_N), F32))
    ms = jnp.mean(o * o, axis=0, keepdims=True)
    o_ref[...] = o * lax.rsqrt(ms + LN_EPS) * nw_ref[...] * _silu(g)
    _store_state_t(so_ref, st)


def _rwkv_dec_kernel(c_ref, sh_ref, mu_ref, w0_ref, w2_ref, a0_ref, a2_ref, g2_ref, kkw_ref, ka_ref,
                     rk_ref, lnw_ref, lnb_ref, s_ref, o_ref, so_ref, vt, st, osc):
    h = pl.program_id(0)
    r, lw, k2, v, kkn, a, g = _rwkv_token_mix(c_ref[...], sh_ref[...], mu_ref, w0_ref, w2_ref, a0_ref, a2_ref,
                                              g2_ref, kkw_ref, ka_ref)
    for idx, arr in enumerate((r, jnp.exp(lw), k2, v, kkn, a, g)):
        vt[idx] = arr.T
    r0 = pl.multiple_of(h * HEAD_DIM, HEAD_DIM)
    rows = pl.ds(r0, HEAD_DIM)
    rh, wh, kh, kkh, ah = vt[0, rows, :], vt[1, rows, :], vt[2, rows, :], vt[4, rows, :], vt[5, rows, :]
    vh, gh = vt[3, rows, :], vt[6, rows, :]
    kka = kkh * ah
    _load_state_t(s_ref, st)

    def body(vi, carry):
        s_old = st[vi]
        sa = jnp.sum(s_old * (-kkh), axis=0, keepdims=True)
        s_new = s_old * wh + sa * kka + vt[3, pl.ds(r0 + vi, 1), :] * kh
        st[vi] = s_new
        osc[pl.ds(vi, 1), :] = jnp.sum(s_new * rh, axis=0, keepdims=True)
        return carry

    lax.fori_loop(0, HEAD_DIM, body, 0)
    o = osc[...]
    mu = jnp.mean(o, axis=0, keepdims=True)
    d = o - mu
    var = jnp.mean(d * d, axis=0, keepdims=True)
    on = d * lax.rsqrt(var + RW_GN_EPS) * lnw_ref[...] + lnb_ref[...]
    bonus = jnp.sum(rh * kh * rk_ref[...], axis=0, keepdims=True) * vh
    o_ref[...] = (on + bonus) * gh
    _store_state_t(so_ref, st)


def _dec_specs():
    head_tab = pl.BlockSpec((HEAD_DIM, DEC_N), lambda h: (h, 0))
    state = pl.BlockSpec((DEC_N, HEAD_STATE), lambda h: (0, h))
    out = pl.BlockSpec((None, HEAD_DIM, DEC_N), lambda h: (h, 0, 0))
    return head_tab, state, out


def _dec_out_shapes():
    return [jax.ShapeDtypeStruct((N_HEADS, HEAD_DIM, DEC_N), F32),
            jax.ShapeDtypeStruct((DEC_N, N_HEADS * HEAD_STATE), F32)]


def _dec_finish(o_t, s_new):
    return o_t.reshape(BRANCH_W, DEC_N).T, s_new.reshape(DEC_N, N_HEADS, HEAD_DIM, HEAD_DIM)


def _ret_decode(cols, state, cos_c, sin_c, lp):
    head_tab, st_spec, out_spec = _dec_specs()
    same = pl.BlockSpec((HEAD_DIM, DEC_N), lambda h: (0, 0))
    o_t, s_new = pl.pallas_call(
        _ret_dec_kernel,
        grid=(N_HEADS,),
        in_specs=[pl.BlockSpec((DEC_N, 1024), lambda h: (0, 0)), same, same, head_tab, head_tab, st_spec],
        out_specs=[out_spec, st_spec],
        out_shape=_dec_out_shapes(),
        scratch_shapes=[pltpu.VMEM((1024, DEC_N), F32), pltpu.VMEM((2, HEAD_DIM, DEC_N), F32),
                        pltpu.VMEM((HEAD_DIM, HEAD_DIM, DEC_N), F32)],
        compiler_params=_cparams(("arbitrary",), 40),
        name="ret_decode",
    )(cols, cos_c, sin_c, _col(lp['ret_gn_w']), _col(lp['ret_gn_b']), state.reshape(DEC_N, -1))
    return _dec_finish(o_t, s_new)


def _hgrn_decode(cols, state, lp):
    head_tab, st_spec, out_spec = _dec_specs()
    o_t, s_new = pl.pallas_call(
        _hgrn_dec_kernel,
        grid=(N_HEADS,),
        in_specs=[pl.BlockSpec((DEC_N, 1024), lambda h: (0, 1)), head_tab, head_tab, st_spec],
        out_specs=[out_spec, st_spec],
        out_shape=_dec_out_shapes(),
        scratch_shapes=[pltpu.VMEM((1024, DEC_N), F32), pltpu.VMEM((3, HEAD_DIM, DEC_N), F32),
                        pltpu.VMEM((HEAD_DIM, HEAD_DIM, DEC_N), F32)],
        compiler_params=_cparams(("arbitrary",), 40),
        name="hgrn_decode",
    )(cols, _col(lp['hg_lb']), _col(lp['hg_norm_w']), state.reshape(DEC_N, -1))
    return _dec_finish(o_t, s_new)


def _rwkv_decode(cols, shift, state, lp):
    head_tab, st_spec, out_spec = _dec_specs()
    row = lambda w: pl.BlockSpec((1, w), lambda h: (0, 0))
    mat = lambda r, w: pl.BlockSpec((r, w), lambda h: (0, 0))
    o_t, s_new = pl.pallas_call(
        _rwkv_dec_kernel,
        grid=(N_HEADS,),
        in_specs=[pl.BlockSpec((DEC_N, 1024), lambda h: (0, 2)), mat(DEC_N, 1024),
                  row(1024), row(256), mat(64, 256), row(256), mat(64, 256), mat(128, 256),
                  row(256), row(256), head_tab, head_tab, head_tab, st_spec],
        out_specs=[out_spec, st_spec],
        out_shape=_dec_out_shapes(),
        scratch_shapes=[pltpu.VMEM((7, BRANCH_W, DEC_N), F32), pltpu.VMEM((HEAD_DIM, HEAD_DIM, DEC_N), F32),
                        pltpu.VMEM((HEAD_DIM, DEC_N), F32)],
        compiler_params=_cparams(("arbitrary",), 40),
        name="rwkv_decode",
    )(cols, shift, lp['rw_mu'], lp['rw_w0'], lp['rw_w2'], lp['rw_a0'], lp['rw_a2'], lp['rw_g2'],
      lp['rw_kk'], lp['rw_ka'], _col(lp['rw_rk']), _col(lp['rw_lnx_w']), _col(lp['rw_lnx_b']),
      state.reshape(DEC_N, -1))
    return _dec_finish(o_t, s_new)


def _lru_dec_kernel(c_ref, conv_ref, h0_ref, cw_ref, cb_ref, wa_ref, ba_ref, wx_ref, bx_ref, lam_ref,
                    y_ref, h_ref, nconv_ref):
    xb = c_ref[:, 0:BRANCH_W]
    gate = c_ref[:, BRANCH_W:2 * BRANCH_W]
    c0, c1, c2 = conv_ref[0], conv_ref[1], conv_ref[2]
    xc = (c0 * cw_ref[0:1, :] + c1 * cw_ref[1:2, :] + c2 * cw_ref[2:3, :] + xb * cw_ref[3:4, :]) + cb_ref[...]
    a, u = _lru_gates(xc, wa_ref, ba_ref, wx_ref, bx_ref, lam_ref)
    hn = a * h0_ref[...] + u
    h_ref[...] = hn
    y_ref[...] = hn * _gelu(gate)
    nconv_ref[0] = c1
    nconv_ref[1] = c2
    nconv_ref[2] = xb


def _lru_decode(cols, conv, h0, lp):
    full = lambda *s: pl.BlockSpec(s, lambda i: (0,) * len(s))
    return pl.pallas_call(
        _lru_dec_kernel,
        grid=(1,),
        in_specs=[pl.BlockSpec((DEC_N, 512), lambda i: (0, 6)), full(3, DEC_N, BRANCH_W), full(DEC_N, BRANCH_W),
                  full(CONV_W, BRANCH_W), full(1, BRANCH_W), full(BRANCH_W, BRANCH_W), full(1, BRANCH_W),
                  full(BRANCH_W, BRANCH_W), full(1, BRANCH_W), full(1, BRANCH_W)],
        out_specs=[full(DEC_N, BRANCH_W), full(DEC_N, BRANCH_W), full(3, DEC_N, BRANCH_W)],
        out_shape=[jax.ShapeDtypeStruct((DEC_N, BRANCH_W), F32), jax.ShapeDtypeStruct((DEC_N, BRANCH_W), F32),
                   jax.ShapeDtypeStruct((3, DEC_N, BRANCH_W), F32)],
        compiler_params=_cparams(("arbitrary",), 32),
        name="lru_decode",
    )(cols, conv, h0, lp['lru_conv_w'], lp['lru_conv_b'], lp['lru_wa_bd'], lp['lru_ba'], lp['lru_wx_bd'],
      lp['lru_bx'], lp['lru_lambda'])


def _prompt_layer(x, p_emb, lp, cos_t, sin_t):
    b, t, d = x.shape
    n = b * t
    xf = x.reshape(n, d)
    o_a, s_ret = _ret_prompt(x, cos_t, sin_t, lp)
    o_b, s_hg = _hgrn_prompt(x, lp['hg_lb'], lp)
    o_c, s_shift, s_rw = _rwkv_prompt(x, lp)
    o_d, s_lru, s_conv = _lru_prompt(x, lp)
    outs = [o.reshape(n, BRANCH_W) for o in (o_a, o_b, o_c, o_d)]
    x1 = _mix(xf, outs, lp, 512)
    x2 = _peer(x1, lp, 512)
    x3 = _ple(x2, p_emb.reshape(n, -1), lp, 512)
    return x3.reshape(b, t, d), (s_ret, s_hg, s_rw, s_shift[:, 0], s_lru[:, 0], s_conv)


def _sample_layer(x, p_emb, state, lp, cos_c, sin_c):
    s_ret, s_hg, s_rw, s_shift, s_lru, s_conv = state
    xf = x.reshape(DEC_N, D_MODEL)
    cols = _proj(xf, lp['w_in_br'], DEC_N)
    o_a, s_ret = _ret_decode(cols, s_ret, cos_c, sin_c, lp)
    o_b, s_hg = _hgrn_decode(cols, s_hg, lp)
    o_c, s_rw = _rwkv_decode(cols, s_shift, s_rw, lp)
    o_d, s_lru, s_conv_t = _lru_decode(cols, jnp.swapaxes(s_conv, 0, 1), s_lru, lp)
    x1 = _mix(xf, [o_a, o_b, o_c, o_d], lp, DEC_N)
    x2 = _peer(x1, lp, DEC_N)
    x3 = _ple(x2, p_emb.reshape(DEC_N, -1), lp, DEC_N)
    new_shift = cols[:, 2048:3072]
    return x3.reshape(x.shape), (s_ret, s_hg, s_rw, new_shift, s_lru, jnp.swapaxes(s_conv_t, 0, 1))


def kernel(x_prompt, x_sample, state_ret, state_hgrn, state_rwkv, state_shift, state_lru, state_conv, p_prompt, p_sample, w_in, ret_gn_w, ret_gn_b, hg_lb, hg_norm_w, rw_mu, rw_w0, rw_w2, rw_a0, rw_a2, rw_g2, rw_kk, rw_ka, rw_rk, rw_lnx_w, rw_lnx_b, lru_conv_w, lru_conv_b, lru_wa, lru_ba, lru_wx, lru_bx, lru_lambda, w_branch, w_out, ln1_w, ln1_b, peer_wq, peer_keys, peer_u, peer_v, ln2_w, ln2_b, ple_w, ple_gate_w, ple_gate_b):
    params = dict(w_in=w_in, ret_gn_w=ret_gn_w, ret_gn_b=ret_gn_b, hg_norm_w=hg_norm_w, rw_mu=rw_mu, rw_w0=rw_w0,
                  rw_w2=rw_w2, rw_a0=rw_a0, rw_a2=rw_a2, rw_g2=rw_g2, rw_kk=rw_kk, rw_ka=rw_ka, rw_rk=rw_rk,
                  rw_lnx_w=rw_lnx_w, rw_lnx_b=rw_lnx_b, lru_conv_w=lru_conv_w, lru_conv_b=lru_conv_b,
                  lru_wa=lru_wa, lru_ba=lru_ba, lru_wx=lru_wx, lru_bx=lru_bx, lru_lambda=lru_lambda,
                  w_branch=w_branch, w_out=w_out, ln1_w=ln1_w, ln1_b=ln1_b, peer_wq=peer_wq, peer_keys=peer_keys,
                  peer_u=peer_u, peer_v=peer_v, ln2_w=ln2_w, ln2_b=ln2_b, ple_w=ple_w, ple_gate_w=ple_gate_w,
                  ple_gate_b=ple_gate_b)
    lb_cum = jnp.cumsum(jax.nn.softmax(hg_lb.astype(F32), axis=0), axis=0)
    lb_all = lb_cum - lb_cum[0:1]
    t_prompt = x_prompt.shape[1]
    past_len = 16384
    cos_t, sin_t = _rope_tables(jnp.arange(t_prompt))
    cos_s, sin_s = _rope_tables(past_len + jnp.arange(1))
    cos_c = _col(cos_s[0, :HEAD_DIM])
    sin_c = _col(sin_s[0, :HEAD_DIM])
    h_p, h_s = x_prompt, x_sample
    new_p, new_s = [], []
    for l in range(DEPTH):
        lp = _layer_params(params, l, lb_all)
        h_p, st_p = _prompt_layer(h_p, p_prompt[l], lp, cos_t, sin_t)
        st_in = (state_ret[l], state_hgrn[l], state_rwkv[l], state_shift[l], state_lru[l], state_conv[l])
        h_s, st_s = _sample_layer(h_s, p_sample[l], st_in, lp, cos_c, sin_c)
        new_p.append(st_p)
        new_s.append(st_s)
    outs_p = [jnp.stack(z) for z in zip(*new_p)]
    outs_s = [jnp.stack(z) for z in zip(*new_s)]
    return (h_p, h_s, *outs_p, *outs_s)
```

```python
import functools
import math

import jax
import jax.numpy as jnp
from jax import lax
from jax.experimental import pallas as pl
from jax.experimental.pallas import tpu as pltpu

F32 = jnp.float32
BF16 = jnp.bfloat16
HIGHEST = lax.Precision.HIGHEST

D_MODEL = 1024
BRANCH_W = 256
N_HEADS = 4
HEAD_DIM = 64
RET_LOG_GAMMA = tuple(math.log1p(-(2.0 ** (-5.0 - h))) for h in range(N_HEADS))
ROPE_BASE = 10000.0
RW_GN_EPS = 64e-5
LN_EPS = 1e-5
LRU_C = 8.0
CONV_W = 4
DEPTH = 2
ALPHA = (2 * DEPTH) ** 0.25
BRANCH_COLS = 3584
MIB = 1024 * 1024


def _cparams(semantics, vmem_mib):
    return pltpu.CompilerParams(dimension_semantics=semantics, vmem_limit_bytes=vmem_mib * MIB)


def _mm(a, b):
    return jnp.dot(a.astype(BF16), b.astype(BF16), preferred_element_type=F32)


def _mm_nt(a, b):
    return lax.dot_general(a.astype(BF16), b.astype(BF16), (((1,), (1,)), ((), ())), preferred_element_type=F32)


def _mm_tn(a, b):
    return lax.dot_general(a.astype(BF16), b.astype(BF16), (((0,), (0,)), ((), ())), preferred_element_type=F32)


def _transpose_exact(a):
    n = a.shape[0]
    eye = (lax.broadcasted_iota(jnp.int32, (n, n), 0) == lax.broadcasted_iota(jnp.int32, (n, n), 1)).astype(F32)
    return lax.dot_general(a, eye, (((0,), (0,)), ((), ())), preferred_element_type=F32, precision=HIGHEST)


def _select_rows_exact(sel, x):
    x1 = x.astype(BF16)
    r1 = x - x1.astype(F32)
    x2 = r1.astype(BF16)
    x3 = (r1 - x2.astype(F32)).astype(BF16)
    dot = lambda p: jnp.dot(sel, p, preferred_element_type=F32)
    return (dot(x1) + dot(x2)) + dot(x3)


def _tril_ones(n):
    r = lax.broadcasted_iota(jnp.int32, (n, n), 0)
    c = lax.broadcasted_iota(jnp.int32, (n, n), 1)
    return (r >= c).astype(BF16)


def _sigmoid(x):
    return 1.0 / (1.0 + jnp.exp(-x))


def _silu(x):
    return x * _sigmoid(x)


def _gelu(x):
    c = 0.7978845608028654
    half = 0.5 * x
    return half + half * jnp.tanh(x * (c + (c * 0.044715) * (x * x)))


def _softplus(x):
    return jnp.maximum(x, 0.0) + jnp.log1p(jnp.exp(-jnp.abs(x)))


def _head(x, h):
    return x[:, h * HEAD_DIM:(h + 1) * HEAD_DIM]


def _project_rows(x_ref, w_ref, nb, tc):
    x = x_ref[...].reshape(nb * tc, x_ref.shape[-1]).astype(BF16)
    cols = jnp.dot(x, w_ref[...], preferred_element_type=F32)
    return [cols[n * tc:(n + 1) * tc] for n in range(nb)]


def _block_diag(w):
    n, c, d = w.shape
    eye = jnp.eye(n, dtype=w.dtype)
    return (eye[:, None, :, None] * w[:, :, None, :]).reshape(n * c, n * d)


def _rope_tables(pos):
    half = HEAD_DIM // 2
    inv_freq = ROPE_BASE ** (-jnp.arange(half, dtype=F32) / half)
    ang = pos.astype(F32)[:, None] * inv_freq[None, :]
    cos = jnp.cos(ang)
    sin = jnp.sin(ang)
    cos_t = jnp.tile(jnp.concatenate([cos, cos], axis=-1), (1, N_HEADS))
    sin_t = jnp.tile(jnp.concatenate([-sin, sin], axis=-1), (1, N_HEADS))
    return cos_t, sin_t


def _layer_params(p, l, lb_all):
    r2 = lambda a: a.reshape(1, -1)
    w_in = p['w_in'][l]
    lp = {
        'w_in_br': w_in[:, :BRANCH_COLS].astype(BF16),
        'w_gate': w_in[:, BRANCH_COLS:].astype(BF16),
        'ret_gn_w': r2(p['ret_gn_w'][l]), 'ret_gn_b': r2(p['ret_gn_b'][l]),
        'hg_lb': r2(lb_all[l]), 'hg_norm_w': r2(p['hg_norm_w'][l]),
        'rw_mu': r2(p['rw_mu'][l]), 'rw_w0': r2(p['rw_w0'][l]), 'rw_w2': p['rw_w2'][l].astype(BF16),
        'rw_a0': r2(p['rw_a0'][l]), 'rw_a2': p['rw_a2'][l].astype(BF16), 'rw_g2': p['rw_g2'][l].astype(BF16),
        'rw_kk': r2(p['rw_kk'][l]), 'rw_ka': r2(p['rw_ka'][l]), 'rw_rk': r2(p['rw_rk'][l]),
        'rw_lnx_w': r2(p['rw_lnx_w'][l]), 'rw_lnx_b': r2(p['rw_lnx_b'][l]),
        'lru_conv_w': p['lru_conv_w'][l], 'lru_conv_b': r2(p['lru_conv_b'][l]),
        'lru_wa_bd': _block_diag(p['lru_wa'][l]).astype(BF16), 'lru_ba': r2(p['lru_ba'][l]),
        'lru_wx_bd': _block_diag(p['lru_wx'][l]).astype(BF16), 'lru_bx': r2(p['lru_bx'][l]),
        'lru_lambda': r2(p['lru_lambda'][l]),
        'w_branch': p['w_branch'][l].astype(BF16), 'w_out': p['w_out'][l].astype(BF16),
        'ln1_w': r2(p['ln1_w'][l]), 'ln1_b': r2(p['ln1_b'][l]),
        'peer_wq_t': p['peer_wq'][l].T.astype(BF16),
        'peer_keys': p['peer_keys'][l].reshape(16, 128, 128).astype(BF16),
        'peer_u': p['peer_u'][l].astype(BF16),
        'peer_raw': (p['peer_wq'][l], p['peer_keys'][l], p['peer_u'][l], p['peer_v'][l]),
        'peer_v_t': jnp.swapaxes(p['peer_v'][l].astype(BF16).reshape(-1, PEER_EB, D_MODEL), 1, 2),
        'ln2_w': r2(p['ln2_w'][l]), 'ln2_b': r2(p['ln2_b'][l]),
        'ple_w': p['ple_w'][l].astype(BF16), 'ple_gate_w': p['ple_gate_w'][l].astype(BF16),
        'ple_gate_b': r2(p['ple_gate_b'][l]),
    }
    return lp


def _proj_kernel(x_ref, w_ref, o_ref):
    o_ref[...] = jnp.dot(x_ref[...].astype(BF16), w_ref[...], preferred_element_type=F32)


def _proj(x, w_bf16, tn):
    n, k = x.shape
    m = w_bf16.shape[1]
    return pl.pallas_call(
        _proj_kernel,
        grid=(n // tn,),
        in_specs=[pl.BlockSpec((tn, k), lambda i: (i, 0)),
                  pl.BlockSpec((k, m), lambda i: (0, 0))],
        out_specs=pl.BlockSpec((tn, m), lambda i: (i, 0)),
        out_shape=jax.ShapeDtypeStruct((n, m), F32),
        compiler_params=_cparams(("parallel",), 48),
        name="in_proj",
    )(x, w_bf16)


def _lru_gates(xc, wa_ref, ba_ref, wx_ref, bx_ref, lam_ref):
    r = _sigmoid(_mm(xc, wa_ref[...]) + ba_ref[...])
    i = _sigmoid(_mm(xc, wx_ref[...]) + bx_ref[...])
    log_a = -LRU_C * r * _softplus(-lam_ref[...])
    a = jnp.exp(log_a)
    u = jnp.sqrt(1.0 - jnp.exp(2.0 * log_a)) * (i * xc)
    return a, u


def _lru_kernel(x_ref, w_ref, cw_ref, cb_ref, wa_ref, ba_ref, wx_ref, bx_ref, lam_ref,
                y_ref, h_ref, conv_ref, xbuf, hcar, a_s, u_s, hs, *, tc):
    c = pl.program_id(1)

    @pl.when(c == 0)
    def _():
        xbuf[0:8, :] = jnp.zeros((8, BRANCH_W), F32)
        hcar[...] = jnp.zeros((1, BRANCH_W), F32)

    cols = jnp.dot(x_ref[...].astype(BF16), w_ref[...], preferred_element_type=F32)
    xb = cols[:, 0:BRANCH_W]
    gate = cols[:, BRANCH_W:2 * BRANCH_W]
    xbuf[8:8 + tc, :] = xb
    xc = (xbuf[pl.ds(5, tc), :] * cw_ref[0:1, :] + xbuf[pl.ds(6, tc), :] * cw_ref[1:2, :]
          + xbuf[pl.ds(7, tc), :] * cw_ref[2:3, :] + xb * cw_ref[3:4, :]) + cb_ref[...]
    a, u = _lru_gates(xc, wa_ref, ba_ref, wx_ref, bx_ref, lam_ref)
    a_s[...] = a
    u_s[...] = u

    def body(t, h):
        h = a_s[pl.ds(t, 1), :] * h + u_s[pl.ds(t, 1), :]
        hs[pl.ds(t, 1), :] = h
        return h

    h = lax.fori_loop(0, tc, body, hcar[...], unroll=8)
    hcar[...] = h
    y_ref[...] = hs[...] * _gelu(gate)
    xbuf[0:8, :] = xbuf[tc:tc + 8, :]

    @pl.when(c == pl.num_programs(1) - 1)
    def _():
        h_ref[...] = h
        conv_ref[...] = xbuf[5:8, :]


def _lru_prompt(x3, lp, tc=256):
    b, t, d = x3.shape
    row = lambda: pl.BlockSpec((1, BRANCH_W), lambda i, j: (0, 0))
    full = lambda r: pl.BlockSpec((r, BRANCH_W), lambda i, j: (0, 0))
    return pl.pallas_call(
        functools.partial(_lru_kernel, tc=tc),
        grid=(b, t // tc),
        in_specs=[pl.BlockSpec((None, tc, d), lambda i, j: (i, j, 0)),
                  pl.BlockSpec((d, 2 * BRANCH_W), lambda i, j: (0, 6)),
                  full(CONV_W), row(), full(BRANCH_W), row(), full(BRANCH_W), row(), row()],
        out_specs=[pl.BlockSpec((None, tc, BRANCH_W), lambda i, j: (i, j, 0)),
                   pl.BlockSpec((None, 1, BRANCH_W), lambda i, j: (i, 0, 0)),
                   pl.BlockSpec((None, CONV_W - 1, BRANCH_W), lambda i, j: (i, 0, 0))],
        out_shape=[jax.ShapeDtypeStruct((b, t, BRANCH_W), F32),
                   jax.ShapeDtypeStruct((b, 1, BRANCH_W), F32),
                   jax.ShapeDtypeStruct((b, CONV_W - 1, BRANCH_W), F32)],
        scratch_shapes=[pltpu.VMEM((tc + 8, BRANCH_W), F32), pltpu.VMEM((1, BRANCH_W), F32),
                        pltpu.VMEM((tc, BRANCH_W), F32), pltpu.VMEM((tc, BRANCH_W), F32),
                        pltpu.VMEM((tc, BRANCH_W), F32)],
        compiler_params=_cparams(("parallel", "arbitrary"), 32),
        name="lru_prompt",
    )(x3, lp['w_in_br'], lp['lru_conv_w'], lp['lru_conv_b'], lp['lru_wa_bd'], lp['lru_ba'], lp['lru_wx_bd'],
      lp['lru_bx'], lp['lru_lambda'])


def _rotary(x, cos, sin_signed, first_half):
    partner = jnp.where(first_half, pltpu.roll(x, BRANCH_W - 32, 1), pltpu.roll(x, 32, 1))
    return x * cos + partner * sin_signed


def _ret_kernel(x_ref, w_ref, cos_ref, sin_ref, gw_ref, gb_ref, o_ref, s_ref, S, *, tc, nb):
    c = pl.program_id(1)

    @pl.when(c == 0)
    def _():
        S[...] = jnp.zeros(S.shape, F32)

    lane = lax.broadcasted_iota(jnp.int32, (tc, BRANCH_W), 1)
    first_half = (lane % HEAD_DIM) < (HEAD_DIM // 2)
    cos = cos_ref[...]
    sin = sin_ref[...]
    row = lax.broadcasted_iota(jnp.int32, (tc, tc), 0)
    col = lax.broadcasted_iota(jnp.int32, (tc, tc), 1)
    causal = row >= col
    dist = jnp.where(causal, row - col, 0).astype(F32)
    tpos = lax.broadcasted_iota(jnp.int32, (tc, HEAD_DIM), 0).astype(F32)
    decay = [jnp.where(causal, jnp.exp(dist * RET_LOG_GAMMA[h]), 0.0) for h in range(N_HEADS)]
    q_in = [jnp.exp((tpos + 1.0) * RET_LOG_GAMMA[h]) for h in range(N_HEADS)]
    k_out = [jnp.exp((tc - 1.0 - tpos) * RET_LOG_GAMMA[h]) for h in range(N_HEADS)]
    cols = _project_rows(x_ref, w_ref, nb, tc)
    q = [_rotary(cols[n][:, 0:256], cos, sin, first_half) for n in range(nb)]
    k = [_rotary(cols[n][:, 256:512], cos, sin, first_half) * (HEAD_DIM ** -0.5) for n in range(nb)]
    chains = [(n, h) for n in range(nb) for h in range(N_HEADS)]
    scores = {(n, h): _mm_nt(_head(q[n], h), _head(k[n], h)) * decay[h] for n, h in chains}
    out = {(n, h): _mm(scores[n, h], _head(cols[n][:, 512:768], h)) + _mm(_head(q[n], h) * q_in[h], S[n, h])
           for n, h in chains}
    for n, h in chains:
        S[n, h] = (math.exp(tc * RET_LOG_GAMMA[h]) * S[n, h]
                   + _mm_tn(_head(k[n], h) * k_out[h], _head(cols[n][:, 512:768], h)))
    for n, h in chains:
        oh = out[n, h]
        mu = jnp.mean(oh, axis=-1, keepdims=True)
        d = oh - mu
        var = jnp.mean(d * d, axis=-1, keepdims=True)
        on = d * lax.rsqrt(var + LN_EPS) * _head(gw_ref[...], h) + _head(gb_ref[...], h)
        o_ref[n, :, h * HEAD_DIM:(h + 1) * HEAD_DIM] = on * _silu(_head(cols[n][:, 768:1024], h))

    @pl.when(c == pl.num_programs(1) - 1)
    def _():
        s_ref[...] = S[...]


def _ret_prompt(x3, cos_t, sin_t, lp, tc=128, nb=2):
    b, t, d = x3.shape
    nb = min(nb, b)
    row = lambda: pl.BlockSpec((1, BRANCH_W), lambda i, j: (0, 0))
    return pl.pallas_call(
        functools.partial(_ret_kernel, tc=tc, nb=nb),
        grid=(b // nb, t // tc),
        in_specs=[pl.BlockSpec((nb, tc, d), lambda i, j: (i, j, 0)),
                  pl.BlockSpec((d, 1024), lambda i, j: (0, 0)),
                  pl.BlockSpec((tc, BRANCH_W), lambda i, j: (j, 0)),
                  pl.BlockSpec((tc, BRANCH_W), lambda i, j: (j, 0)), row(), row()],
        out_specs=[pl.BlockSpec((nb, tc, BRANCH_W), lambda i, j: (i, j, 0)),
                   pl.BlockSpec((nb, N_HEADS, HEAD_DIM, HEAD_DIM), lambda i, j: (i, 0, 0, 0))],
        out_shape=[jax.ShapeDtypeStruct((b, t, BRANCH_W), F32),
                   jax.ShapeDtypeStruct((b, N_HEADS, HEAD_DIM, HEAD_DIM), F32)],
        scratch_shapes=[pltpu.VMEM((nb, N_HEADS, HEAD_DIM, HEAD_DIM), F32)],
        compiler_params=_cparams(("parallel", "arbitrary"), 32),
        name="ret_prompt",
    )(x3, lp['w_in_br'], cos_t, sin_t, lp['ret_gn_w'], lp['ret_gn_b'])


HGRN_CHUNK = 64
HGRN_LEVELS = tuple(HGRN_CHUNK >> l for l in range(1, 7))


def _hgrn_select_matrix():
    t = jnp.arange(HGRN_CHUNK)
    s = jnp.arange(HGRN_CHUNK)
    blocks = [s[None, :] <= t[:, None]]
    for m in HGRN_LEVELS:
        mid = (t // (2 * m)) * (2 * m) + m - 1
        blocks.append(s[None, :] <= mid[:, None])
    return jnp.concatenate(blocks, axis=0).astype(BF16)


def _hgrn_kernel(x_ref, w_ref, lb_ref, nw_ref, sel_ref, o_ref, s_ref, St, *, tc, nb):
    c = pl.program_id(1)

    @pl.when(c == 0)
    def _():
        St[...] = jnp.zeros(St.shape, F32)

    lb = lb_ref[...]
    row = lax.broadcasted_iota(jnp.int32, (tc, BRANCH_W), 0)
    r64 = lax.broadcasted_iota(jnp.int32, (tc, tc), 0)
    c64 = lax.broadcasted_iota(jnp.int32, (tc, tc), 1)
    tok = []
    all_cols = _project_rows(x_ref, w_ref, nb, tc)
    for n in range(nb):
        cols = all_cols[n]
        q = _silu(cols[:, 0:256])
        forget = lb + (1.0 - lb) * _sigmoid(cols[:, 256:512])
        kk = 1.0 - forget
        sums = _select_rows_exact(sel_ref[...], jnp.log(forget))
        tok.append(dict(q=q, kk=kk, sums=sums, b=sums[0:tc],
                        v=cols[:, 512:768], g=cols[:, 768:1024]))
    chains = [(n, h) for n in range(nb) for h in range(N_HEADS)]
    scores = {(n, h): jnp.where(r64 == c64, _mm_nt(_head(tok[n]['q'], h), _head(tok[n]['kk'], h)), 0.0)
              for n, h in chains}
    for lvl, m in enumerate(HGRN_LEVELS):
        sh = m.bit_length() - 1
        upper = ((row >> sh) & 1) == 1
        same = (r64 >> (sh + 1)) == (c64 >> (sh + 1))
        scaled = []
        for n in range(nb):
            t = tok[n]
            e = jnp.exp(-jnp.abs(t['b'] - t['sums'][(lvl + 1) * tc:(lvl + 2) * tc]))
            scaled.append((jnp.where(upper, t['q'] * e, 0.0), jnp.where(upper, 0.0, t['kk'] * e)))
        for n, h in chains:
            scores[n, h] = scores[n, h] + jnp.where(same, _mm_nt(_head(scaled[n][0], h), _head(scaled[n][1], h)), 0.0)
    for n in range(nb):
        t = tok[n]
        b = t['b']
        v = t['v']
        g = t['g']
        b_end = b[tc - 1:tc, :]
        qd = t['q'] * jnp.exp(b)
        kd = t['kk'] * jnp.exp(b_end - b)
        for h in range(N_HEADS):
            vh = _head(v, h)
            oh = _mm(scores[n, h], vh) + _mm_nt(_head(qd, h), St[n, h])
            St[n, h] = St[n, h] * jnp.exp(_head(b_end, h)) + _mm_tn(vh, _head(kd, h))
            ms = jnp.mean(oh * oh, axis=-1, keepdims=True)
            on = oh * lax.rsqrt(ms + LN_EPS) * _head(nw_ref[...], h)
            o_ref[n, :, h * HEAD_DIM:(h + 1) * HEAD_DIM] = on * _silu(_head(g, h))

    @pl.when(c == pl.num_programs(1) - 1)
    def _():
        for n in range(nb):
            for h in range(N_HEADS):
                s_ref[n, h] = _transpose_exact(St[n, h])


def _hgrn_prompt(x3, lb, lp, nb=8):
    b, t, d = x3.shape
    nb = min(nb, b)
    tc = HGRN_CHUNK
    row = lambda: pl.BlockSpec((1, BRANCH_W), lambda i, j: (0, 0))
    return pl.pallas_call(
        functools.partial(_hgrn_kernel, tc=tc, nb=nb),
        grid=(b // nb, t // tc),
        in_specs=[pl.BlockSpec((nb, tc, d), lambda i, j: (i, j, 0)),
                  pl.BlockSpec((d, 1024), lambda i, j: (0, 1)), row(), row(),
                  pl.BlockSpec((7 * tc, tc), lambda i, j: (0, 0))],
        out_specs=[pl.BlockSpec((nb, tc, BRANCH_W), lambda i, j: (i, j, 0)),
                   pl.BlockSpec((nb, N_HEADS, HEAD_DIM, HEAD_DIM), lambda i, j: (i, 0, 0, 0))],
        out_shape=[jax.ShapeDtypeStruct((b, t, BRANCH_W), F32),
                   jax.ShapeDtypeStruct((b, N_HEADS, HEAD_DIM, HEAD_DIM), F32)],
        scratch_shapes=[pltpu.VMEM((nb, N_HEADS, HEAD_DIM, HEAD_DIM), F32)],
        compiler_params=_cparams(("parallel", "arbitrary"), 32),
        name="hgrn_prompt",
    )(x3, lp['w_in_br'], lb, lp['hg_norm_w'], _hgrn_select_matrix())


def _rwkv_token_mix(cols, prev, mu_ref, w0_ref, w2_ref, a0_ref, a2_ref, g2_ref, kkw_ref, ka_ref):
    xs = cols + mu_ref[...] * (prev - cols)
    r = xs[:, 0:256]
    k = xs[:, 256:512]
    v = xs[:, 512:768]
    xg = xs[:, 768:896]
    xw = xs[:, 896:960]
    xa = xs[:, 960:1024]
    w = -_softplus(-(w0_ref[...] + _mm(jnp.tanh(xw), w2_ref[...]))) - 0.5
    lw = -jnp.exp(w)
    a = _sigmoid(a0_ref[...] + _mm(xa, a2_ref[...]))
    g = _mm(_sigmoid(xg), g2_ref[...])
    kk = k * kkw_ref[...]
    parts = []
    for h in range(N_HEADS):
        kh = _head(kk, h)
        nrm = jnp.sqrt(jnp.sum(kh * kh, axis=-1, keepdims=True))
        parts.append(kh / jnp.maximum(nrm, 1e-12))
    kkn = jnp.concatenate(parts, axis=-1)
    k2 = k * (1.0 + (a - 1.0) * ka_ref[...])
    return r, lw, k2, v, kkn, a, g


def _rwkv_out(o, r, k2, v, g, rk_ref, lw_ref, lb_ref, h):
    mu = jnp.mean(o, axis=-1, keepdims=True)
    d = o - mu
    var = jnp.mean(d * d, axis=-1, keepdims=True)
    on = d * lax.rsqrt(var + RW_GN_EPS) * _head(lw_ref[...], h) + _head(lb_ref[...], h)
    bonus = jnp.sum(_head(r, h) * _head(k2, h) * _head(rk_ref[...], h), axis=-1, keepdims=True) * _head(v, h)
    return (on + bonus) * _head(g, h)


def _rwkv_kernel(x_ref, w_ref, mu_ref, w0_ref, w2_ref, a0_ref, a2_ref, g2_ref, kkw_ref, ka_ref, rk_ref,
                 lnw_ref, lnb_ref, y_ref, shift_ref, s_ref, S, last_row, *, tc, nb):
    c = pl.program_id(1)

    @pl.when(c == 0)
    def _():
        S[...] = jnp.zeros(S.shape, F32)
        last_row[...] = jnp.zeros(last_row.shape, F32)

    rr = lax.broadcasted_iota(jnp.int32, (tc, tc), 0)
    cc = lax.broadcasted_iota(jnp.int32, (tc, tc), 1)
    strict = rr > cc
    incl2 = (lax.broadcasted_iota(jnp.int32, (tc, 2 * tc), 0)
             >= (lax.broadcasted_iota(jnp.int32, (tc, 2 * tc), 1) & (tc - 1)))
    tril = _tril_ones(tc)
    n_double = max(1, (tc - 1).bit_length())
    tok = []
    new_shift = []
    all_cols = _project_rows(x_ref, w_ref, nb, tc)
    for n in range(nb):
        cols = all_cols[n]
        new_shift.append(cols[tc - 1:tc, :])
        rowi = lax.broadcasted_iota(jnp.int32, cols.shape, 0)
        prev = jnp.where(rowi == 0, last_row[n], pltpu.roll(cols, 1, 0))
        last_row[n] = cols[tc - 1:tc, :]
        r, lw, k2, v, kkn, a, g = _rwkv_token_mix(cols, prev, mu_ref, w0_ref, w2_ref, a0_ref, a2_ref, g2_ref,
                                                  kkw_ref, ka_ref)
        am = -kkn
        bm = kkn * a
        G = _select_rows_exact(tril, lw)
        g_end = G[tc - 1:tc, :]
        einv = jnp.exp(-G)
        eend = jnp.exp(g_end - G)
        tok.append(dict(r=r, k2=k2, v=v, g=g, g_end=g_end, at=am * jnp.exp(G - lw), rt=r * jnp.exp(G),
                        bt=bm * einv, kt=k2 * einv, bbar=bm * eend, kbar=k2 * eend))
    chains = [(n, h) for n in range(nb) for h in range(N_HEADS)]
    lhs = {(n, h): jnp.concatenate([_head(tok[n]['at'], h), _head(tok[n]['rt'], h)], axis=0) for n, h in chains}
    inter = {(n, h): _mm_nt(lhs[n, h], jnp.concatenate([_head(tok[n]['bt'], h), _head(tok[n]['kt'], h)], axis=0))
             for n, h in chains}
    from_state = {(n, h): _mm_nt(lhs[n, h], S[n, h]) for n, h in chains}
    npow = {ch: jnp.where(strict, inter[ch][0:tc, 0:tc], 0.0) for ch in chains}
    u = {(n, h): from_state[n, h][0:tc]
         + _mm(jnp.where(strict, inter[n, h][0:tc, tc:2 * tc], 0.0), _head(tok[n]['v'], h)) for n, h in chains}
    for j in range(n_double):
        u = {ch: u[ch] + _mm(npow[ch], u[ch]) for ch in chains}
        if j + 1 < n_double:
            npow = {ch: _mm(npow[ch], npow[ch]) for ch in chains}
    for n, h in chains:
        t = tok[n]
        uv = jnp.concatenate([u[n, h], _head(t['v'], h)], axis=0)
        a_r = jnp.where(incl2, inter[n, h][tc:2 * tc, :], 0.0)
        o = from_state[n, h][tc:2 * tc] + _mm(a_r, uv)
        S[n, h] = (S[n, h] * jnp.exp(_head(t['g_end'], h))
                   + _mm_tn(uv, jnp.concatenate([_head(t['bbar'], h), _head(t['kbar'], h)], axis=0)))
        y_ref[n, :, h * HEAD_DIM:(h + 1) * HEAD_DIM] = _rwkv_out(o, t['r'], t['k2'], t['v'], t['g'], rk_ref,
                                                                 lnw_ref, lnb_ref, h)

    @pl.when(c == pl.num_programs(1) - 1)
    def _():
        s_ref[...] = S[...]
        for n in range(nb):
            shift_ref[n] = new_shift[n]


def _rwkv_prompt(x3, lp, tc=64, nb=4):
    b, t, d = x3.shape
    nb = min(nb, b)
    row = lambda w: pl.BlockSpec((1, w), lambda i, j: (0, 0))
    mat = lambda r, w: pl.BlockSpec((r, w), lambda i, j: (0, 0))
    return pl.pallas_call(
        functools.partial(_rwkv_kernel, tc=tc, nb=nb),
        grid=(b // nb, t // tc),
        in_specs=[pl.BlockSpec((nb, tc, d), lambda i, j: (i, j, 0)),
                  pl.BlockSpec((d, 1024), lambda i, j: (0, 2)),
                  row(1024), row(256), mat(64, 256), row(256), mat(64, 256), mat(128, 256),
                  row(256), row(256), row(256), row(256), row(256)],
        out_specs=[pl.BlockSpec((nb, tc, BRANCH_W), lambda i, j: (i, j, 0)),
                   pl.BlockSpec((nb, 1, 1024), lambda i, j: (i, 0, 0)),
                   pl.BlockSpec((nb, N_HEADS, HEAD_DIM, HEAD_DIM), lambda i, j: (i, 0, 0, 0))],
        out_shape=[jax.ShapeDtypeStruct((b, t, BRANCH_W), F32),
                   jax.ShapeDtypeStruct((b, 1, 1024), F32),
                   jax.ShapeDtypeStruct((b, N_HEADS, HEAD_DIM, HEAD_DIM), F32)],
        scratch_shapes=[pltpu.VMEM((nb, N_HEADS, HEAD_DIM, HEAD_DIM), F32), pltpu.VMEM((nb, 1, 1024), F32)],
        compiler_params=_cparams(("parallel", "arbitrary"), 32),
        name="rwkv_prompt",
    )(x3, lp['w_in_br'], lp['rw_mu'], lp['rw_w0'], lp['rw_w2'], lp['rw_a0'], lp['rw_a2'], lp['rw_g2'],
      lp['rw_kk'], lp['rw_ka'], lp['rw_rk'], lp['rw_lnx_w'], lp['rw_lnx_b'])


def _layer_norm(z, w, b):
    mu = jnp.mean(z, axis=-1, keepdims=True)
    d = z - mu
    var = jnp.mean(d * d, axis=-1, keepdims=True)
    return d * lax.rsqrt(var + LN_EPS) * w + b


def _mix_kernel(x_ref, oa_ref, ob_ref, oc_ref, od_ref, wg_ref, wb_ref, wo_ref, lw_ref, lb_ref, out_ref):
    x = x_ref[...]
    xb = x.astype(BF16)
    mixed = None
    for gi, o_ref in enumerate((oa_ref, ob_ref, oc_ref, od_ref)):
        gate = _sigmoid(jnp.dot(xb, wg_ref[:, gi * D_MODEL:(gi + 1) * D_MODEL], preferred_element_type=F32))
        up = jnp.dot(o_ref[...].astype(BF16), wb_ref[gi], preferred_element_type=F32)
        mixed = gate * up if mixed is None else mixed + gate * up
    y = jnp.dot(mixed.astype(BF16), wo_ref[...], preferred_element_type=F32)
    out_ref[...] = _layer_norm(ALPHA * x + y, lw_ref[...], lb_ref[...])


def _mix(x, outs, lp, tn):
    n = x.shape[0]
    tok = lambda w: pl.BlockSpec((tn, w), lambda i: (i, 0))
    const = lambda *s: pl.BlockSpec(s, lambda i: (0,) * len(s))
    return pl.pallas_call(
        _mix_kernel,
        grid=(n // tn,),
        in_specs=[tok(D_MODEL), tok(BRANCH_W), tok(BRANCH_W), tok(BRANCH_W), tok(BRANCH_W),
                  const(D_MODEL, 4 * D_MODEL), const(4, BRANCH_W, D_MODEL), const(D_MODEL, D_MODEL),
                  const(1, D_MODEL), const(1, D_MODEL)],
        out_specs=tok(D_MODEL),
        out_shape=jax.ShapeDtypeStruct((n, D_MODEL), F32),
        compiler_params=_cparams(("parallel",), 48),
        name="mix_ln1",
    )(x, *outs, lp['w_gate'], lp['w_branch'], lp['w_out'], lp['ln1_w'], lp['ln1_b'])


def _ple_kernel(x_ref, p_ref, wg_ref, bg_ref, wp_ref, out_ref):
    x = x_ref[...]
    gate = _sigmoid(jnp.dot(x.astype(BF16), wg_ref[...], preferred_element_type=F32) + bg_ref[...])
    emb = jnp.dot(p_ref[...].astype(BF16), wp_ref[...], preferred_element_type=F32)
    out_ref[...] = x + gate * emb


def _ple(x, p_emb, lp, tn):
    n = x.shape[0]
    tok = lambda w: pl.BlockSpec((tn, w), lambda i: (i, 0))
    const = lambda *s: pl.BlockSpec(s, lambda i: (0,) * len(s))
    return pl.pallas_call(
        _ple_kernel,
        grid=(n // tn,),
        in_specs=[tok(D_MODEL), tok(256), const(D_MODEL, D_MODEL), const(1, D_MODEL), const(256, D_MODEL)],
        out_specs=tok(D_MODEL),
        out_shape=jax.ShapeDtypeStruct((n, D_MODEL), F32),
        compiler_params=_cparams(("parallel",), 32),
        name="ple_gate",
    )(x, p_emb, lp['ple_gate_w'], lp['ple_gate_b'], lp['ple_w'])


PEER_HEADS = 8
PEER_NKEYS = 128
PEER_TOPK = 16
PEER_EB = 1024
PEER_SUB = 2
PEER_TIE_SLOTS = 32
PEER_IGROUP = 2
PEER_JROWS = 64


def _oddeven_merge_sort_pairs(n):
    pairs = []
    p = 1
    while p < n:
        k = p
        while k >= 1:
            for j in range(k % p, n - k, 2 * k):
                for i in range(min(k, n - j - k)):
                    if (i + j) // (p * 2) == (i + j + k) // (p * 2):
                        pairs.append((i + j, i + j + k))
            k //= 2
        p *= 2
    return pairs


def _bitonic_merge_pairs(n):
    pairs = []
    k = n // 2
    while k >= 1:
        pairs.extend((i, i + k) for i in range(n) if (i & k) == 0)
        k //= 2
    return pairs


_SORT16 = _oddeven_merge_sort_pairs(PEER_TOPK)
_MERGE16 = _bitonic_merge_pairs(PEER_TOPK)
_CAND_LEN = tuple(PEER_TOPK // (a + 1) for a in range(PEER_TOPK))


def _network(vals, pairs):
    vals = list(vals)
    for i, j in pairs:
        hi = jnp.maximum(vals[i], vals[j])
        lo = jnp.minimum(vals[i], vals[j])
        vals[i], vals[j] = hi, lo
    return vals


def _top16_merge(x, y):
    return _network([jnp.maximum(x[i], y[PEER_TOPK - 1 - i]) for i in range(PEER_TOPK)], _MERGE16)


def _peer_head_stats(h, s_nat, e_nat, sk, th, tie, g_count):
    rows0 = pl.ds(pl.multiple_of(2 * h * PEER_NKEYS, PEER_NKEYS), PEER_NKEYS)
    rows1 = pl.ds(pl.multiple_of((2 * h + 1) * PEER_NKEYS, PEER_NKEYS), PEER_NKEYS)
    for p, rows in enumerate((rows0, rows1)):
        for g in range(4):
            gg = g % g_count
            sk[:, p * 4 + g, :] = s_nat[gg, rows, :]
    groups = []
    for m in range(PEER_NKEYS // PEER_TOPK):
        groups.append(_network([sk[PEER_TOPK * m + i] for i in range(PEER_TOPK)], _SORT16))
    while len(groups) > 1:
        groups = [_top16_merge(groups[i], groups[i + 1]) for i in range(0, len(groups), 2)]
    top = groups[0]
    low = lax.broadcasted_iota(jnp.int32, (8, 128), 0) < 4
    ta = [jnp.where(low, t, pltpu.roll(t, 4, 0)) for t in top]
    tb = [jnp.where(low, pltpu.roll(t, 4, 0), t) for t in top]
    cand = [[ta[a] + tb[b] for b in range(_CAND_LEN[a])] for a in range(PEER_TOPK)]
    m1 = _network(cand[1] + [cand[a][0] for a in range(PEER_TOPK - 1, 7, -1)], _MERGE16)
    m2 = _network(cand[2] + cand[3] + cand[4] + cand[5] + cand[6], _SORT16)
    t1 = _top16_merge(cand[0], m1)
    t2 = _top16_merge(t1, m2)
    t2[15] = jnp.maximum(t2[15], cand[7][0])
    t2[14] = jnp.maximum(t2[14], cand[7][1])
    theta = t2[0]
    for t in t2[1:]:
        theta = jnp.minimum(theta, t)
    cmax = cand[0][0]
    z = jnp.zeros((8, 128), F32)
    n_kept = jnp.zeros((8, 128), F32)
    for row in cand:
        for cv in row:
            keep = cv >= theta
            z = z + jnp.where(keep, jnp.exp(cv - cmax), 0.0)
            n_kept = n_kept + jnp.where(keep, 1.0, 0.0)
    inv_z = 1.0 / z
    cand_tie = jnp.where(n_kept != float(PEER_TOPK), 1.0, 0.0)
    last1_used = jnp.where(cand[0][PEER_TOPK - 1] >= theta, 1.0, 0.0)
    last0_used = jnp.where(cand[PEER_TOPK - 1][0] >= theta, 1.0, 0.0)
    for g in range(g_count):
        lanes = slice(g * 128, (g + 1) * 128)
        th[h, 0:1, lanes] = theta[g:g + 1, :]
        th[h, 1:2, lanes] = ta[0][g:g + 1, :]
        th[h, 2:3, lanes] = tb[0][g:g + 1, :]
        th[h, 3:4, lanes] = inv_z[g:g + 1, :]
        th[h, 4:5, lanes] = ta[PEER_TOPK - 1][g:g + 1, :]
        th[h, 5:6, lanes] = cand_tie[g:g + 1, :]
        th[h, 6:7, lanes] = last0_used[g:g + 1, :]
        th[h, 7:8, lanes] = last1_used[g:g + 1, :]
        for b in range(PEER_TOPK):
            th[h, 8 + b:9 + b, lanes] = tb[b][g:g + 1, :]
    for g in range(g_count):
        lanes = slice(g * 128, (g + 1) * 128)
        s0 = s_nat[g, rows0, :]
        s1 = s_nat[g, rows1, :]
        e_nat[g, rows0, :] = jnp.exp(s0 - th[h, 1:2, lanes])
        e_nat[g, rows1, :] = jnp.exp(s1 - th[h, 2:3, lanes]) * th[h, 3:4, lanes]
        n0 = jnp.sum(jnp.where(s0 >= th[h, 4:5, lanes], 1.0, 0.0), axis=0, keepdims=True)
        n1 = jnp.sum(jnp.where(s1 >= th[h, 7 + PEER_TOPK:8 + PEER_TOPK, lanes], 1.0, 0.0), axis=0, keepdims=True)
        key_tie = jnp.maximum(jnp.where(n0 != float(PEER_TOPK), th[h, 6:7, lanes], 0.0),
                              jnp.where(n1 != float(PEER_TOPK), th[h, 7:8, lanes], 0.0))
        tie[0:1, lanes] = jnp.maximum(tie[0:1, lanes], jnp.maximum(key_tie, th[h, 5:6, lanes]))
        tau = jnp.full(s0.shape, jnp.inf, F32)
        for b in range(PEER_TOPK):
            sb = th[h, 8 + b:9 + b, lanes]
            tau = jnp.where((s0 + sb) >= th[h, 0:1, lanes], sb, tau)
        s_nat[g, rows0, :] = tau


def _peer_kernel(x_ref, wq_ref, keys_ref, u_ref, vt_ref, lw_ref, lb_ref, out_ref, tie_ref,
                 xtb, s_nat, e_nat, sk, th, wact, wraw, actb, yt, tie, *, tn):
    j = pl.program_id(1)
    g_count = tn // 128

    @pl.when(j == 0)
    def _():
        xtb[...] = x_ref[...].T.astype(BF16)
        for hp in range(2 * PEER_HEADS):
            qt = jnp.dot(wq_ref[hp * 128:(hp + 1) * 128, :], xtb[...], preferred_element_type=F32)
            scores = jnp.dot(keys_ref[hp], qt.astype(BF16), preferred_element_type=F32)
            for g in range(g_count):
                s_nat[g, hp * PEER_NKEYS:(hp + 1) * PEER_NKEYS, :] = scores[:, g * 128:(g + 1) * 128]

        tie[...] = jnp.zeros(tie.shape, F32)

        def head_body(h, carry):
            _peer_head_stats(h, s_nat, e_nat, sk, th, tie, g_count)
            return carry

        lax.fori_loop(0, PEER_HEADS, head_body, 0)
        tie_ref[...] = tie[0:1, :]
        yt[...] = jnp.zeros(yt.shape, F32)

    last = pl.num_programs(1) - 1
    cur = j % 2

    def step(first_stage, second_stage):
        for sb in range(PEER_SUB):
            if first_stage:
                act = jnp.dot(u_ref[sb * PEER_EB:(sb + 1) * PEER_EB, :], xtb[...],
                              preferred_element_type=F32)
                for g in range(g_count):
                    actb[sb, g] = act[:, g * 128:(g + 1) * 128]
            if second_stage:
                yt[...] += jnp.dot(vt_ref[sb], wact[1 - cur, sb], preferred_element_type=F32)
            if not first_stage:
                continue
            i0 = pl.multiple_of((j * PEER_SUB + sb) * (PEER_EB // PEER_NKEYS), 8)
            jr = PEER_JROWS
            for g in range(g_count):
                for jq in range(PEER_NKEYS // jr):
                    for ig in range(PEER_EB // PEER_NKEYS // PEER_IGROUP):
                        acc = [jnp.zeros((jr, 128), F32) for _ in range(PEER_IGROUP)]
                        for h in range(PEER_HEADS):
                            base1 = (2 * h + 1) * PEER_NKEYS + jq * jr
                            s1 = s_nat[g, base1:base1 + jr, :]
                            e1 = e_nat[g, base1:base1 + jr, :]
                            tau0 = s_nat[g, pl.ds(2 * h * PEER_NKEYS + i0, 8), :]
                            e0 = e_nat[g, pl.ds(2 * h * PEER_NKEYS + i0, 8), :]
                            for k in range(PEER_IGROUP):
                                ii = PEER_IGROUP * ig + k
                                sel = s1 >= tau0[ii:ii + 1, :]
                                acc[k] = acc[k] + jnp.where(sel, e0[ii:ii + 1, :] * e1, 0.0)
                        for k in range(PEER_IGROUP):
                            r0 = (PEER_IGROUP * ig + k) * PEER_NKEYS + jq * jr
                            wraw[g, r0:r0 + jr, :] = acc[k]
            for g in range(g_count):
                wact[cur, sb, :, g * 128:(g + 1) * 128] = (wraw[g] * _gelu(actb[sb, g])).astype(BF16)

    @pl.when(j == 0)
    def _():
        step(True, False)

    @pl.when((j > 0) & (j < last))
    def _():
        step(True, True)

    @pl.when(j == last)
    def _():
        step(False, True)
        z = ALPHA * x_ref[...] + yt[...].T
        out_ref[...] = _layer_norm(z, lw_ref[...], lb_ref[...])


def _peer(x, lp, tn):
    n = x.shape[0]
    n_blk = lp['peer_u'].shape[0] // (PEER_EB * PEER_SUB)
    once = lambda *s: pl.BlockSpec(s, lambda i, j: (0,) * len(s), pipeline_mode=pl.Buffered(1))
    const = lambda *s: pl.BlockSpec(s, lambda i, j: (0,) * len(s))
    out, tie_flags = pl.pallas_call(
        functools.partial(_peer_kernel, tn=tn),
        grid=(n // tn, n_blk + 1),
        in_specs=[pl.BlockSpec((tn, D_MODEL), lambda i, j: (i, 0)),
                  once(2 * PEER_HEADS * 128, D_MODEL), once(2 * PEER_HEADS, PEER_NKEYS, 128),
                  pl.BlockSpec((PEER_SUB * PEER_EB, D_MODEL), lambda i, j: (jnp.minimum(j, n_blk - 1), 0)),
                  pl.BlockSpec((PEER_SUB, D_MODEL, PEER_EB), lambda i, j: (jnp.maximum(j - 1, 0), 0, 0)),
                  const(1, D_MODEL), const(1, D_MODEL)],
        out_specs=[pl.BlockSpec((tn, D_MODEL), lambda i, j: (i, 0)),
                   pl.BlockSpec((None, 1, tn), lambda i, j: (i, 0, 0))],
        out_shape=[jax.ShapeDtypeStruct((n, D_MODEL), F32), jax.ShapeDtypeStruct((n // tn, 1, tn), F32)],
        scratch_shapes=[pltpu.VMEM((D_MODEL, tn), BF16),
                        pltpu.VMEM((tn // 128, 2 * PEER_HEADS * PEER_NKEYS, 128), F32),
                        pltpu.VMEM((tn // 128, 2 * PEER_HEADS * PEER_NKEYS, 128), F32),
                        pltpu.VMEM((PEER_NKEYS, 8, 128), F32),
                        pltpu.VMEM((PEER_HEADS, 8 + PEER_TOPK, tn), F32),
                        pltpu.VMEM((2, PEER_SUB, PEER_EB, tn), BF16),
                        pltpu.VMEM((tn // 128, PEER_EB, 128), F32),
                        pltpu.VMEM((PEER_SUB, tn // 128, PEER_EB, 128), F32),
                        pltpu.VMEM((D_MODEL, tn), F32),
                        pltpu.VMEM((8, tn), F32)],
        compiler_params=_cparams(("parallel", "arbitrary"), 56),
        name="peer_ln2",
    )(x, lp['peer_wq_t'], lp['peer_keys'], lp['peer_u'], lp['peer_v_t'], lp['ln2_w'], lp['ln2_b'])
    flags = tie_flags.reshape(n) > 0.0
    count = jnp.sum(flags.astype(jnp.int32))

    def redo_flagged():
        idx = jnp.nonzero(flags, size=PEER_TIE_SLOTS, fill_value=n)[0]
        fixed = _peer_tie_fallback(x[jnp.minimum(idx, n - 1)], lp)
        return out.at[idx].set(fixed, mode='drop')

    return lax.cond(count > PEER_TIE_SLOTS, lambda: _peer_tie_fallback(x, lp),
                    lambda: lax.cond(count > 0, redo_flagged, lambda: out))


def _peer_tie_fallback(x, lp):
    wq, keys, u_tab, v_tab = lp['peer_raw']
    n = x.shape[0]
    blk = min(256, n)

    def block(xb):
        q = (xb @ wq).reshape(blk, PEER_HEADS, 2, PEER_NKEYS)
        s = jnp.einsum('nhpc,hpkc->nhpk', q, keys).astype(F32)
        s_top, i_top = lax.top_k(s, PEER_TOPK)
        cand = (s_top[:, :, 0, :, None] + s_top[:, :, 1, None, :]).reshape(blk, PEER_HEADS, -1)
        cand_idx = (i_top[:, :, 0, :, None] * PEER_NKEYS + i_top[:, :, 1, None, :]).reshape(blk, PEER_HEADS, -1)
        best, pick = lax.top_k(cand, PEER_TOPK)
        eidx = jnp.take_along_axis(cand_idx, pick, axis=-1)
        gw = jax.nn.softmax(best, axis=-1)
        act = jax.nn.gelu(jnp.einsum('nhkd,nd->nhk', u_tab[eidx], xb).astype(F32))
        return jnp.einsum('nhk,nhkd->nd', gw * act, v_tab[eidx])

    y = lax.map(block, x.reshape(-1, blk, D_MODEL)).reshape(n, D_MODEL)
    return _layer_norm(ALPHA * x + y, lp['ln2_w'], lp['ln2_b'])


DEC_N = 128
HEAD_STATE = HEAD_DIM * HEAD_DIM


def _col(v):
    return jnp.broadcast_to(v.reshape(-1, 1), (v.size, DEC_N))


def _load_state_t(s_ref, st):
    st[...] = s_ref[...].T.reshape(HEAD_DIM, HEAD_DIM, DEC_N)


def _store_state_t(so_ref, st):
    so_ref[...] = st[...].reshape(HEAD_STATE, DEC_N).T


def _ret_dec_kernel(c_ref, cos_ref, sin_ref, gw_ref, gb_ref, s_ref, o_ref, so_ref, ct, qk, st):
    h = pl.program_id(0)
    ct[...] = c_ref[...].T
    r0 = pl.multiple_of(h * HEAD_DIM, HEAD_DIM)

    def rot(x):
        partner = jnp.concatenate([x[HEAD_DIM // 2:], x[:HEAD_DIM // 2]], axis=0)
        return x * cos_ref[...] + partner * sin_ref[...]

    qk[0] = rot(ct[pl.ds(r0, HEAD_DIM), :])
    qk[1] = rot(ct[pl.ds(256 + r0, HEAD_DIM), :]) * (HEAD_DIM ** -0.5)
    v = ct[pl.ds(512 + r0, HEAD_DIM), :]
    g = ct[pl.ds(768 + r0, HEAD_DIM), :]
    gamma = jnp.exp(jnp.zeros((1, 1), F32) + jnp.where(
        h == 0, RET_LOG_GAMMA[0], jnp.where(h == 1, RET_LOG_GAMMA[1],
                                            jnp.where(h == 2, RET_LOG_GAMMA[2], RET_LOG_GAMMA[3]))))
    _load_state_t(s_ref, st)

    def body(k, o):
        s_new = gamma * st[k] + qk[1, pl.ds(k, 1), :] * v
        st[k] = s_new
        return o + qk[0, pl.ds(k, 1), :] * s_new

    o = lax.fori_loop(0, HEAD_DIM, body, jnp.zeros((HEAD_DIM, DEC_N), F32))
    mu = jnp.mean(o, axis=0, keepdims=True)
    d = o - mu
    var = jnp.mean(d * d, axis=0, keepdims=True)
    o_ref[...] = (d * lax.rsqrt(var + LN_EPS) * gw_ref[...] + gb_ref[...]) * _silu(g)
    _store_state_t(so_ref, st)


def _hgrn_dec_kernel(c_ref, lb_ref, nw_ref, s_ref, o_ref, so_ref, ct, qk, st):
    h = pl.program_id(0)
    ct[...] = c_ref[...].T
    r0 = pl.multiple_of(h * HEAD_DIM, HEAD_DIM)
    lb = lb_ref[...]
    qk[0] = _silu(ct[pl.ds(r0, HEAD_DIM), :])
    forget = lb + (1.0 - lb) * _sigmoid(ct[pl.ds(256 + r0, HEAD_DIM), :])
    qk[1] = forget
    qk[2] = 1.0 - forget
    v = ct[pl.ds(512 + r0, HEAD_DIM), :]
    g = ct[pl.ds(768 + r0, HEAD_DIM), :]
    _load_state_t(s_ref, st)

    def body(k, o):
        s_new = qk[1, pl.ds(k, 1), :] * st[k] + qk[2, pl.ds(k, 1), :] * v
        st[k] = s_new
        return o + qk[0, pl.ds(k, 1), :] * s_new

    o = lax.fori_loop(0, HEAD_DIM, body, jnp.zeros((HEAD_DIM, DEC_N), F32))
    ms = jnp.mean(o * o, axis=0, keepdims=True)
    o_ref[...] = o * lax.rsqrt(ms + LN_EPS) * nw_ref[...] * _silu(g)
    _store_state_t(so_ref, st)


def _rwkv_dec_kernel(c_ref, sh_ref, mu_ref, w0_ref, w2_ref, a0_ref, a2_ref, g2_ref, kkw_ref, ka_ref,
                     rk_ref, lnw_ref, lnb_ref, s_ref, o_ref, so_ref, vt, st, osc):
    h = pl.program_id(0)
    r, lw, k2, v, kkn, a, g = _rwkv_token_mix(c_ref[...], sh_ref[...], mu_ref, w0_ref, w2_ref, a0_ref, a2_ref,
                                              g2_ref, kkw_ref, ka_ref)
    for idx, arr in enumerate((r, jnp.exp(lw), k2, v, kkn, a, g)):
        vt[idx] = arr.T
    r0 = pl.multiple_of(h * HEAD_DIM, HEAD_DIM)
    rows = pl.ds(r0, HEAD_DIM)
    rh, wh, kh, kkh, ah = vt[0, rows, :], vt[1, rows, :], vt[2, rows, :], vt[4, rows, :], vt[5, rows, :]
    vh, gh = vt[3, rows, :], vt[6, rows, :]
    kka = kkh * ah
    _load_state_t(s_ref, st)

    def body(vi, carry):
        s_old = st[vi]
        sa = jnp.sum(s_old * (-kkh), axis=0, keepdims=True)
        s_new = s_old * wh + sa * kka + vt[3, pl.ds(r0 + vi, 1), :] * kh
        st[vi] = s_new
        osc[pl.ds(vi, 1), :] = jnp.sum(s_new * rh, axis=0, keepdims=True)
        return carry

    lax.fori_loop(0, HEAD_DIM, body, 0)
    o = osc[...]
    mu = jnp.mean(o, axis=0, keepdims=True)
    d = o - mu
    var = jnp.mean(d * d, axis=0, keepdims=True)
    on = d * lax.rsqrt(var + RW_GN_EPS) * lnw_ref[...] + lnb_ref[...]
    bonus = jnp.sum(rh * kh * rk_ref[...], axis=0, keepdims=True) * vh
    o_ref[...] = (on + bonus) * gh
    _store_state_t(so_ref, st)


def _dec_specs():
    head_tab = pl.BlockSpec((HEAD_DIM, DEC_N), lambda h: (h, 0))
    state = pl.BlockSpec((DEC_N, HEAD_STATE), lambda h: (0, h))
    out = pl.BlockSpec((None, HEAD_DIM, DEC_N), lambda h: (h, 0, 0))
    return head_tab, state, out


def _dec_out_shapes():
    return [jax.ShapeDtypeStruct((N_HEADS, HEAD_DIM, DEC_N), F32),
            jax.ShapeDtypeStruct((DEC_N, N_HEADS * HEAD_STATE), F32)]


def _dec_finish(o_t, s_new):
    return o_t.reshape(BRANCH_W, DEC_N).T, s_new.reshape(DEC_N, N_HEADS, HEAD_DIM, HEAD_DIM)


def _ret_decode(cols, state, cos_c, sin_c, lp):
    head_tab, st_spec, out_spec = _dec_specs()
    same = pl.BlockSpec((HEAD_DIM, DEC_N), lambda h: (0, 0))
    o_t, s_new = pl.pallas_call(
        _ret_dec_kernel,
        grid=(N_HEADS,),
        in_specs=[pl.BlockSpec((DEC_N, 1024), lambda h: (0, 0)), same, same, head_tab, head_tab, st_spec],
        out_specs=[out_spec, st_spec],
        out_shape=_dec_out_shapes(),
        scratch_shapes=[pltpu.VMEM((1024, DEC_N), F32), pltpu.VMEM((2, HEAD_DIM, DEC_N), F32),
                        pltpu.VMEM((HEAD_DIM, HEAD_DIM, DEC_N), F32)],
        compiler_params=_cparams(("arbitrary",), 40),
        name="ret_decode",
    )(cols, cos_c, sin_c, _col(lp['ret_gn_w']), _col(lp['ret_gn_b']), state.reshape(DEC_N, -1))
    return _dec_finish(o_t, s_new)


def _hgrn_decode(cols, state, lp):
    head_tab, st_spec, out_spec = _dec_specs()
    o_t, s_new = pl.pallas_call(
        _hgrn_dec_kernel,
        grid=(N_HEADS,),
        in_specs=[pl.BlockSpec((DEC_N, 1024), lambda h: (0, 1)), head_tab, head_tab, st_spec],
        out_specs=[out_spec, st_spec],
        out_shape=_dec_out_shapes(),
        scratch_shapes=[pltpu.VMEM((1024, DEC_N), F32), pltpu.VMEM((3, HEAD_DIM, DEC_N), F32),
                        pltpu.VMEM((HEAD_DIM, HEAD_DIM, DEC_N), F32)],
        compiler_params=_cparams(("arbitrary",), 40),
        name="hgrn_decode",
    )(cols, _col(lp['hg_lb']), _col(lp['hg_norm_w']), state.reshape(DEC_N, -1))
    return _dec_finish(o_t, s_new)


def _rwkv_decode(cols, shift, state, lp):
    head_tab, st_spec, out_spec = _dec_specs()
    row = lambda w: pl.BlockSpec((1, w), lambda h: (0, 0))
    mat = lambda r, w: pl.BlockSpec((r, w), lambda h: (0, 0))
    o_t, s_new = pl.pallas_call(
        _rwkv_dec_kernel,
        grid=(N_HEADS,),
        in_specs=[pl.BlockSpec((DEC_N, 1024), lambda h: (0, 2)), mat(DEC_N, 1024),
                  row(1024), row(256), mat(64, 256), row(256), mat(64, 256), mat(128, 256),
                  row(256), row(256), head_tab, head_tab, head_tab, st_spec],
        out_specs=[out_spec, st_spec],
        out_shape=_dec_out_shapes(),
        scratch_shapes=[pltpu.VMEM((7, BRANCH_W, DEC_N), F32), pltpu.VMEM((HEAD_DIM, HEAD_DIM, DEC_N), F32),
                        pltpu.VMEM((HEAD_DIM, DEC_N), F32)],
        compiler_params=_cparams(("arbitrary",), 40),
        name="rwkv_decode",
    )(cols, shift, lp['rw_mu'], lp['rw_w0'], lp['rw_w2'], lp['rw_a0'], lp['rw_a2'], lp['rw_g2'],
      lp['rw_kk'], lp['rw_ka'], _col(lp['rw_rk']), _col(lp['rw_lnx_w']), _col(lp['rw_lnx_b']),
      state.reshape(DEC_N, -1))
    return _dec_finish(o_t, s_new)


def _lru_dec_kernel(c_ref, conv_ref, h0_ref, cw_ref, cb_ref, wa_ref, ba_ref, wx_ref, bx_ref, lam_ref,
                    y_ref, h_ref, nconv_ref):
    xb = c_ref[:, 0:BRANCH_W]
    gate = c_ref[:, BRANCH_W:2 * BRANCH_W]
    c0, c1, c2 = conv_ref[0], conv_ref[1], conv_ref[2]
    xc = (c0 * cw_ref[0:1, :] + c1 * cw_ref[1:2, :] + c2 * cw_ref[2:3, :] + xb * cw_ref[3:4, :]) + cb_ref[...]
    a, u = _lru_gates(xc, wa_ref, ba_ref, wx_ref, bx_ref, lam_ref)
    hn = a * h0_ref[...] + u
    h_ref[...] = hn
    y_ref[...] = hn * _gelu(gate)
    nconv_ref[0] = c1
    nconv_ref[1] = c2
    nconv_ref[2] = xb


def _lru_decode(cols, conv, h0, lp):
    full = lambda *s: pl.BlockSpec(s, lambda i: (0,) * len(s))
    return pl.pallas_call(
        _lru_dec_kernel,
        grid=(1,),
        in_specs=[pl.BlockSpec((DEC_N, 512), lambda i: (0, 6)), full(3, DEC_N, BRANCH_W), full(DEC_N, BRANCH_W),
                  full(CONV_W, BRANCH_W), full(1, BRANCH_W), full(BRANCH_W, BRANCH_W), full(1, BRANCH_W),
                  full(BRANCH_W, BRANCH_W), full(1, BRANCH_W), full(1, BRANCH_W)],
        out_specs=[full(DEC_N, BRANCH_W), full(DEC_N, BRANCH_W), full(3, DEC_N, BRANCH_W)],
        out_shape=[jax.ShapeDtypeStruct((DEC_N, BRANCH_W), F32), jax.ShapeDtypeStruct((DEC_N, BRANCH_W), F32),
                   jax.ShapeDtypeStruct((3, DEC_N, BRANCH_W), F32)],
        compiler_params=_cparams(("arbitrary",), 32),
        name="lru_decode",
    )(cols, conv, h0, lp['lru_conv_w'], lp['lru_conv_b'], lp['lru_wa_bd'], lp['lru_ba'], lp['lru_wx_bd'],
      lp['lru_bx'], lp['lru_lambda'])


def _prompt_layer(x, p_emb, lp, cos_t, sin_t):
    b, t, d = x.shape
    n = b * t
    xf = x.reshape(n, d)
    o_a, s_ret = _ret_prompt(x, cos_t, sin_t, lp)
    o_b, s_hg = _hgrn_prompt(x, lp['hg_lb'], lp)
    o_c, s_shift, s_rw = _rwkv_prompt(x, lp)
    o_d, s_lru, s_conv = _lru_prompt(x, lp)
    outs = [o.reshape(n, BRANCH_W) for o in (o_a, o_b, o_c, o_d)]
    x1 = _mix(xf, outs, lp, 512)
    x2 = _peer(x1, lp, 512)
    x3 = _ple(x2, p_emb.reshape(n, -1), lp, 512)
    return x3.reshape(b, t, d), (s_ret, s_hg, s_rw, s_shift[:, 0], s_lru[:, 0], s_conv)


def _sample_layer(x, p_emb, state, lp, cos_c, sin_c):
    s_ret, s_hg, s_rw, s_shift, s_lru, s_conv = state
    xf = x.reshape(DEC_N, D_MODEL)
    cols = _proj(xf, lp['w_in_br'], DEC_N)
    o_a, s_ret = _ret_decode(cols, s_ret, cos_c, sin_c, lp)
    o_b, s_hg = _hgrn_decode(cols, s_hg, lp)
    o_c, s_rw = _rwkv_decode(cols, s_shift, s_rw, lp)
    o_d, s_lru, s_conv_t = _lru_decode(cols, jnp.swapaxes(s_conv, 0, 1), s_lru, lp)
    x1 = _mix(xf, [o_a, o_b, o_c, o_d], lp, DEC_N)
    x2 = _peer(x1, lp, DEC_N)
    x3 = _ple(x2, p_emb.reshape(DEC_N, -1), lp, DEC_N)
    new_shift = cols[:, 2048:3072]
    return x3.reshape(x.shape), (s_ret, s_hg, s_rw, new_shift, s_lru, jnp.swapaxes(s_conv_t, 0, 1))


def kernel(x_prompt, x_sample, state_ret, state_hgrn, state_rwkv, state_shift, state_lru, state_conv, p_prompt, p_sample, w_in, ret_gn_w, ret_gn_b, hg_lb, hg_norm_w, rw_mu, rw_w0, rw_w2, rw_a0, rw_a2, rw_g2, rw_kk, rw_ka, rw_rk, rw_lnx_w, rw_lnx_b, lru_conv_w, lru_conv_b, lru_wa, lru_ba, lru_wx, lru_bx, lru_lambda, w_branch, w_out, ln1_w, ln1_b, peer_wq, peer_keys, peer_u, peer_v, ln2_w, ln2_b, ple_w, ple_gate_w, ple_gate_b):
    params = dict(w_in=w_in, ret_gn_w=ret_gn_w, ret_gn_b=ret_gn_b, hg_norm_w=hg_norm_w, rw_mu=rw_mu, rw_w0=rw_w0,
                  rw_w2=rw_w2, rw_a0=rw_a0, rw_a2=rw_a2, rw_g2=rw_g2, rw_kk=rw_kk, rw_ka=rw_ka, rw_rk=rw_rk,
                  rw_lnx_w=rw_lnx_w, rw_lnx_b=rw_lnx_b, lru_conv_w=lru_conv_w, lru_conv_b=lru_conv_b,
                  lru_wa=lru_wa, lru_ba=lru_ba, lru_wx=lru_wx, lru_bx=lru_bx, lru_lambda=lru_lambda,
                  w_branch=w_branch, w_out=w_out, ln1_w=ln1_w, ln1_b=ln1_b, peer_wq=peer_wq, peer_keys=peer_keys,
                  peer_u=peer_u, peer_v=peer_v, ln2_w=ln2_w, ln2_b=ln2_b, ple_w=ple_w, ple_gate_w=ple_gate_w,
                  ple_gate_b=ple_gate_b)
    lb_cum = jnp.cumsum(jax.nn.softmax(hg_lb.astype(F32), axis=0), axis=0)
    lb_all = lb_cum - lb_cum[0:1]
    t_prompt = x_prompt.shape[1]
    past_len = 16384
    cos_t, sin_t = _rope_tables(jnp.arange(t_prompt))
    cos_s, sin_s = _rope_tables(past_len + jnp.arange(1))
    cos_c = _col(cos_s[0, :HEAD_DIM])
    sin_c = _col(sin_s[0, :HEAD_DIM])
    h_p, h_s = x_prompt, x_sample
    new_p, new_s = [], []
    for l in range(DEPTH):
        lp = _layer_params(params, l, lb_all)
        h_p, st_p = _prompt_layer(h_p, p_prompt[l], lp, cos_t, sin_t)
        st_in = (state_ret[l], state_hgrn[l], state_rwkv[l], state_shift[l], state_lru[l], state_conv[l])
        h_s, st_s = _sample_layer(h_s, p_sample[l], st_in, lp, cos_c, sin_c)
        new_p.append(st_p)
        new_s.append(st_s)
    outs_p = [jnp.stack(z) for z in zip(*new_p)]
    outs_s = [jnp.stack(z) for z in zip(*new_s)]
    return (h_p, h_s, *outs_p, *outs_s)
```

```python
import functools
import math

import jax
import jax.numpy as jnp
from jax import lax
from jax.experimental import pallas as pl
from jax.experimental.pallas import tpu as pltpu

F32 = jnp.float32
BF16 = jnp.bfloat16
HIGHEST = lax.Precision.HIGHEST

D_MODEL = 1024
BRANCH_W = 256
N_HEADS = 4
HEAD_DIM = 64
RET_LOG_GAMMA = tuple(math.log1p(-(2.0 ** (-5.0 - h))) for h in range(N_HEADS))
ROPE_BASE = 10000.0
RW_GN_EPS = 64e-5
LN_EPS = 1e-5
LRU_C = 8.0
CONV_W = 4
DEPTH = 2
ALPHA = (2 * DEPTH) ** 0.25
BRANCH_COLS = 3584
MIB = 1024 * 1024


def _cparams(semantics, vmem_mib):
    return pltpu.CompilerParams(dimension_semantics=semantics, vmem_limit_bytes=vmem_mib * MIB)


def _mm(a, b):
    return jnp.dot(a.astype(BF16), b.astype(BF16), preferred_element_type=F32)


def _mm_nt(a, b):
    return lax.dot_general(a.astype(BF16), b.astype(BF16), (((1,), (1,)), ((), ())), preferred_element_type=F32)


def _mm_tn(a, b):
    return lax.dot_general(a.astype(BF16), b.astype(BF16), (((0,), (0,)), ((), ())), preferred_element_type=F32)


def _transpose_exact(a):
    n = a.shape[0]
    eye = (lax.broadcasted_iota(jnp.int32, (n, n), 0) == lax.broadcasted_iota(jnp.int32, (n, n), 1)).astype(F32)
    return lax.dot_general(a, eye, (((0,), (0,)), ((), ())), preferred_element_type=F32, precision=HIGHEST)


def _select_rows_exact(sel, x):
    x1 = x.astype(BF16)
    r1 = x - x1.astype(F32)
    x2 = r1.astype(BF16)
    x3 = (r1 - x2.astype(F32)).astype(BF16)
    dot = lambda p: jnp.dot(sel, p, preferred_element_type=F32)
    return (dot(x1) + dot(x2)) + dot(x3)


def _tril_ones(n):
    r = lax.broadcasted_iota(jnp.int32, (n, n), 0)
    c = lax.broadcasted_iota(jnp.int32, (n, n), 1)
    return (r >= c).astype(BF16)


def _sigmoid(x):
    return 1.0 / (1.0 + jnp.exp(-x))


def _silu(x):
    return x * _sigmoid(x)


def _gelu(x):
    c = 0.7978845608028654
    half = 0.5 * x
    return half + half * jnp.tanh(x * (c + (c * 0.044715) * (x * x)))


def _softplus(x):
    return jnp.maximum(x, 0.0) + jnp.log1p(jnp.exp(-jnp.abs(x)))


def _head(x, h):
    return x[:, h * HEAD_DIM:(h + 1) * HEAD_DIM]


def _project_rows(x_ref, w_ref, nb, tc):
    x = x_ref[...].reshape(nb * tc, x_ref.shape[-1]).astype(BF16)
    cols = jnp.dot(x, w_ref[...], preferred_element_type=F32)
    return [cols[n * tc:(n + 1) * tc] for n in range(nb)]


def _block_diag(w):
    n, c, d = w.shape
    eye = jnp.eye(n, dtype=w.dtype)
    return (eye[:, None, :, None] * w[:, :, None, :]).reshape(n * c, n * d)


def _rope_tables(pos):
    half = HEAD_DIM // 2
    inv_freq = ROPE_BASE ** (-jnp.arange(half, dtype=F32) / half)
    ang = pos.astype(F32)[:, None] * inv_freq[None, :]
    cos = jnp.cos(ang)
    sin = jnp.sin(ang)
    cos_t = jnp.tile(jnp.concatenate([cos, cos], axis=-1), (1, N_HEADS))
    sin_t = jnp.tile(jnp.concatenate([-sin, sin], axis=-1), (1, N_HEADS))
    return cos_t, sin_t


def _layer_params(p, l, lb_all):
    r2 = lambda a: a.reshape(1, -1)
    w_in = p['w_in'][l]
    lp = {
        'w_in_br': w_in[:, :BRANCH_COLS].astype(BF16),
        'w_gate': w_in[:, BRANCH_COLS:].astype(BF16),
        'ret_gn_w': r2(p['ret_gn_w'][l]), 'ret_gn_b': r2(p['ret_gn_b'][l]),
        'hg_lb': r2(lb_all[l]), 'hg_norm_w': r2(p['hg_norm_w'][l]),
        'rw_mu': r2(p['rw_mu'][l]), 'rw_w0': r2(p['rw_w0'][l]), 'rw_w2': p['rw_w2'][l].astype(BF16),
        'rw_a0': r2(p['rw_a0'][l]), 'rw_a2': p['rw_a2'][l].astype(BF16), 'rw_g2': p['rw_g2'][l].astype(BF16),
        'rw_kk': r2(p['rw_kk'][l]), 'rw_ka': r2(p['rw_ka'][l]), 'rw_rk': r2(p['rw_rk'][l]),
        'rw_lnx_w': r2(p['rw_lnx_w'][l]), 'rw_lnx_b': r2(p['rw_lnx_b'][l]),
        'lru_conv_w': p['lru_conv_w'][l], 'lru_conv_b': r2(p['lru_conv_b'][l]),
        'lru_wa_bd': _block_diag(p['lru_wa'][l]).astype(BF16), 'lru_ba': r2(p['lru_ba'][l]),
        'lru_wx_bd': _block_diag(p['lru_wx'][l]).astype(BF16), 'lru_bx': r2(p['lru_bx'][l]),
        'lru_lambda': r2(p['lru_lambda'][l]),
        'w_branch': p['w_branch'][l].astype(BF16), 'w_out': p['w_out'][l].astype(BF16),
        'ln1_w': r2(p['ln1_w'][l]), 'ln1_b': r2(p['ln1_b'][l]),
        'peer_wq_t': p['peer_wq'][l].T.astype(BF16),
        'peer_keys': p['peer_keys'][l].reshape(16, 128, 128).astype(BF16),
        'peer_u': p['peer_u'][l].astype(BF16),
        'peer_raw': (p['peer_wq'][l], p['peer_keys'][l], p['peer_u'], p['peer_v'], l),
        'peer_v_t': jnp.swapaxes(p['peer_v'][l].astype(BF16).reshape(-1, PEER_EB, D_MODEL), 1, 2),
        'ln2_w': r2(p['ln2_w'][l]), 'ln2_b': r2(p['ln2_b'][l]),
        'ple_w': p['ple_w'][l].astype(BF16), 'ple_gate_w': p['ple_gate_w'][l].astype(BF16),
        'ple_gate_b': r2(p['ple_gate_b'][l]),
    }
    return lp


def _proj_kernel(x_ref, w_ref, o_ref):
    o_ref[...] = jnp.dot(x_ref[...].astype(BF16), w_ref[...], preferred_element_type=F32)


def _proj(x, w_bf16, tn):
    n, k = x.shape
    m = w_bf16.shape[1]
    return pl.pallas_call(
        _proj_kernel,
        grid=(n // tn,),
        in_specs=[pl.BlockSpec((tn, k), lambda i: (i, 0)),
                  pl.BlockSpec((k, m), lambda i: (0, 0))],
        out_specs=pl.BlockSpec((tn, m), lambda i: (i, 0)),
        out_shape=jax.ShapeDtypeStruct((n, m), F32),
        compiler_params=_cparams(("parallel",), 48),
        name="in_proj",
    )(x, w_bf16)


def _lru_gates(xc, wa_ref, ba_ref, wx_ref, bx_ref, lam_ref):
    r = _sigmoid(_mm(xc, wa_ref[...]) + ba_ref[...])
    i = _sigmoid(_mm(xc, wx_ref[...]) + bx_ref[...])
    log_a = -LRU_C * r * _softplus(-lam_ref[...])
    a = jnp.exp(log_a)
    u = jnp.sqrt(1.0 - jnp.exp(2.0 * log_a)) * (i * xc)
    return a, u


def _lru_kernel(x_ref, w_ref, cw_ref, cb_ref, wa_ref, ba_ref, wx_ref, bx_ref, lam_ref,
                y_ref, h_ref, conv_ref, xbuf, hcar, a_s, u_s, hs, *, tc):
    c = pl.program_id(1)

    @pl.when(c == 0)
    def _():
        xbuf[0:8, :] = jnp.zeros((8, BRANCH_W), F32)
        hcar[...] = jnp.zeros((1, BRANCH_W), F32)

    cols = jnp.dot(x_ref[...].astype(BF16), w_ref[...], preferred_element_type=F32)
    xb = cols[:, 0:BRANCH_W]
    gate = cols[:, BRANCH_W:2 * BRANCH_W]
    xbuf[8:8 + tc, :] = xb
    xc = (xbuf[pl.ds(5, tc), :] * cw_ref[0:1, :] + xbuf[pl.ds(6, tc), :] * cw_ref[1:2, :]
          + xbuf[pl.ds(7, tc), :] * cw_ref[2:3, :] + xb * cw_ref[3:4, :]) + cb_ref[...]
    a, u = _lru_gates(xc, wa_ref, ba_ref, wx_ref, bx_ref, lam_ref)
    a_s[...] = a
    u_s[...] = u

    def body(t, h):
        h = a_s[pl.ds(t, 1), :] * h + u_s[pl.ds(t, 1), :]
        hs[pl.ds(t, 1), :] = h
        return h

    h = lax.fori_loop(0, tc, body, hcar[...], unroll=8)
    hcar[...] = h
    y_ref[...] = hs[...] * _gelu(gate)
    xbuf[0:8, :] = xbuf[tc:tc + 8, :]

    @pl.when(c == pl.num_programs(1) - 1)
    def _():
        h_ref[...] = h
        conv_ref[...] = xbuf[5:8, :]


def _lru_prompt(x3, lp, tc=256):
    b, t, d = x3.shape
    row = lambda: pl.BlockSpec((1, BRANCH_W), lambda i, j: (0, 0))
    full = lambda r: pl.BlockSpec((r, BRANCH_W), lambda i, j: (0, 0))
    return pl.pallas_call(
        functools.partial(_lru_kernel, tc=tc),
        grid=(b, t // tc),
        in_specs=[pl.BlockSpec((None, tc, d), lambda i, j: (i, j, 0)),
                  pl.BlockSpec((d, 2 * BRANCH_W), lambda i, j: (0, 6)),
                  full(CONV_W), row(), full(BRANCH_W), row(), full(BRANCH_W), row(), row()],
        out_specs=[pl.BlockSpec((None, tc, BRANCH_W), lambda i, j: (i, j, 0)),
                   pl.BlockSpec((None, 1, BRANCH_W), lambda i, j: (i, 0, 0)),
                   pl.BlockSpec((None, CONV_W - 1, BRANCH_W), lambda i, j: (i, 0, 0))],
        out_shape=[jax.ShapeDtypeStruct((b, t, BRANCH_W), F32),
                   jax.ShapeDtypeStruct((b, 1, BRANCH_W), F32),
                   jax.ShapeDtypeStruct((b, CONV_W - 1, BRANCH_W), F32)],
        scratch_shapes=[pltpu.VMEM((tc + 8, BRANCH_W), F32), pltpu.VMEM((1, BRANCH_W), F32),
                        pltpu.VMEM((tc, BRANCH_W), F32), pltpu.VMEM((tc, BRANCH_W), F32),
                        pltpu.VMEM((tc, BRANCH_W), F32)],
        compiler_params=_cparams(("parallel", "arbitrary"), 32),
        name="lru_prompt",
    )(x3, lp['w_in_br'], lp['lru_conv_w'], lp['lru_conv_b'], lp['lru_wa_bd'], lp['lru_ba'], lp['lru_wx_bd'],
      lp['lru_bx'], lp['lru_lambda'])


def _rotary(x, cos, sin_signed, first_half):
    partner = jnp.where(first_half, pltpu.roll(x, BRANCH_W - 32, 1), pltpu.roll(x, 32, 1))
    return x * cos + partner * sin_signed


def _ret_kernel(x_ref, w_ref, cos_ref, sin_ref, gw_ref, gb_ref, o_ref, s_ref, S, *, tc, nb):
    c = pl.program_id(1)

    @pl.when(c == 0)
    def _():
        S[...] = jnp.zeros(S.shape, F32)

    lane = lax.broadcasted_iota(jnp.int32, (tc, BRANCH_W), 1)
    first_half = (lane % HEAD_DIM) < (HEAD_DIM // 2)
    cos = cos_ref[...]
    sin = sin_ref[...]
    row = lax.broadcasted_iota(jnp.int32, (tc, tc), 0)
    col = lax.broadcasted_iota(jnp.int32, (tc, tc), 1)
    causal = row >= col
    dist = jnp.where(causal, row - col, 0).astype(F32)
    tpos = lax.broadcasted_iota(jnp.int32, (tc, HEAD_DIM), 0).astype(F32)
    decay = [jnp.where(causal, jnp.exp(dist * RET_LOG_GAMMA[h]), 0.0) for h in range(N_HEADS)]
    q_in = [jnp.exp((tpos + 1.0) * RET_LOG_GAMMA[h]) for h in range(N_HEADS)]
    k_out = [jnp.exp((tc - 1.0 - tpos) * RET_LOG_GAMMA[h]) for h in range(N_HEADS)]
    cols = _project_rows(x_ref, w_ref, nb, tc)
    q = [_rotary(cols[n][:, 0:256], cos, sin, first_half) for n in range(nb)]
    k = [_rotary(cols[n][:, 256:512], cos, sin, first_half) * (HEAD_DIM ** -0.5) for n in range(nb)]
    chains = [(n, h) for n in range(nb) for h in range(N_HEADS)]
    scores = {(n, h): _mm_nt(_head(q[n], h), _head(k[n], h)) * decay[h] for n, h in chains}
    out = {(n, h): _mm(scores[n, h], _head(cols[n][:, 512:768], h)) + _mm(_head(q[n], h) * q_in[h], S[n, h])
           for n, h in chains}
    for n, h in chains:
        S[n, h] = (math.exp(tc * RET_LOG_GAMMA[h]) * S[n, h]
                   + _mm_tn(_head(k[n], h) * k_out[h], _head(cols[n][:, 512:768], h)))
    for n, h in chains:
        oh = out[n, h]
        mu = jnp.mean(oh, axis=-1, keepdims=True)
        d = oh - mu
        var = jnp.mean(d * d, axis=-1, keepdims=True)
        on = d * lax.rsqrt(var + LN_EPS) * _head(gw_ref[...], h) + _head(gb_ref[...], h)
        o_ref[n, :, h * HEAD_DIM:(h + 1) * HEAD_DIM] = on * _silu(_head(cols[n][:, 768:1024], h))

    @pl.when(c == pl.num_programs(1) - 1)
    def _():
        s_ref[...] = S[...]


def _ret_prompt(x3, cos_t, sin_t, lp, tc=128, nb=2):
    b, t, d = x3.shape
    nb = min(nb, b)
    row = lambda: pl.BlockSpec((1, BRANCH_W), lambda i, j: (0, 0))
    return pl.pallas_call(
        functools.partial(_ret_kernel, tc=tc, nb=nb),
        grid=(b // nb, t // tc),
        in_specs=[pl.BlockSpec((nb, tc, d), lambda i, j: (i, j, 0)),
                  pl.BlockSpec((d, 1024), lambda i, j: (0, 0)),
                  pl.BlockSpec((tc, BRANCH_W), lambda i, j: (j, 0)),
                  pl.BlockSpec((tc, BRANCH_W), lambda i, j: (j, 0)), row(), row()],
        out_specs=[pl.BlockSpec((nb, tc, BRANCH_W), lambda i, j: (i, j, 0)),
                   pl.BlockSpec((nb, N_HEADS, HEAD_DIM, HEAD_DIM), lambda i, j: (i, 0, 0, 0))],
        out_shape=[jax.ShapeDtypeStruct((b, t, BRANCH_W), F32),
                   jax.ShapeDtypeStruct((b, N_HEADS, HEAD_DIM, HEAD_DIM), F32)],
        scratch_shapes=[pltpu.VMEM((nb, N_HEADS, HEAD_DIM, HEAD_DIM), F32)],
        compiler_params=_cparams(("parallel", "arbitrary"), 32),
        name="ret_prompt",
    )(x3, lp['w_in_br'], cos_t, sin_t, lp['ret_gn_w'], lp['ret_gn_b'])


HGRN_CHUNK = 64
HGRN_LEVELS = tuple(HGRN_CHUNK >> l for l in range(1, 7))


def _hgrn_select_matrix():
    t = jnp.arange(HGRN_CHUNK)
    s = jnp.arange(HGRN_CHUNK)
    blocks = [s[None, :] <= t[:, None]]
    for m in HGRN_LEVELS:
        mid = (t // (2 * m)) * (2 * m) + m - 1
        blocks.append(s[None, :] <= mid[:, None])
    return jnp.concatenate(blocks, axis=0).astype(BF16)


def _hgrn_kernel(x_ref, w_ref, lb_ref, nw_ref, sel_ref, o_ref, s_ref, St, *, tc, nb):
    c = pl.program_id(1)

    @pl.when(c == 0)
    def _():
        St[...] = jnp.zeros(St.shape, F32)

    lb = lb_ref[...]
    row = lax.broadcasted_iota(jnp.int32, (tc, BRANCH_W), 0)
    r64 = lax.broadcasted_iota(jnp.int32, (tc, tc), 0)
    c64 = lax.broadcasted_iota(jnp.int32, (tc, tc), 1)
    tok = []
    all_cols = _project_rows(x_ref, w_ref, nb, tc)
    for n in range(nb):
        cols = all_cols[n]
        q = _silu(cols[:, 0:256])
        forget = lb + (1.0 - lb) * _sigmoid(cols[:, 256:512])
        kk = 1.0 - forget
        sums = _select_rows_exact(sel_ref[...], jnp.log(forget))
        tok.append(dict(q=q, kk=kk, sums=sums, b=sums[0:tc],
                        v=cols[:, 512:768], g=cols[:, 768:1024]))
    chains = [(n, h) for n in range(nb) for h in range(N_HEADS)]
    scores = {(n, h): jnp.where(r64 == c64, _mm_nt(_head(tok[n]['q'], h), _head(tok[n]['kk'], h)), 0.0)
              for n, h in chains}
    for lvl, m in enumerate(HGRN_LEVELS):
        sh = m.bit_length() - 1
        upper = ((row >> sh) & 1) == 1
        same = (r64 >> (sh + 1)) == (c64 >> (sh + 1))
        scaled = []
        for n in range(nb):
            t = tok[n]
            e = jnp.exp(-jnp.abs(t['b'] - t['sums'][(lvl + 1) * tc:(lvl + 2) * tc]))
            scaled.append((jnp.where(upper, t['q'] * e, 0.0), jnp.where(upper, 0.0, t['kk'] * e)))
        for n, h in chains:
            scores[n, h] = scores[n, h] + jnp.where(same, _mm_nt(_head(scaled[n][0], h), _head(scaled[n][1], h)), 0.0)
    for n in range(nb):
        t = tok[n]
        b = t['b']
        v = t['v']
        g = t['g']
        b_end = b[tc - 1:tc, :]
        qd = t['q'] * jnp.exp(b)
        kd = t['kk'] * jnp.exp(b_end - b)
        for h in range(N_HEADS):
            vh = _head(v, h)
            oh = _mm(scores[n, h], vh) + _mm_nt(_head(qd, h), St[n, h])
            St[n, h] = St[n, h] * jnp.exp(_head(b_end, h)) + _mm_tn(vh, _head(kd, h))
            ms = jnp.mean(oh * oh, axis=-1, keepdims=True)
            on = oh * lax.rsqrt(ms + LN_EPS) * _head(nw_ref[...], h)
            o_ref[n, :, h * HEAD_DIM:(h + 1) * HEAD_DIM] = on * _silu(_head(g, h))

    @pl.when(c == pl.num_programs(1) - 1)
    def _():
        for n in range(nb):
            for h in range(N_HEADS):
                s_ref[n, h] = _transpose_exact(St[n, h])


def _hgrn_prompt(x3, lb, lp, nb=8):
    b, t, d = x3.shape
    nb = min(nb, b)
    tc = HGRN_CHUNK
    row = lambda: pl.BlockSpec((1, BRANCH_W), lambda i, j: (0, 0))
    return pl.pallas_call(
        functools.partial(_hgrn_kernel, tc=tc, nb=nb),
        grid=(b // nb, t // tc),
        in_specs=[pl.BlockSpec((nb, tc, d), lambda i, j: (i, j, 0)),
                  pl.BlockSpec((d, 1024), lambda i, j: (0, 1)), row(), row(),
                  pl.BlockSpec((7 * tc, tc), lambda i, j: (0, 0))],
        out_specs=[pl.BlockSpec((nb, tc, BRANCH_W), lambda i, j: (i, j, 0)),
                   pl.BlockSpec((nb, N_HEADS, HEAD_DIM, HEAD_DIM), lambda i, j: (i, 0, 0, 0))],
        out_shape=[jax.ShapeDtypeStruct((b, t, BRANCH_W), F32),
                   jax.ShapeDtypeStruct((b, N_HEADS, HEAD_DIM, HEAD_DIM), F32)],
        scratch_shapes=[pltpu.VMEM((nb, N_HEADS, HEAD_DIM, HEAD_DIM), F32)],
        compiler_params=_cparams(("parallel", "arbitrary"), 32),
        name="hgrn_prompt",
    )(x3, lp['w_in_br'], lb, lp['hg_norm_w'], _hgrn_select_matrix())


def _rwkv_token_mix(cols, prev, mu_ref, w0_ref, w2_ref, a0_ref, a2_ref, g2_ref, kkw_ref, ka_ref):
    xs = cols + mu_ref[...] * (prev - cols)
    r = xs[:, 0:256]
    k = xs[:, 256:512]
    v = xs[:, 512:768]
    xg = xs[:, 768:896]
    xw = xs[:, 896:960]
    xa = xs[:, 960:1024]
    w = -_softplus(-(w0_ref[...] + _mm(jnp.tanh(xw), w2_ref[...]))) - 0.5
    lw = -jnp.exp(w)
    a = _sigmoid(a0_ref[...] + _mm(xa, a2_ref[...]))
    g = _mm(_sigmoid(xg), g2_ref[...])
    kk = k * kkw_ref[...]
    parts = []
    for h in range(N_HEADS):
        kh = _head(kk, h)
        nrm = jnp.sqrt(jnp.sum(kh * kh, axis=-1, keepdims=True))
        parts.append(kh / jnp.maximum(nrm, 1e-12))
    kkn = jnp.concatenate(parts, axis=-1)
    k2 = k * (1.0 + (a - 1.0) * ka_ref[...])
    return r, lw, k2, v, kkn, a, g


def _rwkv_out(o, r, k2, v, g, rk_ref, lw_ref, lb_ref, h):
    mu = jnp.mean(o, axis=-1, keepdims=True)
    d = o - mu
    var = jnp.mean(d * d, axis=-1, keepdims=True)
    on = d * lax.rsqrt(var + RW_GN_EPS) * _head(lw_ref[...], h) + _head(lb_ref[...], h)
    bonus = jnp.sum(_head(r, h) * _head(k2, h) * _head(rk_ref[...], h), axis=-1, keepdims=True) * _head(v, h)
    return (on + bonus) * _head(g, h)


def _rwkv_kernel(x_ref, w_ref, mu_ref, w0_ref, w2_ref, a0_ref, a2_ref, g2_ref, kkw_ref, ka_ref, rk_ref,
                 lnw_ref, lnb_ref, y_ref, shift_ref, s_ref, S, last_row, *, tc, nb):
    c = pl.program_id(1)

    @pl.when(c == 0)
    def _():
        S[...] = jnp.zeros(S.shape, F32)
        last_row[...] = jnp.zeros(last_row.shape, F32)

    rr = lax.broadcasted_iota(jnp.int32, (tc, tc), 0)
    cc = lax.broadcasted_iota(jnp.int32, (tc, tc), 1)
    strict = rr > cc
    incl2 = (lax.broadcasted_iota(jnp.int32, (tc, 2 * tc), 0)
             >= (lax.broadcasted_iota(jnp.int32, (tc, 2 * tc), 1) & (tc - 1)))
    tril = _tril_ones(tc)
    n_double = max(1, (tc - 1).bit_length())
    tok = []
    new_shift = []
    all_cols = _project_rows(x_ref, w_ref, nb, tc)
    for n in range(nb):
        cols = all_cols[n]
        new_shift.append(cols[tc - 1:tc, :])
        rowi = lax.broadcasted_iota(jnp.int32, cols.shape, 0)
        prev = jnp.where(rowi == 0, last_row[n], pltpu.roll(cols, 1, 0))
        last_row[n] = cols[tc - 1:tc, :]
        r, lw, k2, v, kkn, a, g = _rwkv_token_mix(cols, prev, mu_ref, w0_ref, w2_ref, a0_ref, a2_ref, g2_ref,
                                                  kkw_ref, ka_ref)
        am = -kkn
        bm = kkn * a
        G = _select_rows_exact(tril, lw)
        g_end = G[tc - 1:tc, :]
        einv = jnp.exp(-G)
        eend = jnp.exp(g_end - G)
        tok.append(dict(r=r, k2=k2, v=v, g=g, g_end=g_end, at=am * jnp.exp(G - lw), rt=r * jnp.exp(G),
                        bt=bm * einv, kt=k2 * einv, bbar=bm * eend, kbar=k2 * eend))
    chains = [(n, h) for n in range(nb) for h in range(N_HEADS)]
    lhs = {(n, h): jnp.concatenate([_head(tok[n]['at'], h), _head(tok[n]['rt'], h)], axis=0) for n, h in chains}
    inter = {(n, h): _mm_nt(lhs[n, h], jnp.concatenate([_head(tok[n]['bt'], h), _head(tok[n]['kt'], h)], axis=0))
             for n, h in chains}
    from_state = {(n, h): _mm_nt(lhs[n, h], S[n, h]) for n, h in chains}
    npow = {ch: jnp.where(strict, inter[ch][0:tc, 0:tc], 0.0) for ch in chains}
    u = {(n, h): from_state[n, h][0:tc]
         + _mm(jnp.where(strict, inter[n, h][0:tc, tc:2 * tc], 0.0), _head(tok[n]['v'], h)) for n, h in chains}
    for j in range(n_double):
        u = {ch: u[ch] + _mm(npow[ch], u[ch]) for ch in chains}
        if j + 1 < n_double:
            npow = {ch: _mm(npow[ch], npow[ch]) for ch in chains}
    for n, h in chains:
        t = tok[n]
        uv = jnp.concatenate([u[n, h], _head(t['v'], h)], axis=0)
        a_r = jnp.where(incl2, inter[n, h][tc:2 * tc, :], 0.0)
        o = from_state[n, h][tc:2 * tc] + _mm(a_r, uv)
        S[n, h] = (S[n, h] * jnp.exp(_head(t['g_end'], h))
                   + _mm_tn(uv, jnp.concatenate([_head(t['bbar'], h), _head(t['kbar'], h)], axis=0)))
        y_ref[n, :, h * HEAD_DIM:(h + 1) * HEAD_DIM] = _rwkv_out(o, t['r'], t['k2'], t['v'], t['g'], rk_ref,
                                                                 lnw_ref, lnb_ref, h)

    @pl.when(c == pl.num_programs(1) - 1)
    def _():
        s_ref[...] = S[...]
        for n in range(nb):
            shift_ref[n] = new_shift[n]


def _rwkv_prompt(x3, lp, tc=64, nb=4):
    b, t, d = x3.shape
    nb = min(nb, b)
    row = lambda w: pl.BlockSpec((1, w), lambda i, j: (0, 0))
    mat = lambda r, w: pl.BlockSpec((r, w), lambda i, j: (0, 0))
    return pl.pallas_call(
        functools.partial(_rwkv_kernel, tc=tc, nb=nb),
        grid=(b // nb, t // tc),
        in_specs=[pl.BlockSpec((nb, tc, d), lambda i, j: (i, j, 0)),
                  pl.BlockSpec((d, 1024), lambda i, j: (0, 2)),
                  row(1024), row(256), mat(64, 256), row(256), mat(64, 256), mat(128, 256),
                  row(256), row(256), row(256), row(256), row(256)],
        out_specs=[pl.BlockSpec((nb, tc, BRANCH_W), lambda i, j: (i, j, 0)),
                   pl.BlockSpec((nb, 1, 1024), lambda i, j: (i, 0, 0)),
                   pl.BlockSpec((nb, N_HEADS, HEAD_DIM, HEAD_DIM), lambda i, j: (i, 0, 0, 0))],
        out_shape=[jax.ShapeDtypeStruct((b, t, BRANCH_W), F32),
                   jax.ShapeDtypeStruct((b, 1, 1024), F32),
                   jax.ShapeDtypeStruct((b, N_HEADS, HEAD_DIM, HEAD_DIM), F32)],
        scratch_shapes=[pltpu.VMEM((nb, N_HEADS, HEAD_DIM, HEAD_DIM), F32), pltpu.VMEM((nb, 1, 1024), F32)],
        compiler_params=_cparams(("parallel", "arbitrary"), 32),
        name="rwkv_prompt",
    )(x3, lp['w_in_br'], lp['rw_mu'], lp['rw_w0'], lp['rw_w2'], lp['rw_a0'], lp['rw_a2'], lp['rw_g2'],
      lp['rw_kk'], lp['rw_ka'], lp['rw_rk'], lp['rw_lnx_w'], lp['rw_lnx_b'])


def _layer_norm(z, w, b):
    mu = jnp.mean(z, axis=-1, keepdims=True)
    d = z - mu
    var = jnp.mean(d * d, axis=-1, keepdims=True)
    return d * lax.rsqrt(var + LN_EPS) * w + b


def _mix_kernel(x_ref, oa_ref, ob_ref, oc_ref, od_ref, wg_ref, wb_ref, wo_ref, lw_ref, lb_ref, out_ref):
    x = x_ref[...]
    xb = x.astype(BF16)
    mixed = None
    for gi, o_ref in enumerate((oa_ref, ob_ref, oc_ref, od_ref)):
        gate = _sigmoid(jnp.dot(xb, wg_ref[:, gi * D_MODEL:(gi + 1) * D_MODEL], preferred_element_type=F32))
        up = jnp.dot(o_ref[...].astype(BF16), wb_ref[gi], preferred_element_type=F32)
        mixed = gate * up if mixed is None else mixed + gate * up
    y = jnp.dot(mixed.astype(BF16), wo_ref[...], preferred_element_type=F32)
    out_ref[...] = _layer_norm(ALPHA * x + y, lw_ref[...], lb_ref[...])


def _mix(x, outs, lp, tn):
    n = x.shape[0]
    tok = lambda w: pl.BlockSpec((tn, w), lambda i: (i, 0))
    const = lambda *s: pl.BlockSpec(s, lambda i: (0,) * len(s))
    return pl.pallas_call(
        _mix_kernel,
        grid=(n // tn,),
        in_specs=[tok(D_MODEL), tok(BRANCH_W), tok(BRANCH_W), tok(BRANCH_W), tok(BRANCH_W),
                  const(D_MODEL, 4 * D_MODEL), const(4, BRANCH_W, D_MODEL), const(D_MODEL, D_MODEL),
                  const(1, D_MODEL), const(1, D_MODEL)],
        out_specs=tok(D_MODEL),
        out_shape=jax.ShapeDtypeStruct((n, D_MODEL), F32),
        compiler_params=_cparams(("parallel",), 48),
        name="mix_ln1",
    )(x, *outs, lp['w_gate'], lp['w_branch'], lp['w_out'], lp['ln1_w'], lp['ln1_b'])


def _ple_kernel(x_ref, p_ref, wg_ref, bg_ref, wp_ref, out_ref):
    x = x_ref[...]
    gate = _sigmoid(jnp.dot(x.astype(BF16), wg_ref[...], preferred_element_type=F32) + bg_ref[...])
    emb = jnp.dot(p_ref[...].astype(BF16), wp_ref[...], preferred_element_type=F32)
    out_ref[...] = x + gate * emb


def _ple(x, p_emb, lp, tn):
    n = x.shape[0]
    tok = lambda w: pl.BlockSpec((tn, w), lambda i: (i, 0))
    const = lambda *s: pl.BlockSpec(s, lambda i: (0,) * len(s))
    return pl.pallas_call(
        _ple_kernel,
        grid=(n // tn,),
        in_specs=[tok(D_MODEL), tok(256), const(D_MODEL, D_MODEL), const(1, D_MODEL), const(256, D_MODEL)],
        out_specs=tok(D_MODEL),
        out_shape=jax.ShapeDtypeStruct((n, D_MODEL), F32),
        compiler_params=_cparams(("parallel",), 32),
        name="ple_gate",
    )(x, p_emb, lp['ple_gate_w'], lp['ple_gate_b'], lp['ple_w'])


PEER_HEADS = 8
PEER_NKEYS = 128
PEER_TOPK = 16
PEER_EB = 1024
PEER_SUB = 2
PEER_TIE_SLOTS = 8
PEER_IGROUP = 2
PEER_JROWS = 64


def _oddeven_merge_sort_pairs(n):
    pairs = []
    p = 1
    while p < n:
        k = p
        while k >= 1:
            for j in range(k % p, n - k, 2 * k):
                for i in range(min(k, n - j - k)):
                    if (i + j) // (p * 2) == (i + j + k) // (p * 2):
                        pairs.append((i + j, i + j + k))
            k //= 2
        p *= 2
    return pairs


def _bitonic_merge_pairs(n):
    pairs = []
    k = n // 2
    while k >= 1:
        pairs.extend((i, i + k) for i in range(n) if (i & k) == 0)
        k //= 2
    return pairs


_SORT16 = _oddeven_merge_sort_pairs(PEER_TOPK)
_MERGE16 = _bitonic_merge_pairs(PEER_TOPK)
_CAND_LEN = tuple(PEER_TOPK // (a + 1) for a in range(PEER_TOPK))


def _network(vals, pairs):
    vals = list(vals)
    for i, j in pairs:
        hi = jnp.maximum(vals[i], vals[j])
        lo = jnp.minimum(vals[i], vals[j])
        vals[i], vals[j] = hi, lo
    return vals


def _top16_merge(x, y):
    return _network([jnp.maximum(x[i], y[PEER_TOPK - 1 - i]) for i in range(PEER_TOPK)], _MERGE16)


def _peer_head_stats(h, s_nat, e_nat, sk, th, tie, g_count):
    rows0 = pl.ds(pl.multiple_of(2 * h * PEER_NKEYS, PEER_NKEYS), PEER_NKEYS)
    rows1 = pl.ds(pl.multiple_of((2 * h + 1) * PEER_NKEYS, PEER_NKEYS), PEER_NKEYS)
    for p, rows in enumerate((rows0, rows1)):
        for g in range(4):
            gg = g % g_count
            sk[:, p * 4 + g, :] = s_nat[gg, rows, :]
    groups = []
    for m in range(PEER_NKEYS // PEER_TOPK):
        groups.append(_network([sk[PEER_TOPK * m + i] for i in range(PEER_TOPK)], _SORT16))
    while len(groups) > 1:
        groups = [_top16_merge(groups[i], groups[i + 1]) for i in range(0, len(groups), 2)]
    top = groups[0]
    low = lax.broadcasted_iota(jnp.int32, (8, 128), 0) < 4
    ta = [jnp.where(low, t, pltpu.roll(t, 4, 0)) for t in top]
    tb = [jnp.where(low, pltpu.roll(t, 4, 0), t) for t in top]
    cand = [[ta[a] + tb[b] for b in range(_CAND_LEN[a])] for a in range(PEER_TOPK)]
    m1 = _network(cand[1] + [cand[a][0] for a in range(PEER_TOPK - 1, 7, -1)], _MERGE16)
    m2 = _network(cand[2] + cand[3] + cand[4] + cand[5] + cand[6], _SORT16)
    t1 = _top16_merge(cand[0], m1)
    t2 = _top16_merge(t1, m2)
    t2[15] = jnp.maximum(t2[15], cand[7][0])
    t2[14] = jnp.maximum(t2[14], cand[7][1])
    theta = t2[0]
    for t in t2[1:]:
        theta = jnp.minimum(theta, t)
    cmax = cand[0][0]
    z = jnp.zeros((8, 128), F32)
    n_kept = jnp.zeros((8, 128), F32)
    for row in cand:
        for cv in row:
            keep = cv >= theta
            z = z + jnp.where(keep, jnp.exp(cv - cmax), 0.0)
            n_kept = n_kept + jnp.where(keep, 1.0, 0.0)
    inv_z = 1.0 / z
    cand_tie = jnp.where(n_kept != float(PEER_TOPK), 1.0, 0.0)
    last1_used = jnp.where(cand[0][PEER_TOPK - 1] >= theta, 1.0, 0.0)
    last0_used = jnp.where(cand[PEER_TOPK - 1][0] >= theta, 1.0, 0.0)
    for g in range(g_count):
        lanes = slice(g * 128, (g + 1) * 128)
        th[h, 0:1, lanes] = theta[g:g + 1, :]
        th[h, 1:2, lanes] = ta[0][g:g + 1, :]
        th[h, 2:3, lanes] = tb[0][g:g + 1, :]
        th[h, 3:4, lanes] = inv_z[g:g + 1, :]
        th[h, 4:5, lanes] = ta[PEER_TOPK - 1][g:g + 1, :]
        th[h, 5:6, lanes] = cand_tie[g:g + 1, :]
        th[h, 6:7, lanes] = last0_used[g:g + 1, :]
        th[h, 7:8, lanes] = last1_used[g:g + 1, :]
        for b in range(PEER_TOPK):
            th[h, 8 + b:9 + b, lanes] = tb[b][g:g + 1, :]
    for g in range(g_count):
        lanes = slice(g * 128, (g + 1) * 128)
        s0 = s_nat[g, rows0, :]
        s1 = s_nat[g, rows1, :]
        e_nat[g, rows0, :] = jnp.exp(s0 - th[h, 1:2, lanes])
        e_nat[g, rows1, :] = jnp.exp(s1 - th[h, 2:3, lanes]) * th[h, 3:4, lanes]
        n0 = jnp.sum(jnp.where(s0 >= th[h, 4:5, lanes], 1.0, 0.0), axis=0, keepdims=True)
        n1 = jnp.sum(jnp.where(s1 >= th[h, 7 + PEER_TOPK:8 + PEER_TOPK, lanes], 1.0, 0.0), axis=0, keepdims=True)
        key_tie = jnp.maximum(jnp.where(n0 != float(PEER_TOPK), th[h, 6:7, lanes], 0.0),
                              jnp.where(n1 != float(PEER_TOPK), th[h, 7:8, lanes], 0.0))
        tie[0:1, lanes] = jnp.maximum(tie[0:1, lanes], jnp.maximum(key_tie, th[h, 5:6, lanes]))
        tau = jnp.full(s0.shape, jnp.inf, F32)
        for b in range(PEER_TOPK):
            sb = th[h, 8 + b:9 + b, lanes]
            tau = jnp.where((s0 + sb) >= th[h, 0:1, lanes], sb, tau)
        s_nat[g, rows0, :] = tau


def _peer_kernel(x_ref, wq_ref, keys_ref, u_ref, vt_ref, lw_ref, lb_ref, out_ref, tie_ref,
                 xtb, s_nat, e_nat, sk, th, wact, wraw, actb, yt, tie, *, tn):
    j = pl.program_id(1)
    g_count = tn // 128

    @pl.when(j == 0)
    def _():
        xtb[...] = x_ref[...].T.astype(BF16)
        for hp in range(2 * PEER_HEADS):
            qt = jnp.dot(wq_ref[hp * 128:(hp + 1) * 128, :], xtb[...], preferred_element_type=F32)
            scores = jnp.dot(keys_ref[hp], qt.astype(BF16), preferred_element_type=F32)
            for g in range(g_count):
                s_nat[g, hp * PEER_NKEYS:(hp + 1) * PEER_NKEYS, :] = scores[:, g * 128:(g + 1) * 128]

        tie[...] = jnp.zeros(tie.shape, F32)

        def head_body(h, carry):
            _peer_head_stats(h, s_nat, e_nat, sk, th, tie, g_count)
            return carry

        lax.fori_loop(0, PEER_HEADS, head_body, 0)
        tie_ref[...] = tie[0:1, :]
        yt[...] = jnp.zeros(yt.shape, F32)

    last = pl.num_programs(1) - 1
    cur = j % 2

    def step(first_stage, second_stage):
        for sb in range(PEER_SUB):
            if first_stage:
                act = jnp.dot(u_ref[sb * PEER_EB:(sb + 1) * PEER_EB, :], xtb[...],
                              preferred_element_type=F32)
                for g in range(g_count):
                    actb[sb, g] = act[:, g * 128:(g + 1) * 128]
            if second_stage:
                yt[...] += jnp.dot(vt_ref[sb], wact[1 - cur, sb], preferred_element_type=F32)
            if not first_stage:
                continue
            i0 = pl.multiple_of((j * PEER_SUB + sb) * (PEER_EB // PEER_NKEYS), 8)
            jr = PEER_JROWS
            for g in range(g_count):
                for jq in range(PEER_NKEYS // jr):
                    for ig in range(PEER_EB // PEER_NKEYS // PEER_IGROUP):
                        acc = [jnp.zeros((jr, 128), F32) for _ in range(PEER_IGROUP)]
                        for h in range(PEER_HEADS):
                            base1 = (2 * h + 1) * PEER_NKEYS + jq * jr
                            s1 = s_nat[g, base1:base1 + jr, :]
                            e1 = e_nat[g, base1:base1 + jr, :]
                            tau0 = s_nat[g, pl.ds(2 * h * PEER_NKEYS + i0, 8), :]
                            e0 = e_nat[g, pl.ds(2 * h * PEER_NKEYS + i0, 8), :]
                            for k in range(PEER_IGROUP):
                                ii = PEER_IGROUP * ig + k
                                sel = s1 >= tau0[ii:ii + 1, :]
                                acc[k] = acc[k] + jnp.where(sel, e0[ii:ii + 1, :] * e1, 0.0)
                        for k in range(PEER_IGROUP):
                            r0 = (PEER_IGROUP * ig + k) * PEER_NKEYS + jq * jr
                            wraw[g, r0:r0 + jr, :] = acc[k]
            for g in range(g_count):
                wact[cur, sb, :, g * 128:(g + 1) * 128] = (wraw[g] * _gelu(actb[sb, g])).astype(BF16)

    @pl.when(j == 0)
    def _():
        step(True, False)

    @pl.when((j > 0) & (j < last))
    def _():
        step(True, True)

    @pl.when(j == last)
    def _():
        step(False, True)
        z = ALPHA * x_ref[...] + yt[...].T
        out_ref[...] = _layer_norm(z, lw_ref[...], lb_ref[...])


def _peer(x, lp, tn):
    n = x.shape[0]
    n_blk = lp['peer_u'].shape[0] // (PEER_EB * PEER_SUB)
    once = lambda *s: pl.BlockSpec(s, lambda i, j: (0,) * len(s), pipeline_mode=pl.Buffered(1))
    const = lambda *s: pl.BlockSpec(s, lambda i, j: (0,) * len(s))
    out, tie_flags = pl.pallas_call(
        functools.partial(_peer_kernel, tn=tn),
        grid=(n // tn, n_blk + 1),
        in_specs=[pl.BlockSpec((tn, D_MODEL), lambda i, j: (i, 0)),
                  once(2 * PEER_HEADS * 128, D_MODEL), once(2 * PEER_HEADS, PEER_NKEYS, 128),
                  pl.BlockSpec((PEER_SUB * PEER_EB, D_MODEL), lambda i, j: (jnp.minimum(j, n_blk - 1), 0)),
                  pl.BlockSpec((PEER_SUB, D_MODEL, PEER_EB), lambda i, j: (jnp.maximum(j - 1, 0), 0, 0)),
                  const(1, D_MODEL), const(1, D_MODEL)],
        out_specs=[pl.BlockSpec((tn, D_MODEL), lambda i, j: (i, 0)),
                   pl.BlockSpec((None, 1, tn), lambda i, j: (i, 0, 0))],
        out_shape=[jax.ShapeDtypeStruct((n, D_MODEL), F32), jax.ShapeDtypeStruct((n // tn, 1, tn), F32)],
        scratch_shapes=[pltpu.VMEM((D_MODEL, tn), BF16),
                        pltpu.VMEM((tn // 128, 2 * PEER_HEADS * PEER_NKEYS, 128), F32),
                        pltpu.VMEM((tn // 128, 2 * PEER_HEADS * PEER_NKEYS, 128), F32),
                        pltpu.VMEM((PEER_NKEYS, 8, 128), F32),
                        pltpu.VMEM((PEER_HEADS, 8 + PEER_TOPK, tn), F32),
                        pltpu.VMEM((2, PEER_SUB, PEER_EB, tn), BF16),
                        pltpu.VMEM((tn // 128, PEER_EB, 128), F32),
                        pltpu.VMEM((PEER_SUB, tn // 128, PEER_EB, 128), F32),
                        pltpu.VMEM((D_MODEL, tn), F32),
                        pltpu.VMEM((8, tn), F32)],
        compiler_params=_cparams(("parallel", "arbitrary"), 56),
        name="peer_ln2",
    )(x, lp['peer_wq_t'], lp['peer_keys'], lp['peer_u'], lp['peer_v_t'], lp['ln2_w'], lp['ln2_b'])
    flags = tie_flags.reshape(n) > 0.0
    count = jnp.sum(flags.astype(jnp.int32))

    def redo_flagged():
        idx = jnp.nonzero(flags, size=PEER_TIE_SLOTS, fill_value=n)[0]
        fixed = _peer_tie_fallback(x[jnp.minimum(idx, n - 1)], lp)
        return out.at[idx].set(fixed, mode='drop')

    return lax.cond(count > PEER_TIE_SLOTS, lambda: _peer_tie_fallback(x, lp),
                    lambda: lax.cond(count > 0, redo_flagged, lambda: out))


def _peer_tie_fallback(x, lp):
    wq, keys, u_all, v_all, layer = lp['peer_raw']
    n_exp = u_all.shape[1]
    u_tab = u_all.reshape(-1, D_MODEL)
    v_tab = v_all.reshape(-1, D_MODEL)
    n = x.shape[0]
    blk = min(256, n)

    def block(xb):
        q = (xb @ wq).reshape(blk, PEER_HEADS, 2, PEER_NKEYS)
        s = jnp.einsum('nhpc,hpkc->nhpk', q, keys).astype(F32)
        s_top, i_top = lax.top_k(s, PEER_TOPK)
        cand = (s_top[:, :, 0, :, None] + s_top[:, :, 1, None, :]).reshape(blk, PEER_HEADS, -1)
        cand_idx = (i_top[:, :, 0, :, None] * PEER_NKEYS + i_top[:, :, 1, None, :]).reshape(blk, PEER_HEADS, -1)
        best, pick = lax.top_k(cand, PEER_TOPK)
        eidx = jnp.take_along_axis(cand_idx, pick, axis=-1) + layer * n_exp
        gw = jax.nn.softmax(best, axis=-1)
        act = jax.nn.gelu(jnp.einsum('nhkd,nd->nhk', u_tab[eidx], xb).astype(F32))
        return jnp.einsum('nhk,nhkd->nd', gw * act, v_tab[eidx])

    y = lax.map(block, x.reshape(-1, blk, D_MODEL)).reshape(n, D_MODEL)
    return _layer_norm(ALPHA * x + y, lp['ln2_w'], lp['ln2_b'])


DEC_N = 128
HEAD_STATE = HEAD_DIM * HEAD_DIM


def _col(v):
    return jnp.broadcast_to(v.reshape(-1, 1), (v.size, DEC_N))


def _load_state_t(s_ref, st):
    st[...] = s_ref[...].T.reshape(HEAD_DIM, HEAD_DIM, DEC_N)


def _store_state_t(so_ref, st):
    so_ref[...] = st[...].reshape(HEAD_STATE, DEC_N).T


def _ret_dec_kernel(c_ref, cos_ref, sin_ref, gw_ref, gb_ref, s_ref, o_ref, so_ref, ct, qk, st):
    h = pl.program_id(0)
    ct[...] = c_ref[...].T
    r0 = pl.multiple_of(h * HEAD_DIM, HEAD_DIM)

    def rot(x):
        partner = jnp.concatenate([x[HEAD_DIM // 2:], x[:HEAD_DIM // 2]], axis=0)
        return x * cos_ref[...] + partner * sin_ref[...]

    qk[0] = rot(ct[pl.ds(r0, HEAD_DIM), :])
    qk[1] = rot(ct[pl.ds(256 + r0, HEAD_DIM), :]) * (HEAD_DIM ** -0.5)
    v = ct[pl.ds(512 + r0, HEAD_DIM), :]
    g = ct[pl.ds(768 + r0, HEAD_DIM), :]
    gamma = jnp.exp(jnp.zeros((1, 1), F32) + jnp.where(
        h == 0, RET_LOG_GAMMA[0], jnp.where(h == 1, RET_LOG_GAMMA[1],
                                            jnp.where(h == 2, RET_LOG_GAMMA[2], RET_LOG_GAMMA[3]))))
    _load_state_t(s_ref, st)

    def body(k, o):
        s_new = gamma * st[k] + qk[1, pl.ds(k, 1), :] * v
        st[k] = s_new
        return o + qk[0, pl.ds(k, 1), :] * s_new

    o = lax.fori_loop(0, HEAD_DIM, body, jnp.zeros((HEAD_DIM, DEC_N), F32))
    mu = jnp.mean(o, axis=0, keepdims=True)
    d = o - mu
    var = jnp.mean(d * d, axis=0, keepdims=True)
    o_ref[...] = (d * lax.rsqrt(var + LN_EPS) * gw_ref[...] + gb_ref[...]) * _silu(g)
    _store_state_t(so_ref, st)


def _hgrn_dec_kernel(c_ref, lb_ref, nw_ref, s_ref, o_ref, so_ref, ct, qk, st):
    h = pl.program_id(0)
    ct[...] = c_ref[...].T
    r0 = pl.multiple_of(h * HEAD_DIM, HEAD_DIM)
    lb = lb_ref[...]
    qk[0] = _silu(ct[pl.ds(r0, HEAD_DIM), :])
    forget = lb + (1.0 - lb) * _sigmoid(ct[pl.ds(256 + r0, HEAD_DIM), :])
    qk[1] = forget
    qk[2] = 1.0 - forget
    v = ct[pl.ds(512 + r0, HEAD_DIM), :]
    g = ct[pl.ds(768 + r0, HEAD_DIM), :]
    _load_state_t(s_ref, st)

    def body(k, o):
        s_new = qk[1, pl.ds(k, 1), :] * st[k] + qk[2, pl.ds(k, 1), :] * v
        st[k] = s_new
        return o + qk[0, pl.ds(k, 1), :] * s_new

    o = lax.fori_loop(0, HEAD_DIM, body, jnp.zeros((HEAD_DIM, DEC_N), F32))
    ms = jnp.mean(o * o, axis=0, keepdims=True)
    o_ref[...] = o * lax.rsqrt(ms + LN_EPS) * nw_ref[...] * _silu(g)
    _store_state_t(so_ref, st)


def _rwkv_dec_kernel(c_ref, sh_ref, mu_ref, w0_ref, w2_ref, a0_ref, a2_ref, g2_ref, kkw_ref, ka_ref,
                     rk_ref, lnw_ref, lnb_ref, s_ref, o_ref, so_ref, vt, st, osc):
    h = pl.program_id(0)
    r, lw, k2, v, kkn, a, g = _rwkv_token_mix(c_ref[...], sh_ref[...], mu_ref, w0_ref, w2_ref, a0_ref, a2_ref,
                                              g2_ref, kkw_ref, ka_ref)
    for idx, arr in enumerate((r, jnp.exp(lw), k2, v, kkn, a, g)):
        vt[idx] = arr.T
    r0 = pl.multiple_of(h * HEAD_DIM, HEAD_DIM)
    rows = pl.ds(r0, HEAD_DIM)
    rh, wh, kh, kkh, ah = vt[0, rows, :], vt[1, rows, :], vt[2, rows, :], vt[4, rows, :], vt[5, rows, :]
    vh, gh = vt[3, rows, :], vt[6, rows, :]
    kka = kkh * ah
    _load_state_t(s_ref, st)

    def body(vi, carry):
        s_old = st[vi]
        sa = jnp.sum(s_old * (-kkh), axis=0, keepdims=True)
        s_new = s_old * wh + sa * kka + vt[3, pl.ds(r0 + vi, 1), :] * kh
        st[vi] = s_new
        osc[pl.ds(vi, 1), :] = jnp.sum(s_new * rh, axis=0, keepdims=True)
        return carry

    lax.fori_loop(0, HEAD_DIM, body, 0)
    o = osc[...]
    mu = jnp.mean(o, axis=0, keepdims=True)
    d = o - mu
    var = jnp.mean(d * d, axis=0, keepdims=True)
    on = d * lax.rsqrt(var + RW_GN_EPS) * lnw_ref[...] + lnb_ref[...]
    bonus = jnp.sum(rh * kh * rk_ref[...], axis=0, keepdims=True) * vh
    o_ref[...] = (on + bonus) * gh
    _store_state_t(so_ref, st)


def _dec_specs():
    head_tab = pl.BlockSpec((HEAD_DIM, DEC_N), lambda h: (h, 0))
    state = pl.BlockSpec((DEC_N, HEAD_STATE), lambda h: (0, h))
    out = pl.BlockSpec((None, HEAD_DIM, DEC_N), lambda h: (h, 0, 0))
    return head_tab, state, out


def _dec_out_shapes():
    return [jax.ShapeDtypeStruct((N_HEADS, HEAD_DIM, DEC_N), F32),
            jax.ShapeDtypeStruct((DEC_N, N_HEADS * HEAD_STATE), F32)]


def _dec_finish(o_t, s_new):
    return o_t.reshape(BRANCH_W, DEC_N).T, s_new.reshape(DEC_N, N_HEADS, HEAD_DIM, HEAD_DIM)


def _ret_decode(cols, state, cos_c, sin_c, lp):
    head_tab, st_spec, out_spec = _dec_specs()
    same = pl.BlockSpec((HEAD_DIM, DEC_N), lambda h: (0, 0))
    o_t, s_new = pl.pallas_call(
        _ret_dec_kernel,
        grid=(N_HEADS,),
        in_specs=[pl.BlockSpec((DEC_N, 1024), lambda h: (0, 0)), same, same, head_tab, head_tab, st_spec],
        out_specs=[out_spec, st_spec],
        out_shape=_dec_out_shapes(),
        scratch_shapes=[pltpu.VMEM((1024, DEC_N), F32), pltpu.VMEM((2, HEAD_DIM, DEC_N), F32),
                        pltpu.VMEM((HEAD_DIM, HEAD_DIM, DEC_N), F32)],
        compiler_params=_cparams(("arbitrary",), 40),
        name="ret_decode",
    )(cols, cos_c, sin_c, _col(lp['ret_gn_w']), _col(lp['ret_gn_b']), state.reshape(DEC_N, -1))
    return _dec_finish(o_t, s_new)


def _hgrn_decode(cols, state, lp):
    head_tab, st_spec, out_spec = _dec_specs()
    o_t, s_new = pl.pallas_call(
        _hgrn_dec_kernel,
        grid=(N_HEADS,),
        in_specs=[pl.BlockSpec((DEC_N, 1024), lambda h: (0, 1)), head_tab, head_tab, st_spec],
        out_specs=[out_spec, st_spec],
        out_shape=_dec_out_shapes(),
        scratch_shapes=[pltpu.VMEM((1024, DEC_N), F32), pltpu.VMEM((3, HEAD_DIM, DEC_N), F32),
                        pltpu.VMEM((HEAD_DIM, HEAD_DIM, DEC_N), F32)],
        compiler_params=_cparams(("arbitrary",), 40),
        name="hgrn_decode",
    )(cols, _col(lp['hg_lb']), _col(lp['hg_norm_w']), state.reshape(DEC_N, -1))
    return _dec_finish(o_t, s_new)


def _rwkv_decode(cols, shift, state, lp):
    head_tab, st_spec, out_spec = _dec_specs()
    row = lambda w: pl.BlockSpec((1, w), lambda h: (0, 0))
    mat = lambda r, w: pl.BlockSpec((r, w), lambda h: (0, 0))
    o_t, s_new = pl.pallas_call(
        _rwkv_dec_kernel,
        grid=(N_HEADS,),
        in_specs=[pl.BlockSpec((DEC_N, 1024), lambda h: (0, 2)), mat(DEC_N, 1024),
                  row(1024), row(256), mat(64, 256), row(256), mat(64, 256), mat(128, 256),
                  row(256), row(256), head_tab, head_tab, head_tab, st_spec],
        out_specs=[out_spec, st_spec],
        out_shape=_dec_out_shapes(),
        scratch_shapes=[pltpu.VMEM((7, BRANCH_W, DEC_N), F32), pltpu.VMEM((HEAD_DIM, HEAD_DIM, DEC_N), F32),
                        pltpu.VMEM((HEAD_DIM, DEC_N), F32)],
        compiler_params=_cparams(("arbitrary",), 40),
        name="rwkv_decode",
    )(cols, shift, lp['rw_mu'], lp['rw_w0'], lp['rw_w2'], lp['rw_a0'], lp['rw_a2'], lp['rw_g2'],
      lp['rw_kk'], lp['rw_ka'], _col(lp['rw_rk']), _col(lp['rw_lnx_w']), _col(lp['rw_lnx_b']),
      state.reshape(DEC_N, -1))
    return _dec_finish(o_t, s_new)


def _lru_dec_kernel(c_ref, conv_ref, h0_ref, cw_ref, cb_ref, wa_ref, ba_ref, wx_ref, bx_ref, lam_ref,
                    y_ref, h_ref, nconv_ref):
    xb = c_ref[:, 0:BRANCH_W]
    gate = c_ref[:, BRANCH_W:2 * BRANCH_W]
    c0, c1, c2 = conv_ref[0], conv_ref[1], conv_ref[2]
    xc = (c0 * cw_ref[0:1, :] + c1 * cw_ref[1:2, :] + c2 * cw_ref[2:3, :] + xb * cw_ref[3:4, :]) + cb_ref[...]
    a, u = _lru_gates(xc, wa_ref, ba_ref, wx_ref, bx_ref, lam_ref)
    hn = a * h0_ref[...] + u
    h_ref[...] = hn
    y_ref[...] = hn * _gelu(gate)
    nconv_ref[0] = c1
    nconv_ref[1] = c2
    nconv_ref[2] = xb


def _lru_decode(cols, conv, h0, lp):
    full = lambda *s: pl.BlockSpec(s, lambda i: (0,) * len(s))
    return pl.pallas_call(
        _lru_dec_kernel,
        grid=(1,),
        in_specs=[pl.BlockSpec((DEC_N, 512), lambda i: (0, 6)), full(3, DEC_N, BRANCH_W), full(DEC_N, BRANCH_W),
                  full(CONV_W, BRANCH_W), full(1, BRANCH_W), full(BRANCH_W, BRANCH_W), full(1, BRANCH_W),
                  full(BRANCH_W, BRANCH_W), full(1, BRANCH_W), full(1, BRANCH_W)],
        out_specs=[full(DEC_N, BRANCH_W), full(DEC_N, BRANCH_W), full(3, DEC_N, BRANCH_W)],
        out_shape=[jax.ShapeDtypeStruct((DEC_N, BRANCH_W), F32), jax.ShapeDtypeStruct((DEC_N, BRANCH_W), F32),
                   jax.ShapeDtypeStruct((3, DEC_N, BRANCH_W), F32)],
        compiler_params=_cparams(("arbitrary",), 32),
        name="lru_decode",
    )(cols, conv, h0, lp['lru_conv_w'], lp['lru_conv_b'], lp['lru_wa_bd'], lp['lru_ba'], lp['lru_wx_bd'],
      lp['lru_bx'], lp['lru_lambda'])


def _prompt_layer(x, p_emb, lp, cos_t, sin_t):
    b, t, d = x.shape
    n = b * t
    xf = x.reshape(n, d)
    o_a, s_ret = _ret_prompt(x, cos_t, sin_t, lp)
    o_b, s_hg = _hgrn_prompt(x, lp['hg_lb'], lp)
    o_c, s_shift, s_rw = _rwkv_prompt(x, lp)
    o_d, s_lru, s_conv = _lru_prompt(x, lp)
    outs = [o.reshape(n, BRANCH_W) for o in (o_a, o_b, o_c, o_d)]
    x1 = _mix(xf, outs, lp, 512)
    x2 = _peer(x1, lp, 512)
    x3 = _ple(x2, p_emb.reshape(n, -1), lp, 512)
    return x3.reshape(b, t, d), (s_ret, s_hg, s_rw, s_shift[:, 0], s_lru[:, 0], s_conv)


def _sample_layer(x, p_emb, state, lp, cos_c, sin_c):
    s_ret, s_hg, s_rw, s_shift, s_lru, s_conv = state
    xf = x.reshape(DEC_N, D_MODEL)
    cols = _proj(xf, lp['w_in_br'], DEC_N)
    o_a, s_ret = _ret_decode(cols, s_ret, cos_c, sin_c, lp)
    o_b, s_hg = _hgrn_decode(cols, s_hg, lp)
    o_c, s_rw = _rwkv_decode(cols, s_shift, s_rw, lp)
    o_d, s_lru, s_conv_t = _lru_decode(cols, jnp.swapaxes(s_conv, 0, 1), s_lru, lp)
    x1 = _mix(xf, [o_a, o_b, o_c, o_d], lp, DEC_N)
    x2 = _peer(x1, lp, DEC_N)
    x3 = _ple(x2, p_emb.reshape(DEC_N, -1), lp, DEC_N)
    new_shift = cols[:, 2048:3072]
    return x3.reshape(x.shape), (s_ret, s_hg, s_rw, new_shift, s_lru, jnp.swapaxes(s_conv_t, 0, 1))


def kernel(x_prompt, x_sample, state_ret, state_hgrn, state_rwkv, state_shift, state_lru, state_conv, p_prompt, p_sample, w_in, ret_gn_w, ret_gn_b, hg_lb, hg_norm_w, rw_mu, rw_w0, rw_w2, rw_a0, rw_a2, rw_g2, rw_kk, rw_ka, rw_rk, rw_lnx_w, rw_lnx_b, lru_conv_w, lru_conv_b, lru_wa, lru_ba, lru_wx, lru_bx, lru_lambda, w_branch, w_out, ln1_w, ln1_b, peer_wq, peer_keys, peer_u, peer_v, ln2_w, ln2_b, ple_w, ple_gate_w, ple_gate_b):
    params = dict(w_in=w_in, ret_gn_w=ret_gn_w, ret_gn_b=ret_gn_b, hg_norm_w=hg_norm_w, rw_mu=rw_mu, rw_w0=rw_w0,
                  rw_w2=rw_w2, rw_a0=rw_a0, rw_a2=rw_a2, rw_g2=rw_g2, rw_kk=rw_kk, rw_ka=rw_ka, rw_rk=rw_rk,
                  rw_lnx_w=rw_lnx_w, rw_lnx_b=rw_lnx_b, lru_conv_w=lru_conv_w, lru_conv_b=lru_conv_b,
                  lru_wa=lru_wa, lru_ba=lru_ba, lru_wx=lru_wx, lru_bx=lru_bx, lru_lambda=lru_lambda,
                  w_branch=w_branch, w_out=w_out, ln1_w=ln1_w, ln1_b=ln1_b, peer_wq=peer_wq, peer_keys=peer_keys,
                  peer_u=peer_u, peer_v=peer_v, ln2_w=ln2_w, ln2_b=ln2_b, ple_w=ple_w, ple_gate_w=ple_gate_w,
                  ple_gate_b=ple_gate_b)
    lb_cum = jnp.cumsum(jax.nn.softmax(hg_lb.astype(F32), axis=0), axis=0)
    lb_all = lb_cum - lb_cum[0:1]
    t_prompt = x_prompt.shape[1]
    past_len = 16384
    cos_t, sin_t = _rope_tables(jnp.arange(t_prompt))
    cos_s, sin_s = _rope_tables(past_len + jnp.arange(1))
    cos_c = _col(cos_s[0, :HEAD_DIM])
    sin_c = _col(sin_s[0, :HEAD_DIM])
    h_p, h_s = x_prompt, x_sample
    new_p, new_s = [], []
    for l in range(DEPTH):
        lp = _layer_params(params, l, lb_all)
        h_p, st_p = _prompt_layer(h_p, p_prompt[l], lp, cos_t, sin_t)
        st_in = (state_ret[l], state_hgrn[l], state_rwkv[l], state_shift[l], state_lru[l], state_conv[l])
        h_s, st_s = _sample_layer(h_s, p_sample[l], st_in, lp, cos_c, sin_c)
        new_p.append(st_p)
        new_s.append(st_s)
    outs_p = [jnp.stack(z) for z in zip(*new_p)]
    outs_s = [jnp.stack(z) for z in zip(*new_s)]
    return (h_p, h_s, *outs_p, *outs_s)
```

```python
import functools
import math

import jax
import jax.numpy as jnp
from jax import lax
from jax.experimental import pallas as pl
from jax.experimental.pallas import tpu as pltpu

F32 = jnp.float32
BF16 = jnp.bfloat16
HIGHEST = lax.Precision.HIGHEST

D_MODEL = 1024
BRANCH_W = 256
N_HEADS = 4
HEAD_DIM = 64
RET_LOG_GAMMA = tuple(math.log1p(-(2.0 ** (-5.0 - h))) for h in range(N_HEADS))
ROPE_BASE = 10000.0
RW_GN_EPS = 64e-5
LN_EPS = 1e-5
LRU_C = 8.0
CONV_W = 4
DEPTH = 2
ALPHA = (2 * DEPTH) ** 0.25
BRANCH_COLS = 3584
MIB = 1024 * 1024


def _cparams(semantics, vmem_mib):
    return pltpu.CompilerParams(dimension_semantics=semantics, vmem_limit_bytes=vmem_mib * MIB)


def _mm(a, b):
    return jnp.dot(a.astype(BF16), b.astype(BF16), preferred_element_type=F32)


def _mm_nt(a, b):
    return lax.dot_general(a.astype(BF16), b.astype(BF16), (((1,), (1,)), ((), ())), preferred_element_type=F32)


def _mm_tn(a, b):
    return lax.dot_general(a.astype(BF16), b.astype(BF16), (((0,), (0,)), ((), ())), preferred_element_type=F32)


def _transpose_exact(a):
    n = a.shape[0]
    eye = (lax.broadcasted_iota(jnp.int32, (n, n), 0) == lax.broadcasted_iota(jnp.int32, (n, n), 1)).astype(F32)
    return lax.dot_general(a, eye, (((0,), (0,)), ((), ())), preferred_element_type=F32, precision=HIGHEST)


def _select_rows_exact(sel, x):
    x1 = x.astype(BF16)
    r1 = x - x1.astype(F32)
    x2 = r1.astype(BF16)
    x3 = (r1 - x2.astype(F32)).astype(BF16)
    dot = lambda p: jnp.dot(sel, p, preferred_element_type=F32)
    return (dot(x1) + dot(x2)) + dot(x3)


def _tril_ones(n):
    r = lax.broadcasted_iota(jnp.int32, (n, n), 0)
    c = lax.broadcasted_iota(jnp.int32, (n, n), 1)
    return (r >= c).astype(BF16)


def _sigmoid(x):
    return 1.0 / (1.0 + jnp.exp(-x))


def _silu(x):
    return x * _sigmoid(x)


def _gelu(x):
    c = 0.7978845608028654
    half = 0.5 * x
    return half + half * jnp.tanh(x * (c + (c * 0.044715) * (x * x)))


def _softplus(x):
    return jnp.maximum(x, 0.0) + jnp.log1p(jnp.exp(-jnp.abs(x)))


def _head(x, h):
    return x[:, h * HEAD_DIM:(h + 1) * HEAD_DIM]


def _project_rows(x_ref, w_ref, nb, tc):
    x = x_ref[...].reshape(nb * tc, x_ref.shape[-1]).astype(BF16)
    cols = jnp.dot(x, w_ref[...], preferred_element_type=F32)
    return [cols[n * tc:(n + 1) * tc] for n in range(nb)]


def _block_diag(w):
    n, c, d = w.shape
    eye = jnp.eye(n, dtype=w.dtype)
    return (eye[:, None, :, None] * w[:, :, None, :]).reshape(n * c, n * d)


def _rope_tables(pos):
    half = HEAD_DIM // 2
    inv_freq = ROPE_BASE ** (-jnp.arange(half, dtype=F32) / half)
    ang = pos.astype(F32)[:, None] * inv_freq[None, :]
    cos = jnp.cos(ang)
    sin = jnp.sin(ang)
    cos_t = jnp.tile(jnp.concatenate([cos, cos], axis=-1), (1, N_HEADS))
    sin_t = jnp.tile(jnp.concatenate([-sin, sin], axis=-1), (1, N_HEADS))
    return cos_t, sin_t


def _layer_params(p, l, lb_all):
    r2 = lambda a: a.reshape(1, -1)
    w_in = p['w_in'][l]
    lp = {
        'w_in_br': w_in[:, :BRANCH_COLS].astype(BF16),
        'w_gate': w_in[:, BRANCH_COLS:].astype(BF16),
        'ret_gn_w': r2(p['ret_gn_w'][l]), 'ret_gn_b': r2(p['ret_gn_b'][l]),
        'hg_lb': r2(lb_all[l]), 'hg_norm_w': r2(p['hg_norm_w'][l]),
        'rw_mu': r2(p['rw_mu'][l]), 'rw_w0': r2(p['rw_w0'][l]), 'rw_w2': p['rw_w2'][l].astype(BF16),
        'rw_a0': r2(p['rw_a0'][l]), 'rw_a2': p['rw_a2'][l].astype(BF16), 'rw_g2': p['rw_g2'][l].astype(BF16),
        'rw_kk': r2(p['rw_kk'][l]), 'rw_ka': r2(p['rw_ka'][l]), 'rw_rk': r2(p['rw_rk'][l]),
        'rw_lnx_w': r2(p['rw_lnx_w'][l]), 'rw_lnx_b': r2(p['rw_lnx_b'][l]),
        'lru_conv_w': p['lru_conv_w'][l], 'lru_conv_b': r2(p['lru_conv_b'][l]),
        'lru_wa_bd': _block_diag(p['lru_wa'][l]).astype(BF16), 'lru_ba': r2(p['lru_ba'][l]),
        'lru_wx_bd': _block_diag(p['lru_wx'][l]).astype(BF16), 'lru_bx': r2(p['lru_bx'][l]),
        'lru_lambda': r2(p['lru_lambda'][l]),
        'w_branch': p['w_branch'][l].astype(BF16), 'w_out': p['w_out'][l].astype(BF16),
        'ln1_w': r2(p['ln1_w'][l]), 'ln1_b': r2(p['ln1_b'][l]),
        'peer_wq_t': p['peer_wq'][l].T.astype(BF16),
        'peer_keys': p['peer_keys'][l].reshape(16, 128, 128).astype(BF16),
        'peer_u': p['peer_u'][l].astype(BF16),
        'peer_raw': (p['peer_wq'][l], p['peer_keys'][l], p['peer_u'], p['peer_v'], l),
        'peer_v_t': jnp.swapaxes(p['peer_v'][l].astype(BF16).reshape(-1, PEER_EB, D_MODEL), 1, 2),
        'ln2_w': r2(p['ln2_w'][l]), 'ln2_b': r2(p['ln2_b'][l]),
        'ple_w': p['ple_w'][l].astype(BF16), 'ple_gate_w': p['ple_gate_w'][l].astype(BF16),
        'ple_gate_b': r2(p['ple_gate_b'][l]),
    }
    return lp


def _proj_kernel(x_ref, w_ref, o_ref):
    o_ref[...] = jnp.dot(x_ref[...].astype(BF16), w_ref[...], preferred_element_type=F32)


def _proj(x, w_bf16, tn):
    n, k = x.shape
    m = w_bf16.shape[1]
    return pl.pallas_call(
        _proj_kernel,
        grid=(n // tn,),
        in_specs=[pl.BlockSpec((tn, k), lambda i: (i, 0)),
                  pl.BlockSpec((k, m), lambda i: (0, 0))],
        out_specs=pl.BlockSpec((tn, m), lambda i: (i, 0)),
        out_shape=jax.ShapeDtypeStruct((n, m), F32),
        compiler_params=_cparams(("parallel",), 48),
        name="in_proj",
    )(x, w_bf16)


def _lru_gates(xc, wa_ref, ba_ref, wx_ref, bx_ref, lam_ref):
    r = _sigmoid(_mm(xc, wa_ref[...]) + ba_ref[...])
    i = _sigmoid(_mm(xc, wx_ref[...]) + bx_ref[...])
    log_a = -LRU_C * r * _softplus(-lam_ref[...])
    a = jnp.exp(log_a)
    u = jnp.sqrt(1.0 - jnp.exp(2.0 * log_a)) * (i * xc)
    return a, u


def _lru_kernel(x_ref, w_ref, cw_ref, cb_ref, wa_ref, ba_ref, wx_ref, bx_ref, lam_ref,
                y_ref, h_ref, conv_ref, xbuf, hcar, a_s, u_s, hs, *, tc):
    c = pl.program_id(1)

    @pl.when(c == 0)
    def _():
        xbuf[0:8, :] = jnp.zeros((8, BRANCH_W), F32)
        hcar[...] = jnp.zeros((1, BRANCH_W), F32)

    cols = jnp.dot(x_ref[...].astype(BF16), w_ref[...], preferred_element_type=F32)
    xb = cols[:, 0:BRANCH_W]
    gate = cols[:, BRANCH_W:2 * BRANCH_W]
    xbuf[8:8 + tc, :] = xb
    xc = (xbuf[pl.ds(5, tc), :] * cw_ref[0:1, :] + xbuf[pl.ds(6, tc), :] * cw_ref[1:2, :]
          + xbuf[pl.ds(7, tc), :] * cw_ref[2:3, :] + xb * cw_ref[3:4, :]) + cb_ref[...]
    a, u = _lru_gates(xc, wa_ref, ba_ref, wx_ref, bx_ref, lam_ref)
    a_s[...] = a
    u_s[...] = u

    def body(t, h):
        h = a_s[pl.ds(t, 1), :] * h + u_s[pl.ds(t, 1), :]
        hs[pl.ds(t, 1), :] = h
        return h

    h = lax.fori_loop(0, tc, body, hcar[...], unroll=8)
    hcar[...] = h
    y_ref[...] = hs[...] * _gelu(gate)
    xbuf[0:8, :] = xbuf[tc:tc + 8, :]

    @pl.when(c == pl.num_programs(1) - 1)
    def _():
        h_ref[...] = h
        conv_ref[...] = xbuf[5:8, :]


def _lru_prompt(x3, lp, tc=256):
    b, t, d = x3.shape
    row = lambda: pl.BlockSpec((1, BRANCH_W), lambda i, j: (0, 0))
    full = lambda r: pl.BlockSpec((r, BRANCH_W), lambda i, j: (0, 0))
    return pl.pallas_call(
        functools.partial(_lru_kernel, tc=tc),
        grid=(b, t // tc),
        in_specs=[pl.BlockSpec((None, tc, d), lambda i, j: (i, j, 0)),
                  pl.BlockSpec((d, 2 * BRANCH_W), lambda i, j: (0, 6)),
                  full(CONV_W), row(), full(BRANCH_W), row(), full(BRANCH_W), row(), row()],
        out_specs=[pl.BlockSpec((None, tc, BRANCH_W), lambda i, j: (i, j, 0)),
                   pl.BlockSpec((None, 1, BRANCH_W), lambda i, j: (i, 0, 0)),
                   pl.BlockSpec((None, CONV_W - 1, BRANCH_W), lambda i, j: (i, 0, 0))],
        out_shape=[jax.ShapeDtypeStruct((b, t, BRANCH_W), F32),
                   jax.ShapeDtypeStruct((b, 1, BRANCH_W), F32),
                   jax.ShapeDtypeStruct((b, CONV_W - 1, BRANCH_W), F32)],
        scratch_shapes=[pltpu.VMEM((tc + 8, BRANCH_W), F32), pltpu.VMEM((1, BRANCH_W), F32),
                        pltpu.VMEM((tc, BRANCH_W), F32), pltpu.VMEM((tc, BRANCH_W), F32),
                        pltpu.VMEM((tc, BRANCH_W), F32)],
        compiler_params=_cparams(("parallel", "arbitrary"), 32),
        name="lru_prompt",
    )(x3, lp['w_in_br'], lp['lru_conv_w'], lp['lru_conv_b'], lp['lru_wa_bd'], lp['lru_ba'], lp['lru_wx_bd'],
      lp['lru_bx'], lp['lru_lambda'])


def _rotary(x, cos, sin_signed, first_half):
    partner = jnp.where(first_half, pltpu.roll(x, BRANCH_W - 32, 1), pltpu.roll(x, 32, 1))
    return x * cos + partner * sin_signed


def _ret_kernel(x_ref, w_ref, cos_ref, sin_ref, gw_ref, gb_ref, o_ref, s_ref, S, *, tc, nb):
    c = pl.program_id(1)

    @pl.when(c == 0)
    def _():
        S[...] = jnp.zeros(S.shape, F32)

    lane = lax.broadcasted_iota(jnp.int32, (tc, BRANCH_W), 1)
    first_half = (lane % HEAD_DIM) < (HEAD_DIM // 2)
    cos = cos_ref[...]
    sin = sin_ref[...]
    row = lax.broadcasted_iota(jnp.int32, (tc, tc), 0)
    col = lax.broadcasted_iota(jnp.int32, (tc, tc), 1)
    causal = row >= col
    dist = jnp.where(causal, row - col, 0).astype(F32)
    tpos = lax.broadcasted_iota(jnp.int32, (tc, HEAD_DIM), 0).astype(F32)
    decay = [jnp.where(causal, jnp.exp(dist * RET_LOG_GAMMA[h]), 0.0) for h in range(N_HEADS)]
    q_in = [jnp.exp((tpos + 1.0) * RET_LOG_GAMMA[h]) for h in range(N_HEADS)]
    k_out = [jnp.exp((tc - 1.0 - tpos) * RET_LOG_GAMMA[h]) for h in range(N_HEADS)]
    cols = _project_rows(x_ref, w_ref, nb, tc)
    q = [_rotary(cols[n][:, 0:256], cos, sin, first_half) for n in range(nb)]
    k = [_rotary(cols[n][:, 256:512], cos, sin, first_half) * (HEAD_DIM ** -0.5) for n in range(nb)]
    chains = [(n, h) for n in range(nb) for h in range(N_HEADS)]
    scores = {(n, h): _mm_nt(_head(q[n], h), _head(k[n], h)) * decay[h] for n, h in chains}
    out = {(n, h): _mm(scores[n, h], _head(cols[n][:, 512:768], h)) + _mm(_head(q[n], h) * q_in[h], S[n, h])
           for n, h in chains}
    for n, h in chains:
        S[n, h] = (math.exp(tc * RET_LOG_GAMMA[h]) * S[n, h]
                   + _mm_tn(_head(k[n], h) * k_out[h], _head(cols[n][:, 512:768], h)))
    for n, h in chains:
        oh = out[n, h]
        mu = jnp.mean(oh, axis=-1, keepdims=True)
        d = oh - mu
        var = jnp.mean(d * d, axis=-1, keepdims=True)
        on = d * lax.rsqrt(var + LN_EPS) * _head(gw_ref[...], h) + _head(gb_ref[...], h)
        o_ref[n, :, h * HEAD_DIM:(h + 1) * HEAD_DIM] = on * _silu(_head(cols[n][:, 768:1024], h))

    @pl.when(c == pl.num_programs(1) - 1)
    def _():
        s_ref[...] = S[...]


def _ret_prompt(x3, cos_t, sin_t, lp, tc=128, nb=2):
    b, t, d = x3.shape
    nb = min(nb, b)
    row = lambda: pl.BlockSpec((1, BRANCH_W), lambda i, j: (0, 0))
    return pl.pallas_call(
        functools.partial(_ret_kernel, tc=tc, nb=nb),
        grid=(b // nb, t // tc),
        in_specs=[pl.BlockSpec((nb, tc, d), lambda i, j: (i, j, 0)),
                  pl.BlockSpec((d, 1024), lambda i, j: (0, 0)),
                  pl.BlockSpec((tc, BRANCH_W), lambda i, j: (j, 0)),
                  pl.BlockSpec((tc, BRANCH_W), lambda i, j: (j, 0)), row(), row()],
        out_specs=[pl.BlockSpec((nb, tc, BRANCH_W), lambda i, j: (i, j, 0)),
                   pl.BlockSpec((nb, N_HEADS, HEAD_DIM, HEAD_DIM), lambda i, j: (i, 0, 0, 0))],
        out_shape=[jax.ShapeDtypeStruct((b, t, BRANCH_W), F32),
                   jax.ShapeDtypeStruct((b, N_HEADS, HEAD_DIM, HEAD_DIM), F32)],
        scratch_shapes=[pltpu.VMEM((nb, N_HEADS, HEAD_DIM, HEAD_DIM), F32)],
        compiler_params=_cparams(("parallel", "arbitrary"), 32),
        name="ret_prompt",
    )(x3, lp['w_in_br'], cos_t, sin_t, lp['ret_gn_w'], lp['ret_gn_b'])


HGRN_CHUNK = 64
HGRN_LEVELS = tuple(HGRN_CHUNK >> l for l in range(1, 7))


def _hgrn_select_matrix():
    t = jnp.arange(HGRN_CHUNK)
    s = jnp.arange(HGRN_CHUNK)
    blocks = [s[None, :] <= t[:, None]]
    for m in HGRN_LEVELS:
        mid = (t // (2 * m)) * (2 * m) + m - 1
        blocks.append(s[None, :] <= mid[:, None])
    return jnp.concatenate(blocks, axis=0).astype(BF16)


def _hgrn_kernel(x_ref, w_ref, lb_ref, nw_ref, sel_ref, o_ref, s_ref, St, *, tc, nb):
    c = pl.program_id(1)

    @pl.when(c == 0)
    def _():
        St[...] = jnp.zeros(St.shape, F32)

    lb = lb_ref[...]
    row = lax.broadcasted_iota(jnp.int32, (tc, BRANCH_W), 0)
    r64 = lax.broadcasted_iota(jnp.int32, (tc, tc), 0)
    c64 = lax.broadcasted_iota(jnp.int32, (tc, tc), 1)
    tok = []
    all_cols = _project_rows(x_ref, w_ref, nb, tc)
    for n in range(nb):
        cols = all_cols[n]
        q = _silu(cols[:, 0:256])
        forget = lb + (1.0 - lb) * _sigmoid(cols[:, 256:512])
        kk = 1.0 - forget
        sums = _select_rows_exact(sel_ref[...], jnp.log(forget))
        tok.append(dict(q=q, kk=kk, sums=sums, b=sums[0:tc],
                        v=cols[:, 512:768], g=cols[:, 768:1024]))
    chains = [(n, h) for n in range(nb) for h in range(N_HEADS)]
    scores = {(n, h): jnp.where(r64 == c64, _mm_nt(_head(tok[n]['q'], h), _head(tok[n]['kk'], h)), 0.0)
              for n, h in chains}
    for lvl, m in enumerate(HGRN_LEVELS):
        sh = m.bit_length() - 1
        upper = ((row >> sh) & 1) == 1
        same = (r64 >> (sh + 1)) == (c64 >> (sh + 1))
        scaled = []
        for n in range(nb):
            t = tok[n]
            e = jnp.exp(-jnp.abs(t['b'] - t['sums'][(lvl + 1) * tc:(lvl + 2) * tc]))
            scaled.append((jnp.where(upper, t['q'] * e, 0.0), jnp.where(upper, 0.0, t['kk'] * e)))
        for n, h in chains:
            scores[n, h] = scores[n, h] + jnp.where(same, _mm_nt(_head(scaled[n][0], h), _head(scaled[n][1], h)), 0.0)
    for n in range(nb):
        t = tok[n]
        b = t['b']
        v = t['v']
        g = t['g']
        b_end = b[tc - 1:tc, :]
        qd = t['q'] * jnp.exp(b)
        kd = t['kk'] * jnp.exp(b_end - b)
        for h in range(N_HEADS):
            vh = _head(v, h)
            oh = _mm(scores[n, h], vh) + _mm_nt(_head(qd, h), St[n, h])
            St[n, h] = St[n, h] * jnp.exp(_head(b_end, h)) + _mm_tn(vh, _head(kd, h))
            ms = jnp.mean(oh * oh, axis=-1, keepdims=True)
            on = oh * lax.rsqrt(ms + LN_EPS) * _head(nw_ref[...], h)
            o_ref[n, :, h * HEAD_DIM:(h + 1) * HEAD_DIM] = on * _silu(_head(g, h))

    @pl.when(c == pl.num_programs(1) - 1)
    def _():
        for n in range(nb):
            for h in range(N_HEADS):
                s_ref[n, h] = _transpose_exact(St[n, h])


def _hgrn_prompt(x3, lb, lp, nb=8):
    b, t, d = x3.shape
    nb = min(nb, b)
    tc = HGRN_CHUNK
    row = lambda: pl.BlockSpec((1, BRANCH_W), lambda i, j: (0, 0))
    return pl.pallas_call(
        functools.partial(_hgrn_kernel, tc=tc, nb=nb),
        grid=(b // nb, t // tc),
        in_specs=[pl.BlockSpec((nb, tc, d), lambda i, j: (i, j, 0)),
                  pl.BlockSpec((d, 1024), lambda i, j: (0, 1)), row(), row(),
                  pl.BlockSpec((7 * tc, tc), lambda i, j: (0, 0))],
        out_specs=[pl.BlockSpec((nb, tc, BRANCH_W), lambda i, j: (i, j, 0)),
                   pl.BlockSpec((nb, N_HEADS, HEAD_DIM, HEAD_DIM), lambda i, j: (i, 0, 0, 0))],
        out_shape=[jax.ShapeDtypeStruct((b, t, BRANCH_W), F32),
                   jax.ShapeDtypeStruct((b, N_HEADS, HEAD_DIM, HEAD_DIM), F32)],
        scratch_shapes=[pltpu.VMEM((nb, N_HEADS, HEAD_DIM, HEAD_DIM), F32)],
        compiler_params=_cparams(("parallel", "arbitrary"), 32),
        name="hgrn_prompt",
    )(x3, lp['w_in_br'], lb, lp['hg_norm_w'], _hgrn_select_matrix())


def _rwkv_token_mix(cols, prev, mu_ref, w0_ref, w2_ref, a0_ref, a2_ref, g2_ref, kkw_ref, ka_ref):
    xs = cols + mu_ref[...] * (prev - cols)
    r = xs[:, 0:256]
    k = xs[:, 256:512]
    v = xs[:, 512:768]
    xg = xs[:, 768:896]
    xw = xs[:, 896:960]
    xa = xs[:, 960:1024]
    w = -_softplus(-(w0_ref[...] + _mm(jnp.tanh(xw), w2_ref[...]))) - 0.5
    lw = -jnp.exp(w)
    a = _sigmoid(a0_ref[...] + _mm(xa, a2_ref[...]))
    g = _mm(_sigmoid(xg), g2_ref[...])
    kk = k * kkw_ref[...]
    parts = []
    for h in range(N_HEADS):
        kh = _head(kk, h)
        nrm = jnp.sqrt(jnp.sum(kh * kh, axis=-1, keepdims=True))
        parts.append(kh / jnp.maximum(nrm, 1e-12))
    kkn = jnp.concatenate(parts, axis=-1)
    k2 = k * (1.0 + (a - 1.0) * ka_ref[...])
    return r, lw, k2, v, kkn, a, g


def _rwkv_out(o, r, k2, v, g, rk_ref, lw_ref, lb_ref, h):
    mu = jnp.mean(o, axis=-1, keepdims=True)
    d = o - mu
    var = jnp.mean(d * d, axis=-1, keepdims=True)
    on = d * lax.rsqrt(var + RW_GN_EPS) * _head(lw_ref[...], h) + _head(lb_ref[...], h)
    bonus = jnp.sum(_head(r, h) * _head(k2, h) * _head(rk_ref[...], h), axis=-1, keepdims=True) * _head(v, h)
    return (on + bonus) * _head(g, h)


def _rwkv_kernel(x_ref, w_ref, mu_ref, w0_ref, w2_ref, a0_ref, a2_ref, g2_ref, kkw_ref, ka_ref, rk_ref,
                 lnw_ref, lnb_ref, y_ref, shift_ref, s_ref, S, last_row, *, tc, nb):
    c = pl.program_id(1)

    @pl.when(c == 0)
    def _():
        S[...] = jnp.zeros(S.shape, F32)
        last_row[...] = jnp.zeros(last_row.shape, F32)

    rr = lax.broadcasted_iota(jnp.int32, (tc, tc), 0)
    cc = lax.broadcasted_iota(jnp.int32, (tc, tc), 1)
    strict = rr > cc
    incl2 = (lax.broadcasted_iota(jnp.int32, (tc, 2 * tc), 0)
             >= (lax.broadcasted_iota(jnp.int32, (tc, 2 * tc), 1) & (tc - 1)))
    tril = _tril_ones(tc)
    n_double = max(1, (tc - 1).bit_length())
    tok = []
    new_shift = []
    all_cols = _project_rows(x_ref, w_ref, nb, tc)
    for n in range(nb):
        cols = all_cols[n]
        new_shift.append(cols[tc - 1:tc, :])
        rowi = lax.broadcasted_iota(jnp.int32, cols.shape, 0)
        prev = jnp.where(rowi == 0, last_row[n], pltpu.roll(cols, 1, 0))
        last_row[n] = cols[tc - 1:tc, :]
        r, lw, k2, v, kkn, a, g = _rwkv_token_mix(cols, prev, mu_ref, w0_ref, w2_ref, a0_ref, a2_ref, g2_ref,
                                                  kkw_ref, ka_ref)
        am = -kkn
        bm = kkn * a
        G = _select_rows_exact(tril, lw)
        g_end = G[tc - 1:tc, :]
        einv = jnp.exp(-G)
        eend = jnp.exp(g_end - G)
        tok.append(dict(r=r, k2=k2, v=v, g=g, g_end=g_end, at=am * jnp.exp(G - lw), rt=r * jnp.exp(G),
                        bt=bm * einv, kt=k2 * einv, bbar=bm * eend, kbar=k2 * eend))
    chains = [(n, h) for n in range(nb) for h in range(N_HEADS)]
    lhs = {(n, h): jnp.concatenate([_head(tok[n]['at'], h), _head(tok[n]['rt'], h)], axis=0) for n, h in chains}
    inter = {(n, h): _mm_nt(lhs[n, h], jnp.concatenate([_head(tok[n]['bt'], h), _head(tok[n]['kt'], h)], axis=0))
             for n, h in chains}
    from_state = {(n, h): _mm_nt(lhs[n, h], S[n, h]) for n, h in chains}
    npow = {ch: jnp.where(strict, inter[ch][0:tc, 0:tc], 0.0) for ch in chains}
    u = {(n, h): from_state[n, h][0:tc]
         + _mm(jnp.where(strict, inter[n, h][0:tc, tc:2 * tc], 0.0), _head(tok[n]['v'], h)) for n, h in chains}
    for j in range(n_double):
        u = {ch: u[ch] + _mm(npow[ch], u[ch]) for ch in chains}
        if j + 1 < n_double:
            npow = {ch: _mm(npow[ch], npow[ch]) for ch in chains}
    for n, h in chains:
        t = tok[n]
        uv = jnp.concatenate([u[n, h], _head(t['v'], h)], axis=0)
        a_r = jnp.where(incl2, inter[n, h][tc:2 * tc, :], 0.0)
        o = from_state[n, h][tc:2 * tc] + _mm(a_r, uv)
        S[n, h] = (S[n, h] * jnp.exp(_head(t['g_end'], h))
                   + _mm_tn(uv, jnp.concatenate([_head(t['bbar'], h), _head(t['kbar'], h)], axis=0)))
        y_ref[n, :, h * HEAD_DIM:(h + 1) * HEAD_DIM] = _rwkv_out(o, t['r'], t['k2'], t['v'], t['g'], rk_ref,
                                                                 lnw_ref, lnb_ref, h)

    @pl.when(c == pl.num_programs(1) - 1)
    def _():
        s_ref[...] = S[...]
        for n in range(nb):
            shift_ref[n] = new_shift[n]


def _rwkv_prompt(x3, lp, tc=64, nb=4):
    b, t, d = x3.shape
    nb = min(nb, b)
    row = lambda w: pl.BlockSpec((1, w), lambda i, j: (0, 0))
    mat = lambda r, w: pl.BlockSpec((r, w), lambda i, j: (0, 0))
    return pl.pallas_call(
        functools.partial(_rwkv_kernel, tc=tc, nb=nb),
        grid=(b // nb, t // tc),
        in_specs=[pl.BlockSpec((nb, tc, d), lambda i, j: (i, j, 0)),
                  pl.BlockSpec((d, 1024), lambda i, j: (0, 2)),
                  row(1024), row(256), mat(64, 256), row(256), mat(64, 256), mat(128, 256),
                  row(256), row(256), row(256), row(256), row(256)],
        out_specs=[pl.BlockSpec((nb, tc, BRANCH_W), lambda i, j: (i, j, 0)),
                   pl.BlockSpec((nb, 1, 1024), lambda i, j: (i, 0, 0)),
                   pl.BlockSpec((nb, N_HEADS, HEAD_DIM, HEAD_DIM), lambda i, j: (i, 0, 0, 0))],
        out_shape=[jax.ShapeDtypeStruct((b, t, BRANCH_W), F32),
                   jax.ShapeDtypeStruct((b, 1, 1024), F32),
                   jax.ShapeDtypeStruct((b, N_HEADS, HEAD_DIM, HEAD_DIM), F32)],
        scratch_shapes=[pltpu.VMEM((nb, N_HEADS, HEAD_DIM, HEAD_DIM), F32), pltpu.VMEM((nb, 1, 1024), F32)],
        compiler_params=_cparams(("parallel", "arbitrary"), 32),
        name="rwkv_prompt",
    )(x3, lp['w_in_br'], lp['rw_mu'], lp['rw_w0'], lp['rw_w2'], lp['rw_a0'], lp['rw_a2'], lp['rw_g2'],
      lp['rw_kk'], lp['rw_ka'], lp['rw_rk'], lp['rw_lnx_w'], lp['rw_lnx_b'])


def _layer_norm(z, w, b):
    mu = jnp.mean(z, axis=-1, keepdims=True)
    d = z - mu
    var = jnp.mean(d * d, axis=-1, keepdims=True)
    return d * lax.rsqrt(var + LN_EPS) * w + b


def _mix_kernel(x_ref, oa_ref, ob_ref, oc_ref, od_ref, wg_ref, wb_ref, wo_ref, lw_ref, lb_ref, out_ref):
    x = x_ref[...]
    xb = x.astype(BF16)
    mixed = None
    for gi, o_ref in enumerate((oa_ref, ob_ref, oc_ref, od_ref)):
        gate = _sigmoid(jnp.dot(xb, wg_ref[:, gi * D_MODEL:(gi + 1) * D_MODEL], preferred_element_type=F32))
        up = jnp.dot(o_ref[...].astype(BF16), wb_ref[gi], preferred_element_type=F32)
        mixed = gate * up if mixed is None else mixed + gate * up
    y = jnp.dot(mixed.astype(BF16), wo_ref[...], preferred_element_type=F32)
    out_ref[...] = _layer_norm(ALPHA * x + y, lw_ref[...], lb_ref[...])


def _mix(x, outs, lp, tn):
    n = x.shape[0]
    tok = lambda w: pl.BlockSpec((tn, w), lambda i: (i, 0))
    const = lambda *s: pl.BlockSpec(s, lambda i: (0,) * len(s))
    return pl.pallas_call(
        _mix_kernel,
        grid=(n // tn,),
        in_specs=[tok(D_MODEL), tok(BRANCH_W), tok(BRANCH_W), tok(BRANCH_W), tok(BRANCH_W),
                  const(D_MODEL, 4 * D_MODEL), const(4, BRANCH_W, D_MODEL), const(D_MODEL, D_MODEL),
                  const(1, D_MODEL), const(1, D_MODEL)],
        out_specs=tok(D_MODEL),
        out_shape=jax.ShapeDtypeStruct((n, D_MODEL), F32),
        compiler_params=_cparams(("parallel",), 48),
        name="mix_ln1",
    )(x, *outs, lp['w_gate'], lp['w_branch'], lp['w_out'], lp['ln1_w'], lp['ln1_b'])


def _ple_kernel(x_ref, p_ref, wg_ref, bg_ref, wp_ref, out_ref):
    x = x_ref[...]
    gate = _sigmoid(jnp.dot(x.astype(BF16), wg_ref[...], preferred_element_type=F32) + bg_ref[...])
    emb = jnp.dot(p_ref[...].astype(BF16), wp_ref[...], preferred_element_type=F32)
    out_ref[...] = x + gate * emb


def _ple(x, p_emb, lp, tn):
    n = x.shape[0]
    tok = lambda w: pl.BlockSpec((tn, w), lambda i: (i, 0))
    const = lambda *s: pl.BlockSpec(s, lambda i: (0,) * len(s))
    return pl.pallas_call(
        _ple_kernel,
        grid=(n // tn,),
        in_specs=[tok(D_MODEL), tok(256), const(D_MODEL, D_MODEL), const(1, D_MODEL), const(256, D_MODEL)],
        out_specs=tok(D_MODEL),
        out_shape=jax.ShapeDtypeStruct((n, D_MODEL), F32),
        compiler_params=_cparams(("parallel",), 32),
        name="ple_gate",
    )(x, p_emb, lp['ple_gate_w'], lp['ple_gate_b'], lp['ple_w'])


PEER_HEADS = 8
PEER_NKEYS = 128
PEER_TOPK = 16
PEER_EB = 1024
PEER_SUB = 2
PEER_TIE_SLOTS = 8
PEER_IGROUP = 2
PEER_JROWS = 64


def _oddeven_merge_sort_pairs(n):
    pairs = []
    p = 1
    while p < n:
        k = p
        while k >= 1:
            for j in range(k % p, n - k, 2 * k):
                for i in range(min(k, n - j - k)):
                    if (i + j) // (p * 2) == (i + j + k) // (p * 2):
                        pairs.append((i + j, i + j + k))
            k //= 2
        p *= 2
    return pairs


def _bitonic_merge_pairs(n):
    pairs = []
    k = n // 2
    while k >= 1:
        pairs.extend((i, i + k) for i in range(n) if (i & k) == 0)
        k //= 2
    return pairs


_SORT16 = _oddeven_merge_sort_pairs(PEER_TOPK)
_MERGE16 = _bitonic_merge_pairs(PEER_TOPK)
_CAND_LEN = tuple(PEER_TOPK // (a + 1) for a in range(PEER_TOPK))


def _network(vals, pairs):
    vals = list(vals)
    for i, j in pairs:
        hi = jnp.maximum(vals[i], vals[j])
        lo = jnp.minimum(vals[i], vals[j])
        vals[i], vals[j] = hi, lo
    return vals


def _top16_merge(x, y):
    return _network([jnp.maximum(x[i], y[PEER_TOPK - 1 - i]) for i in range(PEER_TOPK)], _MERGE16)


def _peer_head_stats(h, s_nat, e_nat, sk, th, tie, g_count):
    rows0 = pl.ds(pl.multiple_of(2 * h * PEER_NKEYS, PEER_NKEYS), PEER_NKEYS)
    rows1 = pl.ds(pl.multiple_of((2 * h + 1) * PEER_NKEYS, PEER_NKEYS), PEER_NKEYS)
    for p, rows in enumerate((rows0, rows1)):
        for g in range(4):
            gg = g % g_count
            sk[:, p * 4 + g, :] = s_nat[gg, rows, :]
    groups = []
    for m in range(PEER_NKEYS // PEER_TOPK):
        groups.append(_network([sk[PEER_TOPK * m + i] for i in range(PEER_TOPK)], _SORT16))
    while len(groups) > 1:
        groups = [_top16_merge(groups[i], groups[i + 1]) for i in range(0, len(groups), 2)]
    top = groups[0]
    low = lax.broadcasted_iota(jnp.int32, (8, 128), 0) < 4
    ta = [jnp.where(low, t, pltpu.roll(t, 4, 0)) for t in top]
    tb = [jnp.where(low, pltpu.roll(t, 4, 0), t) for t in top]
    cand = [[ta[a] + tb[b] for b in range(_CAND_LEN[a])] for a in range(PEER_TOPK)]
    m1 = _network(cand[1] + [cand[a][0] for a in range(PEER_TOPK - 1, 7, -1)], _MERGE16)
    m2 = _network(cand[2] + cand[3] + cand[4] + cand[5] + cand[6], _SORT16)
    t1 = _top16_merge(cand[0], m1)
    t2 = _top16_merge(t1, m2)
    t2[15] = jnp.maximum(t2[15], cand[7][0])
    t2[14] = jnp.maximum(t2[14], cand[7][1])
    theta = t2[0]
    for t in t2[1:]:
        theta = jnp.minimum(theta, t)
    cmax = cand[0][0]
    z = jnp.zeros((8, 128), F32)
    n_kept = jnp.zeros((8, 128), F32)
    for row in cand:
        for cv in row:
            keep = cv >= theta
            z = z + jnp.where(keep, jnp.exp(cv - cmax), 0.0)
            n_kept = n_kept + jnp.where(keep, 1.0, 0.0)
    inv_z = 1.0 / z
    cand_tie = jnp.where(n_kept != float(PEER_TOPK), 1.0, 0.0)
    last1_used = jnp.where(cand[0][PEER_TOPK - 1] >= theta, 1.0, 0.0)
    last0_used = jnp.where(cand[PEER_TOPK - 1][0] >= theta, 1.0, 0.0)
    for g in range(g_count):
        lanes = slice(g * 128, (g + 1) * 128)
        th[h, 0:1, lanes] = theta[g:g + 1, :]
        th[h, 1:2, lanes] = ta[0][g:g + 1, :]
        th[h, 2:3, lanes] = tb[0][g:g + 1, :]
        th[h, 3:4, lanes] = inv_z[g:g + 1, :]
        th[h, 4:5, lanes] = ta[PEER_TOPK - 1][g:g + 1, :]
        th[h, 5:6, lanes] = cand_tie[g:g + 1, :]
        th[h, 6:7, lanes] = last0_used[g:g + 1, :]
        th[h, 7:8, lanes] = last1_used[g:g + 1, :]
        for b in range(PEER_TOPK):
            th[h, 8 + b:9 + b, lanes] = tb[b][g:g + 1, :]
    for g in range(g_count):
        lanes = slice(g * 128, (g + 1) * 128)
        s0 = s_nat[g, rows0, :]
        s1 = s_nat[g, rows1, :]
        e_nat[g, rows0, :] = jnp.exp(s0 - th[h, 1:2, lanes])
        e_nat[g, rows1, :] = jnp.exp(s1 - th[h, 2:3, lanes]) * th[h, 3:4, lanes]
        n0 = jnp.sum(jnp.where(s0 >= th[h, 4:5, lanes], 1.0, 0.0), axis=0, keepdims=True)
        n1 = jnp.sum(jnp.where(s1 >= th[h, 7 + PEER_TOPK:8 + PEER_TOPK, lanes], 1.0, 0.0), axis=0, keepdims=True)
        key_tie = jnp.maximum(jnp.where(n0 != float(PEER_TOPK), th[h, 6:7, lanes], 0.0),
                              jnp.where(n1 != float(PEER_TOPK), th[h, 7:8, lanes], 0.0))
        tie[0:1, lanes] = jnp.maximum(tie[0:1, lanes], jnp.maximum(key_tie, th[h, 5:6, lanes]))
        tau = jnp.full(s0.shape, jnp.inf, F32)
        for b in range(PEER_TOPK):
            sb = th[h, 8 + b:9 + b, lanes]
            tau = jnp.where((s0 + sb) >= th[h, 0:1, lanes], sb, tau)
        s_nat[g, rows0, :] = tau


def _peer_kernel(x_ref, wq_ref, keys_ref, u_ref, vt_ref, lw_ref, lb_ref, out_ref, tie_ref,
                 xtb, s_nat, e_nat, sk, th, wact, wraw, actb, yt, tie, *, tn):
    j = pl.program_id(1)
    g_count = tn // 128

    @pl.when(j == 0)
    def _():
        xtb[...] = x_ref[...].T.astype(BF16)
        for hp in range(2 * PEER_HEADS):
            qt = jnp.dot(wq_ref[hp * 128:(hp + 1) * 128, :], xtb[...], preferred_element_type=F32)
            scores = jnp.dot(keys_ref[hp], qt.astype(BF16), preferred_element_type=F32)
            for g in range(g_count):
                s_nat[g, hp * PEER_NKEYS:(hp + 1) * PEER_NKEYS, :] = scores[:, g * 128:(g + 1) * 128]

        tie[...] = jnp.zeros(tie.shape, F32)

        def head_body(h, carry):
            _peer_head_stats(h, s_nat, e_nat, sk, th, tie, g_count)
            return carry

        lax.fori_loop(0, PEER_HEADS, head_body, 0)
        tie_ref[...] = tie[0:1, :]
        yt[...] = jnp.zeros(yt.shape, F32)

    last = pl.num_programs(1) - 1
    cur = j % 2

    def step(first_stage, second_stage):
        for sb in range(PEER_SUB):
            if first_stage:
                act = jnp.dot(u_ref[sb * PEER_EB:(sb + 1) * PEER_EB, :], xtb[...],
                              preferred_element_type=F32)
                for g in range(g_count):
                    actb[sb, g] = act[:, g * 128:(g + 1) * 128]
            if second_stage:
                yt[...] += jnp.dot(vt_ref[sb], wact[1 - cur, sb], preferred_element_type=F32)
            if not first_stage:
                continue
            i0 = pl.multiple_of((j * PEER_SUB + sb) * (PEER_EB // PEER_NKEYS), 8)
            jr = PEER_JROWS
            for g in range(g_count):
                for jq in range(PEER_NKEYS // jr):
                    for ig in range(PEER_EB // PEER_NKEYS // PEER_IGROUP):
                        acc = [jnp.zeros((jr, 128), F32) for _ in range(PEER_IGROUP)]
                        for h in range(PEER_HEADS):
                            base1 = (2 * h + 1) * PEER_NKEYS + jq * jr
                            s1 = s_nat[g, base1:base1 + jr, :]
                            e1 = e_nat[g, base1:base1 + jr, :]
                            tau0 = s_nat[g, pl.ds(2 * h * PEER_NKEYS + i0, 8), :]
                            e0 = e_nat[g, pl.ds(2 * h * PEER_NKEYS + i0, 8), :]
                            for k in range(PEER_IGROUP):
                                ii = PEER_IGROUP * ig + k
                                sel = s1 >= tau0[ii:ii + 1, :]
                                acc[k] = acc[k] + jnp.where(sel, e0[ii:ii + 1, :] * e1, 0.0)
                        for k in range(PEER_IGROUP):
                            r0 = (PEER_IGROUP * ig + k) * PEER_NKEYS + jq * jr
                            wraw[g, r0:r0 + jr, :] = acc[k]
            for g in range(g_count):
                wact[cur, sb, :, g * 128:(g + 1) * 128] = (wraw[g] * _gelu(actb[sb, g])).astype(BF16)

    @pl.when(j == 0)
    def _():
        step(True, False)

    @pl.when((j > 0) & (j < last))
    def _():
        step(True, True)

    @pl.when(j == last)
    def _():
        step(False, True)
        z = ALPHA * x_ref[...] + yt[...].T
        out_ref[...] = _layer_norm(z, lw_ref[...], lb_ref[...])


def _peer(x, lp, tn):
    n = x.shape[0]
    n_blk = lp['peer_u'].shape[0] // (PEER_EB * PEER_SUB)
    once = lambda *s: pl.BlockSpec(s, lambda i, j: (0,) * len(s), pipeline_mode=pl.Buffered(1))
    const = lambda *s: pl.BlockSpec(s, lambda i, j: (0,) * len(s))
    out, tie_flags = pl.pallas_call(
        functools.partial(_peer_kernel, tn=tn),
        grid=(n // tn, n_blk + 1),
        in_specs=[pl.BlockSpec((tn, D_MODEL), lambda i, j: (i, 0)),
                  once(2 * PEER_HEADS * 128, D_MODEL), once(2 * PEER_HEADS, PEER_NKEYS, 128),
                  pl.BlockSpec((PEER_SUB * PEER_EB, D_MODEL), lambda i, j: (jnp.minimum(j, n_blk - 1), 0)),
                  pl.BlockSpec((PEER_SUB, D_MODEL, PEER_EB), lambda i, j: (jnp.maximum(j - 1, 0), 0, 0)),
                  const(1, D_MODEL), const(1, D_MODEL)],
        out_specs=[pl.BlockSpec((tn, D_MODEL), lambda i, j: (i, 0)),
                   pl.BlockSpec((None, 1, tn), lambda i, j: (i, 0, 0))],
        out_shape=[jax.ShapeDtypeStruct((n, D_MODEL), F32), jax.ShapeDtypeStruct((n // tn, 1, tn), F32)],
        scratch_shapes=[pltpu.VMEM((D_MODEL, tn), BF16),
                        pltpu.VMEM((tn // 128, 2 * PEER_HEADS * PEER_NKEYS, 128), F32),
                        pltpu.VMEM((tn // 128, 2 * PEER_HEADS * PEER_NKEYS, 128), F32),
                        pltpu.VMEM((PEER_NKEYS, 8, 128), F32),
                        pltpu.VMEM((PEER_HEADS, 8 + PEER_TOPK, tn), F32),
                        pltpu.VMEM((2, PEER_SUB, PEER_EB, tn), BF16),
                        pltpu.VMEM((tn // 128, PEER_EB, 128), F32),
                        pltpu.VMEM((PEER_SUB, tn // 128, PEER_EB, 128), F32),
                        pltpu.VMEM((D_MODEL, tn), F32),
                        pltpu.VMEM((8, tn), F32)],
        compiler_params=_cparams(("parallel", "arbitrary"), 56),
        name="peer_ln2",
    )(x, lp['peer_wq_t'], lp['peer_keys'], lp['peer_u'], lp['peer_v_t'], lp['ln2_w'], lp['ln2_b'])
    flags = tie_flags.reshape(n) > 0.0
    count = jnp.sum(flags.astype(jnp.int32))

    def redo_flagged():
        idx = jnp.nonzero(flags, size=PEER_TIE_SLOTS, fill_value=n)[0]
        fixed = _peer_tie_fallback(x[jnp.minimum(idx, n - 1)], lp)
        return out.at[idx].set(fixed, mode='drop')

    return lax.cond(count > PEER_TIE_SLOTS, lambda: _peer_tie_fallback(x, lp),
                    lambda: lax.cond(count > 0, redo_flagged, lambda: out))


def _topk_lowest_index(v, k):
    n = v.shape[-1]
    pos = jnp.arange(n)
    ahead = (v[..., None, :] > v[..., :, None]) | ((v[..., None, :] == v[..., :, None]) & (pos[None, :] < pos[:, None]))
    rank = jnp.sum(ahead, axis=-1)
    onehot = rank[..., None, :] == jnp.arange(k)[:, None]
    vals = jnp.sum(jnp.where(onehot, v[..., None, :], 0.0), axis=-1)
    idx = jnp.sum(jnp.where(onehot, pos, 0), axis=-1)
    return vals, idx


def _peer_tie_fallback(x, lp):
    wq, keys, u_all, v_all, layer = lp['peer_raw']
    n_exp = u_all.shape[1]
    u_tab = u_all.reshape(-1, D_MODEL)
    v_tab = v_all.reshape(-1, D_MODEL)
    n = x.shape[0]
    blk = min(256, n)
    top_k = _topk_lowest_index if n <= PEER_TIE_SLOTS else lax.top_k

    def block(xb):
        q = (xb @ wq).reshape(blk, PEER_HEADS, 2, PEER_NKEYS)
        s = jnp.einsum('nhpc,hpkc->nhpk', q, keys).astype(F32)
        s_top, i_top = top_k(s, PEER_TOPK)
        cand = (s_top[:, :, 0, :, None] + s_top[:, :, 1, None, :]).reshape(blk, PEER_HEADS, -1)
        cand_idx = (i_top[:, :, 0, :, None] * PEER_NKEYS + i_top[:, :, 1, None, :]).reshape(blk, PEER_HEADS, -1)
        best, pick = top_k(cand, PEER_TOPK)
        eidx = jnp.take_along_axis(cand_idx, pick, axis=-1) + layer * n_exp
        gw = jax.nn.softmax(best, axis=-1)
        act = jax.nn.gelu(jnp.einsum('nhkd,nd->nhk', u_tab[eidx], xb).astype(F32))
        return jnp.einsum('nhk,nhkd->nd', gw * act, v_tab[eidx])

    y = lax.map(block, x.reshape(-1, blk, D_MODEL)).reshape(n, D_MODEL)
    return _layer_norm(ALPHA * x + y, lp['ln2_w'], lp['ln2_b'])


DEC_N = 128
HEAD_STATE = HEAD_DIM * HEAD_DIM


def _col(v):
    return jnp.broadcast_to(v.reshape(-1, 1), (v.size, DEC_N))


def _load_state_t(s_ref, st):
    st[...] = s_ref[...].T.reshape(HEAD_DIM, HEAD_DIM, DEC_N)


def _store_state_t(so_ref, st):
    so_ref[...] = st[...].reshape(HEAD_STATE, DEC_N).T


def _ret_dec_kernel(c_ref, cos_ref, sin_ref, gw_ref, gb_ref, s_ref, o_ref, so_ref, ct, qk, st):
    h = pl.program_id(0)
    ct[...] = c_ref[...].T
    r0 = pl.multiple_of(h * HEAD_DIM, HEAD_DIM)

    def rot(x):
        partner = jnp.concatenate([x[HEAD_DIM // 2:], x[:HEAD_DIM // 2]], axis=0)
        return x * cos_ref[...] + partner * sin_ref[...]

    qk[0] = rot(ct[pl.ds(r0, HEAD_DIM), :])
    qk[1] = rot(ct[pl.ds(256 + r0, HEAD_DIM), :]) * (HEAD_DIM ** -0.5)
    v = ct[pl.ds(512 + r0, HEAD_DIM), :]
    g = ct[pl.ds(768 + r0, HEAD_DIM), :]
    gamma = jnp.exp(jnp.zeros((1, 1), F32) + jnp.where(
        h == 0, RET_LOG_GAMMA[0], jnp.where(h == 1, RET_LOG_GAMMA[1],
                                            jnp.where(h == 2, RET_LOG_GAMMA[2], RET_LOG_GAMMA[3]))))
    _load_state_t(s_ref, st)

    def body(k, o):
        s_new = gamma * st[k] + qk[1, pl.ds(k, 1), :] * v
        st[k] = s_new
        return o + qk[0, pl.ds(k, 1), :] * s_new

    o = lax.fori_loop(0, HEAD_DIM, body, jnp.zeros((HEAD_DIM, DEC_N), F32))
    mu = jnp.mean(o, axis=0, keepdims=True)
    d = o - mu
    var = jnp.mean(d * d, axis=0, keepdims=True)
    o_ref[...] = (d * lax.rsqrt(var + LN_EPS) * gw_ref[...] + gb_ref[...]) * _silu(g)
    _store_state_t(so_ref, st)


def _hgrn_dec_kernel(c_ref, lb_ref, nw_ref, s_ref, o_ref, so_ref, ct, qk, st):
    h = pl.program_id(0)
    ct[...] = c_ref[...].T
    r0 = pl.multiple_of(h * HEAD_DIM, HEAD_DIM)
    lb = lb_ref[...]
    qk[0] = _silu(ct[pl.ds(r0, HEAD_DIM), :])
    forget = lb + (1.0 - lb) * _sigmoid(ct[pl.ds(256 + r0, HEAD_DIM), :])
    qk[1] = forget
    qk[2] = 1.0 - forget
    v = ct[pl.ds(512 + r0, HEAD_DIM), :]
    g = ct[pl.ds(768 + r0, HEAD_DIM), :]
    _load_state_t(s_ref, st)

    def body(k, o):
        s_new = qk[1, pl.ds(k, 1), :] * st[k] + qk[2, pl.ds(k, 1), :] * v
        st[k] = s_new
        return o + qk[0, pl.ds(k, 1), :] * s_new

    o = lax.fori_loop(0, HEAD_DIM, body, jnp.zeros((HEAD_DIM, DEC_N), F32))
    ms = jnp.mean(o * o, axis=0, keepdims=True)
    o_ref[...] = o * lax.rsqrt(ms + LN_EPS) * nw_ref[...] * _silu(g)
    _store_state_t(so_ref, st)


def _rwkv_dec_kernel(c_ref, sh_ref, mu_ref, w0_ref, w2_ref, a0_ref, a2_ref, g2_ref, kkw_ref, ka_ref,
                     rk_ref, lnw_ref, lnb_ref, s_ref, o_ref, so_ref, vt, st, osc):
    h = pl.program_id(0)
    r, lw, k2, v, kkn, a, g = _rwkv_token_mix(c_ref[...], sh_ref[...], mu_ref, w0_ref, w2_ref, a0_ref, a2_ref,
                                              g2_ref, kkw_ref, ka_ref)
    for idx, arr in enumerate((r, jnp.exp(lw), k2, v, kkn, a, g)):
        vt[idx] = arr.T
    r0 = pl.multiple_of(h * HEAD_DIM, HEAD_DIM)
    rows = pl.ds(r0, HEAD_DIM)
    rh, wh, kh, kkh, ah = vt[0, rows, :], vt[1, rows, :], vt[2, rows, :], vt[4, rows, :], vt[5, rows, :]
    vh, gh = vt[3, rows, :], vt[6, rows, :]
    kka = kkh * ah
    _load_state_t(s_ref, st)

    def body(vi, carry):
        s_old = st[vi]
        sa = jnp.sum(s_old * (-kkh), axis=0, keepdims=True)
        s_new = s_old * wh + sa * kka + vt[3, pl.ds(r0 + vi, 1), :] * kh
        st[vi] = s_new
        osc[pl.ds(vi, 1), :] = jnp.sum(s_new * rh, axis=0, keepdims=True)
        return carry

    lax.fori_loop(0, HEAD_DIM, body, 0)
    o = osc[...]
    mu = jnp.mean(o, axis=0, keepdims=True)
    d = o - mu
    var = jnp.mean(d * d, axis=0, keepdims=True)
    on = d * lax.rsqrt(var + RW_GN_EPS) * lnw_ref[...] + lnb_ref[...]
    bonus = jnp.sum(rh * kh * rk_ref[...], axis=0, keepdims=True) * vh
    o_ref[...] = (on + bonus) * gh
    _store_state_t(so_ref, st)


def _dec_specs():
    head_tab = pl.BlockSpec((HEAD_DIM, DEC_N), lambda h: (h, 0))
    state = pl.BlockSpec((DEC_N, HEAD_STATE), lambda h: (0, h))
    out = pl.BlockSpec((None, HEAD_DIM, DEC_N), lambda h: (h, 0, 0))
    return head_tab, state, out


def _dec_out_shapes():
    return [jax.ShapeDtypeStruct((N_HEADS, HEAD_DIM, DEC_N), F32),
            jax.ShapeDtypeStruct((DEC_N, N_HEADS * HEAD_STATE), F32)]


def _dec_finish(o_t, s_new):
    return o_t.reshape(BRANCH_W, DEC_N).T, s_new.reshape(DEC_N, N_HEADS, HEAD_DIM, HEAD_DIM)


def _ret_decode(cols, state, cos_c, sin_c, lp):
    head_tab, st_spec, out_spec = _dec_specs()
    same = pl.BlockSpec((HEAD_DIM, DEC_N), lambda h: (0, 0))
    o_t, s_new = pl.pallas_call(
        _ret_dec_kernel,
        grid=(N_HEADS,),
        in_specs=[pl.BlockSpec((DEC_N, 1024), lambda h: (0, 0)), same, same, head_tab, head_tab, st_spec],
        out_specs=[out_spec, st_spec],
        out_shape=_dec_out_shapes(),
        scratch_shapes=[pltpu.VMEM((1024, DEC_N), F32), pltpu.VMEM((2, HEAD_DIM, DEC_N), F32),
                        pltpu.VMEM((HEAD_DIM, HEAD_DIM, DEC_N), F32)],
        compiler_params=_cparams(("arbitrary",), 40),
        name="ret_decode",
    )(cols, cos_c, sin_c, _col(lp['ret_gn_w']), _col(lp['ret_gn_b']), state.reshape(DEC_N, -1))
    return _dec_finish(o_t, s_new)


def _hgrn_decode(cols, state, lp):
    head_tab, st_spec, out_spec = _dec_specs()
    o_t, s_new = pl.pallas_call(
        _hgrn_dec_kernel,
        grid=(N_HEADS,),
        in_specs=[pl.BlockSpec((DEC_N, 1024), lambda h: (0, 1)), head_tab, head_tab, st_spec],
        out_specs=[out_spec, st_spec],
        out_shape=_dec_out_shapes(),
        scratch_shapes=[pltpu.VMEM((1024, DEC_N), F32), pltpu.VMEM((3, HEAD_DIM, DEC_N), F32),
                        pltpu.VMEM((HEAD_DIM, HEAD_DIM, DEC_N), F32)],
        compiler_params=_cparams(("arbitrary",), 40),
        name="hgrn_decode",
    )(cols, _col(lp['hg_lb']), _col(lp['hg_norm_w']), state.reshape(DEC_N, -1))
    return _dec_finish(o_t, s_new)


def _rwkv_decode(cols, shift, state, lp):
    head_tab, st_spec, out_spec = _dec_specs()
    row = lambda w: pl.BlockSpec((1, w), lambda h: (0, 0))
    mat = lambda r, w: pl.BlockSpec((r, w), lambda h: (0, 0))
    o_t, s_new = pl.pallas_call(
        _rwkv_dec_kernel,
        grid=(N_HEADS,),
        in_specs=[pl.BlockSpec((DEC_N, 1024), lambda h: (0, 2)), mat(DEC_N, 1024),
                  row(1024), row(256), mat(64, 256), row(256), mat(64, 256), mat(128, 256),
                  row(256), row(256), head_tab, head_tab, head_tab, st_spec],
        out_specs=[out_spec, st_spec],
        out_shape=_dec_out_shapes(),
        scratch_shapes=[pltpu.VMEM((7, BRANCH_W, DEC_N), F32), pltpu.VMEM((HEAD_DIM, HEAD_DIM, DEC_N), F32),
                        pltpu.VMEM((HEAD_DIM, DEC_N), F32)],
        compiler_params=_cparams(("arbitrary",), 40),
        name="rwkv_decode",
    )(cols, shift, lp['rw_mu'], lp['rw_w0'], lp['rw_w2'], lp['rw_a0'], lp['rw_a2'], lp['rw_g2'],
      lp['rw_kk'], lp['rw_ka'], _col(lp['rw_rk']), _col(lp['rw_lnx_w']), _col(lp['rw_lnx_b']),
      state.reshape(DEC_N, -1))
    return _dec_finish(o_t, s_new)


def _lru_dec_kernel(c_ref, conv_ref, h0_ref, cw_ref, cb_ref, wa_ref, ba_ref, wx_ref, bx_ref, lam_ref,
                    y_ref, h_ref, nconv_ref):
    xb = c_ref[:, 0:BRANCH_W]
    gate = c_ref[:, BRANCH_W:2 * BRANCH_W]
    c0, c1, c2 = conv_ref[0], conv_ref[1], conv_ref[2]
    xc = (c0 * cw_ref[0:1, :] + c1 * cw_ref[1:2, :] + c2 * cw_ref[2:3, :] + xb * cw_ref[3:4, :]) + cb_ref[...]
    a, u = _lru_gates(xc, wa_ref, ba_ref, wx_ref, bx_ref, lam_ref)
    hn = a * h0_ref[...] + u
    h_ref[...] = hn
    y_ref[...] = hn * _gelu(gate)
    nconv_ref[0] = c1
    nconv_ref[1] = c2
    nconv_ref[2] = xb


def _lru_decode(cols, conv, h0, lp):
    full = lambda *s: pl.BlockSpec(s, lambda i: (0,) * len(s))
    return pl.pallas_call(
        _lru_dec_kernel,
        grid=(1,),
        in_specs=[pl.BlockSpec((DEC_N, 512), lambda i: (0, 6)), full(3, DEC_N, BRANCH_W), full(DEC_N, BRANCH_W),
                  full(CONV_W, BRANCH_W), full(1, BRANCH_W), full(BRANCH_W, BRANCH_W), full(1, BRANCH_W),
                  full(BRANCH_W, BRANCH_W), full(1, BRANCH_W), full(1, BRANCH_W)],
        out_specs=[full(DEC_N, BRANCH_W), full(DEC_N, BRANCH_W), full(3, DEC_N, BRANCH_W)],
        out_shape=[jax.ShapeDtypeStruct((DEC_N, BRANCH_W), F32), jax.ShapeDtypeStruct((DEC_N, BRANCH_W), F32),
                   jax.ShapeDtypeStruct((3, DEC_N, BRANCH_W), F32)],
        compiler_params=_cparams(("arbitrary",), 32),
        name="lru_decode",
    )(cols, conv, h0, lp['lru_conv_w'], lp['lru_conv_b'], lp['lru_wa_bd'], lp['lru_ba'], lp['lru_wx_bd'],
      lp['lru_bx'], lp['lru_lambda'])


def _prompt_layer(x, p_emb, lp, cos_t, sin_t):
    b, t, d = x.shape
    n = b * t
    xf = x.reshape(n, d)
    o_a, s_ret = _ret_prompt(x, cos_t, sin_t, lp)
    o_b, s_hg = _hgrn_prompt(x, lp['hg_lb'], lp)
    o_c, s_shift, s_rw = _rwkv_prompt(x, lp)
    o_d, s_lru, s_conv = _lru_prompt(x, lp)
    outs = [o.reshape(n, BRANCH_W) for o in (o_a, o_b, o_c, o_d)]
    x1 = _mix(xf, outs, lp, 512)
    x2 = _peer(x1, lp, 512)
    x3 = _ple(x2, p_emb.reshape(n, -1), lp, 512)
    return x3.reshape(b, t, d), (s_ret, s_hg, s_rw, s_shift[:, 0], s_lru[:, 0], s_conv)


def _sample_layer(x, p_emb, state, lp, cos_c, sin_c):
    s_ret, s_hg, s_rw, s_shift, s_lru, s_conv = state
    xf = x.reshape(DEC_N, D_MODEL)
    cols = _proj(xf, lp['w_in_br'], DEC_N)
    o_a, s_ret = _ret_decode(cols, s_ret, cos_c, sin_c, lp)
    o_b, s_hg = _hgrn_decode(cols, s_hg, lp)
    o_c, s_rw = _rwkv_decode(cols, s_shift, s_rw, lp)
    o_d, s_lru, s_conv_t = _lru_decode(cols, jnp.swapaxes(s_conv, 0, 1), s_lru, lp)
    x1 = _mix(xf, [o_a, o_b, o_c, o_d], lp, DEC_N)
    x2 = _peer(x1, lp, DEC_N)
    x3 = _ple(x2, p_emb.reshape(DEC_N, -1), lp, DEC_N)
    new_shift = cols[:, 2048:3072]
    return x3.reshape(x.shape), (s_ret, s_hg, s_rw, new_shift, s_lru, jnp.swapaxes(s_conv_t, 0, 1))


def kernel(x_prompt, x_sample, state_ret, state_hgrn, state_rwkv, state_shift, state_lru, state_conv, p_prompt, p_sample, w_in, ret_gn_w, ret_gn_b, hg_lb, hg_norm_w, rw_mu, rw_w0, rw_w2, rw_a0, rw_a2, rw_g2, rw_kk, rw_ka, rw_rk, rw_lnx_w, rw_lnx_b, lru_conv_w, lru_conv_b, lru_wa, lru_ba, lru_wx, lru_bx, lru_lambda, w_branch, w_out, ln1_w, ln1_b, peer_wq, peer_keys, peer_u, peer_v, ln2_w, ln2_b, ple_w, ple_gate_w, ple_gate_b):
    params = dict(w_in=w_in, ret_gn_w=ret_gn_w, ret_gn_b=ret_gn_b, hg_norm_w=hg_norm_w, rw_mu=rw_mu, rw_w0=rw_w0,
                  rw_w2=rw_w2, rw_a0=rw_a0, rw_a2=rw_a2, rw_g2=rw_g2, rw_kk=rw_kk, rw_ka=rw_ka, rw_rk=rw_rk,
                  rw_lnx_w=rw_lnx_w, rw_lnx_b=rw_lnx_b, lru_conv_w=lru_conv_w, lru_conv_b=lru_conv_b,
                  lru_wa=lru_wa, lru_ba=lru_ba, lru_wx=lru_wx, lru_bx=lru_bx, lru_lambda=lru_lambda,
                  w_branch=w_branch, w_out=w_out, ln1_w=ln1_w, ln1_b=ln1_b, peer_wq=peer_wq, peer_keys=peer_keys,
                  peer_u=peer_u, peer_v=peer_v, ln2_w=ln2_w, ln2_b=ln2_b, ple_w=ple_w, ple_gate_w=ple_gate_w,
                  ple_gate_b=ple_gate_b)
    lb_cum = jnp.cumsum(jax.nn.softmax(hg_lb.astype(F32), axis=0), axis=0)
    lb_all = lb_cum - lb_cum[0:1]
    t_prompt = x_prompt.shape[1]
    past_len = 16384
    cos_t, sin_t = _rope_tables(jnp.arange(t_prompt))
    cos_s, sin_s = _rope_tables(past_len + jnp.arange(1))
    cos_c = _col(cos_s[0, :HEAD_DIM])
    sin_c = _col(sin_s[0, :HEAD_DIM])
    h_p, h_s = x_prompt, x_sample
    new_p, new_s = [], []
    for l in range(DEPTH):
        lp = _layer_params(params, l, lb_all)
        h_p, st_p = _prompt_layer(h_p, p_prompt[l], lp, cos_t, sin_t)
        st_in = (state_ret[l], state_hgrn[l], state_rwkv[l], state_shift[l], state_lru[l], state_conv[l])
        h_s, st_s = _sample_layer(h_s, p_sample[l], st_in, lp, cos_c, sin_c)
        new_p.append(st_p)
        new_s.append(st_s)
    outs_p = [jnp.stack(z) for z in zip(*new_p)]
    outs_s = [jnp.stack(z) for z in zip(*new_s)]
    return (h_p, h_s, *outs_p, *outs_s)
```

```python
import functools
import math

import jax
import jax.numpy as jnp
from jax import lax
from jax.experimental import pallas as pl
from jax.experimental.pallas import tpu as pltpu

F32 = jnp.float32
BF16 = jnp.bfloat16
HIGHEST = lax.Precision.HIGHEST

D_MODEL = 1024
BRANCH_W = 256
N_HEADS = 4
HEAD_DIM = 64
RET_LOG_GAMMA = tuple(math.log1p(-(2.0 ** (-5.0 - h))) for h in range(N_HEADS))
ROPE_BASE = 10000.0
RW_GN_EPS = 64e-5
LN_EPS = 1e-5
LRU_C = 8.0
CONV_W = 4
DEPTH = 2
ALPHA = (2 * DEPTH) ** 0.25
BRANCH_COLS = 3584
MIB = 1024 * 1024
VMEM_SMALL_MIB = 32
VMEM_WEIGHTS_MIB = 48
VMEM_DECODE_MIB = 40
VMEM_PEER_MIB = 56


def _cparams(semantics, vmem_mib):
    return pltpu.CompilerParams(dimension_semantics=semantics, vmem_limit_bytes=vmem_mib * MIB)


def _mm(a, b):
    return jnp.dot(a.astype(BF16), b.astype(BF16), preferred_element_type=F32)


def _mm_nt(a, b):
    return lax.dot_general(a.astype(BF16), b.astype(BF16), (((1,), (1,)), ((), ())), preferred_element_type=F32)


def _mm_tn(a, b):
    return lax.dot_general(a.astype(BF16), b.astype(BF16), (((0,), (0,)), ((), ())), preferred_element_type=F32)


def _transpose_exact(a):
    n = a.shape[0]
    eye = (lax.broadcasted_iota(jnp.int32, (n, n), 0) == lax.broadcasted_iota(jnp.int32, (n, n), 1)).astype(F32)
    return lax.dot_general(a, eye, (((0,), (0,)), ((), ())), preferred_element_type=F32, precision=HIGHEST)


def _select_rows_exact(sel, x):
    x1 = x.astype(BF16)
    r1 = x - x1.astype(F32)
    x2 = r1.astype(BF16)
    x3 = (r1 - x2.astype(F32)).astype(BF16)
    dot = lambda p: jnp.dot(sel, p, preferred_element_type=F32)
    return (dot(x1) + dot(x2)) + dot(x3)


def _tril_ones(n):
    r = lax.broadcasted_iota(jnp.int32, (n, n), 0)
    c = lax.broadcasted_iota(jnp.int32, (n, n), 1)
    return (r >= c).astype(BF16)


def _sigmoid(x):
    return 1.0 / (1.0 + jnp.exp(-x))


def _silu(x):
    return x * _sigmoid(x)


def _gelu(x):
    c = 0.7978845608028654
    half = 0.5 * x
    return half + half * jnp.tanh(x * (c + (c * 0.044715) * (x * x)))


def _softplus(x):
    return jnp.maximum(x, 0.0) + jnp.log1p(jnp.exp(-jnp.abs(x)))


def _head(x, h):
    return x[:, h * HEAD_DIM:(h + 1) * HEAD_DIM]


def _project_rows(x_ref, w_ref, nb, tc):
    x = x_ref[...].reshape(nb * tc, x_ref.shape[-1]).astype(BF16)
    cols = jnp.dot(x, w_ref[...], preferred_element_type=F32)
    return [cols[n * tc:(n + 1) * tc] for n in range(nb)]


def _block_diag(w):
    n, c, d = w.shape
    eye = jnp.eye(n, dtype=w.dtype)
    return (eye[:, None, :, None] * w[:, :, None, :]).reshape(n * c, n * d)


def _rope_tables(pos):
    half = HEAD_DIM // 2
    inv_freq = ROPE_BASE ** (-jnp.arange(half, dtype=F32) / half)
    ang = pos.astype(F32)[:, None] * inv_freq[None, :]
    cos = jnp.cos(ang)
    sin = jnp.sin(ang)
    cos_t = jnp.tile(jnp.concatenate([cos, cos], axis=-1), (1, N_HEADS))
    sin_t = jnp.tile(jnp.concatenate([-sin, sin], axis=-1), (1, N_HEADS))
    return cos_t, sin_t


def _layer_params(p, l, lb_all):
    r2 = lambda a: a.reshape(1, -1)
    w_in = p['w_in'][l]
    lp = {
        'w_in_br': w_in[:, :BRANCH_COLS].astype(BF16),
        'w_gate': w_in[:, BRANCH_COLS:].astype(BF16),
        'ret_gn_w': r2(p['ret_gn_w'][l]), 'ret_gn_b': r2(p['ret_gn_b'][l]),
        'hg_lb': r2(lb_all[l]), 'hg_norm_w': r2(p['hg_norm_w'][l]),
        'rw_mu': r2(p['rw_mu'][l]), 'rw_w0': r2(p['rw_w0'][l]), 'rw_w2': p['rw_w2'][l].astype(BF16),
        'rw_a0': r2(p['rw_a0'][l]), 'rw_a2': p['rw_a2'][l].astype(BF16), 'rw_g2': p['rw_g2'][l].astype(BF16),
        'rw_kk': r2(p['rw_kk'][l]), 'rw_ka': r2(p['rw_ka'][l]), 'rw_rk': r2(p['rw_rk'][l]),
        'rw_lnx_w': r2(p['rw_lnx_w'][l]), 'rw_lnx_b': r2(p['rw_lnx_b'][l]),
        'lru_conv_w': p['lru_conv_w'][l], 'lru_conv_b': r2(p['lru_conv_b'][l]),
        'lru_wa_bd': _block_diag(p['lru_wa'][l]).astype(BF16), 'lru_ba': r2(p['lru_ba'][l]),
        'lru_wx_bd': _block_diag(p['lru_wx'][l]).astype(BF16), 'lru_bx': r2(p['lru_bx'][l]),
        'lru_lambda': r2(p['lru_lambda'][l]),
        'w_branch': p['w_branch'][l].astype(BF16), 'w_out': p['w_out'][l].astype(BF16),
        'ln1_w': r2(p['ln1_w'][l]), 'ln1_b': r2(p['ln1_b'][l]),
        'peer_wq_t': p['peer_wq'][l].T.astype(BF16),
        'peer_keys': p['peer_keys'][l].reshape(16, 128, 128).astype(BF16),
        'peer_u': p['peer_u'][l].astype(BF16),
        'peer_raw': (p['peer_wq'][l], p['peer_keys'][l], p['peer_u'], p['peer_v'], l),
        'peer_v_t': jnp.swapaxes(p['peer_v'][l].astype(BF16).reshape(-1, PEER_EB, D_MODEL), 1, 2),
        'ln2_w': r2(p['ln2_w'][l]), 'ln2_b': r2(p['ln2_b'][l]),
        'ple_w': p['ple_w'][l].astype(BF16), 'ple_gate_w': p['ple_gate_w'][l].astype(BF16),
        'ple_gate_b': r2(p['ple_gate_b'][l]),
    }
    return lp


def _proj_kernel(x_ref, w_ref, o_ref):
    o_ref[...] = jnp.dot(x_ref[...].astype(BF16), w_ref[...], preferred_element_type=F32)


def _proj(x, w_bf16, tn):
    n, k = x.shape
    m = w_bf16.shape[1]
    return pl.pallas_call(
        _proj_kernel,
        grid=(n // tn,),
        in_specs=[pl.BlockSpec((tn, k), lambda i: (i, 0)),
                  pl.BlockSpec((k, m), lambda i: (0, 0))],
        out_specs=pl.BlockSpec((tn, m), lambda i: (i, 0)),
        out_shape=jax.ShapeDtypeStruct((n, m), F32),
        compiler_params=_cparams(("parallel",), VMEM_WEIGHTS_MIB),
        name="in_proj",
    )(x, w_bf16)


def _lru_gates(xc, wa_ref, ba_ref, wx_ref, bx_ref, lam_ref):
    r = _sigmoid(_mm(xc, wa_ref[...]) + ba_ref[...])
    i = _sigmoid(_mm(xc, wx_ref[...]) + bx_ref[...])
    log_a = -LRU_C * r * _softplus(-lam_ref[...])
    a = jnp.exp(log_a)
    u = jnp.sqrt(1.0 - jnp.exp(2.0 * log_a)) * (i * xc)
    return a, u


def _lru_kernel(x_ref, w_ref, cw_ref, cb_ref, wa_ref, ba_ref, wx_ref, bx_ref, lam_ref,
                y_ref, h_ref, conv_ref, xbuf, hcar, a_s, u_s, hs, *, tc):
    c = pl.program_id(1)

    @pl.when(c == 0)
    def _():
        xbuf[0:8, :] = jnp.zeros((8, BRANCH_W), F32)
        hcar[...] = jnp.zeros((1, BRANCH_W), F32)

    cols = jnp.dot(x_ref[...].astype(BF16), w_ref[...], preferred_element_type=F32)
    xb = cols[:, 0:BRANCH_W]
    gate = cols[:, BRANCH_W:2 * BRANCH_W]
    xbuf[8:8 + tc, :] = xb
    xc = (xbuf[pl.ds(5, tc), :] * cw_ref[0:1, :] + xbuf[pl.ds(6, tc), :] * cw_ref[1:2, :]
          + xbuf[pl.ds(7, tc), :] * cw_ref[2:3, :] + xb * cw_ref[3:4, :]) + cb_ref[...]
    a, u = _lru_gates(xc, wa_ref, ba_ref, wx_ref, bx_ref, lam_ref)
    a_s[...] = a
    u_s[...] = u

    def body(t, h):
        h = a_s[pl.ds(t, 1), :] * h + u_s[pl.ds(t, 1), :]
        hs[pl.ds(t, 1), :] = h
        return h

    h = lax.fori_loop(0, tc, body, hcar[...], unroll=8)
    hcar[...] = h
    y_ref[...] = hs[...] * _gelu(gate)
    xbuf[0:8, :] = xbuf[tc:tc + 8, :]

    @pl.when(c == pl.num_programs(1) - 1)
    def _():
        h_ref[...] = h
        conv_ref[...] = xbuf[5:8, :]


def _lru_prompt(x3, lp, tc=256):
    b, t, d = x3.shape
    row = lambda: pl.BlockSpec((1, BRANCH_W), lambda i, j: (0, 0))
    full = lambda r: pl.BlockSpec((r, BRANCH_W), lambda i, j: (0, 0))
    return pl.pallas_call(
        functools.partial(_lru_kernel, tc=tc),
        grid=(b, t // tc),
        in_specs=[pl.BlockSpec((None, tc, d), lambda i, j: (i, j, 0)),
                  pl.BlockSpec((d, 2 * BRANCH_W), lambda i, j: (0, 6)),
                  full(CONV_W), row(), full(BRANCH_W), row(), full(BRANCH_W), row(), row()],
        out_specs=[pl.BlockSpec((None, tc, BRANCH_W), lambda i, j: (i, j, 0)),
                   pl.BlockSpec((None, 1, BRANCH_W), lambda i, j: (i, 0, 0)),
                   pl.BlockSpec((None, CONV_W - 1, BRANCH_W), lambda i, j: (i, 0, 0))],
        out_shape=[jax.ShapeDtypeStruct((b, t, BRANCH_W), F32),
                   jax.ShapeDtypeStruct((b, 1, BRANCH_W), F32),
                   jax.ShapeDtypeStruct((b, CONV_W - 1, BRANCH_W), F32)],
        scratch_shapes=[pltpu.VMEM((tc + 8, BRANCH_W), F32), pltpu.VMEM((1, BRANCH_W), F32),
                        pltpu.VMEM((tc, BRANCH_W), F32), pltpu.VMEM((tc, BRANCH_W), F32),
                        pltpu.VMEM((tc, BRANCH_W), F32)],
        compiler_params=_cparams(("parallel", "arbitrary"), VMEM_SMALL_MIB),
        name="lru_prompt",
    )(x3, lp['w_in_br'], lp['lru_conv_w'], lp['lru_conv_b'], lp['lru_wa_bd'], lp['lru_ba'], lp['lru_wx_bd'],
      lp['lru_bx'], lp['lru_lambda'])


def _rotary(x, cos, sin_signed, first_half):
    partner = jnp.where(first_half, pltpu.roll(x, BRANCH_W - 32, 1), pltpu.roll(x, 32, 1))
    return x * cos + partner * sin_signed


def _ret_kernel(x_ref, w_ref, cos_ref, sin_ref, gw_ref, gb_ref, o_ref, s_ref, S, *, tc, nb):
    c = pl.program_id(1)

    @pl.when(c == 0)
    def _():
        S[...] = jnp.zeros(S.shape, F32)

    lane = lax.broadcasted_iota(jnp.int32, (tc, BRANCH_W), 1)
    first_half = (lane % HEAD_DIM) < (HEAD_DIM // 2)
    cos = cos_ref[...]
    sin = sin_ref[...]
    row = lax.broadcasted_iota(jnp.int32, (tc, tc), 0)
    col = lax.broadcasted_iota(jnp.int32, (tc, tc), 1)
    causal = row >= col
    dist = jnp.where(causal, row - col, 0).astype(F32)
    tpos = lax.broadcasted_iota(jnp.int32, (tc, HEAD_DIM), 0).astype(F32)
    decay = [jnp.where(causal, jnp.exp(dist * RET_LOG_GAMMA[h]), 0.0) for h in range(N_HEADS)]
    q_in = [jnp.exp((tpos + 1.0) * RET_LOG_GAMMA[h]) for h in range(N_HEADS)]
    k_out = [jnp.exp((tc - 1.0 - tpos) * RET_LOG_GAMMA[h]) for h in range(N_HEADS)]
    cols = _project_rows(x_ref, w_ref, nb, tc)
    q = [_rotary(cols[n][:, 0:256], cos, sin, first_half) for n in range(nb)]
    k = [_rotary(cols[n][:, 256:512], cos, sin, first_half) * (HEAD_DIM ** -0.5) for n in range(nb)]
    chains = [(n, h) for n in range(nb) for h in range(N_HEADS)]
    scores = {(n, h): _mm_nt(_head(q[n], h), _head(k[n], h)) * decay[h] for n, h in chains}
    out = {(n, h): _mm(scores[n, h], _head(cols[n][:, 512:768], h)) + _mm(_head(q[n], h) * q_in[h], S[n, h])
           for n, h in chains}
    for n, h in chains:
        S[n, h] = (math.exp(tc * RET_LOG_GAMMA[h]) * S[n, h]
                   + _mm_tn(_head(k[n], h) * k_out[h], _head(cols[n][:, 512:768], h)))
    for n, h in chains:
        oh = out[n, h]
        mu = jnp.mean(oh, axis=-1, keepdims=True)
        d = oh - mu
        var = jnp.mean(d * d, axis=-1, keepdims=True)
        on = d * lax.rsqrt(var + LN_EPS) * _head(gw_ref[...], h) + _head(gb_ref[...], h)
        o_ref[n, :, h * HEAD_DIM:(h + 1) * HEAD_DIM] = on * _silu(_head(cols[n][:, 768:1024], h))

    @pl.when(c == pl.num_programs(1) - 1)
    def _():
        s_ref[...] = S[...]


def _ret_prompt(x3, cos_t, sin_t, lp, tc=128, nb=2):
    b, t, d = x3.shape
    nb = min(nb, b)
    row = lambda: pl.BlockSpec((1, BRANCH_W), lambda i, j: (0, 0))
    return pl.pallas_call(
        functools.partial(_ret_kernel, tc=tc, nb=nb),
        grid=(b // nb, t // tc),
        in_specs=[pl.BlockSpec((nb, tc, d), lambda i, j: (i, j, 0)),
                  pl.BlockSpec((d, 1024), lambda i, j: (0, 0)),
                  pl.BlockSpec((tc, BRANCH_W), lambda i, j: (j, 0)),
                  pl.BlockSpec((tc, BRANCH_W), lambda i, j: (j, 0)), row(), row()],
        out_specs=[pl.BlockSpec((nb, tc, BRANCH_W), lambda i, j: (i, j, 0)),
                   pl.BlockSpec((nb, N_HEADS, HEAD_DIM, HEAD_DIM), lambda i, j: (i, 0, 0, 0))],
        out_shape=[jax.ShapeDtypeStruct((b, t, BRANCH_W), F32),
                   jax.ShapeDtypeStruct((b, N_HEADS, HEAD_DIM, HEAD_DIM), F32)],
        scratch_shapes=[pltpu.VMEM((nb, N_HEADS, HEAD_DIM, HEAD_DIM), F32)],
        compiler_params=_cparams(("parallel", "arbitrary"), VMEM_SMALL_MIB),
        name="ret_prompt",
    )(x3, lp['w_in_br'], cos_t, sin_t, lp['ret_gn_w'], lp['ret_gn_b'])


HGRN_CHUNK = 64
HGRN_LEVELS = tuple(HGRN_CHUNK >> l for l in range(1, 7))


def _hgrn_select_matrix():
    t = jnp.arange(HGRN_CHUNK)
    s = jnp.arange(HGRN_CHUNK)
    blocks = [s[None, :] <= t[:, None]]
    for m in HGRN_LEVELS:
        mid = (t // (2 * m)) * (2 * m) + m - 1
        blocks.append(s[None, :] <= mid[:, None])
    return jnp.concatenate(blocks, axis=0).astype(BF16)


def _hgrn_kernel(x_ref, w_ref, lb_ref, nw_ref, sel_ref, o_ref, s_ref, St, *, tc, nb):
    c = pl.program_id(1)

    @pl.when(c == 0)
    def _():
        St[...] = jnp.zeros(St.shape, F32)

    lb = lb_ref[...]
    row = lax.broadcasted_iota(jnp.int32, (tc, BRANCH_W), 0)
    r64 = lax.broadcasted_iota(jnp.int32, (tc, tc), 0)
    c64 = lax.broadcasted_iota(jnp.int32, (tc, tc), 1)
    tok = []
    all_cols = _project_rows(x_ref, w_ref, nb, tc)
    for n in range(nb):
        cols = all_cols[n]
        q = _silu(cols[:, 0:256])
        forget = lb + (1.0 - lb) * _sigmoid(cols[:, 256:512])
        kk = 1.0 - forget
        sums = _select_rows_exact(sel_ref[...], jnp.log(forget))
        tok.append(dict(q=q, kk=kk, sums=sums, b=sums[0:tc],
                        v=cols[:, 512:768], g=cols[:, 768:1024]))
    chains = [(n, h) for n in range(nb) for h in range(N_HEADS)]
    scores = {(n, h): jnp.where(r64 == c64, _mm_nt(_head(tok[n]['q'], h), _head(tok[n]['kk'], h)), 0.0)
              for n, h in chains}
    for lvl, m in enumerate(HGRN_LEVELS):
        sh = m.bit_length() - 1
        upper = ((row >> sh) & 1) == 1
        same = (r64 >> (sh + 1)) == (c64 >> (sh + 1))
        scaled = []
        for n in range(nb):
            t = tok[n]
            e = jnp.exp(-jnp.abs(t['b'] - t['sums'][(lvl + 1) * tc:(lvl + 2) * tc]))
            scaled.append((jnp.where(upper, t['q'] * e, 0.0), jnp.where(upper, 0.0, t['kk'] * e)))
        for n, h in chains:
            scores[n, h] = scores[n, h] + jnp.where(same, _mm_nt(_head(scaled[n][0], h), _head(scaled[n][1], h)), 0.0)
    for n in range(nb):
        t = tok[n]
        b = t['b']
        v = t['v']
        g = t['g']
        b_end = b[tc - 1:tc, :]
        qd = t['q'] * jnp.exp(b)
        kd = t['kk'] * jnp.exp(b_end - b)
        for h in range(N_HEADS):
            vh = _head(v, h)
            oh = _mm(scores[n, h], vh) + _mm_nt(_head(qd, h), St[n, h])
            St[n, h] = St[n, h] * jnp.exp(_head(b_end, h)) + _mm_tn(vh, _head(kd, h))
            ms = jnp.mean(oh * oh, axis=-1, keepdims=True)
            on = oh * lax.rsqrt(ms + LN_EPS) * _head(nw_ref[...], h)
            o_ref[n, :, h * HEAD_DIM:(h + 1) * HEAD_DIM] = on * _silu(_head(g, h))

    @pl.when(c == pl.num_programs(1) - 1)
    def _():
        for n in range(nb):
            for h in range(N_HEADS):
                s_ref[n, h] = _transpose_exact(St[n, h])


def _hgrn_prompt(x3, lb, lp, nb=8):
    b, t, d = x3.shape
    nb = min(nb, b)
    tc = HGRN_CHUNK
    row = lambda: pl.BlockSpec((1, BRANCH_W), lambda i, j: (0, 0))
    return pl.pallas_call(
        functools.partial(_hgrn_kernel, tc=tc, nb=nb),
        grid=(b // nb, t // tc),
        in_specs=[pl.BlockSpec((nb, tc, d), lambda i, j: (i, j, 0)),
                  pl.BlockSpec((d, 1024), lambda i, j: (0, 1)), row(), row(),
                  pl.BlockSpec((7 * tc, tc), lambda i, j: (0, 0))],
        out_specs=[pl.BlockSpec((nb, tc, BRANCH_W), lambda i, j: (i, j, 0)),
                   pl.BlockSpec((nb, N_HEADS, HEAD_DIM, HEAD_DIM), lambda i, j: (i, 0, 0, 0))],
        out_shape=[jax.ShapeDtypeStruct((b, t, BRANCH_W), F32),
                   jax.ShapeDtypeStruct((b, N_HEADS, HEAD_DIM, HEAD_DIM), F32)],
        scratch_shapes=[pltpu.VMEM((nb, N_HEADS, HEAD_DIM, HEAD_DIM), F32)],
        compiler_params=_cparams(("parallel", "arbitrary"), VMEM_SMALL_MIB),
        name="hgrn_prompt",
    )(x3, lp['w_in_br'], lb, lp['hg_norm_w'], _hgrn_select_matrix())


def _rwkv_token_mix(cols, prev, mu_ref, w0_ref, w2_ref, a0_ref, a2_ref, g2_ref, kkw_ref, ka_ref):
    xs = cols + mu_ref[...] * (prev - cols)
    r = xs[:, 0:256]
    k = xs[:, 256:512]
    v = xs[:, 512:768]
    xg = xs[:, 768:896]
    xw = xs[:, 896:960]
    xa = xs[:, 960:1024]
    w = -_softplus(-(w0_ref[...] + _mm(jnp.tanh(xw), w2_ref[...]))) - 0.5
    lw = -jnp.exp(w)
    a = _sigmoid(a0_ref[...] + _mm(xa, a2_ref[...]))
    g = _mm(_sigmoid(xg), g2_ref[...])
    kk = k * kkw_ref[...]
    parts = []
    for h in range(N_HEADS):
        kh = _head(kk, h)
        nrm = jnp.sqrt(jnp.sum(kh * kh, axis=-1, keepdims=True))
        parts.append(kh / jnp.maximum(nrm, 1e-12))
    kkn = jnp.concatenate(parts, axis=-1)
    k2 = k * (1.0 + (a - 1.0) * ka_ref[...])
    return r, lw, k2, v, kkn, a, g


def _rwkv_out(o, r, k2, v, g, rk_ref, lw_ref, lb_ref, h):
    mu = jnp.mean(o, axis=-1, keepdims=True)
    d = o - mu
    var = jnp.mean(d * d, axis=-1, keepdims=True)
    on = d * lax.rsqrt(var + RW_GN_EPS) * _head(lw_ref[...], h) + _head(lb_ref[...], h)
    bonus = jnp.sum(_head(r, h) * _head(k2, h) * _head(rk_ref[...], h), axis=-1, keepdims=True) * _head(v, h)
    return (on + bonus) * _head(g, h)


def _rwkv_kernel(x_ref, w_ref, mu_ref, w0_ref, w2_ref, a0_ref, a2_ref, g2_ref, kkw_ref, ka_ref, rk_ref,
                 lnw_ref, lnb_ref, y_ref, shift_ref, s_ref, S, last_row, *, tc, nb):
    c = pl.program_id(1)

    @pl.when(c == 0)
    def _():
        S[...] = jnp.zeros(S.shape, F32)
        last_row[...] = jnp.zeros(last_row.shape, F32)

    rr = lax.broadcasted_iota(jnp.int32, (tc, tc), 0)
    cc = lax.broadcasted_iota(jnp.int32, (tc, tc), 1)
    strict = rr > cc
    incl2 = (lax.broadcasted_iota(jnp.int32, (tc, 2 * tc), 0)
             >= (lax.broadcasted_iota(jnp.int32, (tc, 2 * tc), 1) & (tc - 1)))
    tril = _tril_ones(tc)
    n_double = max(1, (tc - 1).bit_length())
    tok = []
    new_shift = []
    all_cols = _project_rows(x_ref, w_ref, nb, tc)
    for n in range(nb):
        cols = all_cols[n]
        new_shift.append(cols[tc - 1:tc, :])
        rowi = lax.broadcasted_iota(jnp.int32, cols.shape, 0)
        prev = jnp.where(rowi == 0, last_row[n], pltpu.roll(cols, 1, 0))
        last_row[n] = cols[tc - 1:tc, :]
        r, lw, k2, v, kkn, a, g = _rwkv_token_mix(cols, prev, mu_ref, w0_ref, w2_ref, a0_ref, a2_ref, g2_ref,
                                                  kkw_ref, ka_ref)
        am = -kkn
        bm = kkn * a
        G = _select_rows_exact(tril, lw)
        g_end = G[tc - 1:tc, :]
        einv = jnp.exp(-G)
        eend = jnp.exp(g_end - G)
        tok.append(dict(r=r, k2=k2, v=v, g=g, g_end=g_end, at=am * jnp.exp(G - lw), rt=r * jnp.exp(G),
                        bt=bm * einv, kt=k2 * einv, bbar=bm * eend, kbar=k2 * eend))
    chains = [(n, h) for n in range(nb) for h in range(N_HEADS)]
    lhs = {(n, h): jnp.concatenate([_head(tok[n]['at'], h), _head(tok[n]['rt'], h)], axis=0) for n, h in chains}
    inter = {(n, h): _mm_nt(lhs[n, h], jnp.concatenate([_head(tok[n]['bt'], h), _head(tok[n]['kt'], h)], axis=0))
             for n, h in chains}
    from_state = {(n, h): _mm_nt(lhs[n, h], S[n, h]) for n, h in chains}
    npow = {ch: jnp.where(strict, inter[ch][0:tc, 0:tc], 0.0) for ch in chains}
    u = {(n, h): from_state[n, h][0:tc]
         + _mm(jnp.where(strict, inter[n, h][0:tc, tc:2 * tc], 0.0), _head(tok[n]['v'], h)) for n, h in chains}
    for j in range(n_double):
        u = {ch: u[ch] + _mm(npow[ch], u[ch]) for ch in chains}
        if j + 1 < n_double:
            npow = {ch: _mm(npow[ch], npow[ch]) for ch in chains}
    for n, h in chains:
        t = tok[n]
        uv = jnp.concatenate([u[n, h], _head(t['v'], h)], axis=0)
        a_r = jnp.where(incl2, inter[n, h][tc:2 * tc, :], 0.0)
        o = from_state[n, h][tc:2 * tc] + _mm(a_r, uv)
        S[n, h] = (S[n, h] * jnp.exp(_head(t['g_end'], h))
                   + _mm_tn(uv, jnp.concatenate([_head(t['bbar'], h), _head(t['kbar'], h)], axis=0)))
        y_ref[n, :, h * HEAD_DIM:(h + 1) * HEAD_DIM] = _rwkv_out(o, t['r'], t['k2'], t['v'], t['g'], rk_ref,
                                                                 lnw_ref, lnb_ref, h)

    @pl.when(c == pl.num_programs(1) - 1)
    def _():
        s_ref[...] = S[...]
        for n in range(nb):
            shift_ref[n] = new_shift[n]


def _rwkv_prompt(x3, lp, tc=64, nb=4):
    b, t, d = x3.shape
    nb = min(nb, b)
    row = lambda w: pl.BlockSpec((1, w), lambda i, j: (0, 0))
    mat = lambda r, w: pl.BlockSpec((r, w), lambda i, j: (0, 0))
    return pl.pallas_call(
        functools.partial(_rwkv_kernel, tc=tc, nb=nb),
        grid=(b // nb, t // tc),
        in_specs=[pl.BlockSpec((nb, tc, d), lambda i, j: (i, j, 0)),
                  pl.BlockSpec((d, 1024), lambda i, j: (0, 2)),
                  row(1024), row(256), mat(64, 256), row(256), mat(64, 256), mat(128, 256),
                  row(256), row(256), row(256), row(256), row(256)],
        out_specs=[pl.BlockSpec((nb, tc, BRANCH_W), lambda i, j: (i, j, 0)),
                   pl.BlockSpec((nb, 1, 1024), lambda i, j: (i, 0, 0)),
                   pl.BlockSpec((nb, N_HEADS, HEAD_DIM, HEAD_DIM), lambda i, j: (i, 0, 0, 0))],
        out_shape=[jax.ShapeDtypeStruct((b, t, BRANCH_W), F32),
                   jax.ShapeDtypeStruct((b, 1, 1024), F32),
                   jax.ShapeDtypeStruct((b, N_HEADS, HEAD_DIM, HEAD_DIM), F32)],
        scratch_shapes=[pltpu.VMEM((nb, N_HEADS, HEAD_DIM, HEAD_DIM), F32), pltpu.VMEM((nb, 1, 1024), F32)],
        compiler_params=_cparams(("parallel", "arbitrary"), VMEM_SMALL_MIB),
        name="rwkv_prompt",
    )(x3, lp['w_in_br'], lp['rw_mu'], lp['rw_w0'], lp['rw_w2'], lp['rw_a0'], lp['rw_a2'], lp['rw_g2'],
      lp['rw_kk'], lp['rw_ka'], lp['rw_rk'], lp['rw_lnx_w'], lp['rw_lnx_b'])


def _layer_norm(z, w, b):
    mu = jnp.mean(z, axis=-1, keepdims=True)
    d = z - mu
    var = jnp.mean(d * d, axis=-1, keepdims=True)
    return d * lax.rsqrt(var + LN_EPS) * w + b


def _mix_kernel(x_ref, oa_ref, ob_ref, oc_ref, od_ref, wg_ref, wb_ref, wo_ref, lw_ref, lb_ref, out_ref):
    x = x_ref[...]
    xb = x.astype(BF16)
    mixed = None
    for gi, o_ref in enumerate((oa_ref, ob_ref, oc_ref, od_ref)):
        gate = _sigmoid(jnp.dot(xb, wg_ref[:, gi * D_MODEL:(gi + 1) * D_MODEL], preferred_element_type=F32))
        up = jnp.dot(o_ref[...].astype(BF16), wb_ref[gi], preferred_element_type=F32)
        mixed = gate * up if mixed is None else mixed + gate * up
    y = jnp.dot(mixed.astype(BF16), wo_ref[...], preferred_element_type=F32)
    out_ref[...] = _layer_norm(ALPHA * x + y, lw_ref[...], lb_ref[...])


def _mix(x, outs, lp, tn):
    n = x.shape[0]
    tok = lambda w: pl.BlockSpec((tn, w), lambda i: (i, 0))
    const = lambda *s: pl.BlockSpec(s, lambda i: (0,) * len(s))
    return pl.pallas_call(
        _mix_kernel,
        grid=(n // tn,),
        in_specs=[tok(D_MODEL), tok(BRANCH_W), tok(BRANCH_W), tok(BRANCH_W), tok(BRANCH_W),
                  const(D_MODEL, 4 * D_MODEL), const(4, BRANCH_W, D_MODEL), const(D_MODEL, D_MODEL),
                  const(1, D_MODEL), const(1, D_MODEL)],
        out_specs=tok(D_MODEL),
        out_shape=jax.ShapeDtypeStruct((n, D_MODEL), F32),
        compiler_params=_cparams(("parallel",), VMEM_WEIGHTS_MIB),
        name="mix_ln1",
    )(x, *outs, lp['w_gate'], lp['w_branch'], lp['w_out'], lp['ln1_w'], lp['ln1_b'])


def _ple_kernel(x_ref, p_ref, wg_ref, bg_ref, wp_ref, out_ref):
    x = x_ref[...]
    gate = _sigmoid(jnp.dot(x.astype(BF16), wg_ref[...], preferred_element_type=F32) + bg_ref[...])
    emb = jnp.dot(p_ref[...].astype(BF16), wp_ref[...], preferred_element_type=F32)
    out_ref[...] = x + gate * emb


def _ple(x, p_emb, lp, tn):
    n = x.shape[0]
    tok = lambda w: pl.BlockSpec((tn, w), lambda i: (i, 0))
    const = lambda *s: pl.BlockSpec(s, lambda i: (0,) * len(s))
    return pl.pallas_call(
        _ple_kernel,
        grid=(n // tn,),
        in_specs=[tok(D_MODEL), tok(256), const(D_MODEL, D_MODEL), const(1, D_MODEL), const(256, D_MODEL)],
        out_specs=tok(D_MODEL),
        out_shape=jax.ShapeDtypeStruct((n, D_MODEL), F32),
        compiler_params=_cparams(("parallel",), VMEM_SMALL_MIB),
        name="ple_gate",
    )(x, p_emb, lp['ple_gate_w'], lp['ple_gate_b'], lp['ple_w'])


PEER_HEADS = 8
PEER_NKEYS = 128
PEER_TOPK = 16
PEER_EB = 1024
PEER_SUB = 2
PEER_TIE_SLOTS = 8
PEER_IGROUP = 2
PEER_JROWS = 64


def _oddeven_merge_sort_pairs(n):
    pairs = []
    p = 1
    while p < n:
        k = p
        while k >= 1:
            for j in range(k % p, n - k, 2 * k):
                for i in range(min(k, n - j - k)):
                    if (i + j) // (p * 2) == (i + j + k) // (p * 2):
                        pairs.append((i + j, i + j + k))
            k //= 2
        p *= 2
    return pairs


def _bitonic_merge_pairs(n):
    pairs = []
    k = n // 2
    while k >= 1:
        pairs.extend((i, i + k) for i in range(n) if (i & k) == 0)
        k //= 2
    return pairs


_SORT16 = _oddeven_merge_sort_pairs(PEER_TOPK)
_MERGE16 = _bitonic_merge_pairs(PEER_TOPK)
_CAND_LEN = tuple(PEER_TOPK // (a + 1) for a in range(PEER_TOPK))


def _network(vals, pairs):
    vals = list(vals)
    for i, j in pairs:
        hi = jnp.maximum(vals[i], vals[j])
        lo = jnp.minimum(vals[i], vals[j])
        vals[i], vals[j] = hi, lo
    return vals


def _top16_merge(x, y):
    return _network([jnp.maximum(x[i], y[PEER_TOPK - 1 - i]) for i in range(PEER_TOPK)], _MERGE16)


def _peer_head_stats(h, s_nat, e_nat, sk, th, tie, g_count):
    rows0 = pl.ds(pl.multiple_of(2 * h * PEER_NKEYS, PEER_NKEYS), PEER_NKEYS)
    rows1 = pl.ds(pl.multiple_of((2 * h + 1) * PEER_NKEYS, PEER_NKEYS), PEER_NKEYS)
    for p, rows in enumerate((rows0, rows1)):
        for g in range(4):
            gg = g % g_count
            sk[:, p * 4 + g, :] = s_nat[gg, rows, :]
    groups = []
    for m in range(PEER_NKEYS // PEER_TOPK):
        groups.append(_network([sk[PEER_TOPK * m + i] for i in range(PEER_TOPK)], _SORT16))
    while len(groups) > 1:
        groups = [_top16_merge(groups[i], groups[i + 1]) for i in range(0, len(groups), 2)]
    top = groups[0]
    low = lax.broadcasted_iota(jnp.int32, (8, 128), 0) < 4
    ta = [jnp.where(low, t, pltpu.roll(t, 4, 0)) for t in top]
    tb = [jnp.where(low, pltpu.roll(t, 4, 0), t) for t in top]
    cand = [[ta[a] + tb[b] for b in range(_CAND_LEN[a])] for a in range(PEER_TOPK)]
    m1 = _network(cand[1] + [cand[a][0] for a in range(PEER_TOPK - 1, 7, -1)], _MERGE16)
    m2 = _network(cand[2] + cand[3] + cand[4] + cand[5] + cand[6], _SORT16)
    t1 = _top16_merge(cand[0], m1)
    t2 = _top16_merge(t1, m2)
    t2[15] = jnp.maximum(t2[15], cand[7][0])
    t2[14] = jnp.maximum(t2[14], cand[7][1])
    theta = t2[0]
    for t in t2[1:]:
        theta = jnp.minimum(theta, t)
    cmax = cand[0][0]
    z = jnp.zeros((8, 128), F32)
    n_kept = jnp.zeros((8, 128), F32)
    for row in cand:
        for cv in row:
            keep = cv >= theta
            z = z + jnp.where(keep, jnp.exp(cv - cmax), 0.0)
            n_kept = n_kept + jnp.where(keep, 1.0, 0.0)
    inv_z = 1.0 / z
    cand_tie = jnp.where(n_kept != float(PEER_TOPK), 1.0, 0.0)
    last1_used = jnp.where(cand[0][PEER_TOPK - 1] >= theta, 1.0, 0.0)
    last0_used = jnp.where(cand[PEER_TOPK - 1][0] >= theta, 1.0, 0.0)
    tau_top = jnp.full((8, 128), jnp.inf, F32)
    for b in range(PEER_TOPK):
        tau_top = jnp.where(cand[0][b] >= theta, tb[b], tau_top)
    for g in range(g_count):
        lanes = slice(g * 128, (g + 1) * 128)
        th[h, 0:1, lanes] = theta[g:g + 1, :]
        th[h, 1:2, lanes] = ta[0][g:g + 1, :]
        th[h, 2:3, lanes] = tb[0][g:g + 1, :]
        th[h, 3:4, lanes] = inv_z[g:g + 1, :]
        th[h, 4:5, lanes] = ta[PEER_TOPK - 1][g:g + 1, :]
        th[h, 5:6, lanes] = cand_tie[g:g + 1, :]
        th[h, 6:7, lanes] = last0_used[g:g + 1, :]
        th[h, 7:8, lanes] = last1_used[g:g + 1, :]
        th[h, 8 + PEER_TOPK:9 + PEER_TOPK, lanes] = tau_top[g:g + 1, :]
        for b in range(PEER_TOPK):
            th[h, 8 + b:9 + b, lanes] = tb[b][g:g + 1, :]
    for g in range(g_count):
        lanes = slice(g * 128, (g + 1) * 128)
        s0 = s_nat[g, rows0, :]
        s1 = s_nat[g, rows1, :]
        e_nat[g, rows0, :] = jnp.exp(s0 - th[h, 1:2, lanes])
        e_nat[g, rows1, :] = jnp.exp(s1 - th[h, 2:3, lanes]) * th[h, 3:4, lanes]
        n0 = jnp.sum(jnp.where(s0 >= th[h, 4:5, lanes], 1.0, 0.0), axis=0, keepdims=True)
        n1 = jnp.sum(jnp.where(s1 >= th[h, 7 + PEER_TOPK:8 + PEER_TOPK, lanes], 1.0, 0.0), axis=0, keepdims=True)
        key_tie = jnp.maximum(jnp.where(n0 != float(PEER_TOPK), th[h, 6:7, lanes], 0.0),
                              jnp.where(n1 != float(PEER_TOPK), th[h, 7:8, lanes], 0.0))
        tie[0:1, lanes] = jnp.maximum(tie[0:1, lanes], jnp.maximum(key_tie, th[h, 5:6, lanes]))
        tau = jnp.full(s0.shape, jnp.inf, F32)
        for b in range(PEER_TOPK // 2):
            sb = th[h, 8 + b:9 + b, lanes]
            tau = jnp.where((s0 + sb) >= th[h, 0:1, lanes], sb, tau)
        tau = jnp.where(s0 == th[h, 1:2, lanes], th[h, 8 + PEER_TOPK:9 + PEER_TOPK, lanes], tau)
        s_nat[g, rows0, :] = tau


def _peer_kernel(x_ref, wq_ref, keys_ref, u_ref, vt_ref, lw_ref, lb_ref, out_ref, tie_ref,
                 xtb, s_nat, e_nat, sk, th, wact, wraw, actb, yt, tie, *, tn):
    j = pl.program_id(1)
    g_count = tn // 128

    @pl.when(j == 0)
    def _():
        xtb[...] = x_ref[...].T.astype(BF16)
        for hp in range(2 * PEER_HEADS):
            qt = jnp.dot(wq_ref[hp * 128:(hp + 1) * 128, :], xtb[...], preferred_element_type=F32)
            scores = jnp.dot(keys_ref[hp], qt.astype(BF16), preferred_element_type=F32)
            for g in range(g_count):
                s_nat[g, hp * PEER_NKEYS:(hp + 1) * PEER_NKEYS, :] = scores[:, g * 128:(g + 1) * 128]

        tie[...] = jnp.zeros(tie.shape, F32)

        def head_body(h, carry):
            _peer_head_stats(h, s_nat, e_nat, sk, th, tie, g_count)
            return carry

        lax.fori_loop(0, PEER_HEADS, head_body, 0)
        tie_ref[...] = tie[0:1, :]
        yt[...] = jnp.zeros(yt.shape, F32)

    last = pl.num_programs(1) - 1
    cur = j % 2

    def step(first_stage, second_stage):
        for sb in range(PEER_SUB):
            if first_stage:
                act = jnp.dot(u_ref[sb * PEER_EB:(sb + 1) * PEER_EB, :], xtb[...],
                              preferred_element_type=F32)
                for g in range(g_count):
                    actb[sb, g] = act[:, g * 128:(g + 1) * 128]
            if second_stage:
                yt[...] += jnp.dot(vt_ref[sb], wact[1 - cur, sb], preferred_element_type=F32)
            if not first_stage:
                continue
            i0 = pl.multiple_of((j * PEER_SUB + sb) * (PEER_EB // PEER_NKEYS), 8)
            jr = PEER_JROWS
            for g in range(g_count):
                for jq in range(PEER_NKEYS // jr):
                    for ig in range(PEER_EB // PEER_NKEYS // PEER_IGROUP):
                        acc = [jnp.zeros((jr, 128), F32) for _ in range(PEER_IGROUP)]
                        for h in range(PEER_HEADS):
                            base1 = (2 * h + 1) * PEER_NKEYS + jq * jr
                            s1 = s_nat[g, base1:base1 + jr, :]
                            e1 = e_nat[g, base1:base1 + jr, :]
                            tau0 = s_nat[g, pl.ds(2 * h * PEER_NKEYS + i0, 8), :]
                            e0 = e_nat[g, pl.ds(2 * h * PEER_NKEYS + i0, 8), :]
                            for k in range(PEER_IGROUP):
                                ii = PEER_IGROUP * ig + k
                                sel = s1 >= tau0[ii:ii + 1, :]
                                acc[k] = acc[k] + jnp.where(sel, e0[ii:ii + 1, :] * e1, 0.0)
                        for k in range(PEER_IGROUP):
                            r0 = (PEER_IGROUP * ig + k) * PEER_NKEYS + jq * jr
                            wraw[g, r0:r0 + jr, :] = acc[k]
            for g in range(g_count):
                wact[cur, sb, :, g * 128:(g + 1) * 128] = (wraw[g] * _gelu(actb[sb, g])).astype(BF16)

    @pl.when(j == 0)
    def _():
        step(True, False)

    @pl.when((j > 0) & (j < last))
    def _():
        step(True, True)

    @pl.when(j == last)
    def _():
        step(False, True)
        z = ALPHA * x_ref[...] + yt[...].T
        out_ref[...] = _layer_norm(z, lw_ref[...], lb_ref[...])


def _peer(x, lp, tn):
    n = x.shape[0]
    n_blk = lp['peer_u'].shape[0] // (PEER_EB * PEER_SUB)
    once = lambda *s: pl.BlockSpec(s, lambda i, j: (0,) * len(s), pipeline_mode=pl.Buffered(1))
    const = lambda *s: pl.BlockSpec(s, lambda i, j: (0,) * len(s))
    out, tie_flags = pl.pallas_call(
        functools.partial(_peer_kernel, tn=tn),
        grid=(n // tn, n_blk + 1),
        in_specs=[pl.BlockSpec((tn, D_MODEL), lambda i, j: (i, 0)),
                  once(2 * PEER_HEADS * 128, D_MODEL), once(2 * PEER_HEADS, PEER_NKEYS, 128),
                  pl.BlockSpec((PEER_SUB * PEER_EB, D_MODEL), lambda i, j: (jnp.minimum(j, n_blk - 1), 0)),
                  pl.BlockSpec((PEER_SUB, D_MODEL, PEER_EB), lambda i, j: (jnp.maximum(j - 1, 0), 0, 0)),
                  const(1, D_MODEL), const(1, D_MODEL)],
        out_specs=[pl.BlockSpec((tn, D_MODEL), lambda i, j: (i, 0)),
                   pl.BlockSpec((None, 1, tn), lambda i, j: (i, 0, 0))],
        out_shape=[jax.ShapeDtypeStruct((n, D_MODEL), F32), jax.ShapeDtypeStruct((n // tn, 1, tn), F32)],
        scratch_shapes=[pltpu.VMEM((D_MODEL, tn), BF16),
                        pltpu.VMEM((tn // 128, 2 * PEER_HEADS * PEER_NKEYS, 128), F32),
                        pltpu.VMEM((tn // 128, 2 * PEER_HEADS * PEER_NKEYS, 128), F32),
                        pltpu.VMEM((PEER_NKEYS, 8, 128), F32),
                        pltpu.VMEM((PEER_HEADS, 16 + PEER_TOPK, tn), F32),
                        pltpu.VMEM((2, PEER_SUB, PEER_EB, tn), BF16),
                        pltpu.VMEM((tn // 128, PEER_EB, 128), F32),
                        pltpu.VMEM((PEER_SUB, tn // 128, PEER_EB, 128), F32),
                        pltpu.VMEM((D_MODEL, tn), F32),
                        pltpu.VMEM((8, tn), F32)],
        compiler_params=_cparams(("parallel", "arbitrary"), VMEM_PEER_MIB),
        name="peer_ln2",
    )(x, lp['peer_wq_t'], lp['peer_keys'], lp['peer_u'], lp['peer_v_t'], lp['ln2_w'], lp['ln2_b'])
    flags = tie_flags.reshape(n) > 0.0
    count = jnp.sum(flags.astype(jnp.int32))

    def redo_flagged():
        idx = jnp.nonzero(flags, size=PEER_TIE_SLOTS, fill_value=n)[0]
        fixed = _peer_tie_fallback(x[jnp.minimum(idx, n - 1)], lp)
        return out.at[idx].set(fixed, mode='drop')

    return lax.cond(count > PEER_TIE_SLOTS, lambda: _peer_tie_fallback(x, lp),
                    lambda: lax.cond(count > 0, redo_flagged, lambda: out))


def _topk_lowest_index(v, k):
    n = v.shape[-1]
    pos = jnp.arange(n)
    ahead = (v[..., None, :] > v[..., :, None]) | ((v[..., None, :] == v[..., :, None]) & (pos[None, :] < pos[:, None]))
    rank = jnp.sum(ahead, axis=-1)
    onehot = rank[..., None, :] == jnp.arange(k)[:, None]
    vals = jnp.sum(jnp.where(onehot, v[..., None, :], 0.0), axis=-1)
    idx = jnp.sum(jnp.where(onehot, pos, 0), axis=-1)
    return vals, idx


def _peer_tie_fallback(x, lp):
    wq, keys, u_all, v_all, layer = lp['peer_raw']
    n_exp = u_all.shape[1]
    u_tab = u_all.reshape(-1, D_MODEL)
    v_tab = v_all.reshape(-1, D_MODEL)
    n = x.shape[0]
    blk = min(256, n)
    top_k = _topk_lowest_index if n <= PEER_TIE_SLOTS else lax.top_k

    def block(xb):
        q = (xb @ wq).reshape(blk, PEER_HEADS, 2, PEER_NKEYS)
        s = jnp.einsum('nhpc,hpkc->nhpk', q, keys).astype(F32)
        s_top, i_top = top_k(s, PEER_TOPK)
        cand = (s_top[:, :, 0, :, None] + s_top[:, :, 1, None, :]).reshape(blk, PEER_HEADS, -1)
        cand_idx = (i_top[:, :, 0, :, None] * PEER_NKEYS + i_top[:, :, 1, None, :]).reshape(blk, PEER_HEADS, -1)
        best, pick = top_k(cand, PEER_TOPK)
        eidx = jnp.take_along_axis(cand_idx, pick, axis=-1) + layer * n_exp
        gw = jax.nn.softmax(best, axis=-1)
        act = jax.nn.gelu(jnp.einsum('nhkd,nd->nhk', u_tab[eidx], xb).astype(F32))
        return jnp.einsum('nhk,nhkd->nd', gw * act, v_tab[eidx])

    y = lax.map(block, x.reshape(-1, blk, D_MODEL)).reshape(n, D_MODEL)
    return _layer_norm(ALPHA * x + y, lp['ln2_w'], lp['ln2_b'])


DEC_N = 128
HEAD_STATE = HEAD_DIM * HEAD_DIM


def _col(v):
    return jnp.broadcast_to(v.reshape(-1, 1), (v.size, DEC_N))


def _load_state_t(s_ref, st):
    st[...] = s_ref[...].T.reshape(HEAD_DIM, HEAD_DIM, DEC_N)


def _store_state_t(so_ref, st):
    so_ref[...] = st[...].reshape(HEAD_STATE, DEC_N).T


def _ret_dec_kernel(c_ref, cos_ref, sin_ref, gw_ref, gb_ref, s_ref, o_ref, so_ref, ct, qk, st):
    h = pl.program_id(0)
    ct[...] = c_ref[...].T
    r0 = pl.multiple_of(h * HEAD_DIM, HEAD_DIM)

    def rot(x):
        partner = jnp.concatenate([x[HEAD_DIM // 2:], x[:HEAD_DIM // 2]], axis=0)
        return x * cos_ref[...] + partner * sin_ref[...]

    qk[0] = rot(ct[pl.ds(r0, HEAD_DIM), :])
    qk[1] = rot(ct[pl.ds(256 + r0, HEAD_DIM), :]) * (HEAD_DIM ** -0.5)
    v = ct[pl.ds(512 + r0, HEAD_DIM), :]
    g = ct[pl.ds(768 + r0, HEAD_DIM), :]
    gamma = jnp.exp(jnp.zeros((1, 1), F32) + jnp.where(
        h == 0, RET_LOG_GAMMA[0], jnp.where(h == 1, RET_LOG_GAMMA[1],
                                            jnp.where(h == 2, RET_LOG_GAMMA[2], RET_LOG_GAMMA[3]))))
    _load_state_t(s_ref, st)

    def body(k, o):
        s_new = gamma * st[k] + qk[1, pl.ds(k, 1), :] * v
        st[k] = s_new
        return o + qk[0, pl.ds(k, 1), :] * s_new

    o = lax.fori_loop(0, HEAD_DIM, body, jnp.zeros((HEAD_DIM, DEC_N), F32))
    mu = jnp.mean(o, axis=0, keepdims=True)
    d = o - mu
    var = jnp.mean(d * d, axis=0, keepdims=True)
    o_ref[...] = (d * lax.rsqrt(var + LN_EPS) * gw_ref[...] + gb_ref[...]) * _silu(g)
    _store_state_t(so_ref, st)


def _hgrn_dec_kernel(c_ref, lb_ref, nw_ref, s_ref, o_ref, so_ref, ct, qk, st):
    h = pl.program_id(0)
    ct[...] = c_ref[...].T
    r0 = pl.multiple_of(h * HEAD_DIM, HEAD_DIM)
    lb = lb_ref[...]
    qk[0] = _silu(ct[pl.ds(r0, HEAD_DIM), :])
    forget = lb + (1.0 - lb) * _sigmoid(ct[pl.ds(256 + r0, HEAD_DIM), :])
    qk[1] = forget
    qk[2] = 1.0 - forget
    v = ct[pl.ds(512 + r0, HEAD_DIM), :]
    g = ct[pl.ds(768 + r0, HEAD_DIM), :]
    _load_state_t(s_ref, st)

    def body(k, o):
        s_new = qk[1, pl.ds(k, 1), :] * st[k] + qk[2, pl.ds(k, 1), :] * v
        st[k] = s_new
        return o + qk[0, pl.ds(k, 1), :] * s_new

    o = lax.fori_loop(0, HEAD_DIM, body, jnp.zeros((HEAD_DIM, DEC_N), F32))
    ms = jnp.mean(o * o, axis=0, keepdims=True)
    o_ref[...] = o * lax.rsqrt(ms + LN_EPS) * nw_ref[...] * _silu(g)
    _store_state_t(so_ref, st)


def _rwkv_dec_kernel(c_ref, sh_ref, mu_ref, w0_ref, w2_ref, a0_ref, a2_ref, g2_ref, kkw_ref, ka_ref,
                     rk_ref, lnw_ref, lnb_ref, s_ref, o_ref, so_ref, vt, st, osc):
    h = pl.program_id(0)
    r, lw, k2, v, kkn, a, g = _rwkv_token_mix(c_ref[...], sh_ref[...], mu_ref, w0_ref, w2_ref, a0_ref, a2_ref,
                                              g2_ref, kkw_ref, ka_ref)
    for idx, arr in enumerate((r, jnp.exp(lw), k2, v, kkn, a, g)):
        vt[idx] = arr.T
    r0 = pl.multiple_of(h * HEAD_DIM, HEAD_DIM)
    rows = pl.ds(r0, HEAD_DIM)
    rh, wh, kh, kkh, ah = vt[0, rows, :], vt[1, rows, :], vt[2, rows, :], vt[4, rows, :], vt[5, rows, :]
    vh, gh = vt[3, rows, :], vt[6, rows, :]
    kka = kkh * ah
    _load_state_t(s_ref, st)

    def body(vi, carry):
        s_old = st[vi]
        sa = jnp.sum(s_old * (-kkh), axis=0, keepdims=True)
        s_new = s_old * wh + sa * kka + vt[3, pl.ds(r0 + vi, 1), :] * kh
        st[vi] = s_new
        osc[pl.ds(vi, 1), :] = jnp.sum(s_new * rh, axis=0, keepdims=True)
        return carry

    lax.fori_loop(0, HEAD_DIM, body, 0)
    o = osc[...]
    mu = jnp.mean(o, axis=0, keepdims=True)
    d = o - mu
    var = jnp.mean(d * d, axis=0, keepdims=True)
    on = d * lax.rsqrt(var + RW_GN_EPS) * lnw_ref[...] + lnb_ref[...]
    bonus = jnp.sum(rh * kh * rk_ref[...], axis=0, keepdims=True) * vh
    o_ref[...] = (on + bonus) * gh
    _store_state_t(so_ref, st)


def _dec_specs():
    head_tab = pl.BlockSpec((HEAD_DIM, DEC_N), lambda h: (h, 0))
    state = pl.BlockSpec((DEC_N, HEAD_STATE), lambda h: (0, h))
    out = pl.BlockSpec((None, HEAD_DIM, DEC_N), lambda h: (h, 0, 0))
    return head_tab, state, out


def _dec_out_shapes():
    return [jax.ShapeDtypeStruct((N_HEADS, HEAD_DIM, DEC_N), F32),
            jax.ShapeDtypeStruct((DEC_N, N_HEADS * HEAD_STATE), F32)]


def _dec_finish(o_t, s_new):
    return o_t.reshape(BRANCH_W, DEC_N).T, s_new.reshape(DEC_N, N_HEADS, HEAD_DIM, HEAD_DIM)


def _ret_decode(cols, state, cos_c, sin_c, lp):
    head_tab, st_spec, out_spec = _dec_specs()
    same = pl.BlockSpec((HEAD_DIM, DEC_N), lambda h: (0, 0))
    o_t, s_new = pl.pallas_call(
        _ret_dec_kernel,
        grid=(N_HEADS,),
        in_specs=[pl.BlockSpec((DEC_N, 1024), lambda h: (0, 0)), same, same, head_tab, head_tab, st_spec],
        out_specs=[out_spec, st_spec],
        out_shape=_dec_out_shapes(),
        scratch_shapes=[pltpu.VMEM((1024, DEC_N), F32), pltpu.VMEM((2, HEAD_DIM, DEC_N), F32),
                        pltpu.VMEM((HEAD_DIM, HEAD_DIM, DEC_N), F32)],
        compiler_params=_cparams(("arbitrary",), VMEM_DECODE_MIB),
        name="ret_decode",
    )(cols, cos_c, sin_c, _col(lp['ret_gn_w']), _col(lp['ret_gn_b']), state.reshape(DEC_N, -1))
    return _dec_finish(o_t, s_new)


def _hgrn_decode(cols, state, lp):
    head_tab, st_spec, out_spec = _dec_specs()
    o_t, s_new = pl.pallas_call(
        _hgrn_dec_kernel,
        grid=(N_HEADS,),
        in_specs=[pl.BlockSpec((DEC_N, 1024), lambda h: (0, 1)), head_tab, head_tab, st_spec],
        out_specs=[out_spec, st_spec],
        out_shape=_dec_out_shapes(),
        scratch_shapes=[pltpu.VMEM((1024, DEC_N), F32), pltpu.VMEM((3, HEAD_DIM, DEC_N), F32),
                        pltpu.VMEM((HEAD_DIM, HEAD_DIM, DEC_N), F32)],
        compiler_params=_cparams(("arbitrary",), VMEM_DECODE_MIB),
        name="hgrn_decode",
    )(cols, _col(lp['hg_lb']), _col(lp['hg_norm_w']), state.reshape(DEC_N, -1))
    return _dec_finish(o_t, s_new)


def _rwkv_decode(cols, shift, state, lp):
    head_tab, st_spec, out_spec = _dec_specs()
    row = lambda w: pl.BlockSpec((1, w), lambda h: (0, 0))
    mat = lambda r, w: pl.BlockSpec((r, w), lambda h: (0, 0))
    o_t, s_new = pl.pallas_call(
        _rwkv_dec_kernel,
        grid=(N_HEADS,),
        in_specs=[pl.BlockSpec((DEC_N, 1024), lambda h: (0, 2)), mat(DEC_N, 1024),
                  row(1024), row(256), mat(64, 256), row(256), mat(64, 256), mat(128, 256),
                  row(256), row(256), head_tab, head_tab, head_tab, st_spec],
        out_specs=[out_spec, st_spec],
        out_shape=_dec_out_shapes(),
        scratch_shapes=[pltpu.VMEM((7, BRANCH_W, DEC_N), F32), pltpu.VMEM((HEAD_DIM, HEAD_DIM, DEC_N), F32),
                        pltpu.VMEM((HEAD_DIM, DEC_N), F32)],
        compiler_params=_cparams(("arbitrary",), VMEM_DECODE_MIB),
        name="rwkv_decode",
    )(cols, shift, lp['rw_mu'], lp['rw_w0'], lp['rw_w2'], lp['rw_a0'], lp['rw_a2'], lp['rw_g2'],
      lp['rw_kk'], lp['rw_ka'], _col(lp['rw_rk']), _col(lp['rw_lnx_w']), _col(lp['rw_lnx_b']),
      state.reshape(DEC_N, -1))
    return _dec_finish(o_t, s_new)


def _lru_dec_kernel(c_ref, conv_ref, h0_ref, cw_ref, cb_ref, wa_ref, ba_ref, wx_ref, bx_ref, lam_ref,
                    y_ref, h_ref, nconv_ref):
    xb = c_ref[:, 0:BRANCH_W]
    gate = c_ref[:, BRANCH_W:2 * BRANCH_W]
    c0, c1, c2 = conv_ref[0], conv_ref[1], conv_ref[2]
    xc = (c0 * cw_ref[0:1, :] + c1 * cw_ref[1:2, :] + c2 * cw_ref[2:3, :] + xb * cw_ref[3:4, :]) + cb_ref[...]
    a, u = _lru_gates(xc, wa_ref, ba_ref, wx_ref, bx_ref, lam_ref)
    hn = a * h0_ref[...] + u
    h_ref[...] = hn
    y_ref[...] = hn * _gelu(gate)
    nconv_ref[0] = c1
    nconv_ref[1] = c2
    nconv_ref[2] = xb


def _lru_decode(cols, conv, h0, lp):
    full = lambda *s: pl.BlockSpec(s, lambda i: (0,) * len(s))
    return pl.pallas_call(
        _lru_dec_kernel,
        grid=(1,),
        in_specs=[pl.BlockSpec((DEC_N, 512), lambda i: (0, 6)), full(3, DEC_N, BRANCH_W), full(DEC_N, BRANCH_W),
                  full(CONV_W, BRANCH_W), full(1, BRANCH_W), full(BRANCH_W, BRANCH_W), full(1, BRANCH_W),
                  full(BRANCH_W, BRANCH_W), full(1, BRANCH_W), full(1, BRANCH_W)],
        out_specs=[full(DEC_N, BRANCH_W), full(DEC_N, BRANCH_W), full(3, DEC_N, BRANCH_W)],
        out_shape=[jax.ShapeDtypeStruct((DEC_N, BRANCH_W), F32), jax.ShapeDtypeStruct((DEC_N, BRANCH_W), F32),
                   jax.ShapeDtypeStruct((3, DEC_N, BRANCH_W), F32)],
        compiler_params=_cparams(("arbitrary",), VMEM_SMALL_MIB),
        name="lru_decode",
    )(cols, conv, h0, lp['lru_conv_w'], lp['lru_conv_b'], lp['lru_wa_bd'], lp['lru_ba'], lp['lru_wx_bd'],
      lp['lru_bx'], lp['lru_lambda'])


def _prompt_layer(x, p_emb, lp, cos_t, sin_t):
    b, t, d = x.shape
    n = b * t
    xf = x.reshape(n, d)
    o_a, s_ret = _ret_prompt(x, cos_t, sin_t, lp)
    o_b, s_hg = _hgrn_prompt(x, lp['hg_lb'], lp)
    o_c, s_shift, s_rw = _rwkv_prompt(x, lp)
    o_d, s_lru, s_conv = _lru_prompt(x, lp)
    outs = [o.reshape(n, BRANCH_W) for o in (o_a, o_b, o_c, o_d)]
    x1 = _mix(xf, outs, lp, 512)
    x2 = _peer(x1, lp, 512)
    x3 = _ple(x2, p_emb.reshape(n, -1), lp, 512)
    return x3.reshape(b, t, d), (s_ret, s_hg, s_rw, s_shift[:, 0], s_lru[:, 0], s_conv)


def _sample_layer(x, p_emb, state, lp, cos_c, sin_c):
    s_ret, s_hg, s_rw, s_shift, s_lru, s_conv = state
    xf = x.reshape(DEC_N, D_MODEL)
    cols = _proj(xf, lp['w_in_br'], DEC_N)
    o_a, s_ret = _ret_decode(cols, s_ret, cos_c, sin_c, lp)
    o_b, s_hg = _hgrn_decode(cols, s_hg, lp)
    o_c, s_rw = _rwkv_decode(cols, s_shift, s_rw, lp)
    o_d, s_lru, s_conv_t = _lru_decode(cols, jnp.swapaxes(s_conv, 0, 1), s_lru, lp)
    x1 = _mix(xf, [o_a, o_b, o_c, o_d], lp, DEC_N)
    x2 = _peer(x1, lp, DEC_N)
    x3 = _ple(x2, p_emb.reshape(DEC_N, -1), lp, DEC_N)
    new_shift = cols[:, 2048:3072]
    return x3.reshape(x.shape), (s_ret, s_hg, s_rw, new_shift, s_lru, jnp.swapaxes(s_conv_t, 0, 1))


def kernel(x_prompt, x_sample, state_ret, state_hgrn, state_rwkv, state_shift, state_lru, state_conv, p_prompt, p_sample, w_in, ret_gn_w, ret_gn_b, hg_lb, hg_norm_w, rw_mu, rw_w0, rw_w2, rw_a0, rw_a2, rw_g2, rw_kk, rw_ka, rw_rk, rw_lnx_w, rw_lnx_b, lru_conv_w, lru_conv_b, lru_wa, lru_ba, lru_wx, lru_bx, lru_lambda, w_branch, w_out, ln1_w, ln1_b, peer_wq, peer_keys, peer_u, peer_v, ln2_w, ln2_b, ple_w, ple_gate_w, ple_gate_b):
    params = dict(w_in=w_in, ret_gn_w=ret_gn_w, ret_gn_b=ret_gn_b, hg_norm_w=hg_norm_w, rw_mu=rw_mu, rw_w0=rw_w0,
                  rw_w2=rw_w2, rw_a0=rw_a0, rw_a2=rw_a2, rw_g2=rw_g2, rw_kk=rw_kk, rw_ka=rw_ka, rw_rk=rw_rk,
                  rw_lnx_w=rw_lnx_w, rw_lnx_b=rw_lnx_b, lru_conv_w=lru_conv_w, lru_conv_b=lru_conv_b,
                  lru_wa=lru_wa, lru_ba=lru_ba, lru_wx=lru_wx, lru_bx=lru_bx, lru_lambda=lru_lambda,
                  w_branch=w_branch, w_out=w_out, ln1_w=ln1_w, ln1_b=ln1_b, peer_wq=peer_wq, peer_keys=peer_keys,
                  peer_u=peer_u, peer_v=peer_v, ln2_w=ln2_w, ln2_b=ln2_b, ple_w=ple_w, ple_gate_w=ple_gate_w,
                  ple_gate_b=ple_gate_b)
    lb_cum = jnp.cumsum(jax.nn.softmax(hg_lb.astype(F32), axis=0), axis=0)
    lb_all = lb_cum - lb_cum[0:1]
    t_prompt = x_prompt.shape[1]
    past_len = 16384
    cos_t, sin_t = _rope_tables(jnp.arange(t_prompt))
    cos_s, sin_s = _rope_tables(past_len + jnp.arange(1))
    cos_c = _col(cos_s[0, :HEAD_DIM])
    sin_c = _col(sin_s[0, :HEAD_DIM])
    h_p, h_s = x_prompt, x_sample
    new_p, new_s = [], []
    for l in range(DEPTH):
        lp = _layer_params(params, l, lb_all)
        h_p, st_p = _prompt_layer(h_p, p_prompt[l], lp, cos_t, sin_t)
        st_in = (state_ret[l], state_hgrn[l], state_rwkv[l], state_shift[l], state_lru[l], state_conv[l])
        h_s, st_s = _sample_layer(h_s, p_sample[l], st_in, lp, cos_c, sin_c)
        new_p.append(st_p)
        new_s.append(st_s)
    outs_p = [jnp.stack(z) for z in zip(*new_p)]
    outs_s = [jnp.stack(z) for z in zip(*new_s)]
    return (h_p, h_s, *outs_p, *outs_s)
```

```python
import functools
import math

import jax
import jax.numpy as jnp
from jax import lax
from jax.experimental import pallas as pl
from jax.experimental.pallas import tpu as pltpu

F32 = jnp.float32
BF16 = jnp.bfloat16
HIGHEST = lax.Precision.HIGHEST

D_MODEL = 1024
BRANCH_W = 256
N_HEADS = 4
HEAD_DIM = 64
RET_LOG_GAMMA = tuple(math.log1p(-(2.0 ** (-5.0 - h))) for h in range(N_HEADS))
ROPE_BASE = 10000.0
RW_GN_EPS = 64e-5
LN_EPS = 1e-5
LRU_C = 8.0
CONV_W = 4
DEPTH = 2
ALPHA = (2 * DEPTH) ** 0.25
BRANCH_COLS = 3584
MIB = 1024 * 1024
VMEM_SMALL_MIB = 32
VMEM_WEIGHTS_MIB = 48
VMEM_DECODE_MIB = 40
VMEM_PEER_MIB = 56


def _cparams(semantics, vmem_mib):
    return pltpu.CompilerParams(dimension_semantics=semantics, vmem_limit_bytes=vmem_mib * MIB)


def _mm(a, b):
    return jnp.dot(a.astype(BF16), b.astype(BF16), preferred_element_type=F32)


def _mm_nt(a, b):
    return lax.dot_general(a.astype(BF16), b.astype(BF16), (((1,), (1,)), ((), ())), preferred_element_type=F32)


def _mm_tn(a, b):
    return lax.dot_general(a.astype(BF16), b.astype(BF16), (((0,), (0,)), ((), ())), preferred_element_type=F32)


def _transpose_exact(a):
    n = a.shape[0]
    eye = (lax.broadcasted_iota(jnp.int32, (n, n), 0) == lax.broadcasted_iota(jnp.int32, (n, n), 1)).astype(F32)
    return lax.dot_general(a, eye, (((0,), (0,)), ((), ())), preferred_element_type=F32, precision=HIGHEST)


def _select_rows_exact(sel, x):
    x1 = x.astype(BF16)
    r1 = x - x1.astype(F32)
    x2 = r1.astype(BF16)
    x3 = (r1 - x2.astype(F32)).astype(BF16)
    dot = lambda p: jnp.dot(sel, p, preferred_element_type=F32)
    return (dot(x1) + dot(x2)) + dot(x3)


def _tril_ones(n):
    r = lax.broadcasted_iota(jnp.int32, (n, n), 0)
    c = lax.broadcasted_iota(jnp.int32, (n, n), 1)
    return (r >= c).astype(BF16)


def _sigmoid(x):
    return 1.0 / (1.0 + jnp.exp(-x))


def _silu(x):
    return x * _sigmoid(x)


def _gelu(x):
    c = 0.7978845608028654
    half = 0.5 * x
    return half + half * jnp.tanh(x * (c + (c * 0.044715) * (x * x)))


def _softplus(x):
    return jnp.maximum(x, 0.0) + jnp.log1p(jnp.exp(-jnp.abs(x)))


def _head(x, h):
    return x[:, h * HEAD_DIM:(h + 1) * HEAD_DIM]


def _project_rows(x_ref, w_ref, nb, tc):
    x = x_ref[...].reshape(nb * tc, x_ref.shape[-1]).astype(BF16)
    cols = jnp.dot(x, w_ref[...], preferred_element_type=F32)
    return [cols[n * tc:(n + 1) * tc] for n in range(nb)]


def _block_diag(w):
    n, c, d = w.shape
    eye = jnp.eye(n, dtype=w.dtype)
    return (eye[:, None, :, None] * w[:, :, None, :]).reshape(n * c, n * d)


def _rope_tables(pos):
    half = HEAD_DIM // 2
    inv_freq = ROPE_BASE ** (-jnp.arange(half, dtype=F32) / half)
    ang = pos.astype(F32)[:, None] * inv_freq[None, :]
    cos = jnp.cos(ang)
    sin = jnp.sin(ang)
    cos_t = jnp.tile(jnp.concatenate([cos, cos], axis=-1), (1, N_HEADS))
    sin_t = jnp.tile(jnp.concatenate([-sin, sin], axis=-1), (1, N_HEADS))
    return cos_t, sin_t


def _layer_params(p, l, lb_all):
    r2 = lambda a: a.reshape(1, -1)
    w_in = p['w_in'][l]
    lp = {
        'w_in_br': w_in[:, :BRANCH_COLS].astype(BF16),
        'w_gate': w_in[:, BRANCH_COLS:].astype(BF16),
        'ret_gn_w': r2(p['ret_gn_w'][l]), 'ret_gn_b': r2(p['ret_gn_b'][l]),
        'hg_lb': r2(lb_all[l]), 'hg_norm_w': r2(p['hg_norm_w'][l]),
        'rw_mu': r2(p['rw_mu'][l]), 'rw_w0': r2(p['rw_w0'][l]), 'rw_w2': p['rw_w2'][l].astype(BF16),
        'rw_a0': r2(p['rw_a0'][l]), 'rw_a2': p['rw_a2'][l].astype(BF16), 'rw_g2': p['rw_g2'][l].astype(BF16),
        'rw_kk': r2(p['rw_kk'][l]), 'rw_ka': r2(p['rw_ka'][l]), 'rw_rk': r2(p['rw_rk'][l]),
        'rw_lnx_w': r2(p['rw_lnx_w'][l]), 'rw_lnx_b': r2(p['rw_lnx_b'][l]),
        'lru_conv_w': p['lru_conv_w'][l], 'lru_conv_b': r2(p['lru_conv_b'][l]),
        'lru_wa_bd': _block_diag(p['lru_wa'][l]).astype(BF16), 'lru_ba': r2(p['lru_ba'][l]),
        'lru_wx_bd': _block_diag(p['lru_wx'][l]).astype(BF16), 'lru_bx': r2(p['lru_bx'][l]),
        'lru_lambda': r2(p['lru_lambda'][l]),
        'w_branch': p['w_branch'][l].astype(BF16), 'w_out': p['w_out'][l].astype(BF16),
        'ln1_w': r2(p['ln1_w'][l]), 'ln1_b': r2(p['ln1_b'][l]),
        'peer_wq_t': p['peer_wq'][l].T.astype(BF16),
        'peer_keys': p['peer_keys'][l].reshape(16, 128, 128).astype(BF16),
        'peer_u': p['peer_u'][l].astype(BF16),
        'peer_raw': (p['peer_wq'][l], p['peer_keys'][l], p['peer_u'], p['peer_v'], l),
        'peer_v_t': jnp.swapaxes(p['peer_v'][l].astype(BF16).reshape(-1, PEER_EB, D_MODEL), 1, 2),
        'ln2_w': r2(p['ln2_w'][l]), 'ln2_b': r2(p['ln2_b'][l]),
        'ple_w': p['ple_w'][l].astype(BF16), 'ple_gate_w': p['ple_gate_w'][l].astype(BF16),
        'ple_gate_b': r2(p['ple_gate_b'][l]),
    }
    return lp


def _proj_kernel(x_ref, w_ref, o_ref):
    o_ref[...] = jnp.dot(x_ref[...].astype(BF16), w_ref[...], preferred_element_type=F32)


def _proj(x, w_bf16, tn):
    n, k = x.shape
    m = w_bf16.shape[1]
    return pl.pallas_call(
        _proj_kernel,
        grid=(n // tn,),
        in_specs=[pl.BlockSpec((tn, k), lambda i: (i, 0)),
                  pl.BlockSpec((k, m), lambda i: (0, 0))],
        out_specs=pl.BlockSpec((tn, m), lambda i: (i, 0)),
        out_shape=jax.ShapeDtypeStruct((n, m), F32),
        compiler_params=_cparams(("parallel",), VMEM_WEIGHTS_MIB),
        name="in_proj",
    )(x, w_bf16)


def _lru_gates(xc, wa_ref, ba_ref, wx_ref, bx_ref, lam_ref):
    r = _sigmoid(_mm(xc, wa_ref[...]) + ba_ref[...])
    i = _sigmoid(_mm(xc, wx_ref[...]) + bx_ref[...])
    log_a = -LRU_C * r * _softplus(-lam_ref[...])
    a = jnp.exp(log_a)
    u = jnp.sqrt(1.0 - jnp.exp(2.0 * log_a)) * (i * xc)
    return a, u


def _lru_kernel(x_ref, w_ref, cw_ref, cb_ref, wa_ref, ba_ref, wx_ref, bx_ref, lam_ref,
                y_ref, h_ref, conv_ref, xbuf, hcar, a_s, u_s, hs, *, tc, nb):
    c = pl.program_id(1)

    @pl.when(c == 0)
    def _():
        for n in range(nb):
            xbuf[n, 0:8, :] = jnp.zeros((8, BRANCH_W), F32)
        hcar[...] = jnp.zeros(hcar.shape, F32)

    cols = _project_rows(x_ref, w_ref, nb, tc)
    gates = []
    for n in range(nb):
        xb = cols[n][:, 0:BRANCH_W]
        gates.append(cols[n][:, BRANCH_W:2 * BRANCH_W])
        xbuf[n, 8:8 + tc, :] = xb
        xc = (xbuf[n, pl.ds(5, tc), :] * cw_ref[0:1, :] + xbuf[n, pl.ds(6, tc), :] * cw_ref[1:2, :]
              + xbuf[n, pl.ds(7, tc), :] * cw_ref[2:3, :] + xb * cw_ref[3:4, :]) + cb_ref[...]
        a, u = _lru_gates(xc, wa_ref, ba_ref, wx_ref, bx_ref, lam_ref)
        a_s[n] = a
        u_s[n] = u

    def body(t, hcur):
        out = []
        for n in range(nb):
            h = a_s[n, pl.ds(t, 1), :] * hcur[n] + u_s[n, pl.ds(t, 1), :]
            hs[n, pl.ds(t, 1), :] = h
            out.append(h)
        return tuple(out)

    hfin = lax.fori_loop(0, tc, body, tuple(hcar[n] for n in range(nb)), unroll=8)
    for n in range(nb):
        hcar[n] = hfin[n]
        y_ref[n] = hs[n] * _gelu(gates[n])
        xbuf[n, 0:8, :] = xbuf[n, tc:tc + 8, :]

    @pl.when(c == pl.num_programs(1) - 1)
    def _():
        for n in range(nb):
            h_ref[n] = hfin[n]
            conv_ref[n] = xbuf[n, 5:8, :]


def _lru_prompt(x3, lp, tc=256, nb=4):
    b, t, d = x3.shape
    nb = min(nb, b)
    row = lambda: pl.BlockSpec((1, BRANCH_W), lambda i, j: (0, 0))
    full = lambda r: pl.BlockSpec((r, BRANCH_W), lambda i, j: (0, 0))
    return pl.pallas_call(
        functools.partial(_lru_kernel, tc=tc, nb=nb),
        grid=(b // nb, t // tc),
        in_specs=[pl.BlockSpec((nb, tc, d), lambda i, j: (i, j, 0)),
                  pl.BlockSpec((d, 2 * BRANCH_W), lambda i, j: (0, 6)),
                  full(CONV_W), row(), full(BRANCH_W), row(), full(BRANCH_W), row(), row()],
        out_specs=[pl.BlockSpec((nb, tc, BRANCH_W), lambda i, j: (i, j, 0)),
                   pl.BlockSpec((nb, 1, BRANCH_W), lambda i, j: (i, 0, 0)),
                   pl.BlockSpec((nb, CONV_W - 1, BRANCH_W), lambda i, j: (i, 0, 0))],
        out_shape=[jax.ShapeDtypeStruct((b, t, BRANCH_W), F32),
                   jax.ShapeDtypeStruct((b, 1, BRANCH_W), F32),
                   jax.ShapeDtypeStruct((b, CONV_W - 1, BRANCH_W), F32)],
        scratch_shapes=[pltpu.VMEM((nb, tc + 8, BRANCH_W), F32), pltpu.VMEM((nb, 1, BRANCH_W), F32),
                        pltpu.VMEM((nb, tc, BRANCH_W), F32), pltpu.VMEM((nb, tc, BRANCH_W), F32),
                        pltpu.VMEM((nb, tc, BRANCH_W), F32)],
        compiler_params=_cparams(("parallel", "arbitrary"), VMEM_SMALL_MIB),
        name="lru_prompt",
    )(x3, lp['w_in_br'], lp['lru_conv_w'], lp['lru_conv_b'], lp['lru_wa_bd'], lp['lru_ba'], lp['lru_wx_bd'],
      lp['lru_bx'], lp['lru_lambda'])


def _rotary(x, cos, sin_signed, first_half):
    partner = jnp.where(first_half, pltpu.roll(x, BRANCH_W - 32, 1), pltpu.roll(x, 32, 1))
    return x * cos + partner * sin_signed


def _ret_kernel(x_ref, w_ref, cos_ref, sin_ref, gw_ref, gb_ref, o_ref, s_ref, S, *, tc, nb):
    c = pl.program_id(1)

    @pl.when(c == 0)
    def _():
        S[...] = jnp.zeros(S.shape, F32)

    lane = lax.broadcasted_iota(jnp.int32, (tc, BRANCH_W), 1)
    first_half = (lane % HEAD_DIM) < (HEAD_DIM // 2)
    cos = cos_ref[...]
    sin = sin_ref[...]
    row = lax.broadcasted_iota(jnp.int32, (tc, tc), 0)
    col = lax.broadcasted_iota(jnp.int32, (tc, tc), 1)
    causal = row >= col
    dist = jnp.where(causal, row - col, 0).astype(F32)
    tpos = lax.broadcasted_iota(jnp.int32, (tc, HEAD_DIM), 0).astype(F32)
    decay = [jnp.where(causal, jnp.exp(dist * RET_LOG_GAMMA[h]), 0.0) for h in range(N_HEADS)]
    q_in = [jnp.exp((tpos + 1.0) * RET_LOG_GAMMA[h]) for h in range(N_HEADS)]
    k_out = [jnp.exp((tc - 1.0 - tpos) * RET_LOG_GAMMA[h]) for h in range(N_HEADS)]
    cols = _project_rows(x_ref, w_ref, nb, tc)
    q = [_rotary(cols[n][:, 0:256], cos, sin, first_half) for n in range(nb)]
    k = [_rotary(cols[n][:, 256:512], cos, sin, first_half) * (HEAD_DIM ** -0.5) for n in range(nb)]
    chains = [(n, h) for n in range(nb) for h in range(N_HEADS)]
    scores = {(n, h): _mm_nt(_head(q[n], h), _head(k[n], h)) * decay[h] for n, h in chains}
    out = {(n, h): _mm(scores[n, h], _head(cols[n][:, 512:768], h)) + _mm(_head(q[n], h) * q_in[h], S[n, h])
           for n, h in chains}
    for n, h in chains:
        S[n, h] = (math.exp(tc * RET_LOG_GAMMA[h]) * S[n, h]
                   + _mm_tn(_head(k[n], h) * k_out[h], _head(cols[n][:, 512:768], h)))
    for n, h in chains:
        oh = out[n, h]
        mu = jnp.mean(oh, axis=-1, keepdims=True)
        d = oh - mu
        var = jnp.mean(d * d, axis=-1, keepdims=True)
        on = d * lax.rsqrt(var + LN_EPS) * _head(gw_ref[...], h) + _head(gb_ref[...], h)
        o_ref[n, :, h * HEAD_DIM:(h + 1) * HEAD_DIM] = on * _silu(_head(cols[n][:, 768:1024], h))

    @pl.when(c == pl.num_programs(1) - 1)
    def _():
        s_ref[...] = S[...]


def _ret_prompt(x3, cos_t, sin_t, lp, tc=128, nb=2):
    b, t, d = x3.shape
    nb = min(nb, b)
    row = lambda: pl.BlockSpec((1, BRANCH_W), lambda i, j: (0, 0))
    return pl.pallas_call(
        functools.partial(_ret_kernel, tc=tc, nb=nb),
        grid=(b // nb, t // tc),
        in_specs=[pl.BlockSpec((nb, tc, d), lambda i, j: (i, j, 0)),
                  pl.BlockSpec((d, 1024), lambda i, j: (0, 0)),
                  pl.BlockSpec((tc, BRANCH_W), lambda i, j: (j, 0)),
                  pl.BlockSpec((tc, BRANCH_W), lambda i, j: (j, 0)), row(), row()],
        out_specs=[pl.BlockSpec((nb, tc, BRANCH_W), lambda i, j: (i, j, 0)),
                   pl.BlockSpec((nb, N_HEADS, HEAD_DIM, HEAD_DIM), lambda i, j: (i, 0, 0, 0))],
        out_shape=[jax.ShapeDtypeStruct((b, t, BRANCH_W), F32),
                   jax.ShapeDtypeStruct((b, N_HEADS, HEAD_DIM, HEAD_DIM), F32)],
        scratch_shapes=[pltpu.VMEM((nb, N_HEADS, HEAD_DIM, HEAD_DIM), F32)],
        compiler_params=_cparams(("parallel", "arbitrary"), VMEM_SMALL_MIB),
        name="ret_prompt",
    )(x3, lp['w_in_br'], cos_t, sin_t, lp['ret_gn_w'], lp['ret_gn_b'])


HGRN_CHUNK = 64
HGRN_LEVELS = tuple(HGRN_CHUNK >> l for l in range(1, 7))


def _hgrn_select_matrix():
    t = jnp.arange(HGRN_CHUNK)
    s = jnp.arange(HGRN_CHUNK)
    blocks = [s[None, :] <= t[:, None]]
    for m in HGRN_LEVELS:
        mid = (t // (2 * m)) * (2 * m) + m - 1
        blocks.append(s[None, :] <= mid[:, None])
    return jnp.concatenate(blocks, axis=0).astype(BF16)


def _hgrn_kernel(x_ref, w_ref, lb_ref, nw_ref, sel_ref, o_ref, s_ref, St, *, tc, nb):
    c = pl.program_id(1)

    @pl.when(c == 0)
    def _():
        St[...] = jnp.zeros(St.shape, F32)

    lb = lb_ref[...]
    row = lax.broadcasted_iota(jnp.int32, (tc, BRANCH_W), 0)
    r64 = lax.broadcasted_iota(jnp.int32, (tc, tc), 0)
    c64 = lax.broadcasted_iota(jnp.int32, (tc, tc), 1)
    tok = []
    all_cols = _project_rows(x_ref, w_ref, nb, tc)
    for n in range(nb):
        cols = all_cols[n]
        q = _silu(cols[:, 0:256])
        forget = lb + (1.0 - lb) * _sigmoid(cols[:, 256:512])
        kk = 1.0 - forget
        sums = _select_rows_exact(sel_ref[...], jnp.log(forget))
        tok.append(dict(q=q, kk=kk, sums=sums, b=sums[0:tc],
                        v=cols[:, 512:768], g=cols[:, 768:1024]))
    chains = [(n, h) for n in range(nb) for h in range(N_HEADS)]
    scores = {(n, h): jnp.where(r64 == c64, _mm_nt(_head(tok[n]['q'], h), _head(tok[n]['kk'], h)), 0.0)
              for n, h in chains}
    for lvl, m in enumerate(HGRN_LEVELS):
        sh = m.bit_length() - 1
        upper = ((row >> sh) & 1) == 1
        same = (r64 >> (sh + 1)) == (c64 >> (sh + 1))
        scaled = []
        for n in range(nb):
            t = tok[n]
            e = jnp.exp(-jnp.abs(t['b'] - t['sums'][(lvl + 1) * tc:(lvl + 2) * tc]))
            scaled.append((jnp.where(upper, t['q'] * e, 0.0), jnp.where(upper, 0.0, t['kk'] * e)))
        for n, h in chains:
            scores[n, h] = scores[n, h] + jnp.where(same, _mm_nt(_head(scaled[n][0], h), _head(scaled[n][1], h)), 0.0)
    for n in range(nb):
        t = tok[n]
        b = t['b']
        v = t['v']
        g = t['g']
        b_end = b[tc - 1:tc, :]
        qd = t['q'] * jnp.exp(b)
        kd = t['kk'] * jnp.exp(b_end - b)
        for h in range(N_HEADS):
            vh = _head(v, h)
            oh = _mm(scores[n, h], vh) + _mm_nt(_head(qd, h), St[n, h])
            St[n, h] = St[n, h] * jnp.exp(_head(b_end, h)) + _mm_tn(vh, _head(kd, h))
            ms = jnp.mean(oh * oh, axis=-1, keepdims=True)
            on = oh * lax.rsqrt(ms + LN_EPS) * _head(nw_ref[...], h)
            o_ref[n, :, h * HEAD_DIM:(h + 1) * HEAD_DIM] = on * _silu(_head(g, h))

    @pl.when(c == pl.num_programs(1) - 1)
    def _():
        for n in range(nb):
            for h in range(N_HEADS):
                s_ref[n, h] = _transpose_exact(St[n, h])


def _hgrn_prompt(x3, lb, lp, nb=8):
    b, t, d = x3.shape
    nb = min(nb, b)
    tc = HGRN_CHUNK
    row = lambda: pl.BlockSpec((1, BRANCH_W), lambda i, j: (0, 0))
    return pl.pallas_call(
        functools.partial(_hgrn_kernel, tc=tc, nb=nb),
        grid=(b // nb, t // tc),
        in_specs=[pl.BlockSpec((nb, tc, d), lambda i, j: (i, j, 0)),
                  pl.BlockSpec((d, 1024), lambda i, j: (0, 1)), row(), row(),
                  pl.BlockSpec((7 * tc, tc), lambda i, j: (0, 0))],
        out_specs=[pl.BlockSpec((nb, tc, BRANCH_W), lambda i, j: (i, j, 0)),
                   pl.BlockSpec((nb, N_HEADS, HEAD_DIM, HEAD_DIM), lambda i, j: (i, 0, 0, 0))],
        out_shape=[jax.ShapeDtypeStruct((b, t, BRANCH_W), F32),
                   jax.ShapeDtypeStruct((b, N_HEADS, HEAD_DIM, HEAD_DIM), F32)],
        scratch_shapes=[pltpu.VMEM((nb, N_HEADS, HEAD_DIM, HEAD_DIM), F32)],
        compiler_params=_cparams(("parallel", "arbitrary"), VMEM_SMALL_MIB),
        name="hgrn_prompt",
    )(x3, lp['w_in_br'], lb, lp['hg_norm_w'], _hgrn_select_matrix())


def _rwkv_token_mix(cols, prev, mu_ref, w0_ref, w2_ref, a0_ref, a2_ref, g2_ref, kkw_ref, ka_ref):
    xs = cols + mu_ref[...] * (prev - cols)
    r = xs[:, 0:256]
    k = xs[:, 256:512]
    v = xs[:, 512:768]
    xg = xs[:, 768:896]
    xw = xs[:, 896:960]
    xa = xs[:, 960:1024]
    w = -_softplus(-(w0_ref[...] + _mm(jnp.tanh(xw), w2_ref[...]))) - 0.5
    lw = -jnp.exp(w)
    a = _sigmoid(a0_ref[...] + _mm(xa, a2_ref[...]))
    g = _mm(_sigmoid(xg), g2_ref[...])
    kk = k * kkw_ref[...]
    parts = []
    for h in range(N_HEADS):
        kh = _head(kk, h)
        nrm = jnp.sqrt(jnp.sum(kh * kh, axis=-1, keepdims=True))
        parts.append(kh / jnp.maximum(nrm, 1e-12))
    kkn = jnp.concatenate(parts, axis=-1)
    k2 = k * (1.0 + (a - 1.0) * ka_ref[...])
    return r, lw, k2, v, kkn, a, g


def _rwkv_out(o, r, k2, v, g, rk_ref, lw_ref, lb_ref, h):
    mu = jnp.mean(o, axis=-1, keepdims=True)
    d = o - mu
    var = jnp.mean(d * d, axis=-1, keepdims=True)
    on = d * lax.rsqrt(var + RW_GN_EPS) * _head(lw_ref[...], h) + _head(lb_ref[...], h)
    bonus = jnp.sum(_head(r, h) * _head(k2, h) * _head(rk_ref[...], h), axis=-1, keepdims=True) * _head(v, h)
    return (on + bonus) * _head(g, h)


def _rwkv_kernel(x_ref, w_ref, mu_ref, w0_ref, w2_ref, a0_ref, a2_ref, g2_ref, kkw_ref, ka_ref, rk_ref,
                 lnw_ref, lnb_ref, y_ref, shift_ref, s_ref, S, last_row, *, tc, nb):
    c = pl.program_id(1)

    @pl.when(c == 0)
    def _():
        S[...] = jnp.zeros(S.shape, F32)
        last_row[...] = jnp.zeros(last_row.shape, F32)

    rr = lax.broadcasted_iota(jnp.int32, (tc, tc), 0)
    cc = lax.broadcasted_iota(jnp.int32, (tc, tc), 1)
    strict = rr > cc
    incl2 = (lax.broadcasted_iota(jnp.int32, (tc, 2 * tc), 0)
             >= (lax.broadcasted_iota(jnp.int32, (tc, 2 * tc), 1) & (tc - 1)))
    tril = _tril_ones(tc)
    n_double = max(1, (tc - 1).bit_length())
    tok = []
    new_shift = []
    all_cols = _project_rows(x_ref, w_ref, nb, tc)
    for n in range(nb):
        cols = all_cols[n]
        new_shift.append(cols[tc - 1:tc, :])
        rowi = lax.broadcasted_iota(jnp.int32, cols.shape, 0)
        prev = jnp.where(rowi == 0, last_row[n], pltpu.roll(cols, 1, 0))
        last_row[n] = cols[tc - 1:tc, :]
        r, lw, k2, v, kkn, a, g = _rwkv_token_mix(cols, prev, mu_ref, w0_ref, w2_ref, a0_ref, a2_ref, g2_ref,
                                                  kkw_ref, ka_ref)
        am = -kkn
        bm = kkn * a
        G = _select_rows_exact(tril, lw)
        g_end = G[tc - 1:tc, :]
        einv = jnp.exp(-G)
        eend = jnp.exp(g_end - G)
        tok.append(dict(r=r, k2=k2, v=v, g=g, g_end=g_end, at=am * jnp.exp(G - lw), rt=r * jnp.exp(G),
                        bt=bm * einv, kt=k2 * einv, bbar=bm * eend, kbar=k2 * eend))
    chains = [(n, h) for n in range(nb) for h in range(N_HEADS)]
    lhs = {(n, h): jnp.concatenate([_head(tok[n]['at'], h), _head(tok[n]['rt'], h)], axis=0) for n, h in chains}
    inter = {(n, h): _mm_nt(lhs[n, h], jnp.concatenate([_head(tok[n]['bt'], h), _head(tok[n]['kt'], h)], axis=0))
             for n, h in chains}
    from_state = {(n, h): _mm_nt(lhs[n, h], S[n, h]) for n, h in chains}
    npow = {ch: jnp.where(strict, inter[ch][0:tc, 0:tc], 0.0) for ch in chains}
    u = {(n, h): from_state[n, h][0:tc]
         + _mm(jnp.where(strict, inter[n, h][0:tc, tc:2 * tc], 0.0), _head(tok[n]['v'], h)) for n, h in chains}
    for j in range(n_double):
        u = {ch: u[ch] + _mm(npow[ch], u[ch]) for ch in chains}
        if j + 1 < n_double:
            npow = {ch: _mm(npow[ch], npow[ch]) for ch in chains}
    for n, h in chains:
        t = tok[n]
        uv = jnp.concatenate([u[n, h], _head(t['v'], h)], axis=0)
        a_r = jnp.where(incl2, inter[n, h][tc:2 * tc, :], 0.0)
        o = from_state[n, h][tc:2 * tc] + _mm(a_r, uv)
        S[n, h] = (S[n, h] * jnp.exp(_head(t['g_end'], h))
                   + _mm_tn(uv, jnp.concatenate([_head(t['bbar'], h), _head(t['kbar'], h)], axis=0)))
        y_ref[n, :, h * HEAD_DIM:(h + 1) * HEAD_DIM] = _rwkv_out(o, t['r'], t['k2'], t['v'], t['g'], rk_ref,
                                                                 lnw_ref, lnb_ref, h)

    @pl.when(c == pl.num_programs(1) - 1)
    def _():
        s_ref[...] = S[...]
        for n in range(nb):
            shift_ref[n] = new_shift[n]


def _rwkv_prompt(x3, lp, tc=64, nb=4):
    b, t, d = x3.shape
    nb = min(nb, b)
    row = lambda w: pl.BlockSpec((1, w), lambda i, j: (0, 0))
    mat = lambda r, w: pl.BlockSpec((r, w), lambda i, j: (0, 0))
    return pl.pallas_call(
        functools.partial(_rwkv_kernel, tc=tc, nb=nb),
        grid=(b // nb, t // tc),
        in_specs=[pl.BlockSpec((nb, tc, d), lambda i, j: (i, j, 0)),
                  pl.BlockSpec((d, 1024), lambda i, j: (0, 2)),
                  row(1024), row(256), mat(64, 256), row(256), mat(64, 256), mat(128, 256),
                  row(256), row(256), row(256), row(256), row(256)],
        out_specs=[pl.BlockSpec((nb, tc, BRANCH_W), lambda i, j: (i, j, 0)),
                   pl.BlockSpec((nb, 1, 1024), lambda i, j: (i, 0, 0)),
                   pl.BlockSpec((nb, N_HEADS, HEAD_DIM, HEAD_DIM), lambda i, j: (i, 0, 0, 0))],
        out_shape=[jax.ShapeDtypeStruct((b, t, BRANCH_W), F32),
                   jax.ShapeDtypeStruct((b, 1, 1024), F32),
                   jax.ShapeDtypeStruct((b, N_HEADS, HEAD_DIM, HEAD_DIM), F32)],
        scratch_shapes=[pltpu.VMEM((nb, N_HEADS, HEAD_DIM, HEAD_DIM), F32), pltpu.VMEM((nb, 1, 1024), F32)],
        compiler_params=_cparams(("parallel", "arbitrary"), VMEM_SMALL_MIB),
        name="rwkv_prompt",
    )(x3, lp['w_in_br'], lp['rw_mu'], lp['rw_w0'], lp['rw_w2'], lp['rw_a0'], lp['rw_a2'], lp['rw_g2'],
      lp['rw_kk'], lp['rw_ka'], lp['rw_rk'], lp['rw_lnx_w'], lp['rw_lnx_b'])


def _layer_norm(z, w, b):
    mu = jnp.mean(z, axis=-1, keepdims=True)
    d = z - mu
    var = jnp.mean(d * d, axis=-1, keepdims=True)
    return d * lax.rsqrt(var + LN_EPS) * w + b


def _mix_kernel(x_ref, oa_ref, ob_ref, oc_ref, od_ref, wg_ref, wb_ref, wo_ref, lw_ref, lb_ref, out_ref):
    x = x_ref[...]
    xb = x.astype(BF16)
    mixed = None
    for gi, o_ref in enumerate((oa_ref, ob_ref, oc_ref, od_ref)):
        gate = _sigmoid(jnp.dot(xb, wg_ref[:, gi * D_MODEL:(gi + 1) * D_MODEL], preferred_element_type=F32))
        up = jnp.dot(o_ref[...].astype(BF16), wb_ref[gi], preferred_element_type=F32)
        mixed = gate * up if mixed is None else mixed + gate * up
    y = jnp.dot(mixed.astype(BF16), wo_ref[...], preferred_element_type=F32)
    out_ref[...] = _layer_norm(ALPHA * x + y, lw_ref[...], lb_ref[...])


def _mix(x, outs, lp, tn):
    n = x.shape[0]
    tok = lambda w: pl.BlockSpec((tn, w), lambda i: (i, 0))
    const = lambda *s: pl.BlockSpec(s, lambda i: (0,) * len(s))
    return pl.pallas_call(
        _mix_kernel,
        grid=(n // tn,),
        in_specs=[tok(D_MODEL), tok(BRANCH_W), tok(BRANCH_W), tok(BRANCH_W), tok(BRANCH_W),
                  const(D_MODEL, 4 * D_MODEL), const(4, BRANCH_W, D_MODEL), const(D_MODEL, D_MODEL),
                  const(1, D_MODEL), const(1, D_MODEL)],
        out_specs=tok(D_MODEL),
        out_shape=jax.ShapeDtypeStruct((n, D_MODEL), F32),
        compiler_params=_cparams(("parallel",), VMEM_WEIGHTS_MIB),
        name="mix_ln1",
    )(x, *outs, lp['w_gate'], lp['w_branch'], lp['w_out'], lp['ln1_w'], lp['ln1_b'])


def _ple_kernel(x_ref, p_ref, wg_ref, bg_ref, wp_ref, out_ref):
    x = x_ref[...]
    gate = _sigmoid(jnp.dot(x.astype(BF16), wg_ref[...], preferred_element_type=F32) + bg_ref[...])
    emb = jnp.dot(p_ref[...].astype(BF16), wp_ref[...], preferred_element_type=F32)
    out_ref[...] = x + gate * emb


def _ple(x, p_emb, lp, tn):
    n = x.shape[0]
    tok = lambda w: pl.BlockSpec((tn, w), lambda i: (i, 0))
    const = lambda *s: pl.BlockSpec(s, lambda i: (0,) * len(s))
    return pl.pallas_call(
        _ple_kernel,
        grid=(n // tn,),
        in_specs=[tok(D_MODEL), tok(256), const(D_MODEL, D_MODEL), const(1, D_MODEL), const(256, D_MODEL)],
        out_specs=tok(D_MODEL),
        out_shape=jax.ShapeDtypeStruct((n, D_MODEL), F32),
        compiler_params=_cparams(("parallel",), VMEM_SMALL_MIB),
        name="ple_gate",
    )(x, p_emb, lp['ple_gate_w'], lp['ple_gate_b'], lp['ple_w'])


PEER_HEADS = 8
PEER_NKEYS = 128
PEER_TOPK = 16
PEER_EB = 1024
PEER_SUB = 2
PEER_TIE_SLOTS = 8
PEER_IGROUP = 2
PEER_JROWS = 64


def _oddeven_merge_sort_pairs(n):
    pairs = []
    p = 1
    while p < n:
        k = p
        while k >= 1:
            for j in range(k % p, n - k, 2 * k):
                for i in range(min(k, n - j - k)):
                    if (i + j) // (p * 2) == (i + j + k) // (p * 2):
                        pairs.append((i + j, i + j + k))
            k //= 2
        p *= 2
    return pairs


def _bitonic_merge_pairs(n):
    pairs = []
    k = n // 2
    while k >= 1:
        pairs.extend((i, i + k) for i in range(n) if (i & k) == 0)
        k //= 2
    return pairs


_SORT16 = _oddeven_merge_sort_pairs(PEER_TOPK)
_MERGE16 = _bitonic_merge_pairs(PEER_TOPK)
_CAND_LEN = tuple(PEER_TOPK // (a + 1) for a in range(PEER_TOPK))


def _network(vals, pairs):
    vals = list(vals)
    for i, j in pairs:
        hi = jnp.maximum(vals[i], vals[j])
        lo = jnp.minimum(vals[i], vals[j])
        vals[i], vals[j] = hi, lo
    return vals


def _top16_merge(x, y):
    return _network([jnp.maximum(x[i], y[PEER_TOPK - 1 - i]) for i in range(PEER_TOPK)], _MERGE16)


def _peer_head_stats(h, s_nat, e_nat, sk, th, tie, g_count):
    rows0 = pl.ds(pl.multiple_of(2 * h * PEER_NKEYS, PEER_NKEYS), PEER_NKEYS)
    rows1 = pl.ds(pl.multiple_of((2 * h + 1) * PEER_NKEYS, PEER_NKEYS), PEER_NKEYS)
    for p, rows in enumerate((rows0, rows1)):
        for g in range(4):
            gg = g % g_count
            sk[:, p * 4 + g, :] = s_nat[gg, rows, :]
    groups = []
    for m in range(PEER_NKEYS // PEER_TOPK):
        groups.append(_network([sk[PEER_TOPK * m + i] for i in range(PEER_TOPK)], _SORT16))
    while len(groups) > 1:
        groups = [_top16_merge(groups[i], groups[i + 1]) for i in range(0, len(groups), 2)]
    top = groups[0]
    low = lax.broadcasted_iota(jnp.int32, (8, 128), 0) < 4
    ta = [jnp.where(low, t, pltpu.roll(t, 4, 0)) for t in top]
    tb = [jnp.where(low, pltpu.roll(t, 4, 0), t) for t in top]
    cand = [[ta[a] + tb[b] for b in range(_CAND_LEN[a])] for a in range(PEER_TOPK)]
    m1 = _network(cand[1] + [cand[a][0] for a in range(PEER_TOPK - 1, 7, -1)], _MERGE16)
    m2 = _network(cand[2] + cand[3] + cand[4] + cand[5] + cand[6], _SORT16)
    t1 = _top16_merge(cand[0], m1)
    t2 = _top16_merge(t1, m2)
    t2[15] = jnp.maximum(t2[15], cand[7][0])
    t2[14] = jnp.maximum(t2[14], cand[7][1])
    theta = t2[0]
    for t in t2[1:]:
        theta = jnp.minimum(theta, t)
    cmax = cand[0][0]
    z = jnp.zeros((8, 128), F32)
    n_kept = jnp.zeros((8, 128), F32)
    for row in cand:
        for cv in row:
            keep = cv >= theta
            z = z + jnp.where(keep, jnp.exp(cv - cmax), 0.0)
            n_kept = n_kept + jnp.where(keep, 1.0, 0.0)
    inv_z = 1.0 / z
    cand_tie = jnp.where(n_kept != float(PEER_TOPK), 1.0, 0.0)
    last1_used = jnp.where(cand[0][PEER_TOPK - 1] >= theta, 1.0, 0.0)
    last0_used = jnp.where(cand[PEER_TOPK - 1][0] >= theta, 1.0, 0.0)
    tau_top = jnp.full((8, 128), jnp.inf, F32)
    for b in range(PEER_TOPK):
        tau_top = jnp.where(cand[0][b] >= theta, tb[b], tau_top)
    for g in range(g_count):
        lanes = slice(g * 128, (g + 1) * 128)
        th[h, 0:1, lanes] = theta[g:g + 1, :]
        th[h, 1:2, lanes] = ta[0][g:g + 1, :]
        th[h, 2:3, lanes] = tb[0][g:g + 1, :]
        th[h, 3:4, lanes] = inv_z[g:g + 1, :]
        th[h, 4:5, lanes] = ta[PEER_TOPK - 1][g:g + 1, :]
        th[h, 5:6, lanes] = cand_tie[g:g + 1, :]
        th[h, 6:7, lanes] = last0_used[g:g + 1, :]
        th[h, 7:8, lanes] = last1_used[g:g + 1, :]
        th[h, 8 + PEER_TOPK:9 + PEER_TOPK, lanes] = tau_top[g:g + 1, :]
        for b in range(PEER_TOPK):
            th[h, 8 + b:9 + b, lanes] = tb[b][g:g + 1, :]
    for g in range(g_count):
        lanes = slice(g * 128, (g + 1) * 128)
        s0 = s_nat[g, rows0, :]
        s1 = s_nat[g, rows1, :]
        e_nat[g, rows0, :] = jnp.exp(s0 - th[h, 1:2, lanes])
        e_nat[g, rows1, :] = jnp.exp(s1 - th[h, 2:3, lanes]) * th[h, 3:4, lanes]
        n0 = jnp.sum(jnp.where(s0 >= th[h, 4:5, lanes], 1.0, 0.0), axis=0, keepdims=True)
        n1 = jnp.sum(jnp.where(s1 >= th[h, 7 + PEER_TOPK:8 + PEER_TOPK, lanes], 1.0, 0.0), axis=0, keepdims=True)
        key_tie = jnp.maximum(jnp.where(n0 != float(PEER_TOPK), th[h, 6:7, lanes], 0.0),
                              jnp.where(n1 != float(PEER_TOPK), th[h, 7:8, lanes], 0.0))
        tie[0:1, lanes] = jnp.maximum(tie[0:1, lanes], jnp.maximum(key_tie, th[h, 5:6, lanes]))
        tau = jnp.full(s0.shape, jnp.inf, F32)
        for b in range(PEER_TOPK // 2):
            sb = th[h, 8 + b:9 + b, lanes]
            tau = jnp.where((s0 + sb) >= th[h, 0:1, lanes], sb, tau)
        tau = jnp.where(s0 == th[h, 1:2, lanes], th[h, 8 + PEER_TOPK:9 + PEER_TOPK, lanes], tau)
        s_nat[g, rows0, :] = tau


def _peer_kernel(x_ref, wq_ref, keys_ref, u_ref, vt_ref, lw_ref, lb_ref, out_ref, tie_ref,
                 xtb, s_nat, e_nat, sk, th, wact, wraw, actb, yt, tie, *, tn):
    j = pl.program_id(1)
    g_count = tn // 128

    @pl.when(j == 0)
    def _():
        xtb[...] = x_ref[...].T.astype(BF16)
        for hp in range(2 * PEER_HEADS):
            qt = jnp.dot(wq_ref[hp * 128:(hp + 1) * 128, :], xtb[...], preferred_element_type=F32)
            scores = jnp.dot(keys_ref[hp], qt.astype(BF16), preferred_element_type=F32)
            for g in range(g_count):
                s_nat[g, hp * PEER_NKEYS:(hp + 1) * PEER_NKEYS, :] = scores[:, g * 128:(g + 1) * 128]

        tie[...] = jnp.zeros(tie.shape, F32)

        def head_body(h, carry):
            _peer_head_stats(h, s_nat, e_nat, sk, th, tie, g_count)
            return carry

        lax.fori_loop(0, PEER_HEADS, head_body, 0)
        tie_ref[...] = tie[0:1, :]
        yt[...] = jnp.zeros(yt.shape, F32)

    last = pl.num_programs(1) - 1
    cur = j % 2

    def step(first_stage, second_stage):
        for sb in range(PEER_SUB):
            if first_stage:
                act = jnp.dot(u_ref[sb * PEER_EB:(sb + 1) * PEER_EB, :], xtb[...],
                              preferred_element_type=F32)
                for g in range(g_count):
                    actb[sb, g] = act[:, g * 128:(g + 1) * 128]
            if second_stage:
                yt[...] += jnp.dot(vt_ref[sb], wact[1 - cur, sb], preferred_element_type=F32)
            if not first_stage:
                continue
            i0 = pl.multiple_of((j * PEER_SUB + sb) * (PEER_EB // PEER_NKEYS), 8)
            jr = PEER_JROWS
            for g in range(g_count):
                for jq in range(PEER_NKEYS // jr):
                    for ig in range(PEER_EB // PEER_NKEYS // PEER_IGROUP):
                        acc = [jnp.zeros((jr, 128), F32) for _ in range(PEER_IGROUP)]
                        for h in range(PEER_HEADS):
                            base1 = (2 * h + 1) * PEER_NKEYS + jq * jr
                            s1 = s_nat[g, base1:base1 + jr, :]
                            e1 = e_nat[g, base1:base1 + jr, :]
                            tau0 = s_nat[g, pl.ds(2 * h * PEER_NKEYS + i0, 8), :]
                            e0 = e_nat[g, pl.ds(2 * h * PEER_NKEYS + i0, 8), :]
                            for k in range(PEER_IGROUP):
                                ii = PEER_IGROUP * ig + k
                                sel = s1 >= tau0[ii:ii + 1, :]
                                acc[k] = acc[k] + jnp.where(sel, e0[ii:ii + 1, :] * e1, 0.0)
                        for k in range(PEER_IGROUP):
                            r0 = (PEER_IGROUP * ig + k) * PEER_NKEYS + jq * jr
                            wraw[g, r0:r0 + jr, :] = acc[k]
            for g in range(g_count):
                wact[cur, sb, :, g * 128:(g + 1) * 128] = (wraw[g] * _gelu(actb[sb, g])).astype(BF16)

    @pl.when(j == 0)
    def _():
        step(True, False)

    @pl.when((j > 0) & (j < last))
    def _():
        step(True, True)

    @pl.when(j == last)
    def _():
        step(False, True)
        z = ALPHA * x_ref[...] + yt[...].T
        out_ref[...] = _layer_norm(z, lw_ref[...], lb_ref[...])


def _peer(x, lp, tn):
    n = x.shape[0]
    n_blk = lp['peer_u'].shape[0] // (PEER_EB * PEER_SUB)
    once = lambda *s: pl.BlockSpec(s, lambda i, j: (0,) * len(s), pipeline_mode=pl.Buffered(1))
    const = lambda *s: pl.BlockSpec(s, lambda i, j: (0,) * len(s))
    out, tie_flags = pl.pallas_call(
        functools.partial(_peer_kernel, tn=tn),
        grid=(n // tn, n_blk + 1),
        in_specs=[pl.BlockSpec((tn, D_MODEL), lambda i, j: (i, 0)),
                  once(2 * PEER_HEADS * 128, D_MODEL), once(2 * PEER_HEADS, PEER_NKEYS, 128),
                  pl.BlockSpec((PEER_SUB * PEER_EB, D_MODEL), lambda i, j: (jnp.minimum(j, n_blk - 1), 0)),
                  pl.BlockSpec((PEER_SUB, D_MODEL, PEER_EB), lambda i, j: (jnp.maximum(j - 1, 0), 0, 0)),
                  const(1, D_MODEL), const(1, D_MODEL)],
        out_specs=[pl.BlockSpec((tn, D_MODEL), lambda i, j: (i, 0)),
                   pl.BlockSpec((None, 1, tn), lambda i, j: (i, 0, 0))],
        out_shape=[jax.ShapeDtypeStruct((n, D_MODEL), F32), jax.ShapeDtypeStruct((n // tn, 1, tn), F32)],
        scratch_shapes=[pltpu.VMEM((D_MODEL, tn), BF16),
                        pltpu.VMEM((tn // 128, 2 * PEER_HEADS * PEER_NKEYS, 128), F32),
                        pltpu.VMEM((tn // 128, 2 * PEER_HEADS * PEER_NKEYS, 128), F32),
                        pltpu.VMEM((PEER_NKEYS, 8, 128), F32),
                        pltpu.VMEM((PEER_HEADS, 16 + PEER_TOPK, tn), F32),
                        pltpu.VMEM((2, PEER_SUB, PEER_EB, tn), BF16),
                        pltpu.VMEM((tn // 128, PEER_EB, 128), F32),
                        pltpu.VMEM((PEER_SUB, tn // 128, PEER_EB, 128), F32),
                        pltpu.VMEM((D_MODEL, tn), F32),
                        pltpu.VMEM((8, tn), F32)],
        compiler_params=_cparams(("parallel", "arbitrary"), VMEM_PEER_MIB),
        name="peer_ln2",
    )(x, lp['peer_wq_t'], lp['peer_keys'], lp['peer_u'], lp['peer_v_t'], lp['ln2_w'], lp['ln2_b'])
    flags = tie_flags.reshape(n) > 0.0
    count = jnp.sum(flags.astype(jnp.int32))

    def redo_flagged():
        idx = jnp.nonzero(flags, size=PEER_TIE_SLOTS, fill_value=n)[0]
        fixed = _peer_tie_fallback(x[jnp.minimum(idx, n - 1)], lp)
        return out.at[idx].set(fixed, mode='drop')

    return lax.cond(count > PEER_TIE_SLOTS, lambda: _peer_tie_fallback(x, lp),
                    lambda: lax.cond(count > 0, redo_flagged, lambda: out))


def _topk_lowest_index(v, k):
    n = v.shape[-1]
    pos = jnp.arange(n)
    ahead = (v[..., None, :] > v[..., :, None]) | ((v[..., None, :] == v[..., :, None]) & (pos[None, :] < pos[:, None]))
    rank = jnp.sum(ahead, axis=-1)
    onehot = rank[..., None, :] == jnp.arange(k)[:, None]
    vals = jnp.sum(jnp.where(onehot, v[..., None, :], 0.0), axis=-1)
    idx = jnp.sum(jnp.where(onehot, pos, 0), axis=-1)
    return vals, idx


def _peer_tie_fallback(x, lp):
    wq, keys, u_all, v_all, layer = lp['peer_raw']
    n_exp = u_all.shape[1]
    u_tab = u_all.reshape(-1, D_MODEL)
    v_tab = v_all.reshape(-1, D_MODEL)
    n = x.shape[0]
    blk = min(256, n)
    top_k = _topk_lowest_index if n <= PEER_TIE_SLOTS else lax.top_k

    def block(xb):
        q = (xb @ wq).reshape(blk, PEER_HEADS, 2, PEER_NKEYS)
        s = jnp.einsum('nhpc,hpkc->nhpk', q, keys).astype(F32)
        s_top, i_top = top_k(s, PEER_TOPK)
        cand = (s_top[:, :, 0, :, None] + s_top[:, :, 1, None, :]).reshape(blk, PEER_HEADS, -1)
        cand_idx = (i_top[:, :, 0, :, None] * PEER_NKEYS + i_top[:, :, 1, None, :]).reshape(blk, PEER_HEADS, -1)
        best, pick = top_k(cand, PEER_TOPK)
        eidx = jnp.take_along_axis(cand_idx, pick, axis=-1) + layer * n_exp
        gw = jax.nn.softmax(best, axis=-1)
        act = jax.nn.gelu(jnp.einsum('nhkd,nd->nhk', u_tab[eidx], xb).astype(F32))
        return jnp.einsum('nhk,nhkd->nd', gw * act, v_tab[eidx])

    y = lax.map(block, x.reshape(-1, blk, D_MODEL)).reshape(n, D_MODEL)
    return _layer_norm(ALPHA * x + y, lp['ln2_w'], lp['ln2_b'])


DEC_N = 128
HEAD_STATE = HEAD_DIM * HEAD_DIM


def _col(v):
    return jnp.broadcast_to(v.reshape(-1, 1), (v.size, DEC_N))


def _load_state_t(s_ref, st):
    st[...] = s_ref[...].T.reshape(HEAD_DIM, HEAD_DIM, DEC_N)


def _store_state_t(so_ref, st):
    so_ref[...] = st[...].reshape(HEAD_STATE, DEC_N).T


def _ret_dec_kernel(c_ref, cos_ref, sin_ref, gw_ref, gb_ref, s_ref, o_ref, so_ref, ct, qk, st):
    h = pl.program_id(0)
    ct[...] = c_ref[...].T
    r0 = pl.multiple_of(h * HEAD_DIM, HEAD_DIM)

    def rot(x):
        partner = jnp.concatenate([x[HEAD_DIM // 2:], x[:HEAD_DIM // 2]], axis=0)
        return x * cos_ref[...] + partner * sin_ref[...]

    qk[0] = rot(ct[pl.ds(r0, HEAD_DIM), :])
    qk[1] = rot(ct[pl.ds(256 + r0, HEAD_DIM), :]) * (HEAD_DIM ** -0.5)
    v = ct[pl.ds(512 + r0, HEAD_DIM), :]
    g = ct[pl.ds(768 + r0, HEAD_DIM), :]
    gamma = jnp.exp(jnp.zeros((1, 1), F32) + jnp.where(
        h == 0, RET_LOG_GAMMA[0], jnp.where(h == 1, RET_LOG_GAMMA[1],
                                            jnp.where(h == 2, RET_LOG_GAMMA[2], RET_LOG_GAMMA[3]))))
    _load_state_t(s_ref, st)

    def body(k, o):
        s_new = gamma * st[k] + qk[1, pl.ds(k, 1), :] * v
        st[k] = s_new
        return o + qk[0, pl.ds(k, 1), :] * s_new

    o = lax.fori_loop(0, HEAD_DIM, body, jnp.zeros((HEAD_DIM, DEC_N), F32))
    mu = jnp.mean(o, axis=0, keepdims=True)
    d = o - mu
    var = jnp.mean(d * d, axis=0, keepdims=True)
    o_ref[...] = (d * lax.rsqrt(var + LN_EPS) * gw_ref[...] + gb_ref[...]) * _silu(g)
    _store_state_t(so_ref, st)


def _hgrn_dec_kernel(c_ref, lb_ref, nw_ref, s_ref, o_ref, so_ref, ct, qk, st):
    h = pl.program_id(0)
    ct[...] = c_ref[...].T
    r0 = pl.multiple_of(h * HEAD_DIM, HEAD_DIM)
    lb = lb_ref[...]
    qk[0] = _silu(ct[pl.ds(r0, HEAD_DIM), :])
    forget = lb + (1.0 - lb) * _sigmoid(ct[pl.ds(256 + r0, HEAD_DIM), :])
    qk[1] = forget
    qk[2] = 1.0 - forget
    v = ct[pl.ds(512 + r0, HEAD_DIM), :]
    g = ct[pl.ds(768 + r0, HEAD_DIM), :]
    _load_state_t(s_ref, st)

    def body(k, o):
        s_new = qk[1, pl.ds(k, 1), :] * st[k] + qk[2, pl.ds(k, 1), :] * v
        st[k] = s_new
        return o + qk[0, pl.ds(k, 1), :] * s_new

    o = lax.fori_loop(0, HEAD_DIM, body, jnp.zeros((HEAD_DIM, DEC_N), F32))
    ms = jnp.mean(o * o, axis=0, keepdims=True)
    o_ref[...] = o * lax.rsqrt(ms + LN_EPS) * nw_ref[...] * _silu(g)
    _store_state_t(so_ref, st)


def _rwkv_dec_kernel(c_ref, sh_ref, mu_ref, w0_ref, w2_ref, a0_ref, a2_ref, g2_ref, kkw_ref, ka_ref,
                     rk_ref, lnw_ref, lnb_ref, s_ref, o_ref, so_ref, vt, st, osc):
    h = pl.program_id(0)
    r, lw, k2, v, kkn, a, g = _rwkv_token_mix(c_ref[...], sh_ref[...], mu_ref, w0_ref, w2_ref, a0_ref, a2_ref,
                                              g2_ref, kkw_ref, ka_ref)
    for idx, arr in enumerate((r, jnp.exp(lw), k2, v, kkn, a, g)):
        vt[idx] = arr.T
    r0 = pl.multiple_of(h * HEAD_DIM, HEAD_DIM)
    rows = pl.ds(r0, HEAD_DIM)
    rh, wh, kh, kkh, ah = vt[0, rows, :], vt[1, rows, :], vt[2, rows, :], vt[4, rows, :], vt[5, rows, :]
    vh, gh = vt[3, rows, :], vt[6, rows, :]
    kka = kkh * ah
    _load_state_t(s_ref, st)

    def body(vi, carry):
        s_old = st[vi]
        sa = jnp.sum(s_old * (-kkh), axis=0, keepdims=True)
        s_new = s_old * wh + sa * kka + vt[3, pl.ds(r0 + vi, 1), :] * kh
        st[vi] = s_new
        osc[pl.ds(vi, 1), :] = jnp.sum(s_new * rh, axis=0, keepdims=True)
        return carry

    lax.fori_loop(0, HEAD_DIM, body, 0)
    o = osc[...]
    mu = jnp.mean(o, axis=0, keepdims=True)
    d = o - mu
    var = jnp.mean(d * d, axis=0, keepdims=True)
    on = d * lax.rsqrt(var + RW_GN_EPS) * lnw_ref[...] + lnb_ref[...]
    bonus = jnp.sum(rh * kh * rk_ref[...], axis=0, keepdims=True) * vh
    o_ref[...] = (on + bonus) * gh
    _store_state_t(so_ref, st)


def _dec_specs():
    head_tab = pl.BlockSpec((HEAD_DIM, DEC_N), lambda h: (h, 0))
    state = pl.BlockSpec((DEC_N, HEAD_STATE), lambda h: (0, h))
    out = pl.BlockSpec((None, HEAD_DIM, DEC_N), lambda h: (h, 0, 0))
    return head_tab, state, out


def _dec_out_shapes():
    return [jax.ShapeDtypeStruct((N_HEADS, HEAD_DIM, DEC_N), F32),
            jax.ShapeDtypeStruct((DEC_N, N_HEADS * HEAD_STATE), F32)]


def _dec_finish(o_t, s_new):
    return o_t.reshape(BRANCH_W, DEC_N).T, s_new.reshape(DEC_N, N_HEADS, HEAD_DIM, HEAD_DIM)


def _ret_decode(cols, state, cos_c, sin_c, lp):
    head_tab, st_spec, out_spec = _dec_specs()
    same = pl.BlockSpec((HEAD_DIM, DEC_N), lambda h: (0, 0))
    o_t, s_new = pl.pallas_call(
        _ret_dec_kernel,
        grid=(N_HEADS,),
        in_specs=[pl.BlockSpec((DEC_N, 1024), lambda h: (0, 0)), same, same, head_tab, head_tab, st_spec],
        out_specs=[out_spec, st_spec],
        out_shape=_dec_out_shapes(),
        scratch_shapes=[pltpu.VMEM((1024, DEC_N), F32), pltpu.VMEM((2, HEAD_DIM, DEC_N), F32),
                        pltpu.VMEM((HEAD_DIM, HEAD_DIM, DEC_N), F32)],
        compiler_params=_cparams(("arbitrary",), VMEM_DECODE_MIB),
        name="ret_decode",
    )(cols, cos_c, sin_c, _col(lp['ret_gn_w']), _col(lp['ret_gn_b']), state.reshape(DEC_N, -1))
    return _dec_finish(o_t, s_new)


def _hgrn_decode(cols, state, lp):
    head_tab, st_spec, out_spec = _dec_specs()
    o_t, s_new = pl.pallas_call(
        _hgrn_dec_kernel,
        grid=(N_HEADS,),
        in_specs=[pl.BlockSpec((DEC_N, 1024), lambda h: (0, 1)), head_tab, head_tab, st_spec],
        out_specs=[out_spec, st_spec],
        out_shape=_dec_out_shapes(),
        scratch_shapes=[pltpu.VMEM((1024, DEC_N), F32), pltpu.VMEM((3, HEAD_DIM, DEC_N), F32),
                        pltpu.VMEM((HEAD_DIM, HEAD_DIM, DEC_N), F32)],
        compiler_params=_cparams(("arbitrary",), VMEM_DECODE_MIB),
        name="hgrn_decode",
    )(cols, _col(lp['hg_lb']), _col(lp['hg_norm_w']), state.reshape(DEC_N, -1))
    return _dec_finish(o_t, s_new)


def _rwkv_decode(cols, shift, state, lp):
    head_tab, st_spec, out_spec = _dec_specs()
    row = lambda w: pl.BlockSpec((1, w), lambda h: (0, 0))
    mat = lambda r, w: pl.BlockSpec((r, w), lambda h: (0, 0))
    o_t, s_new = pl.pallas_call(
        _rwkv_dec_kernel,
        grid=(N_HEADS,),
        in_specs=[pl.BlockSpec((DEC_N, 1024), lambda h: (0, 2)), mat(DEC_N, 1024),
                  row(1024), row(256), mat(64, 256), row(256), mat(64, 256), mat(128, 256),
                  row(256), row(256), head_tab, head_tab, head_tab, st_spec],
        out_specs=[out_spec, st_spec],
        out_shape=_dec_out_shapes(),
        scratch_shapes=[pltpu.VMEM((7, BRANCH_W, DEC_N), F32), pltpu.VMEM((HEAD_DIM, HEAD_DIM, DEC_N), F32),
                        pltpu.VMEM((HEAD_DIM, DEC_N), F32)],
        compiler_params=_cparams(("arbitrary",), VMEM_DECODE_MIB),
        name="rwkv_decode",
    )(cols, shift, lp['rw_mu'], lp['rw_w0'], lp['rw_w2'], lp['rw_a0'], lp['rw_a2'], lp['rw_g2'],
      lp['rw_kk'], lp['rw_ka'], _col(lp['rw_rk']), _col(lp['rw_lnx_w']), _col(lp['rw_lnx_b']),
      state.reshape(DEC_N, -1))
    return _dec_finish(o_t, s_new)


def _lru_dec_kernel(c_ref, conv_ref, h0_ref, cw_ref, cb_ref, wa_ref, ba_ref, wx_ref, bx_ref, lam_ref,
                    y_ref, h_ref, nconv_ref):
    xb = c_ref[:, 0:BRANCH_W]
    gate = c_ref[:, BRANCH_W:2 * BRANCH_W]
    c0, c1, c2 = conv_ref[0], conv_ref[1], conv_ref[2]
    xc = (c0 * cw_ref[0:1, :] + c1 * cw_ref[1:2, :] + c2 * cw_ref[2:3, :] + xb * cw_ref[3:4, :]) + cb_ref[...]
    a, u = _lru_gates(xc, wa_ref, ba_ref, wx_ref, bx_ref, lam_ref)
    hn = a * h0_ref[...] + u
    h_ref[...] = hn
    y_ref[...] = hn * _gelu(gate)
    nconv_ref[0] = c1
    nconv_ref[1] = c2
    nconv_ref[2] = xb


def _lru_decode(cols, conv, h0, lp):
    full = lambda *s: pl.BlockSpec(s, lambda i: (0,) * len(s))
    return pl.pallas_call(
        _lru_dec_kernel,
        grid=(1,),
        in_specs=[pl.BlockSpec((DEC_N, 512), lambda i: (0, 6)), full(3, DEC_N, BRANCH_W), full(DEC_N, BRANCH_W),
                  full(CONV_W, BRANCH_W), full(1, BRANCH_W), full(BRANCH_W, BRANCH_W), full(1, BRANCH_W),
                  full(BRANCH_W, BRANCH_W), full(1, BRANCH_W), full(1, BRANCH_W)],
        out_specs=[full(DEC_N, BRANCH_W), full(DEC_N, BRANCH_W), full(3, DEC_N, BRANCH_W)],
        out_shape=[jax.ShapeDtypeStruct((DEC_N, BRANCH_W), F32), jax.ShapeDtypeStruct((DEC_N, BRANCH_W), F32),
                   jax.ShapeDtypeStruct((3, DEC_N, BRANCH_W), F32)],
        compiler_params=_cparams(("arbitrary",), VMEM_SMALL_MIB),
        name="lru_decode",
    )(cols, conv, h0, lp['lru_conv_w'], lp['lru_conv_b'], lp['lru_wa_bd'], lp['lru_ba'], lp['lru_wx_bd'],
      lp['lru_bx'], lp['lru_lambda'])


def _prompt_layer(x, p_emb, lp, cos_t, sin_t):
    b, t, d = x.shape
    n = b * t
    xf = x.reshape(n, d)
    o_a, s_ret = _ret_prompt(x, cos_t, sin_t, lp)
    o_b, s_hg = _hgrn_prompt(x, lp['hg_lb'], lp)
    o_c, s_shift, s_rw = _rwkv_prompt(x, lp)
    o_d, s_lru, s_conv = _lru_prompt(x, lp)
    outs = [o.reshape(n, BRANCH_W) for o in (o_a, o_b, o_c, o_d)]
    x1 = _mix(xf, outs, lp, 512)
    x2 = _peer(x1, lp, 512)
    x3 = _ple(x2, p_emb.reshape(n, -1), lp, 512)
    return x3.reshape(b, t, d), (s_ret, s_hg, s_rw, s_shift[:, 0], s_lru[:, 0], s_conv)


def _sample_layer(x, p_emb, state, lp, cos_c, sin_c):
    s_ret, s_hg, s_rw, s_shift, s_lru, s_conv = state
    xf = x.reshape(DEC_N, D_MODEL)
    cols = _proj(xf, lp['w_in_br'], DEC_N)
    o_a, s_ret = _ret_decode(cols, s_ret, cos_c, sin_c, lp)
    o_b, s_hg = _hgrn_decode(cols, s_hg, lp)
    o_c, s_rw = _rwkv_decode(cols, s_shift, s_rw, lp)
    o_d, s_lru, s_conv_t = _lru_decode(cols, jnp.swapaxes(s_conv, 0, 1), s_lru, lp)
    x1 = _mix(xf, [o_a, o_b, o_c, o_d], lp, DEC_N)
    x2 = _peer(x1, lp, DEC_N)
    x3 = _ple(x2, p_emb.reshape(DEC_N, -1), lp, DEC_N)
    new_shift = cols[:, 2048:3072]
    return x3.reshape(x.shape), (s_ret, s_hg, s_rw, new_shift, s_lru, jnp.swapaxes(s_conv_t, 0, 1))


def kernel(x_prompt, x_sample, state_ret, state_hgrn, state_rwkv, state_shift, state_lru, state_conv, p_prompt, p_sample, w_in, ret_gn_w, ret_gn_b, hg_lb, hg_norm_w, rw_mu, rw_w0, rw_w2, rw_a0, rw_a2, rw_g2, rw_kk, rw_ka, rw_rk, rw_lnx_w, rw_lnx_b, lru_conv_w, lru_conv_b, lru_wa, lru_ba, lru_wx, lru_bx, lru_lambda, w_branch, w_out, ln1_w, ln1_b, peer_wq, peer_keys, peer_u, peer_v, ln2_w, ln2_b, ple_w, ple_gate_w, ple_gate_b):
    params = dict(w_in=w_in, ret_gn_w=ret_gn_w, ret_gn_b=ret_gn_b, hg_norm_w=hg_norm_w, rw_mu=rw_mu, rw_w0=rw_w0,
                  rw_w2=rw_w2, rw_a0=rw_a0, rw_a2=rw_a2, rw_g2=rw_g2, rw_kk=rw_kk, rw_ka=rw_ka, rw_rk=rw_rk,
                  rw_lnx_w=rw_lnx_w, rw_lnx_b=rw_lnx_b, lru_conv_w=lru_conv_w, lru_conv_b=lru_conv_b,
                  lru_wa=lru_wa, lru_ba=lru_ba, lru_wx=lru_wx, lru_bx=lru_bx, lru_lambda=lru_lambda,
                  w_branch=w_branch, w_out=w_out, ln1_w=ln1_w, ln1_b=ln1_b, peer_wq=peer_wq, peer_keys=peer_keys,
                  peer_u=peer_u, peer_v=peer_v, ln2_w=ln2_w, ln2_b=ln2_b, ple_w=ple_w, ple_gate_w=ple_gate_w,
                  ple_gate_b=ple_gate_b)
    lb_cum = jnp.cumsum(jax.nn.softmax(hg_lb.astype(F32), axis=0), axis=0)
    lb_all = lb_cum - lb_cum[0:1]
    t_prompt = x_prompt.shape[1]
    past_len = 16384
    cos_t, sin_t = _rope_tables(jnp.arange(t_prompt))
    cos_s, sin_s = _rope_tables(past_len + jnp.arange(1))
    cos_c = _col(cos_s[0, :HEAD_DIM])
    sin_c = _col(sin_s[0, :HEAD_DIM])
    h_p, h_s = x_prompt, x_sample
    new_p, new_s = [], []
    for l in range(DEPTH):
        lp = _layer_params(params, l, lb_all)
        h_p, st_p = _prompt_layer(h_p, p_prompt[l], lp, cos_t, sin_t)
        st_in = (state_ret[l], state_hgrn[l], state_rwkv[l], state_shift[l], state_lru[l], state_conv[l])
        h_s, st_s = _sample_layer(h_s, p_sample[l], st_in, lp, cos_c, sin_c)
        new_p.append(st_p)
        new_s.append(st_s)
    outs_p = [jnp.stack(z) for z in zip(*new_p)]
    outs_s = [jnp.stack(z) for z in zip(*new_s)]
    return (h_p, h_s, *outs_p, *outs_s)
```
